```python
import math
import jax, jax.numpy as jnp
from jax import lax
import numpy as np

D_MODEL = 1024
BATCH = 1
SEQ = 16384
DEPTH = 4

N_EVEN = (DEPTH + 1) // 2
N_ODD = DEPTH // 2
RMS_EPS = 1e-6
DA_HEADS = 4
DA_HEAD_DIM = 64
DA_V_DIM = 2 * DA_HEAD_DIM
DA_QK_WIDTH = 2 * DA_HEADS * DA_HEAD_DIM
DA_V_WIDTH = DA_HEADS * DA_V_DIM
Q_BLOCK = 128
ROPE_THETA = 500000.0
ROT_DIM = DA_HEAD_DIM // 4
SUBLN_EPS = 1e-5
SC_WIDTH = 512
SC_GROUPS = 8
SC_KERNEL = 3
EVEN_IN_WIDTH = 2 * DA_QK_WIDTH + DA_V_WIDTH + 3 * SC_WIDTH
EVEN_MIX_WIDTH = DA_V_WIDTH + SC_WIDTH
DN_HEADS = 8
DN_HEAD_DIM = 128
DN_WIDTH = DN_HEADS * DN_HEAD_DIM
DN_CONV = 4
DN_CHUNK = 64
ODD_IN_WIDTH = 4 * DN_WIDTH + 2 * DN_HEADS
D_FF = 3584
N_EXPERTS = 8
TOP_K = 2
EXPERT_D_FF = 3584
MOE_BLOCK = 256
PLE_DIM = 256

kernel_name = "hybrid_diffattn_shortconv_gdn_moe_trunk"


def rmsnorm(x, g, eps=RMS_EPS):
    xf = x.astype(jnp.float32)
    y = xf * lax.rsqrt(jnp.mean(xf * xf, axis=-1, keepdims=True) + eps)
    return (y * g.astype(jnp.float32)).astype(x.dtype)


def split_cols(t, widths):
    idx = np.cumsum(widths)[:-1].tolist()
    return jnp.split(t, idx, axis=-1)


def causal_dwconv(u, w):
    K = w.shape[0]
    S = u.shape[1]
    up = jnp.pad(u, ((0, 0), (K - 1, 0), (0, 0)))
    out = up[:, 0:S] * w[0]
    for j in range(1, K):
        out = out + up[:, j:j + S] * w[j]
    return out


def partial_rotary(x, positions):
    inv_freq = ROPE_THETA ** (-jnp.arange(0, ROT_DIM, 2, dtype=jnp.float32) / ROT_DIM)
    ang = positions.astype(jnp.float32)[..., None] * inv_freq
    cos = jnp.cos(ang)[:, :, None, :]
    sin = jnp.sin(ang)[:, :, None, :]
    xr = x[..., :ROT_DIM].astype(jnp.float32)
    x1, x2 = xr[..., :ROT_DIM // 2], xr[..., ROT_DIM // 2:]
    rot = jnp.concatenate([x1 * cos - x2 * sin, x2 * cos + x1 * sin], axis=-1).astype(x.dtype)
    return jnp.concatenate([rot, x[..., ROT_DIM:]], axis=-1)


def lambda_init(layer_idx):
    return 0.8 - 0.6 * math.exp(-0.3 * layer_idx)


def diff_attention(q, k, v, lam, positions):
    Bsz, H2, S, dh = q.shape
    H = H2 // 2
    nqb = S // Q_BLOCK
    qf = (q.astype(jnp.float32) * (dh ** -0.5)).reshape(Bsz, H2, nqb, Q_BLOCK, dh).transpose(2, 0, 1, 3, 4)
    qpos = positions.reshape(Bsz, nqb, Q_BLOCK).transpose(1, 0, 2)
    kf = k.astype(jnp.float32)
    vf = v.astype(jnp.float32)

    def block(args):
        qb, pq = args
        s = jnp.einsum('bhqd,bhkd->bhqk', qb, kf)
        mask = positions[:, None, None, :] <= pq[:, None, :, None]
        s = jnp.where(mask, s, -jnp.inf)
        pm = jax.nn.softmax(s, axis=-1).reshape(Bsz, H, 2, Q_BLOCK, S)
        a = pm[:, :, 0] - lam * pm[:, :, 1]
        return jnp.einsum('bhqk,bhkv->bhqv', a, vf)

    out = lax.map(block, (qf, qpos))
    return out.transpose(1, 2, 0, 3, 4).reshape(Bsz, H, S, 2 * dh)


def even_mixer(hn, positions, w_in, w_out, lam_q1, lam_k1, lam_q2, lam_k2, subln_g, conv_w, layer_idx):
    Bsz, S, _ = hn.shape
    proj = hn @ w_in
    q, k, v, b_gate, c_gate, xb = split_cols(
        proj, [DA_QK_WIDTH, DA_QK_WIDTH, DA_V_WIDTH, SC_WIDTH, SC_WIDTH, SC_WIDTH])
    q = partial_rotary(q.reshape(Bsz, S, 2 * DA_HEADS, DA_HEAD_DIM), positions)
    k = partial_rotary(k.reshape(Bsz, S, 2 * DA_HEADS, DA_HEAD_DIM), positions)
    v = v.reshape(Bsz, S, DA_HEADS, DA_V_DIM)
    lam_0 = lambda_init(layer_idx)
    f32 = jnp.float32
    lam = (jnp.exp(jnp.sum(lam_q1.astype(f32) * lam_k1.astype(f32)))
           - jnp.exp(jnp.sum(lam_q2.astype(f32) * lam_k2.astype(f32))) + lam_0)
    o = diff_attention(q.transpose(0, 2, 1, 3), k.transpose(0, 2, 1, 3), v.transpose(0, 2, 1, 3), lam, positions)
    o = rmsnorm(o, subln_g, SUBLN_EPS) * (1.0 - lam_0)
    o = o.transpose(0, 2, 1, 3).reshape(Bsz, S, DA_V_WIDTH).astype(hn.dtype)
    sc = b_gate * causal_dwconv(c_gate * xb, conv_w)
    return jnp.concatenate([o, sc], axis=-1) @ w_out


def l2norm(x, eps=1e-6):
    return x * lax.rsqrt(jnp.sum(x * x, axis=-1, keepdims=True) + eps)


def chunk_gated_delta_rule(q, k, v, g, beta):
    Bsz, S, H, dk = q.shape
    dv = v.shape[-1]
    C = DN_CHUNK
    N = S // C

    def chunks(t):
        t = jnp.moveaxis(t, 2, 1)
        return t.reshape(t.shape[:2] + (N, C) + t.shape[3:])

    q = chunks(q * (dk ** -0.5))
    k = chunks(k)
    v = chunks(v)
    g = jnp.cumsum(chunks(g), axis=-1)
    beta = chunks(beta)
    k_beta = k * beta[..., None]
    v_beta = v * beta[..., None]
    incl = jnp.tril(jnp.ones((C, C), dtype=bool))
    strict = jnp.tril(jnp.ones((C, C), dtype=bool), -1)
    gdiff = g[..., :, None] - g[..., None, :]
    decay = jnp.where(incl, jnp.exp(jnp.where(incl, gdiff, 0.0)), 0.0)
    L = jnp.where(strict, jnp.einsum('bhnid,bhnjd->bhnij', k_beta, k) * decay, 0.0)
    T = L + jnp.eye(C, dtype=jnp.float32)
    u = lax.linalg.triangular_solve(T, v_beta, left_side=True, lower=True, unit_diagonal=True)
    w = lax.linalg.triangular_solve(T, k_beta * jnp.exp(g)[..., None], left_side=True, lower=True,
                                    unit_diagonal=True)
    a_intra = jnp.einsum('bhnid,bhnjd->bhnij', q, k) * decay

    def step(state, xs):
        qi, ki, ui, wi, gi, ai = xs
        v_new = ui - jnp.einsum('bhcd,bhdv->bhcv', wi, state)
        o = (jnp.einsum('bhcd,bhdv->bhcv', qi * jnp.exp(gi)[..., None], state)
             + jnp.einsum('bhij,bhjv->bhiv', ai, v_new))
        g_last = gi[..., -1:]
        state = (state * jnp.exp(g_last)[..., None]
                 + jnp.einsum('bhcd,bhcv->bhdv', ki * jnp.exp(g_last - gi)[..., None], v_new))
        return state, o

    xs = tuple(jnp.moveaxis(t, 2, 0) for t in (q, k, u, w, g, a_intra))
    state0 = jnp.zeros((Bsz, H, dk, dv), jnp.float32)
    _, o = lax.scan(step, state0, xs)
    o = jnp.moveaxis(o, 0, 2).reshape(Bsz, H, S, dv)
    return jnp.moveaxis(o, 1, 2)


def odd_mixer(hn, w_in, conv_w, a_log, dt_bias, onorm_g, w_out):
    Bsz, S, _ = hn.shape
    f32 = jnp.float32
    proj = hn @ w_in
    qkv, z, b, a = split_cols(proj, [3 * DN_WIDTH, DN_WIDTH, DN_HEADS, DN_HEADS])
    qkv = jax.nn.silu(causal_dwconv(qkv, conv_w))
    q, k, v = jnp.split(qkv.astype(f32), 3, axis=-1)
    q = l2norm(q.reshape(Bsz, S, DN_HEADS, DN_HEAD_DIM))
    k = l2norm(k.reshape(Bsz, S, DN_HEADS, DN_HEAD_DIM))
    v = v.reshape(Bsz, S, DN_HEADS, DN_HEAD_DIM)
    beta = jax.nn.sigmoid(b.astype(f32))
    g = -jnp.exp(a_log.astype(f32)) * jax.nn.softplus(a.astype(f32) + dt_bias.astype(f32))
    o = chunk_gated_delta_rule(q, k, v, g, beta)
    o = rmsnorm(o, onorm_g) * jax.nn.silu(z.astype(f32).reshape(Bsz, S, DN_HEADS, DN_HEAD_DIM))
    return o.reshape(Bsz, S, DN_WIDTH).astype(hn.dtype) @ w_out


def swiglu(xn, w1, w3, w2):
    return (jax.nn.silu(xn @ w1) * (xn @ w3)) @ w2


def moe_swiglu(xn, w_router, w1, w3, w2):
    Bsz, S, D = xn.shape
    T = Bsz * S
    xt = xn.reshape(T, D)
    logits = (xt @ w_router).astype(jnp.float32)
    top_logit, top_idx = lax.top_k(logits, TOP_K)
    gates = jax.nn.softmax(top_logit, axis=-1).astype(xn.dtype)
    n_assign = T * TOP_K
    e_flat = top_idx.reshape(-1).astype(jnp.int32)
    tok_flat = jnp.repeat(jnp.arange(T, dtype=jnp.int32), TOP_K)
    g_flat = gates.reshape(-1)
    order = jnp.argsort(e_flat)
    e_sorted = e_flat[order]
    counts = jnp.zeros((N_EXPERTS,), jnp.int32).at[e_flat].add(1)
    starts = jnp.cumsum(counts) - counts
    padded = ((counts + MOE_BLOCK - 1) // MOE_BLOCK) * MOE_BLOCK
    pends = jnp.cumsum(padded)
    pstarts = pends - padded
    dest = pstarts[e_sorted] + (jnp.arange(n_assign, dtype=jnp.int32) - starts[e_sorted])
    P = ((n_assign + MOE_BLOCK - 1) // MOE_BLOCK) * MOE_BLOCK + N_EXPERTS * MOE_BLOCK
    NB = P // MOE_BLOCK
    buf_tok = jnp.full((P,), T, jnp.int32).at[dest].set(tok_flat[order])
    buf_gate = jnp.zeros((P,), xn.dtype).at[dest].set(g_flat[order])
    blk_e = jnp.minimum(
        jnp.searchsorted(pends, jnp.arange(NB, dtype=jnp.int32) * MOE_BLOCK, side='right'), N_EXPERTS - 1)
    x_pad = jnp.concatenate([xt, jnp.zeros((1, D), xt.dtype)], axis=0)

    def run_block(args):
        tok, gate, e = args
        xb = x_pad[tok]
        hb = jax.nn.silu(xb @ w1[e]) * (xb @ w3[e])
        return (hb @ w2[e]) * gate[:, None]

    y = lax.map(run_block, (buf_tok.reshape(NB, MOE_BLOCK), buf_gate.reshape(NB, MOE_BLOCK), blk_e))
    out = jnp.zeros((T + 1, D), xn.dtype).at[buf_tok].add(y.reshape(P, D))
    return out[:T].reshape(Bsz, S, D)


def setup_inputs(seed: int = 0) -> dict:
    key = jax.random.key(seed)
    ks = jax.random.split(key, 32)
    f32 = jnp.float32

    def nrm(k, shape, fan_in):
        return jax.random.normal(k, shape, f32) * (fan_in ** -0.5)

    def gain(k, shape):
        return 1.0 + 0.05 * jax.random.normal(k, shape, f32)

    x = jax.random.normal(ks[0], (BATCH, SEQ, D_MODEL), f32)
    p = jax.random.normal(ks[1], (DEPTH, BATCH, SEQ, PLE_DIM), f32)
    positions = jnp.tile(jnp.arange(SEQ, dtype=jnp.int32)[None, :], (BATCH, 1))
    ln_mix = gain(ks[2], (DEPTH, D_MODEL))
    ln_ffn = gain(ks[3], (DEPTH, D_MODEL))
    ln_ple = gain(ks[4], (DEPTH, D_MODEL))
    ln_final = gain(ks[5], (D_MODEL,))
    w_in_even = nrm(ks[6], (N_EVEN, D_MODEL, EVEN_IN_WIDTH), D_MODEL)
    w_out_even = nrm(ks[7], (N_EVEN, EVEN_MIX_WIDTH, D_MODEL), EVEN_MIX_WIDTH)
    lam_q1 = 0.1 * jax.random.normal(ks[8], (N_EVEN, DA_HEAD_DIM), f32)
    lam_k1 = 0.1 * jax.random.normal(ks[9], (N_EVEN, DA_HEAD_DIM), f32)
    lam_q2 = 0.1 * jax.random.normal(ks[10], (N_EVEN, DA_HEAD_DIM), f32)
    lam_k2 = 0.1 * jax.random.normal(ks[11], (N_EVEN, DA_HEAD_DIM), f32)
    subln_gain = gain(ks[12], (N_EVEN, DA_V_DIM))
    conv_w_short = nrm(ks[13], (N_EVEN, SC_KERNEL, SC_WIDTH), SC_KERNEL)
    w_in_odd = nrm(ks[14], (N_ODD, D_MODEL, ODD_IN_WIDTH), D_MODEL)
    conv_w_qkv = nrm(ks[15], (N_ODD, DN_CONV, 3 * DN_WIDTH), DN_CONV)
    a_log = jnp.log(jax.random.uniform(ks[16], (N_ODD, DN_HEADS), f32, 1.0, 16.0))
    dt = jnp.exp(jax.random.uniform(ks[17], (N_ODD, DN_HEADS), f32, math.log(1e-3), math.log(1e-1)))
    dt_bias = dt + jnp.log(-jnp.expm1(-dt))
    onorm_gain = gain(ks[18], (N_ODD, DN_HEAD_DIM))
    w_out_odd = nrm(ks[19], (N_ODD, DN_WIDTH, D_MODEL), DN_WIDTH)
    w1_dense = nrm(ks[20], (N_EVEN, D_MODEL, D_FF), D_MODEL)
    w3_dense = nrm(ks[21], (N_EVEN, D_MODEL, D_FF), D_MODEL)
    w2_dense = nrm(ks[22], (N_EVEN, D_FF, D_MODEL), D_FF)
    w_router = nrm(ks[23], (N_ODD, D_MODEL, N_EXPERTS), D_MODEL)
    w1_moe = nrm(ks[24], (N_ODD, N_EXPERTS, D_MODEL, EXPERT_D_FF), D_MODEL)
    w3_moe = nrm(ks[25], (N_ODD, N_EXPERTS, D_MODEL, EXPERT_D_FF), D_MODEL)
    w2_moe = nrm(ks[26], (N_ODD, N_EXPERTS, EXPERT_D_FF, D_MODEL), EXPERT_D_FF)
    w_ple_gate = nrm(ks[27], (DEPTH, D_MODEL, D_MODEL), D_MODEL)
    w_ple_proj = nrm(ks[28], (DEPTH, PLE_DIM, D_MODEL), PLE_DIM)
    return {"x": x, "p": p, "positions": positions, "ln_mix": ln_mix, "ln_ffn": ln_ffn,
            "ln_ple": ln_ple, "ln_final": ln_final, "w_in_even": w_in_even, "w_out_even": w_out_even,
            "lam_q1": lam_q1, "lam_k1": lam_k1, "lam_q2": lam_q2, "lam_k2": lam_k2,
            "subln_gain": subln_gain, "conv_w_short": conv_w_short, "w_in_odd": w_in_odd,
            "conv_w_qkv": conv_w_qkv, "a_log": a_log, "dt_bias": dt_bias, "onorm_gain": onorm_gain,
            "w_out_odd": w_out_odd, "w1_dense": w1_dense, "w3_dense": w3_dense, "w2_dense": w2_dense,
            "w_router": w_router, "w1_moe": w1_moe, "w3_moe": w3_moe, "w2_moe": w2_moe,
            "w_ple_gate": w_ple_gate, "w_ple_proj": w_ple_proj}


def reference(x, p, positions, ln_mix, ln_ffn, ln_ple, ln_final, w_in_even, w_out_even,
              lam_q1, lam_k1, lam_q2, lam_k2, subln_gain, conv_w_short, w_in_odd, conv_w_qkv,
              a_log, dt_bias, onorm_gain, w_out_odd, w1_dense, w3_dense, w2_dense,
              w_router, w1_moe, w3_moe, w2_moe, w_ple_gate, w_ple_proj):
    h = x
    for i in range(DEPTH):
        j = i // 2
        hn = rmsnorm(h, ln_mix[i])
        if i % 2 == 0:
            h = h + even_mixer(hn, positions, w_in_even[j], w_out_even[j], lam_q1[j], lam_k1[j],
                               lam_q2[j], lam_k2[j], subln_gain[j], conv_w_short[j], i)
            hn = rmsnorm(h, ln_ffn[i])
            h = h + swiglu(hn, w1_dense[j], w3_dense[j], w2_dense[j])
        else:
            h = h + odd_mixer(hn, w_in_odd[j], conv_w_qkv[j], a_log[j], dt_bias[j], onorm_gain[j],
                              w_out_odd[j])
            hn = rmsnorm(h, ln_ffn[i])
            h = h + moe_swiglu(hn, w_router[j], w1_moe[j], w3_moe[j], w2_moe[j])
        gate = jax.nn.sigmoid(rmsnorm(h, ln_ple[i]) @ w_ple_gate[i])
        h = h + gate * (p[i] @ w_ple_proj[i])
    return rmsnorm(h, ln_final)
```

```python
import functools
import math

import jax
import jax.numpy as jnp
import numpy as np
from jax import lax
from jax.experimental import pallas as pl
from jax.experimental.pallas import tpu as pltpu

F32 = jnp.float32
BF16 = jnp.bfloat16
I32 = jnp.int32

D_MODEL = 1024
DEPTH = 4
RMS_EPS = 1e-6
DA_HEADS = 4
DA_HEAD_DIM = 64
DA_V_DIM = 2 * DA_HEAD_DIM
ROPE_THETA = 500000.0
ROT_DIM = DA_HEAD_DIM // 4
SUBLN_EPS = 1e-5
SC_WIDTH = 512
DN_HEADS = 8
DN_HEAD_DIM = 128
DN_WIDTH = DN_HEADS * DN_HEAD_DIM
DN_CHUNK = 64
D_FF = 3584
N_EXPERTS = 8
PLE_DIM = 256

LANES = 128
SUBLANES = 8
VMEM_LIMIT = 52 * 1024 * 1024

NEG_BIG = -1e30

ROW_TILE = 512
FFN_ROW_TILE = 1024
FF_TILE = 512
ATT_TQ = 1024
ATT_TK = 512
GDN_TILE = 512
MOE_TILE = 512


def _cparams(sem):
    return pltpu.CompilerParams(dimension_semantics=sem, vmem_limit_bytes=VMEM_LIMIT)


def _rms(x, g, eps):
    ms = jnp.mean(x * x, axis=-1, keepdims=True)
    return x * lax.rsqrt(ms + eps) * g


def _silu(x):
    return x * (1.0 / (1.0 + jnp.exp(-x)))


def _sigmoid(x):
    return 1.0 / (1.0 + jnp.exp(-x))


def _norm_matmul_kernel(x_ref, g_ref, w_ref, o_ref, xn_ref):
    @pl.when(pl.program_id(1) == 0)
    def _():
        xn_ref[...] = _rms(x_ref[...], g_ref[...], RMS_EPS).astype(BF16)

    o_ref[...] = jnp.dot(xn_ref[...], w_ref[...], preferred_element_type=F32).astype(o_ref.dtype)


def norm_matmul(x, g, w, tn, out_dtype=F32):
    T, D = x.shape
    N = w.shape[1]
    tm = min(ROW_TILE, T)
    return pl.pallas_call(
        _norm_matmul_kernel,
        grid=(T // tm, N // tn),
        in_specs=[
            pl.BlockSpec((tm, D), lambda i, j: (i, 0)),
            pl.BlockSpec((1, D), lambda i, j: (0, 0)),
            pl.BlockSpec((D, tn), lambda i, j: (0, j)),
        ],
        out_specs=pl.BlockSpec((tm, tn), lambda i, j: (i, j)),
        out_shape=jax.ShapeDtypeStruct((T, N), out_dtype),
        scratch_shapes=[pltpu.VMEM((tm, D), BF16)],
        compiler_params=_cparams(("parallel", "arbitrary")),
        name="norm_matmul",
    )(x, g.reshape(1, D), w)


def _rope_kernel(proj_ref, pos_ref, invf_ref, o_ref):
    pos = pos_ref[...].astype(F32)
    ang = pos * invf_ref[...]
    cos_t = jnp.cos(ang)
    sin_t = jnp.sin(ang)
    d = lax.broadcasted_iota(I32, ang.shape, 1) & (DA_HEAD_DIM - 1)
    half = ROT_DIM // 2
    c_mul = jnp.where(d < ROT_DIM, cos_t, 1.0)
    s_mul = jnp.where(d < half, -sin_t, jnp.where(d < ROT_DIM, sin_t, 0.0))
    n_qk = 2 * DA_HEADS * DA_HEAD_DIM // LANES
    for c in range(2 * n_qk):
        x = proj_ref[:, c * LANES:(c + 1) * LANES]
        swapped = jnp.where(d < half, pltpu.roll(x, LANES - half, 1), pltpu.roll(x, half, 1))
        r = x * c_mul + swapped * s_mul
        if c < n_qk:
            r = r * (DA_HEAD_DIM ** -0.5)
        o_ref[:, c * LANES:(c + 1) * LANES] = r.astype(BF16)
    v0 = 2 * n_qk * LANES
    o_ref[:, v0:] = proj_ref[:, v0:].astype(BF16)


def rope_qkv(proj, positions_col, invf):
    T = proj.shape[0]
    W = 2 * 2 * DA_HEADS * DA_HEAD_DIM + DA_HEADS * DA_V_DIM
    tm = min(ROW_TILE, T)
    return pl.pallas_call(
        _rope_kernel,
        grid=(T // tm,),
        in_specs=[
            pl.BlockSpec((tm, W), lambda i: (i, 0)),
            pl.BlockSpec((tm, 1), lambda i: (i, 0)),
            pl.BlockSpec((1, LANES), lambda i: (0, 0)),
        ],
        out_specs=pl.BlockSpec((tm, W), lambda i: (i, 0)),
        out_shape=jax.ShapeDtypeStruct((T, W), BF16),
        compiler_params=_cparams(("parallel",)),
        name="rope_qkv",
    )(proj, positions_col, invf)


def _attn_kernel(qt_ref, kt_ref, q_ref, k_ref, v_ref, pq_ref, pk_ref, lam_ref, g_ref, o_ref,
                 m_ref, l_ref, acc_ref, *, tq, tk, lam0):
    p = pl.program_id(1)
    qi = qt_ref[p]
    ki = kt_ref[p]

    @pl.when(ki == 0)
    def _():
        m_ref[...] = jnp.full(m_ref.shape, NEG_BIG, F32)
        l_ref[...] = jnp.zeros(l_ref.shape, F32)
        acc_ref[...] = jnp.zeros(acc_ref.shape, F32)

    q = q_ref[...]
    k = k_ref[...]
    v = v_ref[...]
    mask = pk_ref[...] <= pq_ref[...]
    for s in range(2):
        qs = q[:, s * DA_HEAD_DIM:(s + 1) * DA_HEAD_DIM]
        ks = k[:, s * DA_HEAD_DIM:(s + 1) * DA_HEAD_DIM]
        sc = lax.dot_general(qs, ks, (((1,), (1,)), ((), ())), preferred_element_type=F32)
        sc = jnp.where(mask, sc, NEG_BIG)
        m_prev = m_ref[s]
        m_new = jnp.maximum(m_prev, jnp.max(sc, axis=-1, keepdims=True))
        alpha = jnp.exp(m_prev - m_new)
        pr = jnp.exp(sc - m_new)
        l_ref[s] = alpha * l_ref[s] + jnp.sum(pr, axis=-1, keepdims=True)
        acc_ref[s] = alpha * acc_ref[s] + jnp.dot(pr.astype(BF16), v, preferred_element_type=F32)
        m_ref[s] = m_new

    @pl.when((ki + 1) * tk >= (qi + 1) * tq)
    def _():
        lm = lam_ref[...]
        s1 = jnp.sum(lm[0:1] * lm[1:2], axis=-1, keepdims=True)
        s2 = jnp.sum(lm[2:3] * lm[3:4], axis=-1, keepdims=True)
        lam = jnp.exp(s1) - jnp.exp(s2) + lam0
        o = acc_ref[0] / l_ref[0] - lam * (acc_ref[1] / l_ref[1])
        o = _rms(o, g_ref[...], SUBLN_EPS) * (1.0 - lam0)
        o_ref[...] = o.astype(o_ref.dtype)


def diff_attention(qkv, pos_col, pos_row, lam_params, subln_g, lam0):
    T = qkv.shape[0]
    tq = min(ATT_TQ, T)
    tk = min(ATT_TK, T)
    nq = T // tq
    pairs = [(qi, ki) for qi in range(nq) for ki in range(-(-((qi + 1) * tq) // tk))]
    qt = jnp.asarray(np.array([a for a, _ in pairs], np.int32))
    kt = jnp.asarray(np.array([b for _, b in pairs], np.int32))
    H = DA_HEADS
    kern = functools.partial(_attn_kernel, tq=tq, tk=tk, lam0=lam0)
    grid_spec = pltpu.PrefetchScalarGridSpec(
        num_scalar_prefetch=2,
        grid=(H, len(pairs)),
        in_specs=[
            pl.BlockSpec((tq, LANES), lambda h, p, qt, kt: (qt[p], h)),
            pl.BlockSpec((tk, LANES), lambda h, p, qt, kt: (kt[p], H + h)),
            pl.BlockSpec((tk, LANES), lambda h, p, qt, kt: (kt[p], 2 * H + h)),
            pl.BlockSpec((tq, 1), lambda h, p, qt, kt: (qt[p], 0)),
            pl.BlockSpec((1, tk), lambda h, p, qt, kt: (0, kt[p])),
            pl.BlockSpec((4, DA_HEAD_DIM), lambda h, p, qt, kt: (0, 0)),
            pl.BlockSpec((1, DA_V_DIM), lambda h, p, qt, kt: (0, 0)),
        ],
        out_specs=pl.BlockSpec((tq, DA_V_DIM), lambda h, p, qt, kt: (qt[p], h)),
        scratch_shapes=[
            pltpu.VMEM((2, tq, 1), F32),
            pltpu.VMEM((2, tq, 1), F32),
            pltpu.VMEM((2, tq, DA_V_DIM), F32),
        ],
    )
    return pl.pallas_call(
        kern,
        grid_spec=grid_spec,
        out_shape=jax.ShapeDtypeStruct((T, H * DA_V_DIM), BF16),
        compiler_params=_cparams(("parallel", "arbitrary")),
        name="diff_attention",
    )(qt, kt, qkv, qkv, qkv, pos_col, pos_row, lam_params, subln_g.reshape(1, DA_V_DIM))


def _even_out_kernel(o_ref, b_ref, c_ref, x_ref, ch_ref, xh_ref, cw_ref, w_ref, h_ref, out_ref, u_ref):
    tm = o_ref.shape[0]
    u_prev = ch_ref[...] * xh_ref[...]
    u_prev = jnp.where(pl.program_id(0) == 0, 0.0, u_prev)
    u_ref[0:SUBLANES, :] = u_prev
    u_ref[SUBLANES:, :] = c_ref[...] * x_ref[...]
    cw = cw_ref[...]
    conv = (u_ref[SUBLANES - 2:SUBLANES - 2 + tm, :] * cw[0:1]
            + u_ref[SUBLANES - 1:SUBLANES - 1 + tm, :] * cw[1:2]
            + u_ref[SUBLANES:, :] * cw[2:3])
    sc = (b_ref[...] * conv).astype(BF16)
    na = o_ref.shape[1]
    acc = jnp.dot(o_ref[...], w_ref[0:na, :], preferred_element_type=F32)
    acc = acc + jnp.dot(sc, w_ref[na:, :], preferred_element_type=F32)
    out_ref[...] = h_ref[...] + acc


def even_out(attn_o, proj, conv_w, w_out, h):
    T = h.shape[0]
    tm = min(ROW_TILE, T)
    W = SC_WIDTH
    cb = (proj.shape[1] - 3 * W) // W
    hb = tm // SUBLANES
    halo = lambda col: pl.BlockSpec((SUBLANES, W), lambda i: (jnp.maximum(i * hb - 1, 0), col))
    return pl.pallas_call(
        _even_out_kernel,
        grid=(T // tm,),
        in_specs=[
            pl.BlockSpec((tm, attn_o.shape[1]), lambda i: (i, 0)),
            pl.BlockSpec((tm, W), lambda i: (i, cb)),
            pl.BlockSpec((tm, W), lambda i: (i, cb + 1)),
            pl.BlockSpec((tm, W), lambda i: (i, cb + 2)),
            halo(cb + 1),
            halo(cb + 2),
            pl.BlockSpec(conv_w.shape, lambda i: (0, 0)),
            pl.BlockSpec(w_out.shape, lambda i: (0, 0)),
            pl.BlockSpec((tm, D_MODEL), lambda i: (i, 0)),
        ],
        out_specs=pl.BlockSpec((tm, D_MODEL), lambda i: (i, 0)),
        out_shape=jax.ShapeDtypeStruct((T, D_MODEL), F32),
        scratch_shapes=[pltpu.VMEM((tm + SUBLANES, W), F32)],
        compiler_params=_cparams(("parallel",)),
        name="even_out",
    )(attn_o, proj, proj, proj, proj, proj, conv_w, w_out, h)


def _ffn_kernel(h_ref, g_ref, w1_ref, w3_ref, w2_ref, o_ref, xn_ref, acc_ref):
    f = pl.program_id(1)

    @pl.when(f == 0)
    def _():
        xn_ref[...] = _rms(h_ref[...], g_ref[...], RMS_EPS).astype(BF16)
        acc_ref[...] = jnp.zeros(acc_ref.shape, F32)

    xn = xn_ref[...]
    a = jnp.dot(xn, w1_ref[...], preferred_element_type=F32)
    b = jnp.dot(xn, w3_ref[...], preferred_element_type=F32)
    hm = (_silu(a) * b).astype(BF16)
    acc_ref[...] += jnp.dot(hm, w2_ref[...], preferred_element_type=F32)

    @pl.when(f == pl.num_programs(1) - 1)
    def _():
        o_ref[...] = h_ref[...] + acc_ref[...]


def ffn(h, g, w1, w3, w2):
    T, D = h.shape
    F = w1.shape[1]
    tm = min(FFN_ROW_TILE, T)
    tf = FF_TILE
    return pl.pallas_call(
        _ffn_kernel,
        grid=(T // tm, F // tf),
        in_specs=[
            pl.BlockSpec((tm, D), lambda i, f: (i, 0)),
            pl.BlockSpec((1, D), lambda i, f: (0, 0)),
            pl.BlockSpec((D, tf), lambda i, f: (0, f)),
            pl.BlockSpec((D, tf), lambda i, f: (0, f)),
            pl.BlockSpec((tf, D), lambda i, f: (f, 0)),
        ],
        out_specs=pl.BlockSpec((tm, D), lambda i, f: (i, 0)),
        out_shape=jax.ShapeDtypeStruct((T, D), F32),
        scratch_shapes=[pltpu.VMEM((tm, D), BF16), pltpu.VMEM((tm, D), F32)],
        compiler_params=_cparams(("parallel", "arbitrary")),
        name="ffn",
    )(h, g.reshape(1, D), w1, w3, w2)


def _ple_kernel(h_ref, g_ref, wg_ref, p_ref, wp_ref, gf_ref, o_ref, *, final_norm):
    h = h_ref[...]
    xn = _rms(h, g_ref[...], RMS_EPS).astype(BF16)
    gate = _sigmoid(jnp.dot(xn, wg_ref[...], preferred_element_type=F32))
    emb = jnp.dot(p_ref[...].astype(BF16), wp_ref[...], preferred_element_type=F32)
    out = h + gate * emb
    if final_norm:
        out = _rms(out, gf_ref[...], RMS_EPS)
    o_ref[...] = out


def ple(h, g, wg, p, wp, g_final, final_norm):
    T, D = h.shape
    tm = min(ROW_TILE, T)
    return pl.pallas_call(
        functools.partial(_ple_kernel, final_norm=final_norm),
        grid=(T // tm,),
        in_specs=[
            pl.BlockSpec((tm, D), lambda i: (i, 0)),
            pl.BlockSpec((1, D), lambda i: (0, 0)),
            pl.BlockSpec((D, D), lambda i: (0, 0)),
            pl.BlockSpec((tm, PLE_DIM), lambda i: (i, 0)),
            pl.BlockSpec((PLE_DIM, D), lambda i: (0, 0)),
            pl.BlockSpec((1, D), lambda i: (0, 0)),
        ],
        out_specs=pl.BlockSpec((tm, D), lambda i: (i, 0)),
        out_shape=jax.ShapeDtypeStruct((T, D), F32),
        compiler_params=_cparams(("parallel",)),
        name="ple",
    )(h, g.reshape(1, D), wg, p, wp, g_final.reshape(1, D))


def _dot_hi(a, b):
    return jnp.dot(a, b, preferred_element_type=F32, precision=lax.Precision.HIGHEST)


def _gdn_prep_kernel(q_ref, k_ref, v_ref, qh_ref, kh_ref, vh_ref, ba_ref, cw_ref, alog_ref, dtb_ref,
                     qg_ref, kg_ref, w_ref, u_ref, ai_ref, dl_ref,
                     xs_ref, qs_ref, ks_ref, vs_ref, gcol_ref, grow_ref, beta_ref):
    i = pl.program_id(0)
    h = pl.program_id(1)
    R = q_ref.shape[0]
    C = DN_CHUNK
    KC = cw_ref.shape[0]

    def conv_silu(x_ref, halo_ref, j):
        xs_ref[0:SUBLANES, :] = jnp.where(i == 0, 0.0, halo_ref[...])
        xs_ref[SUBLANES:, :] = x_ref[...]
        acc = xs_ref[SUBLANES:, :] * cw_ref[KC - 1:KC, j * LANES:(j + 1) * LANES]
        for t in range(1, KC):
            acc = acc + (xs_ref[SUBLANES - t:SUBLANES - t + R, :]
                         * cw_ref[KC - 1 - t:KC - t, j * LANES:(j + 1) * LANES])
        return _silu(acc)

    def l2n(x):
        return x * lax.rsqrt(jnp.sum(x * x, axis=-1, keepdims=True) + 1e-6)

    qs_ref[...] = l2n(conv_silu(q_ref, qh_ref, 0)) * (DN_HEAD_DIM ** -0.5)
    ks_ref[...] = l2n(conv_silu(k_ref, kh_ref, 1))
    vs_ref[...] = conv_silu(v_ref, vh_ref, 2)

    ba = ba_ref[...]
    lane = lax.broadcasted_iota(I32, ba.shape, 1)
    sp_in = ba + dtb_ref[...]
    softplus = jnp.maximum(sp_in, 0.0) + jnp.log(1.0 + jnp.exp(-jnp.abs(sp_in)))
    g_all = -jnp.exp(alog_ref[...]) * softplus
    beta_ref[...] = jnp.sum(jnp.where(lane == h, _sigmoid(ba), 0.0), axis=-1, keepdims=True)
    ri = lax.broadcasted_iota(I32, (R, R), 0)
    ci = lax.broadcasted_iota(I32, (R, R), 1)
    lg = C.bit_length() - 1
    tri = jnp.where((ri >= ci) & ((ri >> lg) == (ci >> lg)), 1.0, 0.0)
    gc_all = _dot_hi(tri, g_all)
    sel = lane == (DN_HEADS + h)
    gcol_ref[...] = jnp.sum(jnp.where(sel, gc_all, 0.0), axis=-1, keepdims=True)
    sel8 = jnp.where(lax.broadcasted_iota(I32, (SUBLANES, LANES), 1) == (DN_HEADS + h), 1.0, 0.0)
    grow = lax.dot_general(sel8, gc_all, (((1,), (1,)), ((), ())),
                           preferred_element_type=F32, precision=lax.Precision.HIGHEST)
    for cc in range(R // C):
        grow_ref[cc] = grow[:, cc * C:(cc + 1) * C]

    r64 = lax.broadcasted_iota(I32, (C, C), 0)
    c64 = lax.broadcasted_iota(I32, (C, C), 1)
    incl = r64 >= c64
    strict = r64 > c64
    eye = jnp.where(r64 == c64, 1.0, 0.0)
    rcol = lax.broadcasted_iota(I32, (C, 1), 0)

    def chunk(c, carry):
        r0 = pl.multiple_of(c * C, C)
        rows = pl.ds(r0, C)
        q = qs_ref[rows, :]
        k = ks_ref[rows, :]
        v = vs_ref[rows, :]
        beta = beta_ref[rows, :]
        gcol = gcol_ref[rows, :]
        grow = grow_ref[c][0:1, :]
        glast = jnp.sum(jnp.where(rcol == C - 1, gcol, 0.0), axis=0, keepdims=True)
        decay = jnp.where(incl, jnp.exp(jnp.where(incl, gcol - grow, 0.0)), 0.0)
        kb = k * beta
        vb = v * beta
        kbf = k.astype(BF16)
        kk = lax.dot_general(kb.astype(BF16), kbf, (((1,), (1,)), ((), ())), preferred_element_type=F32)
        a_neg = -jnp.where(strict, kk * decay, 0.0)
        tinv = eye + a_neg
        pw = a_neg
        for _ in range(5):
            pw = _dot_hi(pw, pw)
            tinv = tinv + _dot_hi(tinv, pw)
        eg = jnp.exp(gcol)
        rhs = jnp.concatenate([vb, kb * eg], axis=-1)
        uw = _dot_hi(tinv, rhs)
        qk = lax.dot_general(q.astype(BF16), kbf, (((1,), (1,)), ((), ())), preferred_element_type=F32)
        u_ref[rows, :] = uw[:, :DN_HEAD_DIM]
        w_ref[rows, :] = uw[:, DN_HEAD_DIM:].astype(BF16)
        ai_ref[rows, :] = (qk * decay).astype(BF16)
        qg_ref[rows, :] = (q * eg).astype(BF16)
        kg_ref[rows, :] = (k * jnp.exp(glast - gcol)).astype(BF16)
        dl_ref[pl.ds(c, 1), :] = jnp.broadcast_to(jnp.exp(glast), (1, LANES))
        return carry

    lax.fori_loop(0, R // C, chunk, 0)


def gdn_prep(proj, ba, conv_w, alog_l, dtb_l):
    T = proj.shape[0]
    R = min(GDN_TILE, T)
    H = DN_HEADS
    nchunk = T // DN_CHUNK
    hb = R // SUBLANES
    col = lambda j: pl.BlockSpec((R, LANES), lambda i, h: (i, j * H + h))
    halo = lambda j: pl.BlockSpec((SUBLANES, LANES), lambda i, h: (jnp.maximum(i * hb - 1, 0), j * H + h))
    KC = conv_w.shape[0]
    cw = conv_w.reshape(KC, 3, H, LANES).transpose(2, 0, 1, 3).reshape(H, KC, 3 * LANES)
    cw_spec = pl.BlockSpec((None, KC, 3 * LANES), lambda i, h: (h, 0, 0))
    row_out = lambda dt: jax.ShapeDtypeStruct((T, DN_WIDTH), dt)
    out_col = pl.BlockSpec((R, LANES), lambda i, h: (i, h))
    return pl.pallas_call(
        _gdn_prep_kernel,
        grid=(T // R, H),
        in_specs=[col(0), col(1), col(2), halo(0), halo(1), halo(2),
                  pl.BlockSpec((R, LANES), lambda i, h: (i, 0)),
                  cw_spec,
                  pl.BlockSpec((1, LANES), lambda i, h: (0, 0)),
                  pl.BlockSpec((1, LANES), lambda i, h: (0, 0))],
        out_specs=[out_col, out_col, out_col, out_col,
                   pl.BlockSpec((None, R, DN_CHUNK), lambda i, h: (h, i, 0)),
                   pl.BlockSpec((None, R // DN_CHUNK, LANES), lambda i, h: (h, i, 0))],
        out_shape=[row_out(BF16), row_out(BF16), row_out(BF16), row_out(F32),
                   jax.ShapeDtypeStruct((H, T, DN_CHUNK), BF16),
                   jax.ShapeDtypeStruct((H, nchunk, LANES), F32)],
        scratch_shapes=[pltpu.VMEM((R + SUBLANES, LANES), F32),
                        pltpu.VMEM((R, LANES), F32), pltpu.VMEM((R, LANES), F32), pltpu.VMEM((R, LANES), F32),
                        pltpu.VMEM((R, 1), F32), pltpu.VMEM((R // DN_CHUNK, SUBLANES, DN_CHUNK), F32),
                        pltpu.VMEM((R, 1), F32)],
        compiler_params=_cparams(("parallel", "arbitrary")),
        name="gdn_prep",
    )(proj, proj, proj, proj, proj, proj, ba, cw, alog_l, dtb_l)


def _gdn_scan_kernel(qg_ref, kg_ref, w_ref, u_ref, ai_ref, dl_ref, o_ref, s_ref):
    @pl.when(pl.program_id(1) == 0)
    def _():
        s_ref[...] = jnp.zeros(s_ref.shape, F32)

    C = DN_CHUNK

    def chunk(c, carry):
        rows = pl.ds(pl.multiple_of(c * C, C), C)
        s = s_ref[...]
        sb = s.astype(BF16)
        v_new = u_ref[rows, :] - jnp.dot(w_ref[rows, :], sb, preferred_element_type=F32)
        vb = v_new.astype(BF16)
        o = (jnp.dot(qg_ref[rows, :], sb, preferred_element_type=F32)
             + jnp.dot(ai_ref[rows, :], vb, preferred_element_type=F32))
        o_ref[rows, :] = o
        upd = lax.dot_general(kg_ref[rows, :], vb, (((0,), (0,)), ((), ())), preferred_element_type=F32)
        s_ref[...] = s * dl_ref[pl.ds(c, 1), :] + upd
        return carry

    lax.fori_loop(0, qg_ref.shape[0] // C, chunk, 0)


def gdn_scan(qg, kg, w, u, ai, dl):
    T = qg.shape[0]
    R = min(GDN_TILE, T)
    H = DN_HEADS
    col = pl.BlockSpec((R, LANES), lambda h, i: (i, h))
    return pl.pallas_call(
        _gdn_scan_kernel,
        grid=(H, T // R),
        in_specs=[col, col, col, col,
                  pl.BlockSpec((None, R, DN_CHUNK), lambda h, i: (h, i, 0)),
                  pl.BlockSpec((None, R // DN_CHUNK, LANES), lambda h, i: (h, i, 0))],
        out_specs=col,
        out_shape=jax.ShapeDtypeStruct((T, DN_WIDTH), F32),
        scratch_shapes=[pltpu.VMEM((DN_HEAD_DIM, DN_HEAD_DIM), F32)],
        compiler_params=_cparams(("parallel", "arbitrary")),
        name="gdn_scan",
    )(qg, kg, w, u, ai, dl)


def _odd_out_kernel(o_ref, z_ref, g_ref, w_ref, h_ref, out_ref):
    g = g_ref[...]
    parts = []
    for hd in range(DN_HEADS):
        sl = slice(hd * DN_HEAD_DIM, (hd + 1) * DN_HEAD_DIM)
        parts.append((_rms(o_ref[:, sl], g, RMS_EPS) * _silu(z_ref[:, sl])).astype(BF16))
    y = jnp.concatenate(parts, axis=-1)
    out_ref[...] = h_ref[...] + jnp.dot(y, w_ref[...], preferred_element_type=F32)


def odd_out(o, proj, onorm_g, w_out, h):
    T = h.shape[0]
    tm = min(ROW_TILE, T)
    zb = 3 * DN_WIDTH // DN_WIDTH
    return pl.pallas_call(
        _odd_out_kernel,
        grid=(T // tm,),
        in_specs=[
            pl.BlockSpec((tm, DN_WIDTH), lambda i: (i, 0)),
            pl.BlockSpec((tm, DN_WIDTH), lambda i: (i, zb)),
            pl.BlockSpec((1, DN_HEAD_DIM), lambda i: (0, 0)),
            pl.BlockSpec(w_out.shape, lambda i: (0, 0)),
            pl.BlockSpec((tm, D_MODEL), lambda i: (i, 0)),
        ],
        out_specs=pl.BlockSpec((tm, D_MODEL), lambda i: (i, 0)),
        out_shape=jax.ShapeDtypeStruct((T, D_MODEL), F32),
        compiler_params=_cparams(("parallel",)),
        name="odd_out",
    )(o, proj, onorm_g.reshape(1, DN_HEAD_DIM), w_out, h)


def _route_kernel(h_ref, g_ref, wr_ref, ri_ref, gate_ref, cnt_ref, carry_ref):
    i = pl.program_id(0)

    @pl.when(i == 0)
    def _():
        carry_ref[...] = jnp.zeros(carry_ref.shape, F32)

    xn = _rms(h_ref[...], g_ref[...], RMS_EPS)
    logits = _dot_hi(xn, wr_ref[...])
    tm = logits.shape[0]
    lane = lax.broadcasted_iota(I32, logits.shape, 1)
    logits = jnp.where(lane < N_EXPERTS, logits, NEG_BIG)
    lane_f = lane.astype(F32)
    m1 = jnp.max(logits, axis=-1, keepdims=True)
    i1 = jnp.min(jnp.where(logits == m1, lane_f, float(LANES)), axis=-1, keepdims=True)
    rest = jnp.where(lane_f == i1, NEG_BIG, logits)
    m2 = jnp.max(rest, axis=-1, keepdims=True)
    i2 = jnp.min(jnp.where(rest == m2, lane_f, float(LANES)), axis=-1, keepdims=True)
    e = jnp.exp(m2 - m1)
    g1 = 1.0 / (1.0 + e)
    g2 = e / (1.0 + e)
    oh1 = lane_f == i1
    oh2 = lane_f == i2
    i1 = i1.astype(I32)
    i2 = i2.astype(I32)
    oh = jnp.where(oh1 | oh2, 1.0, 0.0)
    ri_ = lax.broadcasted_iota(I32, (tm, tm), 0)
    ci_ = lax.broadcasted_iota(I32, (tm, tm), 1)
    below = jnp.where(ri_ > ci_, 1.0, 0.0).astype(BF16)
    ex = jnp.dot(below, oh.astype(BF16), preferred_element_type=F32) + carry_ref[0:1, :]
    r1 = jnp.sum(jnp.where(oh1, ex, 0.0), axis=-1, keepdims=True).astype(I32)
    r2 = jnp.sum(jnp.where(oh2, ex, 0.0), axis=-1, keepdims=True).astype(I32)
    ri_ref[...] = jnp.where(lane == 0, i1, jnp.where(lane == 1, i2, jnp.where(lane == 2, r1, r2)))
    gate_ref[...] = jnp.where(lane == 0, g1, g2)
    carry_ref[...] = carry_ref[...] + jnp.sum(oh, axis=0, keepdims=True)
    cnt_ref[...] = carry_ref[...]


def route(h, g, wr):
    T, D = h.shape
    tm = min(ROW_TILE, T)
    return pl.pallas_call(
        _route_kernel,
        grid=(T // tm,),
        in_specs=[
            pl.BlockSpec((tm, D), lambda i: (i, 0)),
            pl.BlockSpec((1, D), lambda i: (0, 0)),
            pl.BlockSpec((D, LANES), lambda i: (0, 0)),
        ],
        out_specs=[
            pl.BlockSpec((tm, LANES), lambda i: (i, 0)),
            pl.BlockSpec((tm, LANES), lambda i: (i, 0)),
            pl.BlockSpec((SUBLANES, LANES), lambda i: (0, 0)),
        ],
        out_shape=[
            jax.ShapeDtypeStruct((T, LANES), I32),
            jax.ShapeDtypeStruct((T, LANES), F32),
            jax.ShapeDtypeStruct((SUBLANES, LANES), F32),
        ],
        scratch_shapes=[pltpu.VMEM((SUBLANES, LANES), F32)],
        compiler_params=_cparams(("arbitrary",)),
        name="route",
    )(h, g.reshape(1, D), wr)


def _row_copy(src_ref, s, dst_ref, d, sem):
    return pltpu.make_async_copy(src_ref.at[pl.ds(s, 1), :], dst_ref.at[pl.ds(d, 1), :], sem)


def _dispatch_kernel(dest_ref, h_ref, g_ref, xz_ref, xs_ref, xn_ref, sem):
    del xz_ref
    tm = h_ref.shape[0]
    xn_ref[...] = _rms(h_ref[...], g_ref[...], RMS_EPS)

    def issue(r, carry):
        _row_copy(xn_ref, r, xs_ref, dest_ref[2 * r], sem).start()
        _row_copy(xn_ref, r, xs_ref, dest_ref[2 * r + 1], sem).start()
        return carry

    lax.fori_loop(0, tm, issue, 0)

    def drain(r, carry):
        _row_copy(xn_ref, 0, xs_ref, 0, sem).wait()
        _row_copy(xn_ref, 0, xs_ref, 0, sem).wait()
        return carry

    lax.fori_loop(0, tm, drain, 0)


def dispatch(h, g, dest_flat, n_slots):
    T, D = h.shape
    tm = min(ROW_TILE, T)
    zeros = jnp.zeros((n_slots, D), F32)
    return pl.pallas_call(
        _dispatch_kernel,
        grid=(T // tm,),
        in_specs=[
            pl.BlockSpec((2 * tm,), lambda i: (i,), memory_space=pltpu.SMEM),
            pl.BlockSpec((tm, D), lambda i: (i, 0)),
            pl.BlockSpec((1, D), lambda i: (0, 0)),
            pl.BlockSpec(memory_space=pl.ANY),
        ],
        out_specs=pl.BlockSpec(memory_space=pl.ANY),
        out_shape=jax.ShapeDtypeStruct((n_slots, D), F32),
        scratch_shapes=[pltpu.VMEM((tm, D), F32), pltpu.SemaphoreType.DMA],
        input_output_aliases={3: 0},
        compiler_params=_cparams(("arbitrary",)),
        name="moe_dispatch",
    )(dest_flat, h, g.reshape(1, D), zeros)


def _gmm_kernel(te_ref, na_ref, x_ref, w1_ref, w3_ref, w2_ref, y_ref, xb_ref, acc_ref):
    b = pl.program_id(0)
    f = pl.program_id(1)

    @pl.when(b < na_ref[0])
    def _():
        @pl.when(f == 0)
        def _():
            xb_ref[...] = x_ref[...].astype(BF16)
            acc_ref[...] = jnp.zeros(acc_ref.shape, F32)

        xb = xb_ref[...]
        a = jnp.dot(xb, w1_ref[...], preferred_element_type=F32)
        c = jnp.dot(xb, w3_ref[...], preferred_element_type=F32)
        hm = (_silu(a) * c).astype(BF16)
        acc_ref[...] += jnp.dot(hm, w2_ref[...], preferred_element_type=F32)

        @pl.when(f == pl.num_programs(1) - 1)
        def _():
            y_ref[...] = acc_ref[...]

    @pl.when((b >= na_ref[0]) & (f == pl.num_programs(1) - 1))
    def _():
        y_ref[...] = jnp.zeros(y_ref.shape, F32)


def gmm(xs, w1, w3, w2, tile_e, n_active):
    P, D = xs.shape
    G = MOE_TILE
    F = w1.shape[2]
    tf = FF_TILE
    nf = F // tf

    def row_idx(b, f, te, na):
        return (jnp.minimum(b, na[0] - 1), 0)

    def f_idx(b, f, na):
        return jnp.where(b < na[0], f, nf - 1)

    grid_spec = pltpu.PrefetchScalarGridSpec(
        num_scalar_prefetch=2,
        grid=(P // G, nf),
        in_specs=[
            pl.BlockSpec((G, D), row_idx),
            pl.BlockSpec((None, D, tf), lambda b, f, te, na: (te[b], 0, f_idx(b, f, na))),
            pl.BlockSpec((None, D, tf), lambda b, f, te, na: (te[b], 0, f_idx(b, f, na))),
            pl.BlockSpec((None, tf, D), lambda b, f, te, na: (te[b], f_idx(b, f, na), 0)),
        ],
        out_specs=pl.BlockSpec((G, D), lambda b, f, te, na: (b, 0)),
        scratch_shapes=[pltpu.VMEM((G, D), BF16), pltpu.VMEM((G, D), F32)],
    )
    return pl.pallas_call(
        _gmm_kernel,
        grid_spec=grid_spec,
        out_shape=jax.ShapeDtypeStruct((P, D), F32),
        compiler_params=_cparams(("arbitrary", "arbitrary")),
        name="moe_gmm",
    )(tile_e, n_active, xs, w1, w3, w2)


def _combine_kernel(dest_ref, h_ref, gate_ref, y_ref, o_ref, ya_ref, yb_ref, sem):
    tm = h_ref.shape[0]

    def issue(r, carry):
        _row_copy(y_ref, dest_ref[2 * r], ya_ref, r, sem).start()
        _row_copy(y_ref, dest_ref[2 * r + 1], yb_ref, r, sem).start()
        return carry

    lax.fori_loop(0, tm, issue, 0)

    def drain(r, carry):
        _row_copy(y_ref, 0, ya_ref, 0, sem).wait()
        _row_copy(y_ref, 0, yb_ref, 0, sem).wait()
        return carry

    lax.fori_loop(0, tm, drain, 0)
    gt = gate_ref[...]
    o_ref[...] = h_ref[...] + gt[:, 0:1] * ya_ref[...] + gt[:, 1:2] * yb_ref[...]


def combine(h, gates, y, dest_flat):
    T, D = h.shape
    tm = min(ROW_TILE, T)
    return pl.pallas_call(
        _combine_kernel,
        grid=(T // tm,),
        in_specs=[
            pl.BlockSpec((2 * tm,), lambda i: (i,), memory_space=pltpu.SMEM),
            pl.BlockSpec((tm, D), lambda i: (i, 0)),
            pl.BlockSpec((tm, LANES), lambda i: (i, 0)),
            pl.BlockSpec(memory_space=pl.ANY),
        ],
        out_specs=pl.BlockSpec((tm, D), lambda i: (i, 0)),
        out_shape=jax.ShapeDtypeStruct((T, D), F32),
        scratch_shapes=[pltpu.VMEM((tm, D), F32), pltpu.VMEM((tm, D), F32), pltpu.SemaphoreType.DMA],
        compiler_params=_cparams(("arbitrary",)),
        name="moe_combine",
    )(dest_flat, h, gates, y)


def _lambda_init(layer_idx):
    return 0.8 - 0.6 * math.exp(-0.3 * layer_idx)


def _even_layer(h, pos_col, pos_row, invf, ln_mix, w_in, w_out, lam_params, subln_g, conv_w, layer_idx):
    proj = norm_matmul(h, ln_mix, w_in.astype(BF16), tn=512)
    qkv = rope_qkv(proj, pos_col, invf)
    o = diff_attention(qkv, pos_col, pos_row, lam_params, subln_g, _lambda_init(layer_idx))
    return even_out(o, proj, conv_w, w_out.astype(BF16), h)


def _odd_mixer(h, ln_mix, w_in, conv_w, a_log, dt_bias, onorm_g, w_out):
    main_w = 4 * DN_WIDTH
    w_main = w_in[:, :main_w].astype(BF16)
    w_ba = jnp.pad(w_in[:, main_w:], ((0, 0), (0, LANES - 2 * DN_HEADS))).astype(BF16)
    proj = norm_matmul(h, ln_mix, w_main, tn=512)
    ba = norm_matmul(h, ln_mix, w_ba, tn=LANES)
    pad8 = lambda v: jnp.pad(v.astype(F32), (DN_HEADS, LANES - 2 * DN_HEADS)).reshape(1, LANES)
    qg, kg, w, u, ai, dl = gdn_prep(proj, ba, conv_w, pad8(a_log), pad8(dt_bias))
    o = gdn_scan(qg, kg, w, u, ai, dl)
    return odd_out(o, proj, onorm_g, w_out.astype(BF16), h)


def _moe(h, ln_ffn, w_router, w1, w3, w2):
    T = h.shape[0]
    G = MOE_TILE
    wr = jnp.pad(w_router, ((0, 0), (0, LANES - N_EXPERTS)))
    ri, gates, cnt = route(h, ln_ffn, wr)
    counts = cnt[0, :N_EXPERTS].astype(I32)
    padded = ((counts + G - 1) // G) * G
    pends = jnp.cumsum(padded)
    pstarts = pends - padded
    dest = (jnp.take(pstarts, ri[:, 0:2]) + ri[:, 2:4]).reshape(-1)
    n_tiles = (2 * T) // G + N_EXPERTS
    tile_e = jnp.minimum(
        jnp.searchsorted(pends, jnp.arange(n_tiles, dtype=I32) * G, side="right"), N_EXPERTS - 1).astype(I32)
    n_active = (pends[-1:] // G).astype(I32)
    xs = dispatch(h, ln_ffn, dest, n_tiles * G)
    y = gmm(xs, w1.astype(BF16), w3.astype(BF16), w2.astype(BF16), tile_e, n_active)
    return combine(h, gates, y, dest)


def kernel(x, p, positions, ln_mix, ln_ffn, ln_ple, ln_final, w_in_even, w_out_even, lam_q1, lam_k1, lam_q2, lam_k2, subln_gain, conv_w_short, w_in_odd, conv_w_qkv, a_log, dt_bias, onorm_gain, w_out_odd, w1_dense, w3_dense, w2_dense, w_router, w1_moe, w3_moe, w2_moe, w_ple_gate, w_ple_proj):
    B, S, D = x.shape
    T = B * S
    depth = p.shape[0]
    h = x.reshape(T, D)
    pos_col = positions.reshape(T, 1).astype(I32)
    pos_row = positions.reshape(1, T).astype(I32)
    inv_freq = ROPE_THETA ** (-jnp.arange(0, ROT_DIM, 2, dtype=F32) / ROT_DIM)
    invf = jnp.tile(inv_freq, LANES // (ROT_DIM // 2)).reshape(1, LANES)
    for i in range(depth):
        j = i // 2
        if i % 2 == 0:
            lam_params = jnp.stack([lam_q1[j], lam_k1[j], lam_q2[j], lam_k2[j]]).astype(F32)
            h = _even_layer(h, pos_col, pos_row, invf, ln_mix[i], w_in_even[j], w_out_even[j], lam_params,
                            subln_gain[j], conv_w_short[j], i)
            h = ffn(h, ln_ffn[i], w1_dense[j].astype(BF16), w3_dense[j].astype(BF16), w2_dense[j].astype(BF16))
        else:
            h = _odd_mixer(h, ln_mix[i], w_in_odd[j], conv_w_qkv[j], a_log[j], dt_bias[j], onorm_gain[j],
                           w_out_odd[j])
            h = _moe(h, ln_ffn[i], w_router[j], w1_moe[j], w3_moe[j], w2_moe[j])
        h = ple(h, ln_ple[i], w_ple_gate[i].astype(BF16), p[i].reshape(T, PLE_DIM), w_ple_proj[i].astype(BF16),
                ln_final, final_norm=(i == depth - 1))
    return h.reshape(B, S, D)
```

```python
import functools
import math

import jax
import jax.numpy as jnp
import numpy as np
from jax import lax
from jax.experimental import pallas as pl
from jax.experimental.pallas import tpu as pltpu

F32 = jnp.float32
BF16 = jnp.bfloat16
I32 = jnp.int32

D_MODEL = 1024
DEPTH = 4
RMS_EPS = 1e-6
DA_HEADS = 4
DA_HEAD_DIM = 64
DA_V_DIM = 2 * DA_HEAD_DIM
ROPE_THETA = 500000.0
ROT_DIM = DA_HEAD_DIM // 4
SUBLN_EPS = 1e-5
SC_WIDTH = 512
DN_HEADS = 8
DN_HEAD_DIM = 128
DN_WIDTH = DN_HEADS * DN_HEAD_DIM
DN_CHUNK = 64
D_FF = 3584
N_EXPERTS = 8
PLE_DIM = 256

LANES = 128
SUBLANES = 8
VMEM_LIMIT = 52 * 1024 * 1024

NEG_BIG = -1e30

ROW_TILE = 512
FFN_ROW_TILE = 1024
FF_TILE = 512
ATT_TQ = 1024
ATT_TK = 512
GDN_TILE = 512
GDN_SCAN_HEADS = 4
MOE_TILE = 512


def _cparams(sem):
    return pltpu.CompilerParams(dimension_semantics=sem, vmem_limit_bytes=VMEM_LIMIT)


def _rms(x, g, eps):
    ms = jnp.mean(x * x, axis=-1, keepdims=True)
    return x * lax.rsqrt(ms + eps) * g


def _silu(x):
    return x * (1.0 / (1.0 + jnp.exp(-x)))


def _sigmoid(x):
    return 1.0 / (1.0 + jnp.exp(-x))


def _norm_matmul_kernel(x_ref, g_ref, w_ref, o_ref, xn_ref):
    @pl.when(pl.program_id(1) == 0)
    def _():
        xn_ref[...] = _rms(x_ref[...], g_ref[...], RMS_EPS).astype(BF16)

    o_ref[...] = jnp.dot(xn_ref[...], w_ref[...], preferred_element_type=F32).astype(o_ref.dtype)


def norm_matmul(x, g, w, tn, out_dtype=F32):
    T, D = x.shape
    N = w.shape[1]
    tm = min(ROW_TILE, T)
    return pl.pallas_call(
        _norm_matmul_kernel,
        grid=(T // tm, N // tn),
        in_specs=[
            pl.BlockSpec((tm, D), lambda i, j: (i, 0)),
            pl.BlockSpec((1, D), lambda i, j: (0, 0)),
            pl.BlockSpec((D, tn), lambda i, j: (0, j)),
        ],
        out_specs=pl.BlockSpec((tm, tn), lambda i, j: (i, j)),
        out_shape=jax.ShapeDtypeStruct((T, N), out_dtype),
        scratch_shapes=[pltpu.VMEM((tm, D), BF16)],
        compiler_params=_cparams(("parallel", "arbitrary")),
        name="norm_matmul",
    )(x, g.reshape(1, D), w)


def _rope_kernel(proj_ref, pos_ref, invf_ref, o_ref):
    pos = pos_ref[...].astype(F32)
    ang = pos * invf_ref[...]
    cos_t = jnp.cos(ang)
    sin_t = jnp.sin(ang)
    d = lax.broadcasted_iota(I32, ang.shape, 1) & (DA_HEAD_DIM - 1)
    half = ROT_DIM // 2
    c_mul = jnp.where(d < ROT_DIM, cos_t, 1.0)
    s_mul = jnp.where(d < half, -sin_t, jnp.where(d < ROT_DIM, sin_t, 0.0))
    n_qk = 2 * DA_HEADS * DA_HEAD_DIM // LANES
    for c in range(2 * n_qk):
        x = proj_ref[:, c * LANES:(c + 1) * LANES]
        swapped = jnp.where(d < half, pltpu.roll(x, LANES - half, 1), pltpu.roll(x, half, 1))
        r = x * c_mul + swapped * s_mul
        if c < n_qk:
            r = r * (DA_HEAD_DIM ** -0.5)
        o_ref[:, c * LANES:(c + 1) * LANES] = r.astype(BF16)
    v0 = 2 * n_qk * LANES
    o_ref[:, v0:] = proj_ref[:, v0:].astype(BF16)


def rope_qkv(proj, positions_col, invf):
    T = proj.shape[0]
    W = 2 * 2 * DA_HEADS * DA_HEAD_DIM + DA_HEADS * DA_V_DIM
    tm = min(ROW_TILE, T)
    return pl.pallas_call(
        _rope_kernel,
        grid=(T // tm,),
        in_specs=[
            pl.BlockSpec((tm, W), lambda i: (i, 0)),
            pl.BlockSpec((tm, 1), lambda i: (i, 0)),
            pl.BlockSpec((1, LANES), lambda i: (0, 0)),
        ],
        out_specs=pl.BlockSpec((tm, W), lambda i: (i, 0)),
        out_shape=jax.ShapeDtypeStruct((T, W), BF16),
        compiler_params=_cparams(("parallel",)),
        name="rope_qkv",
    )(proj, positions_col, invf)


def _attn_kernel(qt_ref, kt_ref, q_ref, k_ref, v_ref, pq_ref, pk_ref, lam_ref, g_ref, o_ref,
                 m_ref, l_ref, acc_ref, *, tq, tk, lam0):
    p = pl.program_id(1)
    qi = qt_ref[p]
    ki = kt_ref[p]

    @pl.when(ki == 0)
    def _():
        m_ref[...] = jnp.full(m_ref.shape, NEG_BIG, F32)
        l_ref[...] = jnp.zeros(l_ref.shape, F32)
        acc_ref[...] = jnp.zeros(acc_ref.shape, F32)

    q = q_ref[...]
    k = k_ref[...]
    v = v_ref[...]
    mask = pk_ref[...] <= pq_ref[...]
    for s in range(2):
        qs = q[:, s * DA_HEAD_DIM:(s + 1) * DA_HEAD_DIM]
        ks = k[:, s * DA_HEAD_DIM:(s + 1) * DA_HEAD_DIM]
        sc = lax.dot_general(qs, ks, (((1,), (1,)), ((), ())), preferred_element_type=F32)
        sc = jnp.where(mask, sc, NEG_BIG)
        m_prev = m_ref[s]
        m_new = jnp.maximum(m_prev, jnp.max(sc, axis=-1, keepdims=True))
        alpha = jnp.exp(m_prev - m_new)
        pr = jnp.exp(sc - m_new)
        l_ref[s] = alpha * l_ref[s] + jnp.sum(pr, axis=-1, keepdims=True)
        acc_ref[s] = alpha * acc_ref[s] + jnp.dot(pr.astype(BF16), v, preferred_element_type=F32)
        m_ref[s] = m_new

    @pl.when((ki + 1) * tk >= (qi + 1) * tq)
    def _():
        lm = lam_ref[...]
        s1 = jnp.sum(lm[0:1] * lm[1:2], axis=-1, keepdims=True)
        s2 = jnp.sum(lm[2:3] * lm[3:4], axis=-1, keepdims=True)
        lam = jnp.exp(s1) - jnp.exp(s2) + lam0
        o = acc_ref[0] / l_ref[0] - lam * (acc_ref[1] / l_ref[1])
        o = _rms(o, g_ref[...], SUBLN_EPS) * (1.0 - lam0)
        o_ref[...] = o.astype(o_ref.dtype)


def diff_attention(qkv, pos_col, pos_row, lam_params, subln_g, lam0):
    T = qkv.shape[0]
    tq = min(ATT_TQ, T)
    tk = min(ATT_TK, T)
    nq = T // tq
    pairs = [(qi, ki) for qi in range(nq) for ki in range(-(-((qi + 1) * tq) // tk))]
    qt = jnp.asarray(np.array([a for a, _ in pairs], np.int32))
    kt = jnp.asarray(np.array([b for _, b in pairs], np.int32))
    H = DA_HEADS
    kern = functools.partial(_attn_kernel, tq=tq, tk=tk, lam0=lam0)
    grid_spec = pltpu.PrefetchScalarGridSpec(
        num_scalar_prefetch=2,
        grid=(H, len(pairs)),
        in_specs=[
            pl.BlockSpec((tq, LANES), lambda h, p, qt, kt: (qt[p], h)),
            pl.BlockSpec((tk, LANES), lambda h, p, qt, kt: (kt[p], H + h)),
            pl.BlockSpec((tk, LANES), lambda h, p, qt, kt: (kt[p], 2 * H + h)),
            pl.BlockSpec((tq, 1), lambda h, p, qt, kt: (qt[p], 0)),
            pl.BlockSpec((1, tk), lambda h, p, qt, kt: (0, kt[p])),
            pl.BlockSpec((4, DA_HEAD_DIM), lambda h, p, qt, kt: (0, 0)),
            pl.BlockSpec((1, DA_V_DIM), lambda h, p, qt, kt: (0, 0)),
        ],
        out_specs=pl.BlockSpec((tq, DA_V_DIM), lambda h, p, qt, kt: (qt[p], h)),
        scratch_shapes=[
            pltpu.VMEM((2, tq, 1), F32),
            pltpu.VMEM((2, tq, 1), F32),
            pltpu.VMEM((2, tq, DA_V_DIM), F32),
        ],
    )
    return pl.pallas_call(
        kern,
        grid_spec=grid_spec,
        out_shape=jax.ShapeDtypeStruct((T, H * DA_V_DIM), BF16),
        compiler_params=_cparams(("parallel", "arbitrary")),
        name="diff_attention",
    )(qt, kt, qkv, qkv, qkv, pos_col, pos_row, lam_params, subln_g.reshape(1, DA_V_DIM))


def _even_out_kernel(o_ref, b_ref, c_ref, x_ref, ch_ref, xh_ref, cw_ref, w_ref, h_ref, out_ref, u_ref):
    tm = o_ref.shape[0]
    u_prev = ch_ref[...] * xh_ref[...]
    u_prev = jnp.where(pl.program_id(0) == 0, 0.0, u_prev)
    u_ref[0:SUBLANES, :] = u_prev
    u_ref[SUBLANES:, :] = c_ref[...] * x_ref[...]
    cw = cw_ref[...]
    conv = (u_ref[SUBLANES - 2:SUBLANES - 2 + tm, :] * cw[0:1]
            + u_ref[SUBLANES - 1:SUBLANES - 1 + tm, :] * cw[1:2]
            + u_ref[SUBLANES:, :] * cw[2:3])
    sc = (b_ref[...] * conv).astype(BF16)
    na = o_ref.shape[1]
    acc = jnp.dot(o_ref[...], w_ref[0:na, :], preferred_element_type=F32)
    acc = acc + jnp.dot(sc, w_ref[na:, :], preferred_element_type=F32)
    out_ref[...] = h_ref[...] + acc


def even_out(attn_o, proj, conv_w, w_out, h):
    T = h.shape[0]
    tm = min(ROW_TILE, T)
    W = SC_WIDTH
    cb = (proj.shape[1] - 3 * W) // W
    hb = tm // SUBLANES
    halo = lambda col: pl.BlockSpec((SUBLANES, W), lambda i: (jnp.maximum(i * hb - 1, 0), col))
    return pl.pallas_call(
        _even_out_kernel,
        grid=(T // tm,),
        in_specs=[
            pl.BlockSpec((tm, attn_o.shape[1]), lambda i: (i, 0)),
            pl.BlockSpec((tm, W), lambda i: (i, cb)),
            pl.BlockSpec((tm, W), lambda i: (i, cb + 1)),
            pl.BlockSpec((tm, W), lambda i: (i, cb + 2)),
            halo(cb + 1),
            halo(cb + 2),
            pl.BlockSpec(conv_w.shape, lambda i: (0, 0)),
            pl.BlockSpec(w_out.shape, lambda i: (0, 0)),
            pl.BlockSpec((tm, D_MODEL), lambda i: (i, 0)),
        ],
        out_specs=pl.BlockSpec((tm, D_MODEL), lambda i: (i, 0)),
        out_shape=jax.ShapeDtypeStruct((T, D_MODEL), F32),
        scratch_shapes=[pltpu.VMEM((tm + SUBLANES, W), F32)],
        compiler_params=_cparams(("parallel",)),
        name="even_out",
    )(attn_o, proj, proj, proj, proj, proj, conv_w, w_out, h)


def _ffn_kernel(h_ref, g_ref, w1_ref, w3_ref, w2_ref, o_ref, xn_ref, acc_ref):
    f = pl.program_id(1)

    @pl.when(f == 0)
    def _():
        xn_ref[...] = _rms(h_ref[...], g_ref[...], RMS_EPS).astype(BF16)
        acc_ref[...] = jnp.zeros(acc_ref.shape, F32)

    xn = xn_ref[...]
    a = jnp.dot(xn, w1_ref[...], preferred_element_type=F32)
    b = jnp.dot(xn, w3_ref[...], preferred_element_type=F32)
    hm = (_silu(a) * b).astype(BF16)
    acc_ref[...] += jnp.dot(hm, w2_ref[...], preferred_element_type=F32)

    @pl.when(f == pl.num_programs(1) - 1)
    def _():
        o_ref[...] = h_ref[...] + acc_ref[...]


def ffn(h, g, w1, w3, w2):
    T, D = h.shape
    F = w1.shape[1]
    tm = min(FFN_ROW_TILE, T)
    tf = FF_TILE
    return pl.pallas_call(
        _ffn_kernel,
        grid=(T // tm, F // tf),
        in_specs=[
            pl.BlockSpec((tm, D), lambda i, f: (i, 0)),
            pl.BlockSpec((1, D), lambda i, f: (0, 0)),
            pl.BlockSpec((D, tf), lambda i, f: (0, f)),
            pl.BlockSpec((D, tf), lambda i, f: (0, f)),
            pl.BlockSpec((tf, D), lambda i, f: (f, 0)),
        ],
        out_specs=pl.BlockSpec((tm, D), lambda i, f: (i, 0)),
        out_shape=jax.ShapeDtypeStruct((T, D), F32),
        scratch_shapes=[pltpu.VMEM((tm, D), BF16), pltpu.VMEM((tm, D), F32)],
        compiler_params=_cparams(("parallel", "arbitrary")),
        name="ffn",
    )(h, g.reshape(1, D), w1, w3, w2)


def _ple_kernel(h_ref, g_ref, wg_ref, p_ref, wp_ref, gf_ref, o_ref, *, final_norm):
    h = h_ref[...]
    xn = _rms(h, g_ref[...], RMS_EPS).astype(BF16)
    gate = _sigmoid(jnp.dot(xn, wg_ref[...], preferred_element_type=F32))
    emb = jnp.dot(p_ref[...].astype(BF16), wp_ref[...], preferred_element_type=F32)
    out = h + gate * emb
    if final_norm:
        out = _rms(out, gf_ref[...], RMS_EPS)
    o_ref[...] = out


def ple(h, g, wg, p, wp, g_final, final_norm):
    T, D = h.shape
    tm = min(ROW_TILE, T)
    return pl.pallas_call(
        functools.partial(_ple_kernel, final_norm=final_norm),
        grid=(T // tm,),
        in_specs=[
            pl.BlockSpec((tm, D), lambda i: (i, 0)),
            pl.BlockSpec((1, D), lambda i: (0, 0)),
            pl.BlockSpec((D, D), lambda i: (0, 0)),
            pl.BlockSpec((tm, PLE_DIM), lambda i: (i, 0)),
            pl.BlockSpec((PLE_DIM, D), lambda i: (0, 0)),
            pl.BlockSpec((1, D), lambda i: (0, 0)),
        ],
        out_specs=pl.BlockSpec((tm, D), lambda i: (i, 0)),
        out_shape=jax.ShapeDtypeStruct((T, D), F32),
        compiler_params=_cparams(("parallel",)),
        name="ple",
    )(h, g.reshape(1, D), wg, p, wp, g_final.reshape(1, D))


def _dot_hi(a, b):
    return jnp.dot(a, b, preferred_element_type=F32, precision=lax.Precision.HIGHEST)


def _gdn_prep_kernel(q_ref, k_ref, v_ref, qh_ref, kh_ref, vh_ref, ba_ref, cw_ref, alog_ref, dtb_ref,
                     qg_ref, kg_ref, w_ref, u_ref, ai_ref, dl_ref,
                     xs_ref, gc_ref, gl_ref, sg_ref, gct_ref):
    i = pl.program_id(0)
    h = pl.program_id(1)
    R = q_ref.shape[0]
    C = DN_CHUNK
    KC = cw_ref.shape[0]

    def conv_silu(x_ref, halo_ref, j):
        xs_ref[0:SUBLANES, :] = jnp.where(i == 0, 0.0, halo_ref[...])
        xs_ref[SUBLANES:, :] = x_ref[...]
        acc = xs_ref[SUBLANES:, :] * cw_ref[KC - 1:KC, j * LANES:(j + 1) * LANES]
        for t in range(1, KC):
            acc = acc + (xs_ref[SUBLANES - t:SUBLANES - t + R, :]
                         * cw_ref[KC - 1 - t:KC - t, j * LANES:(j + 1) * LANES])
        return _silu(acc)

    def l2n(x):
        return x * lax.rsqrt(jnp.sum(x * x, axis=-1, keepdims=True) + 1e-6)

    chunks = range(R // C)
    rows = [slice(c * C, (c + 1) * C) for c in chunks]
    r64 = lax.broadcasted_iota(I32, (C, C), 0)
    c64 = lax.broadcasted_iota(I32, (C, C), 1)
    incl = r64 >= c64
    strict = r64 > c64

    @pl.when(h == 0)
    def _():
        ba = ba_ref[...]
        sp_in = ba + dtb_ref[...]
        softplus = jnp.maximum(sp_in, 0.0) + jnp.log(1.0 + jnp.exp(-jnp.abs(sp_in)))
        g_all = -jnp.exp(alog_ref[...]) * softplus
        g_wide = jnp.concatenate([g_all[r] for r in rows], axis=1)
        gc_wide = _dot_hi(jnp.where(incl, 1.0, 0.0), g_wide)
        for c in chunks:
            gc_c = gc_wide[:, c * LANES:(c + 1) * LANES]
            gc_ref[rows[c], :] = gc_c
            gl_ref[rows[c], :] = jnp.broadcast_to(gc_c[C - 1:C, :], (C, LANES))
        sg_ref[...] = _sigmoid(ba)
        gct_ref[...] = gc_ref[...].T

    q = l2n(conv_silu(q_ref, qh_ref, 0)) * (DN_HEAD_DIM ** -0.5)
    k = l2n(conv_silu(k_ref, kh_ref, 1))
    v = conv_silu(v_ref, vh_ref, 2)

    lane = lax.broadcasted_iota(I32, (R, LANES), 1)

    def pick(ref, l):
        return jnp.sum(jnp.where(lane == l, ref[...], 0.0), axis=-1, keepdims=True)

    beta = pick(sg_ref, h)
    gcol = pick(gc_ref, DN_HEADS + h)
    glast = pick(gl_ref, DN_HEADS + h)
    grow = gct_ref[pl.ds(DN_HEADS + h, 1), :]
    eg = jnp.exp(gcol)
    kb = k * beta
    vb = v * beta
    kbg = kb * eg
    qg_ref[...] = (q * eg).astype(BF16)
    kg_ref[...] = (k * jnp.exp(glast - gcol)).astype(BF16)
    edl = jnp.broadcast_to(jnp.exp(glast), (R, LANES))

    kbf = k.astype(BF16)
    lhs = jnp.concatenate([kb.astype(BF16).reshape(R // C, C, LANES),
                           q.astype(BF16).reshape(R // C, C, LANES)], axis=1)
    rhs = jnp.concatenate([vb, kbg], axis=-1)

    def bdot(a, b):
        return jnp.dot(a.astype(BF16), b.astype(BF16), preferred_element_type=F32)

    decay = [jnp.where(incl, jnp.exp(jnp.where(incl, gcol[r] - grow[:, r], 0.0)), 0.0) for r in rows]
    kq = [lax.dot_general(lhs[c], kbf[rows[c]], (((1,), (1,)), ((), ())), preferred_element_type=F32)
          for c in chunks]
    pw = [-jnp.where(strict, kq[c][:C] * decay[c], 0.0) for c in chunks]
    n = list(pw)
    for _ in range(5):
        pw = [bdot(pw[c], pw[c]) for c in chunks]
        n = [n[c] + pw[c] + bdot(n[c], pw[c]) for c in chunks]
    for c in chunks:
        rc = rhs[rows[c]]
        uw = rc + bdot(n[c], rc)
        u_ref[rows[c], :] = uw[:, :DN_HEAD_DIM]
        w_ref[rows[c], :] = uw[:, DN_HEAD_DIM:].astype(BF16)
        ai_ref[rows[c], :] = (kq[c][C:] * decay[c]).astype(BF16)
        dl_ref[c:c + 1, :] = edl[c * C:c * C + 1, :]


def gdn_prep(proj, ba, conv_w, alog_l, dtb_l):
    T = proj.shape[0]
    R = min(GDN_TILE, T)
    H = DN_HEADS
    nchunk = T // DN_CHUNK
    hb = R // SUBLANES
    col = lambda j: pl.BlockSpec((R, LANES), lambda i, h: (i, j * H + h))
    halo = lambda j: pl.BlockSpec((SUBLANES, LANES), lambda i, h: (jnp.maximum(i * hb - 1, 0), j * H + h))
    KC = conv_w.shape[0]
    cw = conv_w.reshape(KC, 3, H, LANES).transpose(2, 0, 1, 3).reshape(H, KC, 3 * LANES)
    cw_spec = pl.BlockSpec((None, KC, 3 * LANES), lambda i, h: (h, 0, 0))
    row_out = lambda dt: jax.ShapeDtypeStruct((T, DN_WIDTH), dt)
    out_col = pl.BlockSpec((R, LANES), lambda i, h: (i, h))
    return pl.pallas_call(
        _gdn_prep_kernel,
        grid=(T // R, H),
        in_specs=[col(0), col(1), col(2), halo(0), halo(1), halo(2),
                  pl.BlockSpec((R, LANES), lambda i, h: (i, 0)),
                  cw_spec,
                  pl.BlockSpec((1, LANES), lambda i, h: (0, 0)),
                  pl.BlockSpec((1, LANES), lambda i, h: (0, 0))],
        out_specs=[out_col, out_col, out_col, out_col,
                   pl.BlockSpec((None, R, DN_CHUNK), lambda i, h: (h, i, 0)),
                   pl.BlockSpec((None, R // DN_CHUNK, LANES), lambda i, h: (h, i, 0))],
        out_shape=[row_out(BF16), row_out(BF16), row_out(BF16), row_out(F32),
                   jax.ShapeDtypeStruct((H, T, DN_CHUNK), BF16),
                   jax.ShapeDtypeStruct((H, nchunk, LANES), F32)],
        scratch_shapes=[pltpu.VMEM((R + SUBLANES, LANES), F32),
                        pltpu.VMEM((R, LANES), F32), pltpu.VMEM((R, LANES), F32), pltpu.VMEM((R, LANES), F32),
                        pltpu.VMEM((LANES, R), F32)],
        compiler_params=_cparams(("parallel", "arbitrary")),
        name="gdn_prep",
    )(proj, proj, proj, proj, proj, proj, ba, cw, alog_l, dtb_l)


def _gdn_scan_kernel(qg_ref, kg_ref, w_ref, u_ref, ai_ref, dl_ref, o_ref, s_ref):
    @pl.when(pl.program_id(1) == 0)
    def _():
        s_ref[...] = jnp.zeros(s_ref.shape, F32)

    C = DN_CHUNK
    D = DN_HEAD_DIM
    heads = range(s_ref.shape[0])

    def chunk(c, carry):
        rows = pl.ds(pl.multiple_of(c * C, C), C)
        s = [s_ref[hh] for hh in heads]
        sb = [x.astype(BF16) for x in s]
        cols = [slice(hh * D, (hh + 1) * D) for hh in heads]
        r1 = [jnp.dot(jnp.concatenate([w_ref[rows, cols[hh]], qg_ref[rows, cols[hh]]], axis=0), sb[hh],
                      preferred_element_type=F32) for hh in heads]
        vb = [(u_ref[rows, cols[hh]] - r1[hh][:C]).astype(BF16) for hh in heads]
        for hh in heads:
            o_ref[rows, cols[hh]] = r1[hh][C:] + jnp.dot(ai_ref[hh, rows, :], vb[hh], preferred_element_type=F32)
        upd = [lax.dot_general(kg_ref[rows, cols[hh]], vb[hh], (((0,), (0,)), ((), ())),
                               preferred_element_type=F32) for hh in heads]
        for hh in heads:
            s_ref[hh] = s[hh] * dl_ref[hh, pl.ds(c, 1), :] + upd[hh]
        return carry

    lax.fori_loop(0, qg_ref.shape[0] // C, chunk, 0)


def gdn_scan(qg, kg, w, u, ai, dl):
    T = qg.shape[0]
    R = min(GDN_TILE, T)
    H = DN_HEADS
    HB = GDN_SCAN_HEADS
    col = pl.BlockSpec((R, HB * LANES), lambda h, i: (i, h))
    return pl.pallas_call(
        _gdn_scan_kernel,
        grid=(H // HB, T // R),
        in_specs=[col, col, col, col,
                  pl.BlockSpec((HB, R, DN_CHUNK), lambda h, i: (h, i, 0)),
                  pl.BlockSpec((HB, R // DN_CHUNK, LANES), lambda h, i: (h, i, 0))],
        out_specs=col,
        out_shape=jax.ShapeDtypeStruct((T, DN_WIDTH), F32),
        scratch_shapes=[pltpu.VMEM((HB, DN_HEAD_DIM, DN_HEAD_DIM), F32)],
        compiler_params=_cparams(("parallel", "arbitrary")),
        name="gdn_scan",
    )(qg, kg, w, u, ai, dl)


def _odd_out_kernel(o_ref, z_ref, g_ref, w_ref, h_ref, out_ref):
    g = g_ref[...]
    parts = []
    for hd in range(DN_HEADS):
        sl = slice(hd * DN_HEAD_DIM, (hd + 1) * DN_HEAD_DIM)
        parts.append((_rms(o_ref[:, sl], g, RMS_EPS) * _silu(z_ref[:, sl])).astype(BF16))
    y = jnp.concatenate(parts, axis=-1)
    out_ref[...] = h_ref[...] + jnp.dot(y, w_ref[...], preferred_element_type=F32)


def odd_out(o, proj, onorm_g, w_out, h):
    T = h.shape[0]
    tm = min(ROW_TILE, T)
    zb = 3 * DN_WIDTH // DN_WIDTH
    return pl.pallas_call(
        _odd_out_kernel,
        grid=(T // tm,),
        in_specs=[
            pl.BlockSpec((tm, DN_WIDTH), lambda i: (i, 0)),
            pl.BlockSpec((tm, DN_WIDTH), lambda i: (i, zb)),
            pl.BlockSpec((1, DN_HEAD_DIM), lambda i: (0, 0)),
            pl.BlockSpec(w_out.shape, lambda i: (0, 0)),
            pl.BlockSpec((tm, D_MODEL), lambda i: (i, 0)),
        ],
        out_specs=pl.BlockSpec((tm, D_MODEL), lambda i: (i, 0)),
        out_shape=jax.ShapeDtypeStruct((T, D_MODEL), F32),
        compiler_params=_cparams(("parallel",)),
        name="odd_out",
    )(o, proj, onorm_g.reshape(1, DN_HEAD_DIM), w_out, h)


def _route_kernel(h_ref, g_ref, wr_ref, ri_ref, gate_ref, cnt_ref, carry_ref):
    i = pl.program_id(0)

    @pl.when(i == 0)
    def _():
        carry_ref[...] = jnp.zeros(carry_ref.shape, F32)

    xn = _rms(h_ref[...], g_ref[...], RMS_EPS)
    logits = jnp.dot(xn.astype(BF16), wr_ref[...].astype(BF16), preferred_element_type=F32)
    tm = logits.shape[0]
    lane = lax.broadcasted_iota(I32, logits.shape, 1)
    logits = jnp.where(lane < N_EXPERTS, logits, NEG_BIG)
    lane_f = lane.astype(F32)
    m1 = jnp.max(logits, axis=-1, keepdims=True)
    i1 = jnp.min(jnp.where(logits == m1, lane_f, float(LANES)), axis=-1, keepdims=True)
    rest = jnp.where(lane_f == i1, NEG_BIG, logits)
    m2 = jnp.max(rest, axis=-1, keepdims=True)
    i2 = jnp.min(jnp.where(rest == m2, lane_f, float(LANES)), axis=-1, keepdims=True)
    e = jnp.exp(m2 - m1)
    g1 = 1.0 / (1.0 + e)
    g2 = e / (1.0 + e)
    oh1 = lane_f == i1
    oh2 = lane_f == i2
    i1 = i1.astype(I32)
    i2 = i2.astype(I32)
    oh = jnp.where(oh1 | oh2, 1.0, 0.0)
    ri_ = lax.broadcasted_iota(I32, (tm, tm), 0)
    ci_ = lax.broadcasted_iota(I32, (tm, tm), 1)
    below = jnp.where(ri_ > ci_, 1.0, 0.0).astype(BF16)
    ex = jnp.dot(below, oh.astype(BF16), preferred_element_type=F32) + carry_ref[0:1, :]
    r1 = jnp.sum(jnp.where(oh1, ex, 0.0), axis=-1, keepdims=True).astype(I32)
    r2 = jnp.sum(jnp.where(oh2, ex, 0.0), axis=-1, keepdims=True).astype(I32)
    ri_ref[...] = jnp.where(lane == 0, i1, jnp.where(lane == 1, i2, jnp.where(lane == 2, r1, r2)))
    gate_ref[...] = jnp.where(lane == 0, g1, g2)
    carry_ref[...] = carry_ref[...] + jnp.sum(oh, axis=0, keepdims=True)
    cnt_ref[...] = carry_ref[...]


def route(h, g, wr):
    T, D = h.shape
    tm = min(ROW_TILE, T)
    return pl.pallas_call(
        _route_kernel,
        grid=(T // tm,),
        in_specs=[
            pl.BlockSpec((tm, D), lambda i: (i, 0)),
            pl.BlockSpec((1, D), lambda i: (0, 0)),
            pl.BlockSpec((D, LANES), lambda i: (0, 0)),
        ],
        out_specs=[
            pl.BlockSpec((tm, LANES), lambda i: (i, 0)),
            pl.BlockSpec((tm, LANES), lambda i: (i, 0)),
            pl.BlockSpec((SUBLANES, LANES), lambda i: (0, 0)),
        ],
        out_shape=[
            jax.ShapeDtypeStruct((T, LANES), I32),
            jax.ShapeDtypeStruct((T, LANES), F32),
            jax.ShapeDtypeStruct((SUBLANES, LANES), F32),
        ],
        scratch_shapes=[pltpu.VMEM((SUBLANES, LANES), F32)],
        compiler_params=_cparams(("arbitrary",)),
        name="route",
    )(h, g.reshape(1, D), wr)


def _row_copy(src_ref, s, dst_ref, d, sem):
    return pltpu.make_async_copy(src_ref.at[pl.ds(s, 1), :], dst_ref.at[pl.ds(d, 1), :], sem)


def _dispatch_kernel(dest_ref, h_ref, g_ref, xz_ref, xs_ref, xn_ref, sem):
    del xz_ref
    tm = h_ref.shape[0]
    xn_ref[...] = _rms(h_ref[...], g_ref[...], RMS_EPS)

    def issue(r, carry):
        _row_copy(xn_ref, r, xs_ref, dest_ref[2 * r], sem).start()
        _row_copy(xn_ref, r, xs_ref, dest_ref[2 * r + 1], sem).start()
        return carry

    lax.fori_loop(0, tm, issue, 0)

    def drain(r, carry):
        _row_copy(xn_ref, 0, xs_ref, 0, sem).wait()
        _row_copy(xn_ref, 0, xs_ref, 0, sem).wait()
        return carry

    lax.fori_loop(0, tm, drain, 0)


def dispatch(h, g, dest_flat, n_slots):
    T, D = h.shape
    tm = min(ROW_TILE, T)
    zeros = jnp.zeros((n_slots, D), F32)
    return pl.pallas_call(
        _dispatch_kernel,
        grid=(T // tm,),
        in_specs=[
            pl.BlockSpec((2 * tm,), lambda i: (i,), memory_space=pltpu.SMEM),
            pl.BlockSpec((tm, D), lambda i: (i, 0)),
            pl.BlockSpec((1, D), lambda i: (0, 0)),
            pl.BlockSpec(memory_space=pl.ANY),
        ],
        out_specs=pl.BlockSpec(memory_space=pl.ANY),
        out_shape=jax.ShapeDtypeStruct((n_slots, D), F32),
        scratch_shapes=[pltpu.VMEM((tm, D), F32), pltpu.SemaphoreType.DMA],
        input_output_aliases={3: 0},
        compiler_params=_cparams(("arbitrary",)),
        name="moe_dispatch",
    )(dest_flat, h, g.reshape(1, D), zeros)


def _gmm_kernel(te_ref, na_ref, x_ref, w1_ref, w3_ref, w2_ref, y_ref, xb_ref, acc_ref):
    b = pl.program_id(0)
    f = pl.program_id(1)

    @pl.when(b < na_ref[0])
    def _():
        @pl.when(f == 0)
        def _():
            xb_ref[...] = x_ref[...].astype(BF16)
            acc_ref[...] = jnp.zeros(acc_ref.shape, F32)

        xb = xb_ref[...]
        a = jnp.dot(xb, w1_ref[...], preferred_element_type=F32)
        c = jnp.dot(xb, w3_ref[...], preferred_element_type=F32)
        hm = (_silu(a) * c).astype(BF16)
        acc_ref[...] += jnp.dot(hm, w2_ref[...], preferred_element_type=F32)

        @pl.when(f == pl.num_programs(1) - 1)
        def _():
            y_ref[...] = acc_ref[...]

    @pl.when((b >= na_ref[0]) & (f == pl.num_programs(1) - 1))
    def _():
        y_ref[...] = jnp.zeros(y_ref.shape, F32)


def gmm(xs, w1, w3, w2, tile_e, n_active):
    P, D = xs.shape
    G = MOE_TILE
    F = w1.shape[2]
    tf = FF_TILE
    nf = F // tf

    def row_idx(b, f, te, na):
        return (jnp.minimum(b, na[0] - 1), 0)

    def f_idx(b, f, na):
        return jnp.where(b < na[0], f, nf - 1)

    grid_spec = pltpu.PrefetchScalarGridSpec(
        num_scalar_prefetch=2,
        grid=(P // G, nf),
        in_specs=[
            pl.BlockSpec((G, D), row_idx),
            pl.BlockSpec((None, D, tf), lambda b, f, te, na: (te[b], 0, f_idx(b, f, na))),
            pl.BlockSpec((None, D, tf), lambda b, f, te, na: (te[b], 0, f_idx(b, f, na))),
            pl.BlockSpec((None, tf, D), lambda b, f, te, na: (te[b], f_idx(b, f, na), 0)),
        ],
        out_specs=pl.BlockSpec((G, D), lambda b, f, te, na: (b, 0)),
        scratch_shapes=[pltpu.VMEM((G, D), BF16), pltpu.VMEM((G, D), F32)],
    )
    return pl.pallas_call(
        _gmm_kernel,
        grid_spec=grid_spec,
        out_shape=jax.ShapeDtypeStruct((P, D), F32),
        compiler_params=_cparams(("arbitrary", "arbitrary")),
        name="moe_gmm",
    )(tile_e, n_active, xs, w1, w3, w2)


def _combine_kernel(dest_ref, h_ref, gate_ref, y_ref, o_ref, ya_ref, yb_ref, sem):
    tm = h_ref.shape[0]

    def issue(r, carry):
        _row_copy(y_ref, dest_ref[2 * r], ya_ref, r, sem).start()
        _row_copy(y_ref, dest_ref[2 * r + 1], yb_ref, r, sem).start()
        return carry

    lax.fori_loop(0, tm, issue, 0)

    def drain(r, carry):
        _row_copy(y_ref, 0, ya_ref, 0, sem).wait()
        _row_copy(y_ref, 0, yb_ref, 0, sem).wait()
        return carry

    lax.fori_loop(0, tm, drain, 0)
    gt = gate_ref[...]
    o_ref[...] = h_ref[...] + gt[:, 0:1] * ya_ref[...] + gt[:, 1:2] * yb_ref[...]


def combine(h, gates, y, dest_flat):
    T, D = h.shape
    tm = min(ROW_TILE, T)
    return pl.pallas_call(
        _combine_kernel,
        grid=(T // tm,),
        in_specs=[
            pl.BlockSpec((2 * tm,), lambda i: (i,), memory_space=pltpu.SMEM),
            pl.BlockSpec((tm, D), lambda i: (i, 0)),
            pl.BlockSpec((tm, LANES), lambda i: (i, 0)),
            pl.BlockSpec(memory_space=pl.ANY),
        ],
        out_specs=pl.BlockSpec((tm, D), lambda i: (i, 0)),
        out_shape=jax.ShapeDtypeStruct((T, D), F32),
        scratch_shapes=[pltpu.VMEM((tm, D), F32), pltpu.VMEM((tm, D), F32), pltpu.SemaphoreType.DMA],
        compiler_params=_cparams(("arbitrary",)),
        name="moe_combine",
    )(dest_flat, h, gates, y)


def _lambda_init(layer_idx):
    return 0.8 - 0.6 * math.exp(-0.3 * layer_idx)


def _even_layer(h, pos_col, pos_row, invf, ln_mix, w_in, w_out, lam_params, subln_g, conv_w, layer_idx):
    proj = norm_matmul(h, ln_mix, w_in.astype(BF16), tn=512)
    qkv = rope_qkv(proj, pos_col, invf)
    o = diff_attention(qkv, pos_col, pos_row, lam_params, subln_g, _lambda_init(layer_idx))
    return even_out(o, proj, conv_w, w_out.astype(BF16), h)


def _odd_mixer(h, ln_mix, w_in, conv_w, a_log, dt_bias, onorm_g, w_out):
    main_w = 4 * DN_WIDTH
    w_main = w_in[:, :main_w].astype(BF16)
    w_ba = jnp.pad(w_in[:, main_w:], ((0, 0), (0, LANES - 2 * DN_HEADS))).astype(BF16)
    proj = norm_matmul(h, ln_mix, w_main, tn=512)
    ba = norm_matmul(h, ln_mix, w_ba, tn=LANES)
    pad8 = lambda v: jnp.pad(v.astype(F32), (DN_HEADS, LANES - 2 * DN_HEADS)).reshape(1, LANES)
    qg, kg, w, u, ai, dl = gdn_prep(proj, ba, conv_w, pad8(a_log), pad8(dt_bias))
    o = gdn_scan(qg, kg, w, u, ai, dl)
    return odd_out(o, proj, onorm_g, w_out.astype(BF16), h)


def _moe(h, ln_ffn, w_router, w1, w3, w2):
    T = h.shape[0]
    G = MOE_TILE
    wr = jnp.pad(w_router, ((0, 0), (0, LANES - N_EXPERTS)))
    ri, gates, cnt = route(h, ln_ffn, wr)
    counts = cnt[0, :N_EXPERTS].astype(I32)
    padded = ((counts + G - 1) // G) * G
    pends = jnp.cumsum(padded)
    pstarts = pends - padded
    dest = (jnp.take(pstarts, ri[:, 0:2]) + ri[:, 2:4]).reshape(-1)
    n_tiles = (2 * T) // G + N_EXPERTS
    tile_e = jnp.minimum(
        jnp.searchsorted(pends, jnp.arange(n_tiles, dtype=I32) * G, side="right"), N_EXPERTS - 1).astype(I32)
    n_active = (pends[-1:] // G).astype(I32)
    xs = dispatch(h, ln_ffn, dest, n_tiles * G)
    y = gmm(xs, w1.astype(BF16), w3.astype(BF16), w2.astype(BF16), tile_e, n_active)
    return combine(h, gates, y, dest)


def kernel(x, p, positions, ln_mix, ln_ffn, ln_ple, ln_final, w_in_even, w_out_even, lam_q1, lam_k1, lam_q2, lam_k2, subln_gain, conv_w_short, w_in_odd, conv_w_qkv, a_log, dt_bias, onorm_gain, w_out_odd, w1_dense, w3_dense, w2_dense, w_router, w1_moe, w3_moe, w2_moe, w_ple_gate, w_ple_proj):
    B, S, D = x.shape
    T = B * S
    depth = p.shape[0]
    h = x.reshape(T, D)
    pos_col = positions.reshape(T, 1).astype(I32)
    pos_row = positions.reshape(1, T).astype(I32)
    inv_freq = ROPE_THETA ** (-jnp.arange(0, ROT_DIM, 2, dtype=F32) / ROT_DIM)
    invf = jnp.tile(inv_freq, LANES // (ROT_DIM // 2)).reshape(1, LANES)
    for i in range(depth):
        j = i // 2
        if i % 2 == 0:
            lam_params = jnp.stack([lam_q1[j], lam_k1[j], lam_q2[j], lam_k2[j]]).astype(F32)
            h = _even_layer(h, pos_col, pos_row, invf, ln_mix[i], w_in_even[j], w_out_even[j], lam_params,
                            subln_gain[j], conv_w_short[j], i)
            h = ffn(h, ln_ffn[i], w1_dense[j].astype(BF16), w3_dense[j].astype(BF16), w2_dense[j].astype(BF16))
        else:
            h = _odd_mixer(h, ln_mix[i], w_in_odd[j], conv_w_qkv[j], a_log[j], dt_bias[j], onorm_gain[j],
                           w_out_odd[j])
            h = _moe(h, ln_ffn[i], w_router[j], w1_moe[j], w3_moe[j], w2_moe[j])
        h = ple(h, ln_ple[i], w_ple_gate[i].astype(BF16), p[i].reshape(T, PLE_DIM), w_ple_proj[i].astype(BF16),
                ln_final, final_norm=(i == depth - 1))
    return h.reshape(B, S, D)
```

```python
import functools
import math

import jax
import jax.numpy as jnp
import numpy as np
from jax import lax
from jax.experimental import pallas as pl
from jax.experimental.pallas import tpu as pltpu

F32 = jnp.float32
BF16 = jnp.bfloat16
I32 = jnp.int32

D_MODEL = 1024
DEPTH = 4
RMS_EPS = 1e-6
DA_HEADS = 4
DA_HEAD_DIM = 64
DA_V_DIM = 2 * DA_HEAD_DIM
ROPE_THETA = 500000.0
ROT_DIM = DA_HEAD_DIM // 4
SUBLN_EPS = 1e-5
SC_WIDTH = 512
DN_HEADS = 8
DN_HEAD_DIM = 128
DN_WIDTH = DN_HEADS * DN_HEAD_DIM
DN_CHUNK = 64
D_FF = 3584
N_EXPERTS = 8
PLE_DIM = 256

LANES = 128
SUBLANES = 8
VMEM_LIMIT = 52 * 1024 * 1024

NEG_BIG = -1e30

ROW_TILE = 512
PROJ_ROW_TILE = 1024
PROJ_COL_TILE = 1024
FFN_ROW_TILE = 1024
FF_TILE = 512
ATT_TQ = 1024
ATT_TK = 512
ATT_TQ_SPLIT = 256
GDN_TILE = 512
GDN_SCAN_HEADS = 4
MOE_TILE = 512


def _cparams(sem):
    return pltpu.CompilerParams(dimension_semantics=sem, vmem_limit_bytes=VMEM_LIMIT)


def _rms(x, g, eps):
    ms = jnp.mean(x * x, axis=-1, keepdims=True)
    return x * lax.rsqrt(ms + eps) * g


def _silu(x):
    return x * (1.0 / (1.0 + jnp.exp(-x)))


def _sigmoid(x):
    return 1.0 / (1.0 + jnp.exp(-x))


def _norm_matmul_kernel(x_ref, g_ref, w_ref, o_ref, xn_ref):
    @pl.when(pl.program_id(1) == 0)
    def _():
        xn_ref[...] = _rms(x_ref[...], g_ref[...], RMS_EPS).astype(BF16)

    o_ref[...] = jnp.dot(xn_ref[...], w_ref[...], preferred_element_type=F32).astype(o_ref.dtype)


def norm_matmul(x, g, w, tn, out_dtype=F32):
    T, D = x.shape
    N = w.shape[1]
    tm = min(PROJ_ROW_TILE, T)
    return pl.pallas_call(
        _norm_matmul_kernel,
        grid=(T // tm, N // tn),
        in_specs=[
            pl.BlockSpec((tm, D), lambda i, j: (i, 0)),
            pl.BlockSpec((1, D), lambda i, j: (0, 0)),
            pl.BlockSpec((D, tn), lambda i, j: (0, j)),
        ],
        out_specs=pl.BlockSpec((tm, tn), lambda i, j: (i, j)),
        out_shape=jax.ShapeDtypeStruct((T, N), out_dtype),
        scratch_shapes=[pltpu.VMEM((tm, D), BF16)],
        compiler_params=_cparams(("parallel", "arbitrary")),
        name="norm_matmul",
    )(x, g.reshape(1, D), w)


def _rope_kernel(proj_ref, pos_ref, invf_ref, o_ref, vt_ref):
    pos = pos_ref[...].astype(F32)
    ang = pos * invf_ref[...]
    cos_t = jnp.cos(ang)
    sin_t = jnp.sin(ang)
    d = lax.broadcasted_iota(I32, ang.shape, 1) & (DA_HEAD_DIM - 1)
    half = ROT_DIM // 2
    c_mul = jnp.where(d < ROT_DIM, cos_t, 1.0)
    s_mul = jnp.where(d < half, -sin_t, jnp.where(d < ROT_DIM, sin_t, 0.0))
    n_qk = 2 * DA_HEADS * DA_HEAD_DIM // LANES
    for c in range(2 * n_qk):
        x = proj_ref[:, c * LANES:(c + 1) * LANES]
        swapped = jnp.where(d < half, pltpu.roll(x, LANES - half, 1), pltpu.roll(x, half, 1))
        r = x * c_mul + swapped * s_mul
        if c < n_qk:
            r = r * (DA_HEAD_DIM ** -0.5 * math.log2(math.e))
        o_ref[:, c * LANES:(c + 1) * LANES] = r.astype(BF16)
    v0 = 2 * n_qk * LANES
    for hd in range(DA_HEADS):
        vt_ref[hd * DA_V_DIM:(hd + 1) * DA_V_DIM, :] = (
            proj_ref[:, v0 + hd * DA_V_DIM:v0 + (hd + 1) * DA_V_DIM].T.astype(BF16))


def rope_qkv(proj, positions_col, invf):
    T = proj.shape[0]
    WQK = 2 * 2 * DA_HEADS * DA_HEAD_DIM
    WV = DA_HEADS * DA_V_DIM
    tm = min(ROW_TILE, T)
    return pl.pallas_call(
        _rope_kernel,
        grid=(T // tm,),
        in_specs=[
            pl.BlockSpec((tm, WQK + WV), lambda i: (i, 0)),
            pl.BlockSpec((tm, 1), lambda i: (i, 0)),
            pl.BlockSpec((1, LANES), lambda i: (0, 0)),
        ],
        out_specs=[pl.BlockSpec((tm, WQK), lambda i: (i, 0)),
                   pl.BlockSpec((WV, tm), lambda i: (0, i))],
        out_shape=[jax.ShapeDtypeStruct((T, WQK), BF16), jax.ShapeDtypeStruct((WV, T), BF16)],
        compiler_params=_cparams(("parallel",)),
        name="rope_qkv",
    )(proj, positions_col, invf)


def _attn_kernel(qt_ref, kt_ref, q_ref, k_ref, v_ref, pq_ref, pk_ref, lam_ref, g_ref, o_ref,
                 m_ref, l_ref, acc_ref, *, tq, tk, lam0):
    p = pl.program_id(1)
    qi = qt_ref[p]
    ki = kt_ref[p]

    @pl.when(ki == 0)
    def _():
        m_ref[...] = jnp.full(m_ref.shape, NEG_BIG, F32)
        l_ref[...] = jnp.zeros(l_ref.shape, F32)
        acc_ref[...] = jnp.zeros(acc_ref.shape, F32)

    tqs = min(tq, ATT_TQ_SPLIT)
    items = [(s, j) for s in range(2) for j in range(tq // tqs)]

    def step(masked):
        q = q_ref[...]
        k = k_ref[...]
        v = v_ref[...]
        st = {}
        for s, j in items:
            qs = q[j * tqs:(j + 1) * tqs, s * DA_HEAD_DIM:(s + 1) * DA_HEAD_DIM]
            ks = k[:, s * DA_HEAD_DIM:(s + 1) * DA_HEAD_DIM]
            st[s, j] = lax.dot_general(ks, qs, (((1,), (1,)), ((), ())), preferred_element_type=F32)
        for s, j in items:
            cols = slice(j * tqs, (j + 1) * tqs)
            sc = st[s, j]
            if masked:
                sc = jnp.where(pk_ref[...] <= pq_ref[:, cols], sc, NEG_BIG)
            m_prev = m_ref[s, :, cols]
            m_new = jnp.maximum(m_prev, jnp.max(sc, axis=0, keepdims=True))
            alpha = jnp.exp2(m_prev - m_new)
            pr = jnp.exp2(sc - m_new)
            l_ref[s, :, cols] = alpha * l_ref[s, :, cols] + jnp.sum(pr, axis=0, keepdims=True)
            pv = jnp.dot(v, pr.astype(BF16), preferred_element_type=F32)
            acc_ref[s, :, cols] = alpha * acc_ref[s, :, cols] + pv
            m_ref[s, :, cols] = m_new

    crosses = (ki + 1) * tk - 1 > qi * tq

    @pl.when(crosses)
    def _():
        step(True)

    @pl.when(jnp.logical_not(crosses))
    def _():
        step(False)

    @pl.when((ki + 1) * tk >= (qi + 1) * tq)
    def _():
        lm = lam_ref[...]
        s1 = jnp.sum(lm[0:1] * lm[1:2], axis=-1, keepdims=True)
        s2 = jnp.sum(lm[2:3] * lm[3:4], axis=-1, keepdims=True)
        lam = jnp.exp(s1) - jnp.exp(s2) + lam0
        ot = acc_ref[0] * (1.0 / l_ref[0]) - lam * (acc_ref[1] * (1.0 / l_ref[1]))
        ms = jnp.mean(ot * ot, axis=0, keepdims=True)
        ot = ot * lax.rsqrt(ms + SUBLN_EPS) * g_ref[...] * (1.0 - lam0)
        o_ref[...] = ot.T.astype(o_ref.dtype)


def diff_attention(qk, vt, pos_col, pos_row, lam_params, subln_g, lam0):
    T = qk.shape[0]
    tq = min(ATT_TQ, T)
    tk = min(ATT_TK, T)
    nq = T // tq
    pairs = [(qi, ki) for qi in range(nq) for ki in range(-(-((qi + 1) * tq) // tk))]
    qt = jnp.asarray(np.array([a for a, _ in pairs], np.int32))
    kt = jnp.asarray(np.array([b for _, b in pairs], np.int32))
    H = DA_HEADS
    kern = functools.partial(_attn_kernel, tq=tq, tk=tk, lam0=lam0)
    grid_spec = pltpu.PrefetchScalarGridSpec(
        num_scalar_prefetch=2,
        grid=(H, len(pairs)),
        in_specs=[
            pl.BlockSpec((tq, LANES), lambda h, p, qt, kt: (qt[p], h)),
            pl.BlockSpec((tk, LANES), lambda h, p, qt, kt: (kt[p], H + h)),
            pl.BlockSpec((DA_V_DIM, tk), lambda h, p, qt, kt: (h, kt[p])),
            pl.BlockSpec((1, tq), lambda h, p, qt, kt: (0, qt[p])),
            pl.BlockSpec((tk, 1), lambda h, p, qt, kt: (kt[p], 0)),
            pl.BlockSpec((4, DA_HEAD_DIM), lambda h, p, qt, kt: (0, 0)),
            pl.BlockSpec((DA_V_DIM, 1), lambda h, p, qt, kt: (0, 0)),
        ],
        out_specs=pl.BlockSpec((tq, DA_V_DIM), lambda h, p, qt, kt: (qt[p], h)),
        scratch_shapes=[
            pltpu.VMEM((2, 1, tq), F32),
            pltpu.VMEM((2, 1, tq), F32),
            pltpu.VMEM((2, DA_V_DIM, tq), F32),
        ],
    )
    return pl.pallas_call(
        kern,
        grid_spec=grid_spec,
        out_shape=jax.ShapeDtypeStruct((T, H * DA_V_DIM), BF16),
        compiler_params=_cparams(("parallel", "arbitrary")),
        name="diff_attention",
    )(qt, kt, qk, qk, vt, pos_row, pos_col, lam_params, subln_g.reshape(DA_V_DIM, 1))


def _even_out_kernel(o_ref, b_ref, c_ref, x_ref, ch_ref, xh_ref, cw_ref, w_ref, h_ref, out_ref, u_ref):
    tm = o_ref.shape[0]
    u_prev = ch_ref[...] * xh_ref[...]
    u_prev = jnp.where(pl.program_id(0) == 0, 0.0, u_prev)
    u_ref[0:SUBLANES, :] = u_prev
    u_ref[SUBLANES:, :] = c_ref[...] * x_ref[...]
    cw = cw_ref[...]
    conv = (u_ref[SUBLANES - 2:SUBLANES - 2 + tm, :] * cw[0:1]
            + u_ref[SUBLANES - 1:SUBLANES - 1 + tm, :] * cw[1:2]
            + u_ref[SUBLANES:, :] * cw[2:3])
    sc = (b_ref[...] * conv).astype(BF16)
    na = o_ref.shape[1]
    acc = jnp.dot(o_ref[...], w_ref[0:na, :], preferred_element_type=F32)
    acc = acc + jnp.dot(sc, w_ref[na:, :], preferred_element_type=F32)
    out_ref[...] = h_ref[...] + acc


def even_out(attn_o, proj, conv_w, w_out, h):
    T = h.shape[0]
    tm = min(ROW_TILE, T)
    W = SC_WIDTH
    cb = (proj.shape[1] - 3 * W) // W
    hb = tm // SUBLANES
    halo = lambda col: pl.BlockSpec((SUBLANES, W), lambda i: (jnp.maximum(i * hb - 1, 0), col))
    return pl.pallas_call(
        _even_out_kernel,
        grid=(T // tm,),
        in_specs=[
            pl.BlockSpec((tm, attn_o.shape[1]), lambda i: (i, 0)),
            pl.BlockSpec((tm, W), lambda i: (i, cb)),
            pl.BlockSpec((tm, W), lambda i: (i, cb + 1)),
            pl.BlockSpec((tm, W), lambda i: (i, cb + 2)),
            halo(cb + 1),
            halo(cb + 2),
            pl.BlockSpec(conv_w.shape, lambda i: (0, 0)),
            pl.BlockSpec(w_out.shape, lambda i: (0, 0)),
            pl.BlockSpec((tm, D_MODEL), lambda i: (i, 0)),
        ],
        out_specs=pl.BlockSpec((tm, D_MODEL), lambda i: (i, 0)),
        out_shape=jax.ShapeDtypeStruct((T, D_MODEL), F32),
        scratch_shapes=[pltpu.VMEM((tm + SUBLANES, W), F32)],
        compiler_params=_cparams(("parallel",)),
        name="even_out",
    )(attn_o, proj, proj, proj, proj, proj, conv_w, w_out, h)


def _ffn_kernel(h_ref, g_ref, w1_ref, w3_ref, w2_ref, o_ref, xn_ref, acc_ref):
    f = pl.program_id(1)

    @pl.when(f == 0)
    def _():
        xn_ref[...] = _rms(h_ref[...], g_ref[...], RMS_EPS).astype(BF16)
        acc_ref[...] = jnp.zeros(acc_ref.shape, F32)

    xn = xn_ref[...]
    a = jnp.dot(xn, w1_ref[...], preferred_element_type=F32)
    b = jnp.dot(xn, w3_ref[...], preferred_element_type=F32)
    hm = (_silu(a) * b).astype(BF16)
    acc_ref[...] += jnp.dot(hm, w2_ref[...], preferred_element_type=F32)

    @pl.when(f == pl.num_programs(1) - 1)
    def _():
        o_ref[...] = h_ref[...] + acc_ref[...]


def ffn(h, g, w1, w3, w2):
    T, D = h.shape
    F = w1.shape[1]
    tm = min(FFN_ROW_TILE, T)
    tf = FF_TILE
    return pl.pallas_call(
        _ffn_kernel,
        grid=(T // tm, F // tf),
        in_specs=[
            pl.BlockSpec((tm, D), lambda i, f: (i, 0)),
            pl.BlockSpec((1, D), lambda i, f: (0, 0)),
            pl.BlockSpec((D, tf), lambda i, f: (0, f)),
            pl.BlockSpec((D, tf), lambda i, f: (0, f)),
            pl.BlockSpec((tf, D), lambda i, f: (f, 0)),
        ],
        out_specs=pl.BlockSpec((tm, D), lambda i, f: (i, 0)),
        out_shape=jax.ShapeDtypeStruct((T, D), F32),
        scratch_shapes=[pltpu.VMEM((tm, D), BF16), pltpu.VMEM((tm, D), F32)],
        compiler_params=_cparams(("parallel", "arbitrary")),
        name="ffn",
    )(h, g.reshape(1, D), w1, w3, w2)


def _ple_kernel(h_ref, g_ref, wg_ref, p_ref, wp_ref, gf_ref, o_ref, *, final_norm):
    h = h_ref[...]
    xn = _rms(h, g_ref[...], RMS_EPS).astype(BF16)
    gate = _sigmoid(jnp.dot(xn, wg_ref[...], preferred_element_type=F32))
    emb = jnp.dot(p_ref[...].astype(BF16), wp_ref[...], preferred_element_type=F32)
    out = h + gate * emb
    if final_norm:
        out = _rms(out, gf_ref[...], RMS_EPS)
    o_ref[...] = out


def ple(h, g, wg, p, wp, g_final, final_norm):
    T, D = h.shape
    tm = min(ROW_TILE, T)
    return pl.pallas_call(
        functools.partial(_ple_kernel, final_norm=final_norm),
        grid=(T // tm,),
        in_specs=[
            pl.BlockSpec((tm, D), lambda i: (i, 0)),
            pl.BlockSpec((1, D), lambda i: (0, 0)),
            pl.BlockSpec((D, D), lambda i: (0, 0)),
            pl.BlockSpec((tm, PLE_DIM), lambda i: (i, 0)),
            pl.BlockSpec((PLE_DIM, D), lambda i: (0, 0)),
            pl.BlockSpec((1, D), lambda i: (0, 0)),
        ],
        out_specs=pl.BlockSpec((tm, D), lambda i: (i, 0)),
        out_shape=jax.ShapeDtypeStruct((T, D), F32),
        compiler_params=_cparams(("parallel",)),
        name="ple",
    )(h, g.reshape(1, D), wg, p, wp, g_final.reshape(1, D))


def _dot_hi(a, b):
    return jnp.dot(a, b, preferred_element_type=F32, precision=lax.Precision.HIGHEST)


def _gdn_prep_kernel(q_ref, k_ref, v_ref, qh_ref, kh_ref, vh_ref, ba_ref, cw_ref, alog_ref, dtb_ref,
                     qg_ref, kg_ref, w_ref, u_ref, ai_ref, dl_ref,
                     xs_ref, gc_ref, gl_ref, sg_ref, gct_ref):
    i = pl.program_id(0)
    h = pl.program_id(1)
    R = q_ref.shape[0]
    C = DN_CHUNK
    KC = cw_ref.shape[0]

    def conv_silu(x_ref, halo_ref, j):
        xs_ref[0:SUBLANES, :] = jnp.where(i == 0, 0.0, halo_ref[...])
        xs_ref[SUBLANES:, :] = x_ref[...]
        acc = xs_ref[SUBLANES:, :] * cw_ref[KC - 1:KC, j * LANES:(j + 1) * LANES]
        for t in range(1, KC):
            acc = acc + (xs_ref[SUBLANES - t:SUBLANES - t + R, :]
                         * cw_ref[KC - 1 - t:KC - t, j * LANES:(j + 1) * LANES])
        return _silu(acc)

    def l2n(x):
        return x * lax.rsqrt(jnp.sum(x * x, axis=-1, keepdims=True) + 1e-6)

    chunks = range(R // C)
    rows = [slice(c * C, (c + 1) * C) for c in chunks]
    r64 = lax.broadcasted_iota(I32, (C, C), 0)
    c64 = lax.broadcasted_iota(I32, (C, C), 1)
    incl = r64 >= c64
    strict = r64 > c64

    @pl.when(h == 0)
    def _():
        ba = ba_ref[...]
        sp_in = ba + dtb_ref[...]
        softplus = jnp.maximum(sp_in, 0.0) + jnp.log(1.0 + jnp.exp(-jnp.abs(sp_in)))
        g_all = -jnp.exp(alog_ref[...]) * softplus
        g_wide = jnp.concatenate([g_all[r] for r in rows], axis=1)
        gc_wide = _dot_hi(jnp.where(incl, 1.0, 0.0), g_wide)
        for c in chunks:
            gc_c = gc_wide[:, c * LANES:(c + 1) * LANES]
            gc_ref[rows[c], :] = gc_c
            gl_ref[rows[c], :] = jnp.broadcast_to(gc_c[C - 1:C, :], (C, LANES))
        sg_ref[...] = _sigmoid(ba)
        gct_ref[...] = gc_ref[...].T

    q = l2n(conv_silu(q_ref, qh_ref, 0)) * (DN_HEAD_DIM ** -0.5)
    k = l2n(conv_silu(k_ref, kh_ref, 1))
    v = conv_silu(v_ref, vh_ref, 2)

    lane = lax.broadcasted_iota(I32, (R, LANES), 1)

    def pick(ref, l):
        return jnp.sum(jnp.where(lane == l, ref[...], 0.0), axis=-1, keepdims=True)

    beta = pick(sg_ref, h)
    gcol = pick(gc_ref, DN_HEADS + h)
    glast = pick(gl_ref, DN_HEADS + h)
    grow = gct_ref[pl.ds(DN_HEADS + h, 1), :]
    eg = jnp.exp(gcol)
    kb = k * beta
    vb = v * beta
    kbg = kb * eg
    qg_ref[...] = (q * eg).astype(BF16)
    kg_ref[...] = (k * jnp.exp(glast - gcol)).astype(BF16)
    edl = jnp.broadcast_to(jnp.exp(glast), (R, LANES))

    kbf = k.astype(BF16)
    lhs = jnp.concatenate([kb.astype(BF16).reshape(R // C, C, LANES),
                           q.astype(BF16).reshape(R // C, C, LANES)], axis=1)
    rhs = jnp.concatenate([vb, kbg], axis=-1)

    def bdot(a, b):
        return jnp.dot(a.astype(BF16), b.astype(BF16), preferred_element_type=F32)

    decay = [jnp.where(incl, jnp.exp(jnp.where(incl, gcol[r] - grow[:, r], 0.0)), 0.0) for r in rows]
    kq = [lax.dot_general(lhs[c], kbf[rows[c]], (((1,), (1,)), ((), ())), preferred_element_type=F32)
          for c in chunks]
    pw = [-jnp.where(strict, kq[c][:C] * decay[c], 0.0) for c in chunks]
    n = list(pw)
    for _ in range(5):
        pw = [bdot(pw[c], pw[c]) for c in chunks]
        n = [n[c] + pw[c] + bdot(n[c], pw[c]) for c in chunks]
    for c in chunks:
        rc = rhs[rows[c]]
        uw = rc + bdot(n[c], rc)
        u_ref[rows[c], :] = uw[:, :DN_HEAD_DIM]
        w_ref[rows[c], :] = uw[:, DN_HEAD_DIM:].astype(BF16)
        ai_ref[rows[c], :] = (kq[c][C:] * decay[c]).astype(BF16)
        dl_ref[c:c + 1, :] = edl[c * C:c * C + 1, :]


def gdn_prep(proj, ba, conv_w, alog_l, dtb_l):
    T = proj.shape[0]
    R = min(GDN_TILE, T)
    H = DN_HEADS
    nchunk = T // DN_CHUNK
    hb = R // SUBLANES
    col = lambda j: pl.BlockSpec((R, LANES), lambda i, h: (i, j * H + h))
    halo = lambda j: pl.BlockSpec((SUBLANES, LANES), lambda i, h: (jnp.maximum(i * hb - 1, 0), j * H + h))
    KC = conv_w.shape[0]
    cw = conv_w.reshape(KC, 3, H, LANES).transpose(2, 0, 1, 3).reshape(H, KC, 3 * LANES)
    cw_spec = pl.BlockSpec((None, KC, 3 * LANES), lambda i, h: (h, 0, 0))
    row_out = lambda dt: jax.ShapeDtypeStruct((T, DN_WIDTH), dt)
    out_col = pl.BlockSpec((R, LANES), lambda i, h: (i, h))
    return pl.pallas_call(
        _gdn_prep_kernel,
        grid=(T // R, H),
        in_specs=[col(0), col(1), col(2), halo(0), halo(1), halo(2),
                  pl.BlockSpec((R, LANES), lambda i, h: (i, 0)),
                  cw_spec,
                  pl.BlockSpec((1, LANES), lambda i, h: (0, 0)),
                  pl.BlockSpec((1, LANES), lambda i, h: (0, 0))],
        out_specs=[out_col, out_col, out_col, out_col,
                   pl.BlockSpec((None, R, DN_CHUNK), lambda i, h: (h, i, 0)),
                   pl.BlockSpec((None, R // DN_CHUNK, LANES), lambda i, h: (h, i, 0))],
        out_shape=[row_out(BF16), row_out(BF16), row_out(BF16), row_out(F32),
                   jax.ShapeDtypeStruct((H, T, DN_CHUNK), BF16),
                   jax.ShapeDtypeStruct((H, nchunk, LANES), F32)],
        scratch_shapes=[pltpu.VMEM((R + SUBLANES, LANES), F32),
                        pltpu.VMEM((R, LANES), F32), pltpu.VMEM((R, LANES), F32), pltpu.VMEM((R, LANES), F32),
                        pltpu.VMEM((LANES, R), F32)],
        compiler_params=_cparams(("parallel", "arbitrary")),
        name="gdn_prep",
    )(proj, proj, proj, proj, proj, proj, ba, cw, alog_l, dtb_l)


def _gdn_scan_kernel(qg_ref, kg_ref, w_ref, u_ref, ai_ref, dl_ref, o_ref, s_ref):
    @pl.when(pl.program_id(1) == 0)
    def _():
        s_ref[...] = jnp.zeros(s_ref.shape, F32)

    C = DN_CHUNK
    D = DN_HEAD_DIM
    heads = range(s_ref.shape[0])

    def chunk(c, carry):
        rows = pl.ds(pl.multiple_of(c * C, C), C)
        s = [s_ref[hh] for hh in heads]
        sb = [x.astype(BF16) for x in s]
        cols = [slice(hh * D, (hh + 1) * D) for hh in heads]
        r1 = [jnp.dot(jnp.concatenate([w_ref[rows, cols[hh]], qg_ref[rows, cols[hh]]], axis=0), sb[hh],
                      preferred_element_type=F32) for hh in heads]
        vb = [(u_ref[rows, cols[hh]] - r1[hh][:C]).astype(BF16) for hh in heads]
        for hh in heads:
            o_ref[rows, cols[hh]] = r1[hh][C:] + jnp.dot(ai_ref[hh, rows, :], vb[hh], preferred_element_type=F32)
        upd = [lax.dot_general(kg_ref[rows, cols[hh]], vb[hh], (((0,), (0,)), ((), ())),
                               preferred_element_type=F32) for hh in heads]
        for hh in heads:
            s_ref[hh] = s[hh] * dl_ref[hh, pl.ds(c, 1), :] + upd[hh]
        return carry

    lax.fori_loop(0, qg_ref.shape[0] // C, chunk, 0)


def gdn_scan(qg, kg, w, u, ai, dl):
    T = qg.shape[0]
    R = min(GDN_TILE, T)
    H = DN_HEADS
    HB = GDN_SCAN_HEADS
    col = pl.BlockSpec((R, HB * LANES), lambda h, i: (i, h))
    return pl.pallas_call(
        _gdn_scan_kernel,
        grid=(H // HB, T // R),
        in_specs=[col, col, col, col,
                  pl.BlockSpec((HB, R, DN_CHUNK), lambda h, i: (h, i, 0)),
                  pl.BlockSpec((HB, R // DN_CHUNK, LANES), lambda h, i: (h, i, 0))],
        out_specs=col,
        out_shape=jax.ShapeDtypeStruct((T, DN_WIDTH), F32),
        scratch_shapes=[pltpu.VMEM((HB, DN_HEAD_DIM, DN_HEAD_DIM), F32)],
        compiler_params=_cparams(("parallel", "arbitrary")),
        name="gdn_scan",
    )(qg, kg, w, u, ai, dl)


def _odd_out_kernel(o_ref, z_ref, g_ref, w_ref, h_ref, out_ref):
    g = g_ref[...]
    parts = []
    for hd in range(DN_HEADS):
        sl = slice(hd * DN_HEAD_DIM, (hd + 1) * DN_HEAD_DIM)
        parts.append((_rms(o_ref[:, sl], g, RMS_EPS) * _silu(z_ref[:, sl])).astype(BF16))
    y = jnp.concatenate(parts, axis=-1)
    out_ref[...] = h_ref[...] + jnp.dot(y, w_ref[...], preferred_element_type=F32)


def odd_out(o, proj, onorm_g, w_out, h):
    T = h.shape[0]
    tm = min(ROW_TILE, T)
    zb = 3 * DN_WIDTH // DN_WIDTH
    return pl.pallas_call(
        _odd_out_kernel,
        grid=(T // tm,),
        in_specs=[
            pl.BlockSpec((tm, DN_WIDTH), lambda i: (i, 0)),
            pl.BlockSpec((tm, DN_WIDTH), lambda i: (i, zb)),
            pl.BlockSpec((1, DN_HEAD_DIM), lambda i: (0, 0)),
            pl.BlockSpec(w_out.shape, lambda i: (0, 0)),
            pl.BlockSpec((tm, D_MODEL), lambda i: (i, 0)),
        ],
        out_specs=pl.BlockSpec((tm, D_MODEL), lambda i: (i, 0)),
        out_shape=jax.ShapeDtypeStruct((T, D_MODEL), F32),
        compiler_params=_cparams(("parallel",)),
        name="odd_out",
    )(o, proj, onorm_g.reshape(1, DN_HEAD_DIM), w_out, h)


def _route_kernel(h_ref, g_ref, wr_ref, ri_ref, gate_ref, cnt_ref, carry_ref):
    i = pl.program_id(0)

    @pl.when(i == 0)
    def _():
        carry_ref[...] = jnp.zeros(carry_ref.shape, F32)

    xn = _rms(h_ref[...], g_ref[...], RMS_EPS)
    logits = jnp.dot(xn.astype(BF16), wr_ref[...].astype(BF16), preferred_element_type=F32)
    tm = logits.shape[0]
    lane = lax.broadcasted_iota(I32, logits.shape, 1)
    logits = jnp.where(lane < N_EXPERTS, logits, NEG_BIG)
    lane_f = lane.astype(F32)
    m1 = jnp.max(logits, axis=-1, keepdims=True)
    i1 = jnp.min(jnp.where(logits == m1, lane_f, float(LANES)), axis=-1, keepdims=True)
    rest = jnp.where(lane_f == i1, NEG_BIG, logits)
    m2 = jnp.max(rest, axis=-1, keepdims=True)
    i2 = jnp.min(jnp.where(rest == m2, lane_f, float(LANES)), axis=-1, keepdims=True)
    e = jnp.exp(m2 - m1)
    g1 = 1.0 / (1.0 + e)
    g2 = e / (1.0 + e)
    oh1 = lane_f == i1
    oh2 = lane_f == i2
    i1 = i1.astype(I32)
    i2 = i2.astype(I32)
    oh = jnp.where(oh1 | oh2, 1.0, 0.0)
    ri_ = lax.broadcasted_iota(I32, (tm, tm), 0)
    ci_ = lax.broadcasted_iota(I32, (tm, tm), 1)
    below = jnp.where(ri_ > ci_, 1.0, 0.0).astype(BF16)
    ex = jnp.dot(below, oh.astype(BF16), preferred_element_type=F32) + carry_ref[0:1, :]
    r1 = jnp.sum(jnp.where(oh1, ex, 0.0), axis=-1, keepdims=True).astype(I32)
    r2 = jnp.sum(jnp.where(oh2, ex, 0.0), axis=-1, keepdims=True).astype(I32)
    ri_ref[...] = jnp.where(lane == 0, i1, jnp.where(lane == 1, i2, jnp.where(lane == 2, r1, r2)))
    gate_ref[...] = jnp.where(lane == 0, g1, g2)
    carry_ref[...] = carry_ref[...] + jnp.sum(oh, axis=0, keepdims=True)
    cnt_ref[...] = carry_ref[...]


def route(h, g, wr):
    T, D = h.shape
    tm = min(ROW_TILE, T)
    return pl.pallas_call(
        _route_kernel,
        grid=(T // tm,),
        in_specs=[
            pl.BlockSpec((tm, D), lambda i: (i, 0)),
            pl.BlockSpec((1, D), lambda i: (0, 0)),
            pl.BlockSpec((D, LANES), lambda i: (0, 0)),
        ],
        out_specs=[
            pl.BlockSpec((tm, LANES), lambda i: (i, 0)),
            pl.BlockSpec((tm, LANES), lambda i: (i, 0)),
            pl.BlockSpec((SUBLANES, LANES), lambda i: (0, 0)),
        ],
        out_shape=[
            jax.ShapeDtypeStruct((T, LANES), I32),
            jax.ShapeDtypeStruct((T, LANES), F32),
            jax.ShapeDtypeStruct((SUBLANES, LANES), F32),
        ],
        scratch_shapes=[pltpu.VMEM((SUBLANES, LANES), F32)],
        compiler_params=_cparams(("arbitrary",)),
        name="route",
    )(h, g.reshape(1, D), wr)


def _row_copy(src_ref, s, dst_ref, d, sem):
    return pltpu.make_async_copy(src_ref.at[pl.ds(s, 1), :], dst_ref.at[pl.ds(d, 1), :], sem)


def _dispatch_kernel(dest_ref, h_ref, g_ref, xz_ref, xs_ref, xn_ref, sem):
    del xz_ref
    tm = h_ref.shape[0]
    xn_ref[...] = _rms(h_ref[...], g_ref[...], RMS_EPS)

    def issue(r, carry):
        _row_copy(xn_ref, r, xs_ref, dest_ref[2 * r], sem).start()
        _row_copy(xn_ref, r, xs_ref, dest_ref[2 * r + 1], sem).start()
        return carry

    lax.fori_loop(0, tm, issue, 0)

    def drain(r, carry):
        _row_copy(xn_ref, 0, xs_ref, 0, sem).wait()
        _row_copy(xn_ref, 0, xs_ref, 0, sem).wait()
        return carry

    lax.fori_loop(0, tm, drain, 0)


def dispatch(h, g, dest_flat, n_slots):
    T, D = h.shape
    tm = min(ROW_TILE, T)
    zeros = jnp.zeros((n_slots, D), F32)
    return pl.pallas_call(
        _dispatch_kernel,
        grid=(T // tm,),
        in_specs=[
            pl.BlockSpec((2 * tm,), lambda i: (i,), memory_space=pltpu.SMEM),
            pl.BlockSpec((tm, D), lambda i: (i, 0)),
            pl.BlockSpec((1, D), lambda i: (0, 0)),
            pl.BlockSpec(memory_space=pl.ANY),
        ],
        out_specs=pl.BlockSpec(memory_space=pl.ANY),
        out_shape=jax.ShapeDtypeStruct((n_slots, D), F32),
        scratch_shapes=[pltpu.VMEM((tm, D), F32), pltpu.SemaphoreType.DMA],
        input_output_aliases={3: 0},
        compiler_params=_cparams(("arbitrary",)),
        name="moe_dispatch",
    )(dest_flat, h, g.reshape(1, D), zeros)


def _gmm_kernel(te_ref, na_ref, x_ref, w1_ref, w3_ref, w2_ref, y_ref, xb_ref, acc_ref):
    b = pl.program_id(0)
    f = pl.program_id(1)

    @pl.when(b < na_ref[0])
    def _():
        @pl.when(f == 0)
        def _():
            xb_ref[...] = x_ref[...].astype(BF16)
            acc_ref[...] = jnp.zeros(acc_ref.shape, F32)

        xb = xb_ref[...]
        a = jnp.dot(xb, w1_ref[...], preferred_element_type=F32)
        c = jnp.dot(xb, w3_ref[...], preferred_element_type=F32)
        hm = (_silu(a) * c).astype(BF16)
        acc_ref[...] += jnp.dot(hm, w2_ref[...], preferred_element_type=F32)

        @pl.when(f == pl.num_programs(1) - 1)
        def _():
            y_ref[...] = acc_ref[...]

    @pl.when((b >= na_ref[0]) & (f == pl.num_programs(1) - 1))
    def _():
        y_ref[...] = jnp.zeros(y_ref.shape, F32)


def gmm(xs, w1, w3, w2, tile_e, n_active):
    P, D = xs.shape
    G = MOE_TILE
    F = w1.shape[2]
    tf = FF_TILE
    nf = F // tf

    def row_idx(b, f, te, na):
        return (jnp.minimum(b, na[0] - 1), 0)

    def f_idx(b, f, na):
        return jnp.where(b < na[0], f, nf - 1)

    grid_spec = pltpu.PrefetchScalarGridSpec(
        num_scalar_prefetch=2,
        grid=(P // G, nf),
        in_specs=[
            pl.BlockSpec((G, D), row_idx),
            pl.BlockSpec((None, D, tf), lambda b, f, te, na: (te[b], 0, f_idx(b, f, na))),
            pl.BlockSpec((None, D, tf), lambda b, f, te, na: (te[b], 0, f_idx(b, f, na))),
            pl.BlockSpec((None, tf, D), lambda b, f, te, na: (te[b], f_idx(b, f, na), 0)),
        ],
        out_specs=pl.BlockSpec((G, D), lambda b, f, te, na: (b, 0)),
        scratch_shapes=[pltpu.VMEM((G, D), BF16), pltpu.VMEM((G, D), F32)],
    )
    return pl.pallas_call(
        _gmm_kernel,
        grid_spec=grid_spec,
        out_shape=jax.ShapeDtypeStruct((P, D), F32),
        compiler_params=_cparams(("arbitrary", "arbitrary")),
        name="moe_gmm",
    )(tile_e, n_active, xs, w1, w3, w2)


def _combine_kernel(dest_ref, h_ref, gate_ref, y_ref, o_ref, ya_ref, yb_ref, sem):
    tm = h_ref.shape[0]

    def issue(r, carry):
        _row_copy(y_ref, dest_ref[2 * r], ya_ref, r, sem).start()
        _row_copy(y_ref, dest_ref[2 * r + 1], yb_ref, r, sem).start()
        return carry

    lax.fori_loop(0, tm, issue, 0)

    def drain(r, carry):
        _row_copy(y_ref, 0, ya_ref, 0, sem).wait()
        _row_copy(y_ref, 0, yb_ref, 0, sem).wait()
        return carry

    lax.fori_loop(0, tm, drain, 0)
    gt = gate_ref[...]
    o_ref[...] = h_ref[...] + gt[:, 0:1] * ya_ref[...] + gt[:, 1:2] * yb_ref[...]


def combine(h, gates, y, dest_flat):
    T, D = h.shape
    tm = min(ROW_TILE, T)
    return pl.pallas_call(
        _combine_kernel,
        grid=(T // tm,),
        in_specs=[
            pl.BlockSpec((2 * tm,), lambda i: (i,), memory_space=pltpu.SMEM),
            pl.BlockSpec((tm, D), lambda i: (i, 0)),
            pl.BlockSpec((tm, LANES), lambda i: (i, 0)),
            pl.BlockSpec(memory_space=pl.ANY),
        ],
        out_specs=pl.BlockSpec((tm, D), lambda i: (i, 0)),
        out_shape=jax.ShapeDtypeStruct((T, D), F32),
        scratch_shapes=[pltpu.VMEM((tm, D), F32), pltpu.VMEM((tm, D), F32), pltpu.SemaphoreType.DMA],
        compiler_params=_cparams(("arbitrary",)),
        name="moe_combine",
    )(dest_flat, h, gates, y)


def _lambda_init(layer_idx):
    return 0.8 - 0.6 * math.exp(-0.3 * layer_idx)


def _even_layer(h, pos_col, pos_row, invf, ln_mix, w_in, w_out, lam_params, subln_g, conv_w, layer_idx):
    proj = norm_matmul(h, ln_mix, w_in.astype(BF16), tn=PROJ_COL_TILE)
    qk, vt = rope_qkv(proj, pos_col, invf)
    o = diff_attention(qk, vt, pos_col, pos_row, lam_params, subln_g, _lambda_init(layer_idx))
    return even_out(o, proj, conv_w, w_out.astype(BF16), h)


def _odd_mixer(h, ln_mix, w_in, conv_w, a_log, dt_bias, onorm_g, w_out):
    main_w = 4 * DN_WIDTH
    w_main = w_in[:, :main_w].astype(BF16)
    w_ba = jnp.pad(w_in[:, main_w:], ((0, 0), (0, LANES - 2 * DN_HEADS))).astype(BF16)
    proj = norm_matmul(h, ln_mix, w_main, tn=PROJ_COL_TILE)
    ba = norm_matmul(h, ln_mix, w_ba, tn=LANES)
    pad8 = lambda v: jnp.pad(v.astype(F32), (DN_HEADS, LANES - 2 * DN_HEADS)).reshape(1, LANES)
    qg, kg, w, u, ai, dl = gdn_prep(proj, ba, conv_w, pad8(a_log), pad8(dt_bias))
    o = gdn_scan(qg, kg, w, u, ai, dl)
    return odd_out(o, proj, onorm_g, w_out.astype(BF16), h)


def _moe(h, ln_ffn, w_router, w1, w3, w2):
    T = h.shape[0]
    G = MOE_TILE
    wr = jnp.pad(w_router, ((0, 0), (0, LANES - N_EXPERTS)))
    ri, gates, cnt = route(h, ln_ffn, wr)
    counts = cnt[0, :N_EXPERTS].astype(I32)
    padded = ((counts + G - 1) // G) * G
    pends = jnp.cumsum(padded)
    pstarts = pends - padded
    dest = (jnp.take(pstarts, ri[:, 0:2]) + ri[:, 2:4]).reshape(-1)
    n_tiles = (2 * T) // G + N_EXPERTS
    tile_e = jnp.minimum(
        jnp.searchsorted(pends, jnp.arange(n_tiles, dtype=I32) * G, side="right"), N_EXPERTS - 1).astype(I32)
    n_active = (pends[-1:] // G).astype(I32)
    xs = dispatch(h, ln_ffn, dest, n_tiles * G)
    y = gmm(xs, w1.astype(BF16), w3.astype(BF16), w2.astype(BF16), tile_e, n_active)
    return combine(h, gates, y, dest)


def kernel(x, p, positions, ln_mix, ln_ffn, ln_ple, ln_final, w_in_even, w_out_even, lam_q1, lam_k1, lam_q2, lam_k2, subln_gain, conv_w_short, w_in_odd, conv_w_qkv, a_log, dt_bias, onorm_gain, w_out_odd, w1_dense, w3_dense, w2_dense, w_router, w1_moe, w3_moe, w2_moe, w_ple_gate, w_ple_proj):
    B, S, D = x.shape
    T = B * S
    depth = p.shape[0]
    h = x.reshape(T, D)
    pos_col = positions.reshape(T, 1).astype(I32)
    pos_row = positions.reshape(1, T).astype(I32)
    inv_freq = ROPE_THETA ** (-jnp.arange(0, ROT_DIM, 2, dtype=F32) / ROT_DIM)
    invf = jnp.tile(inv_freq, LANES // (ROT_DIM // 2)).reshape(1, LANES)
    for i in range(depth):
        j = i // 2
        if i % 2 == 0:
            lam_params = jnp.stack([lam_q1[j], lam_k1[j], lam_q2[j], lam_k2[j]]).astype(F32)
            h = _even_layer(h, pos_col, pos_row, invf, ln_mix[i], w_in_even[j], w_out_even[j], lam_params,
                            subln_gain[j], conv_w_short[j], i)
            h = ffn(h, ln_ffn[i], w1_dense[j].astype(BF16), w3_dense[j].astype(BF16), w2_dense[j].astype(BF16))
        else:
            h = _odd_mixer(h, ln_mix[i], w_in_odd[j], conv_w_qkv[j], a_log[j], dt_bias[j], onorm_gain[j],
                           w_out_odd[j])
            h = _moe(h, ln_ffn[i], w_router[j], w1_moe[j], w3_moe[j], w2_moe[j])
        h = ple(h, ln_ple[i], w_ple_gate[i].astype(BF16), p[i].reshape(T, PLE_DIM), w_ple_proj[i].astype(BF16),
                ln_final, final_norm=(i == depth - 1))
    return h.reshape(B, S, D)
```

```python
import functools
import math

import jax
import jax.numpy as jnp
import numpy as np
from jax import lax
from jax.experimental import pallas as pl
from jax.experimental.pallas import tpu as pltpu

F32 = jnp.float32
BF16 = jnp.bfloat16
I32 = jnp.int32

D_MODEL = 1024
DEPTH = 4
RMS_EPS = 1e-6
DA_HEADS = 4
DA_HEAD_DIM = 64
DA_V_DIM = 2 * DA_HEAD_DIM
ROPE_THETA = 500000.0
ROT_DIM = DA_HEAD_DIM // 4
SUBLN_EPS = 1e-5
SC_WIDTH = 512
DN_HEADS = 8
DN_HEAD_DIM = 128
DN_WIDTH = DN_HEADS * DN_HEAD_DIM
DN_CHUNK = 64
D_FF = 3584
N_EXPERTS = 8
PLE_DIM = 256

LANES = 128
SUBLANES = 8
VMEM_LIMIT = 52 * 1024 * 1024

NEG_BIG = -1e30

ROW_TILE = 512
PROJ_ROW_TILE = 1024
PROJ_COL_TILE = 1024
FFN_ROW_TILE = 1024
FF_TILE = 512
MOE_FF_TILE = 1792
ATT_TQ = 1024
ATT_TK = 1024
ATT_TQ_SPLIT = 256
GDN_TILE = 512
GDN_SCAN_HEADS = 4
MOE_TILE = 512
ROW_DMA_UNROLL = 8


def _cparams(sem):
    return pltpu.CompilerParams(dimension_semantics=sem, vmem_limit_bytes=VMEM_LIMIT)


def _rms(x, g, eps):
    ms = jnp.mean(x * x, axis=-1, keepdims=True)
    return x * lax.rsqrt(ms + eps) * g


def _silu(x):
    return x * (1.0 / (1.0 + jnp.exp(-x)))


def _sigmoid(x):
    return 1.0 / (1.0 + jnp.exp(-x))


def _norm_matmul_kernel(x_ref, g_ref, w_ref, o_ref, xn_ref):
    @pl.when(pl.program_id(1) == 0)
    def _():
        xn_ref[...] = _rms(x_ref[...], g_ref[...], RMS_EPS).astype(BF16)

    o_ref[...] = jnp.dot(xn_ref[...], w_ref[...], preferred_element_type=F32).astype(o_ref.dtype)


def norm_matmul(x, g, w, tn, out_dtype=F32):
    T, D = x.shape
    N = w.shape[1]
    tm = min(PROJ_ROW_TILE, T)
    return pl.pallas_call(
        _norm_matmul_kernel,
        grid=(T // tm, N // tn),
        in_specs=[
            pl.BlockSpec((tm, D), lambda i, j: (i, 0)),
            pl.BlockSpec((1, D), lambda i, j: (0, 0)),
            pl.BlockSpec((D, tn), lambda i, j: (0, j)),
        ],
        out_specs=pl.BlockSpec((tm, tn), lambda i, j: (i, j)),
        out_shape=jax.ShapeDtypeStruct((T, N), out_dtype),
        scratch_shapes=[pltpu.VMEM((tm, D), BF16)],
        compiler_params=_cparams(("parallel", "arbitrary")),
        name="norm_matmul",
    )(x, g.reshape(1, D), w)


def _rope_kernel(proj_ref, pos_ref, invf_ref, o_ref, vt_ref):
    pos = pos_ref[...].astype(F32)
    ang = pos * invf_ref[...]
    cos_t = jnp.cos(ang)
    sin_t = jnp.sin(ang)
    d = lax.broadcasted_iota(I32, ang.shape, 1) & (DA_HEAD_DIM - 1)
    half = ROT_DIM // 2
    c_mul = jnp.where(d < ROT_DIM, cos_t, 1.0)
    s_mul = jnp.where(d < half, -sin_t, jnp.where(d < ROT_DIM, sin_t, 0.0))
    n_qk = 2 * DA_HEADS * DA_HEAD_DIM // LANES
    for c in range(2 * n_qk):
        x = proj_ref[:, c * LANES:(c + 1) * LANES]
        swapped = jnp.where(d < half, pltpu.roll(x, LANES - half, 1), pltpu.roll(x, half, 1))
        r = x * c_mul + swapped * s_mul
        if c < n_qk:
            r = r * (DA_HEAD_DIM ** -0.5 * math.log2(math.e))
        o_ref[:, c * LANES:(c + 1) * LANES] = r.astype(BF16)
    v0 = 2 * n_qk * LANES
    for hd in range(DA_HEADS):
        vt_ref[hd * DA_V_DIM:(hd + 1) * DA_V_DIM, :] = (
            proj_ref[:, v0 + hd * DA_V_DIM:v0 + (hd + 1) * DA_V_DIM].T.astype(BF16))


def rope_qkv(proj, positions_col, invf):
    T = proj.shape[0]
    WQK = 2 * 2 * DA_HEADS * DA_HEAD_DIM
    WV = DA_HEADS * DA_V_DIM
    tm = min(ROW_TILE, T)
    return pl.pallas_call(
        _rope_kernel,
        grid=(T // tm,),
        in_specs=[
            pl.BlockSpec((tm, WQK + WV), lambda i: (i, 0)),
            pl.BlockSpec((tm, 1), lambda i: (i, 0)),
            pl.BlockSpec((1, LANES), lambda i: (0, 0)),
        ],
        out_specs=[pl.BlockSpec((tm, WQK), lambda i: (i, 0)),
                   pl.BlockSpec((WV, tm), lambda i: (0, i))],
        out_shape=[jax.ShapeDtypeStruct((T, WQK), BF16), jax.ShapeDtypeStruct((WV, T), BF16)],
        compiler_params=_cparams(("parallel",)),
        name="rope_qkv",
    )(proj, positions_col, invf)


def _attn_kernel(qt_ref, kt_ref, q_ref, k_ref, v_ref, pq_ref, pk_ref, lam_ref, g_ref, o_ref,
                 m_ref, l_ref, acc_ref, *, tq, tk, lam0):
    p = pl.program_id(1)
    qi = qt_ref[p]
    ki = kt_ref[p]

    @pl.when(ki == 0)
    def _():
        m_ref[...] = jnp.full(m_ref.shape, NEG_BIG, F32)
        l_ref[...] = jnp.zeros(l_ref.shape, F32)
        acc_ref[...] = jnp.zeros(acc_ref.shape, F32)

    tqs = min(tq, ATT_TQ_SPLIT)
    items = [(s, j) for s in range(2) for j in range(tq // tqs)]

    def step(masked):
        q = q_ref[...]
        k = k_ref[...]
        v = v_ref[...]
        st = {}
        for s, j in items:
            qs = q[j * tqs:(j + 1) * tqs, s * DA_HEAD_DIM:(s + 1) * DA_HEAD_DIM]
            ks = k[:, s * DA_HEAD_DIM:(s + 1) * DA_HEAD_DIM]
            st[s, j] = lax.dot_general(ks, qs, (((1,), (1,)), ((), ())), preferred_element_type=F32)
        for s, j in items:
            cols = slice(j * tqs, (j + 1) * tqs)
            sc = st[s, j]
            if masked:
                sc = jnp.where(pk_ref[...] <= pq_ref[:, cols], sc, NEG_BIG)
            m_prev = m_ref[s, :, cols]
            m_new = jnp.maximum(m_prev, jnp.max(sc, axis=0, keepdims=True))
            alpha = jnp.exp2(m_prev - m_new)
            pr = jnp.exp2(sc - m_new)
            l_ref[s, :, cols] = alpha * l_ref[s, :, cols] + jnp.sum(pr, axis=0, keepdims=True)
            pv = jnp.dot(v, pr.astype(BF16), preferred_element_type=F32)
            acc_ref[s, :, cols] = alpha * acc_ref[s, :, cols] + pv
            m_ref[s, :, cols] = m_new

    crosses = (ki + 1) * tk - 1 > qi * tq

    @pl.when(crosses)
    def _():
        step(True)

    @pl.when(jnp.logical_not(crosses))
    def _():
        step(False)

    @pl.when((ki + 1) * tk >= (qi + 1) * tq)
    def _():
        lm = lam_ref[...]
        s1 = jnp.sum(lm[0:1] * lm[1:2], axis=-1, keepdims=True)
        s2 = jnp.sum(lm[2:3] * lm[3:4], axis=-1, keepdims=True)
        lam = jnp.exp(s1) - jnp.exp(s2) + lam0
        ot = acc_ref[0] * (1.0 / l_ref[0]) - lam * (acc_ref[1] * (1.0 / l_ref[1]))
        ms = jnp.mean(ot * ot, axis=0, keepdims=True)
        ot = ot * lax.rsqrt(ms + SUBLN_EPS) * g_ref[...] * (1.0 - lam0)
        o_ref[...] = ot.T.astype(o_ref.dtype)


def diff_attention(qk, vt, pos_col, pos_row, lam_params, subln_g, lam0):
    T = qk.shape[0]
    tq = min(ATT_TQ, T)
    tk = min(ATT_TK, T)
    nq = T // tq
    pairs = [(qi, ki) for qi in range(nq) for ki in range(-(-((qi + 1) * tq) // tk))]
    qt = jnp.asarray(np.array([a for a, _ in pairs], np.int32))
    kt = jnp.asarray(np.array([b for _, b in pairs], np.int32))
    H = DA_HEADS
    kern = functools.partial(_attn_kernel, tq=tq, tk=tk, lam0=lam0)
    grid_spec = pltpu.PrefetchScalarGridSpec(
        num_scalar_prefetch=2,
        grid=(H, len(pairs)),
        in_specs=[
            pl.BlockSpec((tq, LANES), lambda h, p, qt, kt: (qt[p], h)),
            pl.BlockSpec((tk, LANES), lambda h, p, qt, kt: (kt[p], H + h)),
            pl.BlockSpec((DA_V_DIM, tk), lambda h, p, qt, kt: (h, kt[p])),
            pl.BlockSpec((1, tq), lambda h, p, qt, kt: (0, qt[p])),
            pl.BlockSpec((tk, 1), lambda h, p, qt, kt: (kt[p], 0)),
            pl.BlockSpec((4, DA_HEAD_DIM), lambda h, p, qt, kt: (0, 0)),
            pl.BlockSpec((DA_V_DIM, 1), lambda h, p, qt, kt: (0, 0)),
        ],
        out_specs=pl.BlockSpec((tq, DA_V_DIM), lambda h, p, qt, kt: (qt[p], h)),
        scratch_shapes=[
            pltpu.VMEM((2, 1, tq), F32),
            pltpu.VMEM((2, 1, tq), F32),
            pltpu.VMEM((2, DA_V_DIM, tq), F32),
        ],
    )
    return pl.pallas_call(
        kern,
        grid_spec=grid_spec,
        out_shape=jax.ShapeDtypeStruct((T, H * DA_V_DIM), BF16),
        compiler_params=_cparams(("parallel", "arbitrary")),
        name="diff_attention",
    )(qt, kt, qk, qk, vt, pos_row, pos_col, lam_params, subln_g.reshape(DA_V_DIM, 1))


def _even_out_kernel(o_ref, b_ref, c_ref, x_ref, ch_ref, xh_ref, cw_ref, w_ref, h_ref, out_ref, u_ref):
    tm = o_ref.shape[0]
    u_prev = ch_ref[...] * xh_ref[...]
    u_prev = jnp.where(pl.program_id(0) == 0, 0.0, u_prev)
    u_ref[0:SUBLANES, :] = u_prev
    u_ref[SUBLANES:, :] = c_ref[...] * x_ref[...]
    cw = cw_ref[...]
    conv = (u_ref[SUBLANES - 2:SUBLANES - 2 + tm, :] * cw[0:1]
            + u_ref[SUBLANES - 1:SUBLANES - 1 + tm, :] * cw[1:2]
            + u_ref[SUBLANES:, :] * cw[2:3])
    sc = (b_ref[...] * conv).astype(BF16)
    na = o_ref.shape[1]
    acc = jnp.dot(o_ref[...], w_ref[0:na, :], preferred_element_type=F32)
    acc = acc + jnp.dot(sc, w_ref[na:, :], preferred_element_type=F32)
    out_ref[...] = h_ref[...] + acc


def even_out(attn_o, proj, conv_w, w_out, h):
    T = h.shape[0]
    tm = min(ROW_TILE, T)
    W = SC_WIDTH
    cb = (proj.shape[1] - 3 * W) // W
    hb = tm // SUBLANES
    halo = lambda col: pl.BlockSpec((SUBLANES, W), lambda i: (jnp.maximum(i * hb - 1, 0), col))
    return pl.pallas_call(
        _even_out_kernel,
        grid=(T // tm,),
        in_specs=[
            pl.BlockSpec((tm, attn_o.shape[1]), lambda i: (i, 0)),
            pl.BlockSpec((tm, W), lambda i: (i, cb)),
            pl.BlockSpec((tm, W), lambda i: (i, cb + 1)),
            pl.BlockSpec((tm, W), lambda i: (i, cb + 2)),
            halo(cb + 1),
            halo(cb + 2),
            pl.BlockSpec(conv_w.shape, lambda i: (0, 0)),
            pl.BlockSpec(w_out.shape, lambda i: (0, 0)),
            pl.BlockSpec((tm, D_MODEL), lambda i: (i, 0)),
        ],
        out_specs=pl.BlockSpec((tm, D_MODEL), lambda i: (i, 0)),
        out_shape=jax.ShapeDtypeStruct((T, D_MODEL), F32),
        scratch_shapes=[pltpu.VMEM((tm + SUBLANES, W), F32)],
        compiler_params=_cparams(("parallel",)),
        name="even_out",
    )(attn_o, proj, proj, proj, proj, proj, conv_w, w_out, h)


def _ffn_kernel(h_ref, g_ref, w1_ref, w3_ref, w2_ref, o_ref, xn_ref, acc_ref):
    f = pl.program_id(1)

    @pl.when(f == 0)
    def _():
        xn_ref[...] = _rms(h_ref[...], g_ref[...], RMS_EPS).astype(BF16)
        acc_ref[...] = jnp.zeros(acc_ref.shape, F32)

    xn = xn_ref[...]
    a = jnp.dot(xn, w1_ref[...], preferred_element_type=F32)
    b = jnp.dot(xn, w3_ref[...], preferred_element_type=F32)
    hm = (_silu(a) * b).astype(BF16)
    acc_ref[...] += jnp.dot(hm, w2_ref[...], preferred_element_type=F32)

    @pl.when(f == pl.num_programs(1) - 1)
    def _():
        o_ref[...] = h_ref[...] + acc_ref[...]


def ffn(h, g, w1, w3, w2, layer):
    T, D = h.shape
    F = w1.shape[2]
    tm = min(FFN_ROW_TILE, T)
    tf = FF_TILE
    return pl.pallas_call(
        _ffn_kernel,
        grid=(T // tm, F // tf),
        in_specs=[
            pl.BlockSpec((tm, D), lambda i, f: (i, 0)),
            pl.BlockSpec((1, D), lambda i, f: (0, 0)),
            pl.BlockSpec((None, D, tf), lambda i, f: (layer, 0, f)),
            pl.BlockSpec((None, D, tf), lambda i, f: (layer, 0, f)),
            pl.BlockSpec((None, tf, D), lambda i, f: (layer, f, 0)),
        ],
        out_specs=pl.BlockSpec((tm, D), lambda i, f: (i, 0)),
        out_shape=jax.ShapeDtypeStruct((T, D), F32),
        scratch_shapes=[pltpu.VMEM((tm, D), BF16), pltpu.VMEM((tm, D), F32)],
        compiler_params=_cparams(("parallel", "arbitrary")),
        name="ffn",
    )(h, g.reshape(1, D), w1, w3, w2)


def _ple_kernel(h_ref, g_ref, wg_ref, p_ref, wp_ref, gf_ref, o_ref, *, final_norm):
    h = h_ref[...]
    xn = _rms(h, g_ref[...], RMS_EPS).astype(BF16)
    gate = _sigmoid(jnp.dot(xn, wg_ref[...], preferred_element_type=F32))
    emb = jnp.dot(p_ref[...].astype(BF16), wp_ref[...], preferred_element_type=F32)
    out = h + gate * emb
    if final_norm:
        out = _rms(out, gf_ref[...], RMS_EPS)
    o_ref[...] = out


def ple(h, g, wg, p, wp, g_final, final_norm, layer):
    T, D = h.shape
    tm = min(ROW_TILE, T)
    return pl.pallas_call(
        functools.partial(_ple_kernel, final_norm=final_norm),
        grid=(T // tm,),
        in_specs=[
            pl.BlockSpec((tm, D), lambda i: (i, 0)),
            pl.BlockSpec((1, D), lambda i: (0, 0)),
            pl.BlockSpec((None, D, D), lambda i: (layer, 0, 0)),
            pl.BlockSpec((None, tm, PLE_DIM), lambda i: (layer, i, 0)),
            pl.BlockSpec((None, PLE_DIM, D), lambda i: (layer, 0, 0)),
            pl.BlockSpec((1, D), lambda i: (0, 0)),
        ],
        out_specs=pl.BlockSpec((tm, D), lambda i: (i, 0)),
        out_shape=jax.ShapeDtypeStruct((T, D), F32),
        compiler_params=_cparams(("parallel",)),
        name="ple",
    )(h, g.reshape(1, D), wg, p, wp, g_final.reshape(1, D))


def _dot_hi(a, b):
    return jnp.dot(a, b, preferred_element_type=F32, precision=lax.Precision.HIGHEST)


def _gdn_prep_kernel(q_ref, k_ref, v_ref, qh_ref, kh_ref, vh_ref, ba_ref, cw_ref, alog_ref, dtb_ref,
                     qg_ref, kg_ref, w_ref, u_ref, ai_ref, dl_ref,
                     xs_ref, gc_ref, gl_ref, sg_ref, gct_ref):
    i = pl.program_id(0)
    h = pl.program_id(1)
    R = q_ref.shape[0]
    C = DN_CHUNK
    KC = cw_ref.shape[0]

    def conv_silu(x_ref, halo_ref, j):
        xs_ref[0:SUBLANES, :] = jnp.where(i == 0, 0.0, halo_ref[...])
        xs_ref[SUBLANES:, :] = x_ref[...]
        acc = xs_ref[SUBLANES:, :] * cw_ref[KC - 1:KC, j * LANES:(j + 1) * LANES]
        for t in range(1, KC):
            acc = acc + (xs_ref[SUBLANES - t:SUBLANES - t + R, :]
                         * cw_ref[KC - 1 - t:KC - t, j * LANES:(j + 1) * LANES])
        return _silu(acc)

    def l2n(x):
        return x * lax.rsqrt(jnp.sum(x * x, axis=-1, keepdims=True) + 1e-6)

    chunks = range(R // C)
    rows = [slice(c * C, (c + 1) * C) for c in chunks]
    r64 = lax.broadcasted_iota(I32, (C, C), 0)
    c64 = lax.broadcasted_iota(I32, (C, C), 1)
    incl = r64 >= c64
    strict = r64 > c64

    @pl.when(h == 0)
    def _():
        ba = ba_ref[...]
        sp_in = ba + dtb_ref[...]
        softplus = jnp.maximum(sp_in, 0.0) + jnp.log(1.0 + jnp.exp(-jnp.abs(sp_in)))
        g_all = -jnp.exp(alog_ref[...]) * softplus
        g_wide = jnp.concatenate([g_all[r] for r in rows], axis=1)
        gc_wide = _dot_hi(jnp.where(incl, 1.0, 0.0), g_wide)
        for c in chunks:
            gc_c = gc_wide[:, c * LANES:(c + 1) * LANES]
            gc_ref[rows[c], :] = gc_c
            gl_ref[rows[c], :] = jnp.broadcast_to(gc_c[C - 1:C, :], (C, LANES))
        sg_ref[...] = _sigmoid(ba)
        gct_ref[...] = gc_ref[...].T

    q = l2n(conv_silu(q_ref, qh_ref, 0)) * (DN_HEAD_DIM ** -0.5)
    k = l2n(conv_silu(k_ref, kh_ref, 1))
    v = conv_silu(v_ref, vh_ref, 2)

    lane = lax.broadcasted_iota(I32, (R, LANES), 1)

    def pick(ref, l):
        return jnp.sum(jnp.where(lane == l, ref[...], 0.0), axis=-1, keepdims=True)

    beta = pick(sg_ref, h)
    gcol = pick(gc_ref, DN_HEADS + h)
    glast = pick(gl_ref, DN_HEADS + h)
    grow = gct_ref[pl.ds(DN_HEADS + h, 1), :]
    eg = jnp.exp(gcol)
    kb = k * beta
    vb = v * beta
    kbg = kb * eg
    qg_ref[...] = (q * eg).astype(BF16)
    kg_ref[...] = (k * jnp.exp(glast - gcol)).astype(BF16)
    edl = jnp.broadcast_to(jnp.exp(glast), (R, LANES))

    kbf = k.astype(BF16)
    lhs = jnp.concatenate([kb.astype(BF16).reshape(R // C, C, LANES),
                           q.astype(BF16).reshape(R // C, C, LANES)], axis=1)
    rhs = jnp.concatenate([vb, kbg], axis=-1)

    def bdot(a, b):
        return jnp.dot(a.astype(BF16), b.astype(BF16), preferred_element_type=F32)

    decay = [jnp.where(incl, jnp.exp(jnp.where(incl, gcol[r] - grow[:, r], 0.0)), 0.0) for r in rows]
    kq = [lax.dot_general(lhs[c], kbf[rows[c]], (((1,), (1,)), ((), ())), preferred_element_type=F32)
          for c in chunks]
    pw = [-jnp.where(strict, kq[c][:C] * decay[c], 0.0) for c in chunks]
    n = list(pw)
    for _ in range(5):
        pw = [bdot(pw[c], pw[c]) for c in chunks]
        n = [n[c] + pw[c] + bdot(n[c], pw[c]) for c in chunks]
    for c in chunks:
        rc = rhs[rows[c]]
        uw = rc + bdot(n[c], rc)
        u_ref[rows[c], :] = uw[:, :DN_HEAD_DIM]
        w_ref[rows[c], :] = uw[:, DN_HEAD_DIM:].astype(BF16)
        ai_ref[rows[c], :] = (kq[c][C:] * decay[c]).astype(BF16)
        dl_ref[c:c + 1, :] = edl[c * C:c * C + 1, :]


def gdn_prep(proj, ba, conv_w, alog_l, dtb_l):
    T = proj.shape[0]
    R = min(GDN_TILE, T)
    H = DN_HEADS
    nchunk = T // DN_CHUNK
    hb = R // SUBLANES
    col = lambda j: pl.BlockSpec((R, LANES), lambda i, h: (i, j * H + h))
    halo = lambda j: pl.BlockSpec((SUBLANES, LANES), lambda i, h: (jnp.maximum(i * hb - 1, 0), j * H + h))
    KC = conv_w.shape[0]
    cw = conv_w.reshape(KC, 3, H, LANES).transpose(2, 0, 1, 3).reshape(H, KC, 3 * LANES)
    cw_spec = pl.BlockSpec((None, KC, 3 * LANES), lambda i, h: (h, 0, 0))
    row_out = lambda dt: jax.ShapeDtypeStruct((T, DN_WIDTH), dt)
    out_col = pl.BlockSpec((R, LANES), lambda i, h: (i, h))
    return pl.pallas_call(
        _gdn_prep_kernel,
        grid=(T // R, H),
        in_specs=[col(0), col(1), col(2), halo(0), halo(1), halo(2),
                  pl.BlockSpec((R, LANES), lambda i, h: (i, 0)),
                  cw_spec,
                  pl.BlockSpec((1, LANES), lambda i, h: (0, 0)),
                  pl.BlockSpec((1, LANES), lambda i, h: (0, 0))],
        out_specs=[out_col, out_col, out_col, out_col,
                   pl.BlockSpec((None, R, DN_CHUNK), lambda i, h: (h, i, 0)),
                   pl.BlockSpec((None, R // DN_CHUNK, LANES), lambda i, h: (h, i, 0))],
        out_shape=[row_out(BF16), row_out(BF16), row_out(BF16), row_out(F32),
                   jax.ShapeDtypeStruct((H, T, DN_CHUNK), BF16),
                   jax.ShapeDtypeStruct((H, nchunk, LANES), F32)],
        scratch_shapes=[pltpu.VMEM((R + SUBLANES, LANES), F32),
                        pltpu.VMEM((R, LANES), F32), pltpu.VMEM((R, LANES), F32), pltpu.VMEM((R, LANES), F32),
                        pltpu.VMEM((LANES, R), F32)],
        compiler_params=_cparams(("parallel", "arbitrary")),
        name="gdn_prep",
    )(proj, proj, proj, proj, proj, proj, ba, cw, alog_l, dtb_l)


def _gdn_scan_kernel(qg_ref, kg_ref, w_ref, u_ref, ai_ref, dl_ref, o_ref, s_ref):
    @pl.when(pl.program_id(1) == 0)
    def _():
        s_ref[...] = jnp.zeros(s_ref.shape, F32)

    C = DN_CHUNK
    D = DN_HEAD_DIM
    heads = range(s_ref.shape[0])

    def chunk(c, carry):
        rows = pl.ds(pl.multiple_of(c * C, C), C)
        s = [s_ref[hh] for hh in heads]
        sb = [x.astype(BF16) for x in s]
        cols = [slice(hh * D, (hh + 1) * D) for hh in heads]
        r1 = [jnp.dot(jnp.concatenate([w_ref[rows, cols[hh]], qg_ref[rows, cols[hh]]], axis=0), sb[hh],
                      preferred_element_type=F32) for hh in heads]
        vb = [(u_ref[rows, cols[hh]] - r1[hh][:C]).astype(BF16) for hh in heads]
        for hh in heads:
            o_ref[rows, cols[hh]] = r1[hh][C:] + jnp.dot(ai_ref[hh, rows, :], vb[hh], preferred_element_type=F32)
        upd = [lax.dot_general(kg_ref[rows, cols[hh]], vb[hh], (((0,), (0,)), ((), ())),
                               preferred_element_type=F32) for hh in heads]
        for hh in heads:
            s_ref[hh] = s[hh] * dl_ref[hh, pl.ds(c, 1), :] + upd[hh]
        return carry

    lax.fori_loop(0, qg_ref.shape[0] // C, chunk, 0)


def gdn_scan(qg, kg, w, u, ai, dl):
    T = qg.shape[0]
    R = min(GDN_TILE, T)
    H = DN_HEADS
    HB = GDN_SCAN_HEADS
    col = pl.BlockSpec((R, HB * LANES), lambda h, i: (i, h))
    return pl.pallas_call(
        _gdn_scan_kernel,
        grid=(H // HB, T // R),
        in_specs=[col, col, col, col,
                  pl.BlockSpec((HB, R, DN_CHUNK), lambda h, i: (h, i, 0)),
                  pl.BlockSpec((HB, R // DN_CHUNK, LANES), lambda h, i: (h, i, 0))],
        out_specs=col,
        out_shape=jax.ShapeDtypeStruct((T, DN_WIDTH), F32),
        scratch_shapes=[pltpu.VMEM((HB, DN_HEAD_DIM, DN_HEAD_DIM), F32)],
        compiler_params=_cparams(("parallel", "arbitrary")),
        name="gdn_scan",
    )(qg, kg, w, u, ai, dl)


def _odd_out_kernel(o_ref, z_ref, g_ref, w_ref, h_ref, out_ref):
    g = g_ref[...]
    parts = []
    for hd in range(DN_HEADS):
        sl = slice(hd * DN_HEAD_DIM, (hd + 1) * DN_HEAD_DIM)
        parts.append((_rms(o_ref[:, sl], g, RMS_EPS) * _silu(z_ref[:, sl])).astype(BF16))
    y = jnp.concatenate(parts, axis=-1)
    out_ref[...] = h_ref[...] + jnp.dot(y, w_ref[...], preferred_element_type=F32)


def odd_out(o, proj, onorm_g, w_out, h):
    T = h.shape[0]
    tm = min(ROW_TILE, T)
    zb = 3 * DN_WIDTH // DN_WIDTH
    return pl.pallas_call(
        _odd_out_kernel,
        grid=(T // tm,),
        in_specs=[
            pl.BlockSpec((tm, DN_WIDTH), lambda i: (i, 0)),
            pl.BlockSpec((tm, DN_WIDTH), lambda i: (i, zb)),
            pl.BlockSpec((1, DN_HEAD_DIM), lambda i: (0, 0)),
            pl.BlockSpec(w_out.shape, lambda i: (0, 0)),
            pl.BlockSpec((tm, D_MODEL), lambda i: (i, 0)),
        ],
        out_specs=pl.BlockSpec((tm, D_MODEL), lambda i: (i, 0)),
        out_shape=jax.ShapeDtypeStruct((T, D_MODEL), F32),
        compiler_params=_cparams(("parallel",)),
        name="odd_out",
    )(o, proj, onorm_g.reshape(1, DN_HEAD_DIM), w_out, h)


def _route_kernel(h_ref, g_ref, wr_ref, ri_ref, gate_ref, cnt_ref, carry_ref):
    i = pl.program_id(0)

    @pl.when(i == 0)
    def _():
        carry_ref[...] = jnp.zeros(carry_ref.shape, F32)

    xn = _rms(h_ref[...], g_ref[...], RMS_EPS)
    logits = jnp.dot(xn.astype(BF16), wr_ref[...].astype(BF16), preferred_element_type=F32)
    tm = logits.shape[0]
    lane = lax.broadcasted_iota(I32, logits.shape, 1)
    logits = jnp.where(lane < N_EXPERTS, logits, NEG_BIG)
    lane_f = lane.astype(F32)
    m1 = jnp.max(logits, axis=-1, keepdims=True)
    i1 = jnp.min(jnp.where(logits == m1, lane_f, float(LANES)), axis=-1, keepdims=True)
    rest = jnp.where(lane_f == i1, NEG_BIG, logits)
    m2 = jnp.max(rest, axis=-1, keepdims=True)
    i2 = jnp.min(jnp.where(rest == m2, lane_f, float(LANES)), axis=-1, keepdims=True)
    e = jnp.exp(m2 - m1)
    g1 = 1.0 / (1.0 + e)
    g2 = e / (1.0 + e)
    oh1 = lane_f == i1
    oh2 = lane_f == i2
    i1 = i1.astype(I32)
    i2 = i2.astype(I32)
    oh = jnp.where(oh1 | oh2, 1.0, 0.0)
    ri_ = lax.broadcasted_iota(I32, (tm, tm), 0)
    ci_ = lax.broadcasted_iota(I32, (tm, tm), 1)
    below = jnp.where(ri_ > ci_, 1.0, 0.0).astype(BF16)
    ex = jnp.dot(below, oh.astype(BF16), preferred_element_type=F32) + carry_ref[0:1, :]
    r1 = jnp.sum(jnp.where(oh1, ex, 0.0), axis=-1, keepdims=True).astype(I32)
    r2 = jnp.sum(jnp.where(oh2, ex, 0.0), axis=-1, keepdims=True).astype(I32)
    ri_ref[...] = jnp.where(lane == 0, i1, jnp.where(lane == 1, i2, jnp.where(lane == 2, r1, r2)))
    gate_ref[...] = jnp.where(lane == 0, g1, g2)
    carry_ref[...] = carry_ref[...] + jnp.sum(oh, axis=0, keepdims=True)
    cnt_ref[...] = carry_ref[...]


def route(h, g, wr):
    T, D = h.shape
    tm = min(ROW_TILE, T)
    return pl.pallas_call(
        _route_kernel,
        grid=(T // tm,),
        in_specs=[
            pl.BlockSpec((tm, D), lambda i: (i, 0)),
            pl.BlockSpec((1, D), lambda i: (0, 0)),
            pl.BlockSpec((D, LANES), lambda i: (0, 0)),
        ],
        out_specs=[
            pl.BlockSpec((tm, LANES), lambda i: (i, 0)),
            pl.BlockSpec((tm, LANES), lambda i: (i, 0)),
            pl.BlockSpec((SUBLANES, LANES), lambda i: (0, 0)),
        ],
        out_shape=[
            jax.ShapeDtypeStruct((T, LANES), I32),
            jax.ShapeDtypeStruct((T, LANES), F32),
            jax.ShapeDtypeStruct((SUBLANES, LANES), F32),
        ],
        scratch_shapes=[pltpu.VMEM((SUBLANES, LANES), F32)],
        compiler_params=_cparams(("arbitrary",)),
        name="route",
    )(h, g.reshape(1, D), wr)


def _row_copy(src_ref, s, dst_ref, d, sem):
    return pltpu.make_async_copy(src_ref.at[pl.ds(s, 1), :], dst_ref.at[pl.ds(d, 1), :], sem)


def _dispatch_kernel(dest_ref, h_ref, g_ref, xz_ref, xs_ref, xn_ref, sem):
    del xz_ref
    tm = h_ref.shape[0]
    xn_ref[...] = _rms(h_ref[...], g_ref[...], RMS_EPS)

    def issue(r, carry):
        _row_copy(xn_ref, r, xs_ref, dest_ref[2 * r], sem).start(priority=0)
        _row_copy(xn_ref, r, xs_ref, dest_ref[2 * r + 1], sem).start(priority=1)
        return carry

    lax.fori_loop(0, tm, issue, 0, unroll=ROW_DMA_UNROLL)

    def drain(r, carry):
        _row_copy(xn_ref, 0, xs_ref, 0, sem).wait()
        _row_copy(xn_ref, 0, xs_ref, 0, sem).wait()
        return carry

    lax.fori_loop(0, tm, drain, 0, unroll=ROW_DMA_UNROLL)


def dispatch(h, g, dest_flat, n_slots):
    T, D = h.shape
    tm = min(ROW_TILE, T)
    zeros = jnp.zeros((n_slots, D), F32)
    return pl.pallas_call(
        _dispatch_kernel,
        grid=(T // tm,),
        in_specs=[
            pl.BlockSpec((2 * tm,), lambda i: (i,), memory_space=pltpu.SMEM),
            pl.BlockSpec((tm, D), lambda i: (i, 0)),
            pl.BlockSpec((1, D), lambda i: (0, 0)),
            pl.BlockSpec(memory_space=pl.ANY),
        ],
        out_specs=pl.BlockSpec(memory_space=pl.ANY),
        out_shape=jax.ShapeDtypeStruct((n_slots, D), F32),
        scratch_shapes=[pltpu.VMEM((tm, D), F32), pltpu.SemaphoreType.DMA],
        input_output_aliases={3: 0},
        compiler_params=_cparams(("arbitrary",)),
        name="moe_dispatch",
    )(dest_flat, h, g.reshape(1, D), zeros)


def _gmm_kernel(te_ref, na_ref, x_ref, w1_ref, w3_ref, w2_ref, y_ref, xb_ref, acc_ref):
    b = pl.program_id(0)
    f = pl.program_id(1)

    @pl.when(b < na_ref[0])
    def _():
        @pl.when(f == 0)
        def _():
            xb_ref[...] = x_ref[...].astype(BF16)
            acc_ref[...] = jnp.zeros(acc_ref.shape, F32)

        xb = xb_ref[...]
        a = jnp.dot(xb, w1_ref[...], preferred_element_type=F32)
        c = jnp.dot(xb, w3_ref[...], preferred_element_type=F32)
        hm = (_silu(a) * c).astype(BF16)
        acc_ref[...] += jnp.dot(hm, w2_ref[...], preferred_element_type=F32)

        @pl.when(f == pl.num_programs(1) - 1)
        def _():
            y_ref[...] = acc_ref[...]

    @pl.when((b >= na_ref[0]) & (f == pl.num_programs(1) - 1))
    def _():
        y_ref[...] = jnp.zeros(y_ref.shape, F32)


def gmm(xs, w1, w3, w2, tile_e, n_active, layer):
    P, D = xs.shape
    G = MOE_TILE
    F = w1.shape[3]
    tf = MOE_FF_TILE
    nf = F // tf

    def row_idx(b, f, te, na):
        return (jnp.minimum(b, na[0] - 1), 0)

    def f_idx(b, f, na):
        return jnp.where(b < na[0], f, nf - 1)

    grid_spec = pltpu.PrefetchScalarGridSpec(
        num_scalar_prefetch=2,
        grid=(P // G, nf),
        in_specs=[
            pl.BlockSpec((G, D), row_idx),
            pl.BlockSpec((None, None, D, tf), lambda b, f, te, na: (layer, te[b], 0, f_idx(b, f, na))),
            pl.BlockSpec((None, None, D, tf), lambda b, f, te, na: (layer, te[b], 0, f_idx(b, f, na))),
            pl.BlockSpec((None, None, tf, D), lambda b, f, te, na: (layer, te[b], f_idx(b, f, na), 0)),
        ],
        out_specs=pl.BlockSpec((G, D), lambda b, f, te, na: (b, 0)),
        scratch_shapes=[pltpu.VMEM((G, D), BF16), pltpu.VMEM((G, D), F32)],
    )
    return pl.pallas_call(
        _gmm_kernel,
        grid_spec=grid_spec,
        out_shape=jax.ShapeDtypeStruct((P, D), F32),
        compiler_params=_cparams(("arbitrary", "arbitrary")),
        name="moe_gmm",
    )(tile_e, n_active, xs, w1, w3, w2)


def _combine_kernel(dest_ref, h_ref, gate_ref, y_ref, o_ref, ya_ref, yb_ref, sem):
    tm = h_ref.shape[0]

    def issue(r, carry):
        _row_copy(y_ref, dest_ref[2 * r], ya_ref, r, sem).start(priority=0)
        _row_copy(y_ref, dest_ref[2 * r + 1], yb_ref, r, sem).start(priority=1)
        return carry

    lax.fori_loop(0, tm, issue, 0, unroll=ROW_DMA_UNROLL)

    def drain(r, carry):
        _row_copy(y_ref, 0, ya_ref, 0, sem).wait()
        _row_copy(y_ref, 0, yb_ref, 0, sem).wait()
        return carry

    lax.fori_loop(0, tm, drain, 0, unroll=ROW_DMA_UNROLL)
    gt = gate_ref[...]
    o_ref[...] = h_ref[...] + gt[:, 0:1] * ya_ref[...] + gt[:, 1:2] * yb_ref[...]


def combine(h, gates, y, dest_flat):
    T, D = h.shape
    tm = min(ROW_TILE, T)
    return pl.pallas_call(
        _combine_kernel,
        grid=(T // tm,),
        in_specs=[
            pl.BlockSpec((2 * tm,), lambda i: (i,), memory_space=pltpu.SMEM),
            pl.BlockSpec((tm, D), lambda i: (i, 0)),
            pl.BlockSpec((tm, LANES), lambda i: (i, 0)),
            pl.BlockSpec(memory_space=pl.ANY),
        ],
        out_specs=pl.BlockSpec((tm, D), lambda i: (i, 0)),
        out_shape=jax.ShapeDtypeStruct((T, D), F32),
        scratch_shapes=[pltpu.VMEM((tm, D), F32), pltpu.VMEM((tm, D), F32), pltpu.SemaphoreType.DMA],
        compiler_params=_cparams(("arbitrary",)),
        name="moe_combine",
    )(dest_flat, h, gates, y)


def _lambda_init(layer_idx):
    return 0.8 - 0.6 * math.exp(-0.3 * layer_idx)


def _even_layer(h, pos_col, pos_row, invf, ln_mix, w_in, w_out, lam_params, subln_g, conv_w, layer_idx):
    proj = norm_matmul(h, ln_mix, w_in.astype(BF16), tn=PROJ_COL_TILE)
    qk, vt = rope_qkv(proj, pos_col, invf)
    o = diff_attention(qk, vt, pos_col, pos_row, lam_params, subln_g, _lambda_init(layer_idx))
    return even_out(o, proj, conv_w, w_out.astype(BF16), h)


def _odd_mixer(h, ln_mix, w_in, conv_w, a_log, dt_bias, onorm_g, w_out):
    main_w = 4 * DN_WIDTH
    w_main = w_in[:, :main_w].astype(BF16)
    w_ba = jnp.pad(w_in[:, main_w:], ((0, 0), (0, LANES - 2 * DN_HEADS))).astype(BF16)
    proj = norm_matmul(h, ln_mix, w_main, tn=PROJ_COL_TILE)
    ba = norm_matmul(h, ln_mix, w_ba, tn=LANES)
    pad8 = lambda v: jnp.pad(v.astype(F32), (DN_HEADS, LANES - 2 * DN_HEADS)).reshape(1, LANES)
    qg, kg, w, u, ai, dl = gdn_prep(proj, ba, conv_w, pad8(a_log), pad8(dt_bias))
    o = gdn_scan(qg, kg, w, u, ai, dl)
    return odd_out(o, proj, onorm_g, w_out.astype(BF16), h)


def _moe(h, ln_ffn, w_router, w1, w3, w2, layer):
    T = h.shape[0]
    G = MOE_TILE
    wr = jnp.pad(w_router, ((0, 0), (0, LANES - N_EXPERTS)))
    ri, gates, cnt = route(h, ln_ffn, wr)
    counts = cnt[0, :N_EXPERTS].astype(I32)
    padded = ((counts + G - 1) // G) * G
    pends = jnp.cumsum(padded)
    pstarts = pends - padded
    dest = (jnp.take(pstarts, ri[:, 0:2]) + ri[:, 2:4]).reshape(-1)
    n_tiles = (2 * T) // G + N_EXPERTS
    tile_e = jnp.minimum(
        jnp.searchsorted(pends, jnp.arange(n_tiles, dtype=I32) * G, side="right"), N_EXPERTS - 1).astype(I32)
    n_active = (pends[-1:] // G).astype(I32)
    xs = dispatch(h, ln_ffn, dest, n_tiles * G)
    y = gmm(xs, w1, w3, w2, tile_e, n_active, layer)
    return combine(h, gates, y, dest)


def kernel(x, p, positions, ln_mix, ln_ffn, ln_ple, ln_final, w_in_even, w_out_even, lam_q1, lam_k1, lam_q2, lam_k2, subln_gain, conv_w_short, w_in_odd, conv_w_qkv, a_log, dt_bias, onorm_gain, w_out_odd, w1_dense, w3_dense, w2_dense, w_router, w1_moe, w3_moe, w2_moe, w_ple_gate, w_ple_proj):
    B, S, D = x.shape
    T = B * S
    depth = p.shape[0]
    h = x.reshape(T, D)
    pos_col = positions.reshape(T, 1).astype(I32)
    pos_row = positions.reshape(1, T).astype(I32)
    inv_freq = ROPE_THETA ** (-jnp.arange(0, ROT_DIM, 2, dtype=F32) / ROT_DIM)
    invf = jnp.tile(inv_freq, LANES // (ROT_DIM // 2)).reshape(1, LANES)
    w1d, w3d, w2d = (w.astype(BF16) for w in (w1_dense, w3_dense, w2_dense))
    w1m, w3m, w2m = (w.astype(BF16) for w in (w1_moe, w3_moe, w2_moe))
    wpg, wpp = w_ple_gate.astype(BF16), w_ple_proj.astype(BF16)
    p_rows = p.reshape(depth, T, PLE_DIM)
    for i in range(depth):
        j = i // 2
        if i % 2 == 0:
            lam_params = jnp.stack([lam_q1[j], lam_k1[j], lam_q2[j], lam_k2[j]]).astype(F32)
            h = _even_layer(h, pos_col, pos_row, invf, ln_mix[i], w_in_even[j], w_out_even[j], lam_params,
                            subln_gain[j], conv_w_short[j], i)
            h = ffn(h, ln_ffn[i], w1d, w3d, w2d, j)
        else:
            h = _odd_mixer(h, ln_mix[i], w_in_odd[j], conv_w_qkv[j], a_log[j], dt_bias[j], onorm_gain[j],
                           w_out_odd[j])
            h = _moe(h, ln_ffn[i], w_router[j], w1m, w3m, w2m, j)
        h = ple(h, ln_ple[i], wpg, p_rows, wpp, ln_final, final_norm=(i == depth - 1), layer=i)
    return h.reshape(B, S, D)
```

```python
import functools
import math

import jax
import jax.numpy as jnp
import numpy as np
from jax import lax
from jax.experimental import pallas as pl
from jax.experimental.pallas import tpu as pltpu

F32 = jnp.float32
BF16 = jnp.bfloat16
I32 = jnp.int32

D_MODEL = 1024
DEPTH = 4
RMS_EPS = 1e-6
DA_HEADS = 4
DA_HEAD_DIM = 64
DA_V_DIM = 2 * DA_HEAD_DIM
ROPE_THETA = 500000.0
ROT_DIM = DA_HEAD_DIM // 4
SUBLN_EPS = 1e-5
SC_WIDTH = 512
DN_HEADS = 8
DN_HEAD_DIM = 128
DN_WIDTH = DN_HEADS * DN_HEAD_DIM
DN_CHUNK = 64
D_FF = 3584
N_EXPERTS = 8
PLE_DIM = 256

LANES = 128
SUBLANES = 8
HALO = 16
VMEM_LIMIT = 52 * 1024 * 1024

NEG_BIG = -1e30

ROW_TILE = 512
PROJ_ROW_TILE = 1024
PROJ_COL_TILE = 1024
FFN_ROW_TILE = 1024
FF_TILE = 512
MOE_FF_TILE = 1792
ATT_TQ = 1024
ATT_TK = 1024
ATT_EXP_ROWS = 128
ATT_TQ_SPLIT = 256
GDN_TILE = 1024
GDN_SCAN_HEADS = 4
MOE_TILE = 512
ROW_DMA_UNROLL = 8


def _cparams(sem):
    return pltpu.CompilerParams(dimension_semantics=sem, vmem_limit_bytes=VMEM_LIMIT)


def _rms(x, g, eps):
    ms = jnp.mean(x * x, axis=-1, keepdims=True)
    return x * lax.rsqrt(ms + eps) * g


def _silu(x):
    return x * (1.0 / (1.0 + jnp.exp(-x)))


def _sigmoid(x):
    return 1.0 / (1.0 + jnp.exp(-x))


def _norm_matmul_kernel(x_ref, g_ref, w_ref, o_ref, xn_ref):
    @pl.when(pl.program_id(1) == 0)
    def _():
        xn_ref[...] = _rms(x_ref[...], g_ref[...], RMS_EPS).astype(BF16)

    o_ref[...] = jnp.dot(xn_ref[...], w_ref[...], preferred_element_type=F32).astype(o_ref.dtype)


def norm_matmul(x, g, w, tn, out_dtype=F32):
    T, D = x.shape
    N = w.shape[1]
    tm = min(PROJ_ROW_TILE, T)
    return pl.pallas_call(
        _norm_matmul_kernel,
        grid=(T // tm, N // tn),
        in_specs=[
            pl.BlockSpec((tm, D), lambda i, j: (i, 0)),
            pl.BlockSpec((1, D), lambda i, j: (0, 0)),
            pl.BlockSpec((D, tn), lambda i, j: (0, j)),
        ],
        out_specs=pl.BlockSpec((tm, tn), lambda i, j: (i, j)),
        out_shape=jax.ShapeDtypeStruct((T, N), out_dtype),
        scratch_shapes=[pltpu.VMEM((tm, D), BF16)],
        compiler_params=_cparams(("parallel", "arbitrary")),
        name="norm_matmul",
    )(x, g.reshape(1, D), w)


def _rope_kernel(proj_ref, pos_ref, invf_ref, o_ref, vt_ref):
    pos = pos_ref[...].astype(F32)
    ang = pos * invf_ref[...]
    cos_t = jnp.cos(ang)
    sin_t = jnp.sin(ang)
    d = lax.broadcasted_iota(I32, ang.shape, 1) & (DA_HEAD_DIM - 1)
    half = ROT_DIM // 2
    c_mul = jnp.where(d < ROT_DIM, cos_t, 1.0)
    s_mul = jnp.where(d < half, -sin_t, jnp.where(d < ROT_DIM, sin_t, 0.0))
    n_qk = 2 * DA_HEADS * DA_HEAD_DIM // LANES
    for c in range(2 * n_qk):
        x = proj_ref[:, c * LANES:(c + 1) * LANES].astype(F32)
        swapped = jnp.where(d < half, pltpu.roll(x, LANES - half, 1), pltpu.roll(x, half, 1))
        r = x * c_mul + swapped * s_mul
        if c < n_qk:
            r = r * (DA_HEAD_DIM ** -0.5 * math.log2(math.e))
        o_ref[:, c * LANES:(c + 1) * LANES] = r.astype(BF16)
    v0 = 2 * n_qk * LANES
    for hd in range(DA_HEADS):
        vt_ref[hd * DA_V_DIM:(hd + 1) * DA_V_DIM, :] = (
            proj_ref[:, v0 + hd * DA_V_DIM:v0 + (hd + 1) * DA_V_DIM].astype(F32).T.astype(BF16))


def rope_qkv(proj, positions_col, invf):
    T = proj.shape[0]
    WQK = 2 * 2 * DA_HEADS * DA_HEAD_DIM
    WV = DA_HEADS * DA_V_DIM
    tm = min(ROW_TILE, T)
    return pl.pallas_call(
        _rope_kernel,
        grid=(T // tm,),
        in_specs=[
            pl.BlockSpec((tm, WQK + WV), lambda i: (i, 0)),
            pl.BlockSpec((tm, 1), lambda i: (i, 0)),
            pl.BlockSpec((1, LANES), lambda i: (0, 0)),
        ],
        out_specs=[pl.BlockSpec((tm, WQK), lambda i: (i, 0)),
                   pl.BlockSpec((WV, tm), lambda i: (0, i))],
        out_shape=[jax.ShapeDtypeStruct((T, WQK), BF16), jax.ShapeDtypeStruct((WV, T), BF16)],
        compiler_params=_cparams(("parallel",)),
        name="rope_qkv",
    )(proj, positions_col, invf)


def _attn_kernel(qt_ref, kt_ref, q_ref, k_ref, v_ref, pq_ref, pk_ref, lam_ref, g_ref, o_ref,
                 m_ref, l_ref, acc_ref, *, tq, tk, lam0):
    p = pl.program_id(1)
    qi = qt_ref[p]
    ki = kt_ref[p]

    @pl.when(ki == 0)
    def _():
        m_ref[...] = jnp.full(m_ref.shape, NEG_BIG, F32)
        l_ref[...] = jnp.zeros(l_ref.shape, F32)
        acc_ref[...] = jnp.zeros(acc_ref.shape, F32)

    tqs = min(tq, ATT_TQ_SPLIT)
    items = [(s, j) for s in range(2) for j in range(tq // tqs)]

    def step(masked):
        q = q_ref[...]
        k = k_ref[...]
        v = v_ref[...]
        st = {}
        for s, j in items:
            cols = slice(j * tqs, (j + 1) * tqs)
            qs = q[j * tqs:(j + 1) * tqs, s * DA_HEAD_DIM:(s + 1) * DA_HEAD_DIM]
            ks = k[:, s * DA_HEAD_DIM:(s + 1) * DA_HEAD_DIM]
            sc = lax.dot_general(ks, qs, (((1,), (1,)), ((), ())), preferred_element_type=F32)
            if masked:
                sc = jnp.where(pk_ref[...] <= pq_ref[:, cols], sc, NEG_BIG)
            st[s, j] = sc
        for s, j in items:
            cols = slice(j * tqs, (j + 1) * tqs)
            sc = st[s, j]
            m_prev = m_ref[s, :, cols]
            m_new = jnp.maximum(m_prev, jnp.max(sc, axis=0, keepdims=True))
            alpha = jnp.exp2(m_prev - m_new)
            psum = jnp.zeros((1, tqs), F32)
            pchunks = []
            for r0 in range(0, tk, ATT_EXP_ROWS):
                pc = jnp.exp2(sc[r0:r0 + ATT_EXP_ROWS] - m_new)
                psum = psum + jnp.sum(pc, axis=0, keepdims=True)
                pchunks.append(pc.astype(BF16))
            pb = jnp.concatenate(pchunks, axis=0)
            l_ref[s, :, cols] = alpha * l_ref[s, :, cols] + psum
            pv = jnp.dot(v, pb, preferred_element_type=F32)
            acc_ref[s, :, cols] = alpha * acc_ref[s, :, cols] + pv
            m_ref[s, :, cols] = m_new

    crosses = (ki + 1) * tk - 1 > qi * tq

    @pl.when(crosses)
    def _():
        step(True)

    @pl.when(jnp.logical_not(crosses))
    def _():
        step(False)

    @pl.when((ki + 1) * tk >= (qi + 1) * tq)
    def _():
        lm = lam_ref[...]
        s1 = jnp.sum(lm[0:1] * lm[1:2], axis=-1, keepdims=True)
        s2 = jnp.sum(lm[2:3] * lm[3:4], axis=-1, keepdims=True)
        lam = jnp.exp(s1) - jnp.exp(s2) + lam0
        ot = acc_ref[0] * (1.0 / l_ref[0]) - lam * (acc_ref[1] * (1.0 / l_ref[1]))
        ms = jnp.mean(ot * ot, axis=0, keepdims=True)
        ot = ot * lax.rsqrt(ms + SUBLN_EPS) * g_ref[...] * (1.0 - lam0)
        o_ref[...] = ot.T.astype(o_ref.dtype)


def diff_attention(qk, vt, pos_col, pos_row, lam_params, subln_g, lam0):
    T = qk.shape[0]
    tq = min(ATT_TQ, T)
    tk = min(ATT_TK, T)
    nq = T // tq
    pairs = [(qi, ki) for qi in range(nq) for ki in range(-(-((qi + 1) * tq) // tk))]
    qt = jnp.asarray(np.array([a for a, _ in pairs], np.int32))
    kt = jnp.asarray(np.array([b for _, b in pairs], np.int32))
    H = DA_HEADS
    kern = functools.partial(_attn_kernel, tq=tq, tk=tk, lam0=lam0)
    grid_spec = pltpu.PrefetchScalarGridSpec(
        num_scalar_prefetch=2,
        grid=(H, len(pairs)),
        in_specs=[
            pl.BlockSpec((tq, LANES), lambda h, p, qt, kt: (qt[p], h)),
            pl.BlockSpec((tk, LANES), lambda h, p, qt, kt: (kt[p], H + h)),
            pl.BlockSpec((DA_V_DIM, tk), lambda h, p, qt, kt: (h, kt[p])),
            pl.BlockSpec((1, tq), lambda h, p, qt, kt: (0, qt[p])),
            pl.BlockSpec((tk, 1), lambda h, p, qt, kt: (kt[p], 0)),
            pl.BlockSpec((4, DA_HEAD_DIM), lambda h, p, qt, kt: (0, 0)),
            pl.BlockSpec((DA_V_DIM, 1), lambda h, p, qt, kt: (0, 0)),
        ],
        out_specs=pl.BlockSpec((tq, DA_V_DIM), lambda h, p, qt, kt: (qt[p], h)),
        scratch_shapes=[
            pltpu.VMEM((2, 1, tq), F32),
            pltpu.VMEM((2, 1, tq), F32),
            pltpu.VMEM((2, DA_V_DIM, tq), F32),
        ],
    )
    return pl.pallas_call(
        kern,
        grid_spec=grid_spec,
        out_shape=jax.ShapeDtypeStruct((T, H * DA_V_DIM), BF16),
        compiler_params=_cparams(("parallel", "arbitrary")),
        name="diff_attention",
    )(qt, kt, qk, qk, vt, pos_row, pos_col, lam_params, subln_g.reshape(DA_V_DIM, 1))


def _even_out_kernel(o_ref, b_ref, c_ref, x_ref, ch_ref, xh_ref, cw_ref, w_ref, h_ref, out_ref, u_ref):
    tm = o_ref.shape[0]
    u_prev = ch_ref[...].astype(F32) * xh_ref[...].astype(F32)
    u_prev = jnp.where(pl.program_id(0) == 0, 0.0, u_prev)
    u_ref[0:HALO, :] = u_prev
    u_ref[HALO:, :] = c_ref[...].astype(F32) * x_ref[...].astype(F32)
    cw = cw_ref[...]
    conv = (u_ref[HALO - 2:HALO - 2 + tm, :] * cw[0:1]
            + u_ref[HALO - 1:HALO - 1 + tm, :] * cw[1:2]
            + u_ref[HALO:, :] * cw[2:3])
    sc = (b_ref[...].astype(F32) * conv).astype(BF16)
    na = o_ref.shape[1]
    acc = jnp.dot(o_ref[...], w_ref[0:na, :], preferred_element_type=F32)
    acc = acc + jnp.dot(sc, w_ref[na:, :], preferred_element_type=F32)
    out_ref[...] = h_ref[...] + acc


def even_out(attn_o, proj, conv_w, w_out, h):
    T = h.shape[0]
    tm = min(ROW_TILE, T)
    W = SC_WIDTH
    cb = (proj.shape[1] - 3 * W) // W
    hb = tm // HALO
    halo = lambda col: pl.BlockSpec((HALO, W), lambda i: (jnp.maximum(i * hb - 1, 0), col))
    return pl.pallas_call(
        _even_out_kernel,
        grid=(T // tm,),
        in_specs=[
            pl.BlockSpec((tm, attn_o.shape[1]), lambda i: (i, 0)),
            pl.BlockSpec((tm, W), lambda i: (i, cb)),
            pl.BlockSpec((tm, W), lambda i: (i, cb + 1)),
            pl.BlockSpec((tm, W), lambda i: (i, cb + 2)),
            halo(cb + 1),
            halo(cb + 2),
            pl.BlockSpec(conv_w.shape, lambda i: (0, 0)),
            pl.BlockSpec(w_out.shape, lambda i: (0, 0)),
            pl.BlockSpec((tm, D_MODEL), lambda i: (i, 0)),
        ],
        out_specs=pl.BlockSpec((tm, D_MODEL), lambda i: (i, 0)),
        out_shape=jax.ShapeDtypeStruct((T, D_MODEL), F32),
        scratch_shapes=[pltpu.VMEM((tm + HALO, W), F32)],
        compiler_params=_cparams(("parallel",)),
        name="even_out",
    )(attn_o, proj, proj, proj, proj, proj, conv_w, w_out, h)


def _ffn_kernel(h_ref, g_ref, w1_ref, w3_ref, w2_ref, o_ref, xn_ref, acc_ref):
    f = pl.program_id(1)

    @pl.when(f == 0)
    def _():
        xn_ref[...] = _rms(h_ref[...], g_ref[...], RMS_EPS).astype(BF16)
        acc_ref[...] = jnp.zeros(acc_ref.shape, F32)

    xn = xn_ref[...]
    a = jnp.dot(xn, w1_ref[...], preferred_element_type=F32)
    b = jnp.dot(xn, w3_ref[...], preferred_element_type=F32)
    hm = (_silu(a) * b).astype(BF16)
    acc_ref[...] += jnp.dot(hm, w2_ref[...], preferred_element_type=F32)

    @pl.when(f == pl.num_programs(1) - 1)
    def _():
        o_ref[...] = h_ref[...] + acc_ref[...]


def ffn(h, g, w1, w3, w2, layer):
    T, D = h.shape
    F = w1.shape[2]
    tm = min(FFN_ROW_TILE, T)
    tf = FF_TILE
    return pl.pallas_call(
        _ffn_kernel,
        grid=(T // tm, F // tf),
        in_specs=[
            pl.BlockSpec((tm, D), lambda i, f: (i, 0)),
            pl.BlockSpec((1, D), lambda i, f: (0, 0)),
            pl.BlockSpec((None, D, tf), lambda i, f: (layer, 0, f)),
            pl.BlockSpec((None, D, tf), lambda i, f: (layer, 0, f)),
            pl.BlockSpec((None, tf, D), lambda i, f: (layer, f, 0)),
        ],
        out_specs=pl.BlockSpec((tm, D), lambda i, f: (i, 0)),
        out_shape=jax.ShapeDtypeStruct((T, D), F32),
        scratch_shapes=[pltpu.VMEM((tm, D), BF16), pltpu.VMEM((tm, D), F32)],
        compiler_params=_cparams(("parallel", "arbitrary")),
        name="ffn",
    )(h, g.reshape(1, D), w1, w3, w2)


def _ple_kernel(h_ref, g_ref, wg_ref, p_ref, wp_ref, gf_ref, o_ref, *, final_norm):
    h = h_ref[...]
    xn = _rms(h, g_ref[...], RMS_EPS).astype(BF16)
    gate = _sigmoid(jnp.dot(xn, wg_ref[...], preferred_element_type=F32))
    emb = jnp.dot(p_ref[...].astype(BF16), wp_ref[...], preferred_element_type=F32)
    out = h + gate * emb
    if final_norm:
        out = _rms(out, gf_ref[...], RMS_EPS)
    o_ref[...] = out


def ple(h, g, wg, p, wp, g_final, final_norm, layer):
    T, D = h.shape
    tm = min(ROW_TILE, T)
    return pl.pallas_call(
        functools.partial(_ple_kernel, final_norm=final_norm),
        grid=(T // tm,),
        in_specs=[
            pl.BlockSpec((tm, D), lambda i: (i, 0)),
            pl.BlockSpec((1, D), lambda i: (0, 0)),
            pl.BlockSpec((None, D, D), lambda i: (layer, 0, 0)),
            pl.BlockSpec((None, tm, PLE_DIM), lambda i: (layer, i, 0)),
            pl.BlockSpec((None, PLE_DIM, D), lambda i: (layer, 0, 0)),
            pl.BlockSpec((1, D), lambda i: (0, 0)),
        ],
        out_specs=pl.BlockSpec((tm, D), lambda i: (i, 0)),
        out_shape=jax.ShapeDtypeStruct((T, D), F32),
        compiler_params=_cparams(("parallel",)),
        name="ple",
    )(h, g.reshape(1, D), wg, p, wp, g_final.reshape(1, D))


def _dot_hi(a, b):
    return jnp.dot(a, b, preferred_element_type=F32, precision=lax.Precision.HIGHEST)


def _gdn_prep_kernel(q_ref, k_ref, v_ref, qh_ref, kh_ref, vh_ref, ba_ref, cw_ref, alog_ref, dtb_ref,
                     qg_ref, kg_ref, w_ref, u_ref, ai_ref, dl_ref,
                     xs_ref, gc_ref, gl_ref, sg_ref, gct_ref):
    i = pl.program_id(0)
    h = pl.program_id(1)
    R = q_ref.shape[0]
    C = DN_CHUNK
    KC = cw_ref.shape[0]

    def conv_silu(x_ref, halo_ref, j):
        xs_ref[0:HALO, :] = jnp.where(i == 0, 0.0, halo_ref[...].astype(F32))
        xs_ref[HALO:, :] = x_ref[...].astype(F32)
        acc = xs_ref[HALO:, :] * cw_ref[KC - 1:KC, j * LANES:(j + 1) * LANES]
        for t in range(1, KC):
            acc = acc + (xs_ref[HALO - t:HALO - t + R, :]
                         * cw_ref[KC - 1 - t:KC - t, j * LANES:(j + 1) * LANES])
        return _silu(acc)

    def l2n(x):
        return x * lax.rsqrt(jnp.sum(x * x, axis=-1, keepdims=True) + 1e-6)

    chunks = range(R // C)
    rows = [slice(c * C, (c + 1) * C) for c in chunks]
    r64 = lax.broadcasted_iota(I32, (C, C), 0)
    c64 = lax.broadcasted_iota(I32, (C, C), 1)
    incl = r64 >= c64
    strict = r64 > c64

    @pl.when(h == 0)
    def _():
        ba = ba_ref[...]
        sp_in = ba + dtb_ref[...]
        softplus = jnp.maximum(sp_in, 0.0) + jnp.log(1.0 + jnp.exp(-jnp.abs(sp_in)))
        g_all = -jnp.exp(alog_ref[...]) * softplus
        g_wide = jnp.concatenate([g_all[r] for r in rows], axis=1)
        gc_wide = _dot_hi(jnp.where(incl, 1.0, 0.0), g_wide)
        for c in chunks:
            gc_c = gc_wide[:, c * LANES:(c + 1) * LANES]
            gc_ref[rows[c], :] = gc_c
            gl_ref[rows[c], :] = jnp.broadcast_to(gc_c[C - 1:C, :], (C, LANES))
        sg_ref[...] = _sigmoid(ba)
        gct_ref[...] = gc_ref[...].T

    q = l2n(conv_silu(q_ref, qh_ref, 0)) * (DN_HEAD_DIM ** -0.5)
    k = l2n(conv_silu(k_ref, kh_ref, 1))
    v = conv_silu(v_ref, vh_ref, 2)

    lane = lax.broadcasted_iota(I32, (R, LANES), 1)

    def pick(ref, l):
        return jnp.sum(jnp.where(lane == l, ref[...], 0.0), axis=-1, keepdims=True)

    beta = pick(sg_ref, h)
    gcol = pick(gc_ref, DN_HEADS + h)
    glast = pick(gl_ref, DN_HEADS + h)
    grow = gct_ref[pl.ds(DN_HEADS + h, 1), :]
    eg = jnp.exp(gcol)
    kb = k * beta
    vb = v * beta
    kbg = kb * eg
    qg_ref[...] = (q * eg).astype(BF16)
    kg_ref[...] = (k * jnp.exp(glast - gcol)).astype(BF16)
    edl = jnp.broadcast_to(jnp.exp(glast), (R, LANES))

    kbf = k.astype(BF16)
    lhs = jnp.concatenate([kb.astype(BF16).reshape(R // C, C, LANES),
                           q.astype(BF16).reshape(R // C, C, LANES)], axis=1)
    rhs = jnp.concatenate([vb, kbg], axis=-1)

    def bdot(a, b):
        return jnp.dot(a.astype(BF16), b.astype(BF16), preferred_element_type=F32)

    decay = [jnp.where(incl, jnp.exp(jnp.where(incl, gcol[r] - grow[:, r], 0.0)), 0.0) for r in rows]
    kq = [lax.dot_general(lhs[c], kbf[rows[c]], (((1,), (1,)), ((), ())), preferred_element_type=F32)
          for c in chunks]
    pw = [-jnp.where(strict, kq[c][:C] * decay[c], 0.0) for c in chunks]
    n = list(pw)
    for _ in range(5):
        pw = [bdot(pw[c], pw[c]) for c in chunks]
        n = [n[c] + pw[c] + bdot(n[c], pw[c]) for c in chunks]
    for c in chunks:
        rc = rhs[rows[c]]
        uw = rc + bdot(n[c], rc)
        u_ref[rows[c], :] = uw[:, :DN_HEAD_DIM]
        w_ref[rows[c], :] = uw[:, DN_HEAD_DIM:].astype(BF16)
        ai_ref[rows[c], :] = (kq[c][C:] * decay[c]).astype(BF16)
        dl_ref[c:c + 1, :] = edl[c * C:c * C + 1, :]


def gdn_prep(proj, ba, conv_w, alog_l, dtb_l):
    T = proj.shape[0]
    R = min(GDN_TILE, T)
    H = DN_HEADS
    nchunk = T // DN_CHUNK
    hb = R // HALO
    col = lambda j: pl.BlockSpec((R, LANES), lambda i, h: (i, j * H + h))
    halo = lambda j: pl.BlockSpec((HALO, LANES), lambda i, h: (jnp.maximum(i * hb - 1, 0), j * H + h))
    KC = conv_w.shape[0]
    cw = conv_w.reshape(KC, 3, H, LANES).transpose(2, 0, 1, 3).reshape(H, KC, 3 * LANES)
    cw_spec = pl.BlockSpec((None, KC, 3 * LANES), lambda i, h: (h, 0, 0))
    row_out = lambda dt: jax.ShapeDtypeStruct((T, DN_WIDTH), dt)
    out_col = pl.BlockSpec((R, LANES), lambda i, h: (i, h))
    return pl.pallas_call(
        _gdn_prep_kernel,
        grid=(T // R, H),
        in_specs=[col(0), col(1), col(2), halo(0), halo(1), halo(2),
                  pl.BlockSpec((R, LANES), lambda i, h: (i, 0)),
                  cw_spec,
                  pl.BlockSpec((1, LANES), lambda i, h: (0, 0)),
                  pl.BlockSpec((1, LANES), lambda i, h: (0, 0))],
        out_specs=[out_col, out_col, out_col, out_col,
                   pl.BlockSpec((None, R, DN_CHUNK), lambda i, h: (h, i, 0)),
                   pl.BlockSpec((None, R // DN_CHUNK, LANES), lambda i, h: (h, i, 0))],
        out_shape=[row_out(BF16), row_out(BF16), row_out(BF16), row_out(F32),
                   jax.ShapeDtypeStruct((H, T, DN_CHUNK), BF16),
                   jax.ShapeDtypeStruct((H, nchunk, LANES), F32)],
        scratch_shapes=[pltpu.VMEM((R + HALO, LANES), F32),
                        pltpu.VMEM((R, LANES), F32), pltpu.VMEM((R, LANES), F32), pltpu.VMEM((R, LANES), F32),
                        pltpu.VMEM((LANES, R), F32)],
        compiler_params=_cparams(("parallel", "arbitrary")),
        name="gdn_prep",
    )(proj, proj, proj, proj, proj, proj, ba, cw, alog_l, dtb_l)


def _gdn_scan_kernel(qg_ref, kg_ref, w_ref, u_ref, ai_ref, dl_ref, o_ref, s_ref):
    @pl.when(pl.program_id(1) == 0)
    def _():
        s_ref[...] = jnp.zeros(s_ref.shape, F32)

    C = DN_CHUNK
    D = DN_HEAD_DIM
    heads = range(s_ref.shape[0])

    def chunk(c, carry):
        rows = pl.ds(pl.multiple_of(c * C, C), C)
        s = [s_ref[hh] for hh in heads]
        sb = [x.astype(BF16) for x in s]
        cols = [slice(hh * D, (hh + 1) * D) for hh in heads]
        r1 = [jnp.dot(jnp.concatenate([w_ref[rows, cols[hh]], qg_ref[rows, cols[hh]]], axis=0), sb[hh],
                      preferred_element_type=F32) for hh in heads]
        vb = [(u_ref[rows, cols[hh]] - r1[hh][:C]).astype(BF16) for hh in heads]
        for hh in heads:
            o_ref[rows, cols[hh]] = r1[hh][C:] + jnp.dot(ai_ref[hh, rows, :], vb[hh], preferred_element_type=F32)
        upd = [lax.dot_general(kg_ref[rows, cols[hh]], vb[hh], (((0,), (0,)), ((), ())),
                               preferred_element_type=F32) for hh in heads]
        for hh in heads:
            s_ref[hh] = s[hh] * dl_ref[hh, pl.ds(c, 1), :] + upd[hh]
        return carry

    lax.fori_loop(0, qg_ref.shape[0] // C, chunk, 0)


def gdn_scan(qg, kg, w, u, ai, dl):
    T = qg.shape[0]
    R = min(GDN_TILE, T)
    H = DN_HEADS
    HB = GDN_SCAN_HEADS
    col = pl.BlockSpec((R, HB * LANES), lambda h, i: (i, h))
    return pl.pallas_call(
        _gdn_scan_kernel,
        grid=(H // HB, T // R),
        in_specs=[col, col, col, col,
                  pl.BlockSpec((HB, R, DN_CHUNK), lambda h, i: (h, i, 0)),
                  pl.BlockSpec((HB, R // DN_CHUNK, LANES), lambda h, i: (h, i, 0))],
        out_specs=col,
        out_shape=jax.ShapeDtypeStruct((T, DN_WIDTH), F32),
        scratch_shapes=[pltpu.VMEM((HB, DN_HEAD_DIM, DN_HEAD_DIM), F32)],
        compiler_params=_cparams(("parallel", "arbitrary")),
        name="gdn_scan",
    )(qg, kg, w, u, ai, dl)


def _odd_out_kernel(o_ref, z_ref, g_ref, w_ref, h_ref, out_ref):
    g = g_ref[...]
    parts = []
    for hd in range(DN_HEADS):
        sl = slice(hd * DN_HEAD_DIM, (hd + 1) * DN_HEAD_DIM)
        parts.append((_rms(o_ref[:, sl], g, RMS_EPS) * _silu(z_ref[:, sl].astype(F32))).astype(BF16))
    y = jnp.concatenate(parts, axis=-1)
    out_ref[...] = h_ref[...] + jnp.dot(y, w_ref[...], preferred_element_type=F32)


def odd_out(o, proj, onorm_g, w_out, h):
    T = h.shape[0]
    tm = min(ROW_TILE, T)
    zb = 3 * DN_WIDTH // DN_WIDTH
    return pl.pallas_call(
        _odd_out_kernel,
        grid=(T // tm,),
        in_specs=[
            pl.BlockSpec((tm, DN_WIDTH), lambda i: (i, 0)),
            pl.BlockSpec((tm, DN_WIDTH), lambda i: (i, zb)),
            pl.BlockSpec((1, DN_HEAD_DIM), lambda i: (0, 0)),
            pl.BlockSpec(w_out.shape, lambda i: (0, 0)),
            pl.BlockSpec((tm, D_MODEL), lambda i: (i, 0)),
        ],
        out_specs=pl.BlockSpec((tm, D_MODEL), lambda i: (i, 0)),
        out_shape=jax.ShapeDtypeStruct((T, D_MODEL), F32),
        compiler_params=_cparams(("parallel",)),
        name="odd_out",
    )(o, proj, onorm_g.reshape(1, DN_HEAD_DIM), w_out, h)


def _route_kernel(h_ref, g_ref, wr_ref, ri_ref, gate_ref, cnt_ref, carry_ref):
    i = pl.program_id(0)

    @pl.when(i == 0)
    def _():
        carry_ref[...] = jnp.zeros(carry_ref.shape, F32)

    xn = _rms(h_ref[...], g_ref[...], RMS_EPS)
    logits = jnp.dot(xn.astype(BF16), wr_ref[...].astype(BF16), preferred_element_type=F32)
    tm = logits.shape[0]
    lane = lax.broadcasted_iota(I32, logits.shape, 1)
    logits = jnp.where(lane < N_EXPERTS, logits, NEG_BIG)
    lane_f = lane.astype(F32)
    m1 = jnp.max(logits, axis=-1, keepdims=True)
    i1 = jnp.min(jnp.where(logits == m1, lane_f, float(LANES)), axis=-1, keepdims=True)
    rest = jnp.where(lane_f == i1, NEG_BIG, logits)
    m2 = jnp.max(rest, axis=-1, keepdims=True)
    i2 = jnp.min(jnp.where(rest == m2, lane_f, float(LANES)), axis=-1, keepdims=True)
    e = jnp.exp(m2 - m1)
    g1 = 1.0 / (1.0 + e)
    g2 = e / (1.0 + e)
    oh1 = lane_f == i1
    oh2 = lane_f == i2
    i1 = i1.astype(I32)
    i2 = i2.astype(I32)
    oh = jnp.where(oh1 | oh2, 1.0, 0.0)
    ri_ = lax.broadcasted_iota(I32, (tm, tm), 0)
    ci_ = lax.broadcasted_iota(I32, (tm, tm), 1)
    below = jnp.where(ri_ > ci_, 1.0, 0.0).astype(BF16)
    ex = jnp.dot(below, oh.astype(BF16), preferred_element_type=F32) + carry_ref[0:1, :]
    r1 = jnp.sum(jnp.where(oh1, ex, 0.0), axis=-1, keepdims=True).astype(I32)
    r2 = jnp.sum(jnp.where(oh2, ex, 0.0), axis=-1, keepdims=True).astype(I32)
    ri_ref[...] = jnp.where(lane == 0, i1, jnp.where(lane == 1, i2, jnp.where(lane == 2, r1, r2)))
    gate_ref[...] = jnp.where(lane == 0, g1, g2)
    carry_ref[...] = carry_ref[...] + jnp.sum(oh, axis=0, keepdims=True)
    cnt_ref[...] = carry_ref[...]


def route(h, g, wr):
    T, D = h.shape
    tm = min(ROW_TILE, T)
    return pl.pallas_call(
        _route_kernel,
        grid=(T // tm,),
        in_specs=[
            pl.BlockSpec((tm, D), lambda i: (i, 0)),
            pl.BlockSpec((1, D), lambda i: (0, 0)),
            pl.BlockSpec((D, LANES), lambda i: (0, 0)),
        ],
        out_specs=[
            pl.BlockSpec((tm, LANES), lambda i: (i, 0)),
            pl.BlockSpec((tm, LANES), lambda i: (i, 0)),
            pl.BlockSpec((SUBLANES, LANES), lambda i: (0, 0)),
        ],
        out_shape=[
            jax.ShapeDtypeStruct((T, LANES), I32),
            jax.ShapeDtypeStruct((T, LANES), F32),
            jax.ShapeDtypeStruct((SUBLANES, LANES), F32),
        ],
        scratch_shapes=[pltpu.VMEM((SUBLANES, LANES), F32)],
        compiler_params=_cparams(("arbitrary",)),
        name="route",
    )(h, g.reshape(1, D), wr)


def _row_copy(src_ref, s, dst_ref, d, sem):
    return pltpu.make_async_copy(src_ref.at[pl.ds(s, 1), :], dst_ref.at[pl.ds(d, 1), :], sem)


def _dispatch_kernel(dest_ref, h_ref, g_ref, xz_ref, xs_ref, xn_ref, sem):
    del xz_ref
    tm = h_ref.shape[0]
    xn_ref[...] = _rms(h_ref[...], g_ref[...], RMS_EPS)

    def issue(r, carry):
        _row_copy(xn_ref, r, xs_ref, dest_ref[2 * r], sem).start(priority=0)
        _row_copy(xn_ref, r, xs_ref, dest_ref[2 * r + 1], sem).start(priority=1)
        return carry

    lax.fori_loop(0, tm, issue, 0, unroll=ROW_DMA_UNROLL)

    def drain(r, carry):
        _row_copy(xn_ref, 0, xs_ref, 0, sem).wait()
        _row_copy(xn_ref, 0, xs_ref, 0, sem).wait()
        return carry

    lax.fori_loop(0, tm, drain, 0, unroll=ROW_DMA_UNROLL)


def dispatch(h, g, dest_flat, n_slots):
    T, D = h.shape
    tm = min(ROW_TILE, T)
    zeros = jnp.zeros((n_slots, D), F32)
    return pl.pallas_call(
        _dispatch_kernel,
        grid=(T // tm,),
        in_specs=[
            pl.BlockSpec((2 * tm,), lambda i: (i,), memory_space=pltpu.SMEM),
            pl.BlockSpec((tm, D), lambda i: (i, 0)),
            pl.BlockSpec((1, D), lambda i: (0, 0)),
            pl.BlockSpec(memory_space=pl.ANY),
        ],
        out_specs=pl.BlockSpec(memory_space=pl.ANY),
        out_shape=jax.ShapeDtypeStruct((n_slots, D), F32),
        scratch_shapes=[pltpu.VMEM((tm, D), F32), pltpu.SemaphoreType.DMA],
        input_output_aliases={3: 0},
        compiler_params=_cparams(("arbitrary",)),
        name="moe_dispatch",
    )(dest_flat, h, g.reshape(1, D), zeros)


def _gmm_kernel(te_ref, na_ref, x_ref, w1_ref, w3_ref, w2_ref, y_ref, xb_ref, acc_ref):
    b = pl.program_id(0)
    f = pl.program_id(1)

    @pl.when(b < na_ref[0])
    def _():
        @pl.when(f == 0)
        def _():
            xb_ref[...] = x_ref[...].astype(BF16)
            acc_ref[...] = jnp.zeros(acc_ref.shape, F32)

        xb = xb_ref[...]
        a = jnp.dot(xb, w1_ref[...], preferred_element_type=F32)
        c = jnp.dot(xb, w3_ref[...], preferred_element_type=F32)
        hm = (_silu(a) * c).astype(BF16)
        acc_ref[...] += jnp.dot(hm, w2_ref[...], preferred_element_type=F32)

        @pl.when(f == pl.num_programs(1) - 1)
        def _():
            y_ref[...] = acc_ref[...]

    @pl.when((b >= na_ref[0]) & (f == pl.num_programs(1) - 1))
    def _():
        y_ref[...] = jnp.zeros(y_ref.shape, F32)


def gmm(xs, w1, w3, w2, tile_e, n_active, layer):
    P, D = xs.shape
    G = MOE_TILE
    F = w1.shape[3]
    tf = MOE_FF_TILE
    nf = F // tf

    def row_idx(b, f, te, na):
        return (jnp.minimum(b, na[0] - 1), 0)

    def f_idx(b, f, na):
        return jnp.where(b < na[0], f, nf - 1)

    grid_spec = pltpu.PrefetchScalarGridSpec(
        num_scalar_prefetch=2,
        grid=(P // G, nf),
        in_specs=[
            pl.BlockSpec((G, D), row_idx),
            pl.BlockSpec((None, None, D, tf), lambda b, f, te, na: (layer, te[b], 0, f_idx(b, f, na))),
            pl.BlockSpec((None, None, D, tf), lambda b, f, te, na: (layer, te[b], 0, f_idx(b, f, na))),
            pl.BlockSpec((None, None, tf, D), lambda b, f, te, na: (layer, te[b], f_idx(b, f, na), 0)),
        ],
        out_specs=pl.BlockSpec((G, D), lambda b, f, te, na: (b, 0)),
        scratch_shapes=[pltpu.VMEM((G, D), BF16), pltpu.VMEM((G, D), F32)],
    )
    return pl.pallas_call(
        _gmm_kernel,
        grid_spec=grid_spec,
        out_shape=jax.ShapeDtypeStruct((P, D), F32),
        compiler_params=_cparams(("arbitrary", "arbitrary")),
        name="moe_gmm",
    )(tile_e, n_active, xs, w1, w3, w2)


def _combine_kernel(dest_ref, h_ref, gate_ref, y_ref, o_ref, ya_ref, yb_ref, sem):
    tm = h_ref.shape[0]

    def issue(r, carry):
        _row_copy(y_ref, dest_ref[2 * r], ya_ref, r, sem).start(priority=0)
        _row_copy(y_ref, dest_ref[2 * r + 1], yb_ref, r, sem).start(priority=1)
        return carry

    lax.fori_loop(0, tm, issue, 0, unroll=ROW_DMA_UNROLL)

    def drain(r, carry):
        _row_copy(y_ref, 0, ya_ref, 0, sem).wait()
        _row_copy(y_ref, 0, yb_ref, 0, sem).wait()
        return carry

    lax.fori_loop(0, tm, drain, 0, unroll=ROW_DMA_UNROLL)
    gt = gate_ref[...]
    o_ref[...] = h_ref[...] + gt[:, 0:1] * ya_ref[...] + gt[:, 1:2] * yb_ref[...]


def combine(h, gates, y, dest_flat):
    T, D = h.shape
    tm = min(ROW_TILE, T)
    return pl.pallas_call(
        _combine_kernel,
        grid=(T // tm,),
        in_specs=[
            pl.BlockSpec((2 * tm,), lambda i: (i,), memory_space=pltpu.SMEM),
            pl.BlockSpec((tm, D), lambda i: (i, 0)),
            pl.BlockSpec((tm, LANES), lambda i: (i, 0)),
            pl.BlockSpec(memory_space=pl.ANY),
        ],
        out_specs=pl.BlockSpec((tm, D), lambda i: (i, 0)),
        out_shape=jax.ShapeDtypeStruct((T, D), F32),
        scratch_shapes=[pltpu.VMEM((tm, D), F32), pltpu.VMEM((tm, D), F32), pltpu.SemaphoreType.DMA],
        compiler_params=_cparams(("arbitrary",)),
        name="moe_combine",
    )(dest_flat, h, gates, y)


def _lambda_init(layer_idx):
    return 0.8 - 0.6 * math.exp(-0.3 * layer_idx)


def _even_layer(h, pos_col, pos_row, invf, ln_mix, w_in, w_out, lam_params, subln_g, conv_w, layer_idx):
    proj = norm_matmul(h, ln_mix, w_in.astype(BF16), tn=PROJ_COL_TILE, out_dtype=BF16)
    qk, vt = rope_qkv(proj, pos_col, invf)
    o = diff_attention(qk, vt, pos_col, pos_row, lam_params, subln_g, _lambda_init(layer_idx))
    return even_out(o, proj, conv_w, w_out.astype(BF16), h)


def _odd_mixer(h, ln_mix, w_in, conv_w, a_log, dt_bias, onorm_g, w_out):
    main_w = 4 * DN_WIDTH
    w_main = w_in[:, :main_w].astype(BF16)
    w_ba = jnp.pad(w_in[:, main_w:], ((0, 0), (0, LANES - 2 * DN_HEADS))).astype(BF16)
    proj = norm_matmul(h, ln_mix, w_main, tn=PROJ_COL_TILE, out_dtype=BF16)
    ba = norm_matmul(h, ln_mix, w_ba, tn=LANES)
    pad8 = lambda v: jnp.pad(v.astype(F32), (DN_HEADS, LANES - 2 * DN_HEADS)).reshape(1, LANES)
    qg, kg, w, u, ai, dl = gdn_prep(proj, ba, conv_w, pad8(a_log), pad8(dt_bias))
    o = gdn_scan(qg, kg, w, u, ai, dl)
    return odd_out(o, proj, onorm_g, w_out.astype(BF16), h)


def _moe(h, ln_ffn, w_router, w1, w3, w2, layer):
    T = h.shape[0]
    G = MOE_TILE
    wr = jnp.pad(w_router, ((0, 0), (0, LANES - N_EXPERTS)))
    ri, gates, cnt = route(h, ln_ffn, wr)
    counts = cnt[0, :N_EXPERTS].astype(I32)
    padded = ((counts + G - 1) // G) * G
    pends = jnp.cumsum(padded)
    pstarts = pends - padded
    dest = (jnp.take(pstarts, ri[:, 0:2]) + ri[:, 2:4]).reshape(-1)
    n_tiles = (2 * T) // G + N_EXPERTS
    tile_start = jnp.arange(n_tiles, dtype=I32) * G
    tile_e = jnp.minimum(jnp.sum(pends[None, :] <= tile_start[:, None], axis=1), N_EXPERTS - 1).astype(I32)
    n_active = (pends[-1:] // G).astype(I32)
    xs = dispatch(h, ln_ffn, dest, n_tiles * G)
    y = gmm(xs, w1, w3, w2, tile_e, n_active, layer)
    return combine(h, gates, y, dest)


def kernel(x, p, positions, ln_mix, ln_ffn, ln_ple, ln_final, w_in_even, w_out_even, lam_q1, lam_k1, lam_q2, lam_k2, subln_gain, conv_w_short, w_in_odd, conv_w_qkv, a_log, dt_bias, onorm_gain, w_out_odd, w1_dense, w3_dense, w2_dense, w_router, w1_moe, w3_moe, w2_moe, w_ple_gate, w_ple_proj):
    B, S, D = x.shape
    T = B * S
    depth = p.shape[0]
    h = x.reshape(T, D)
    pos_col = positions.reshape(T, 1).astype(I32)
    pos_row = positions.reshape(1, T).astype(I32)
    inv_freq = ROPE_THETA ** (-jnp.arange(0, ROT_DIM, 2, dtype=F32) / ROT_DIM)
    invf = jnp.tile(inv_freq, LANES // (ROT_DIM // 2)).reshape(1, LANES)
    w1d, w3d, w2d = (w.astype(BF16) for w in (w1_dense, w3_dense, w2_dense))
    w1m, w3m, w2m = (w.astype(BF16) for w in (w1_moe, w3_moe, w2_moe))
    wpg, wpp = w_ple_gate.astype(BF16), w_ple_proj.astype(BF16)
    p_rows = p.reshape(depth, T, PLE_DIM)
    for i in range(depth):
        j = i // 2
        if i % 2 == 0:
            lam_params = jnp.stack([lam_q1[j], lam_k1[j], lam_q2[j], lam_k2[j]]).astype(F32)
            h = _even_layer(h, pos_col, pos_row, invf, ln_mix[i], w_in_even[j], w_out_even[j], lam_params,
                            subln_gain[j], conv_w_short[j], i)
            h = ffn(h, ln_ffn[i], w1d, w3d, w2d, j)
        else:
            h = _odd_mixer(h, ln_mix[i], w_in_odd[j], conv_w_qkv[j], a_log[j], dt_bias[j], onorm_gain[j],
                           w_out_odd[j])
            h = _moe(h, ln_ffn[i], w_router[j], w1m, w3m, w2m, j)
        h = ple(h, ln_ple[i], wpg, p_rows, wpp, ln_final, final_norm=(i == depth - 1), layer=i)
    return h.reshape(B, S, D)
```

```python
import functools
import math

import jax
import jax.numpy as jnp
import numpy as np
from jax import lax
from jax.experimental import pallas as pl
from jax.experimental.pallas import tpu as pltpu

F32 = jnp.float32
BF16 = jnp.bfloat16
I32 = jnp.int32

D_MODEL = 1024
DEPTH = 4
RMS_EPS = 1e-6
DA_HEADS = 4
DA_HEAD_DIM = 64
DA_V_DIM = 2 * DA_HEAD_DIM
ROPE_THETA = 500000.0
ROT_DIM = DA_HEAD_DIM // 4
SUBLN_EPS = 1e-5
SC_WIDTH = 512
DN_HEADS = 8
DN_HEAD_DIM = 128
DN_WIDTH = DN_HEADS * DN_HEAD_DIM
DN_CHUNK = 64
D_FF = 3584
N_EXPERTS = 8
PLE_DIM = 256

LANES = 128
SUBLANES = 8
HALO = 16
VMEM_LIMIT = 52 * 1024 * 1024

NEG_BIG = -1e30

ROW_TILE = 512
PROJ_ROW_TILE = 1024
PROJ_COL_TILE = 1024
FFN_ROW_TILE = 1024
FF_TILE = 512
MOE_FF_TILE = 1792
ATT_TQ = 1024
ATT_TK = 1024
ATT_EXP_ROWS = 128
ATT_TQ_SPLIT = 256
GDN_TILE = 1024
GDN_PREP_HEADS = 2
GDN_SCAN_HEADS = 4
MOE_TILE = 512
ROW_DMA_UNROLL = 8


def _cparams(sem):
    return pltpu.CompilerParams(dimension_semantics=sem, vmem_limit_bytes=VMEM_LIMIT)


def _rms(x, g, eps):
    ms = jnp.mean(x * x, axis=-1, keepdims=True)
    return x * lax.rsqrt(ms + eps) * g


def _silu(x):
    return x * (1.0 / (1.0 + jnp.exp(-x)))


def _sigmoid(x):
    return 1.0 / (1.0 + jnp.exp(-x))


def _norm_matmul_kernel(x_ref, g_ref, w_ref, o_ref, xn_ref):
    @pl.when(pl.program_id(1) == 0)
    def _():
        xn_ref[...] = _rms(x_ref[...], g_ref[...], RMS_EPS).astype(BF16)

    o_ref[...] = jnp.dot(xn_ref[...], w_ref[...], preferred_element_type=F32).astype(o_ref.dtype)


def norm_matmul(x, g, w, tn, out_dtype=F32):
    T, D = x.shape
    N = w.shape[1]
    tm = min(PROJ_ROW_TILE, T)
    return pl.pallas_call(
        _norm_matmul_kernel,
        grid=(T // tm, N // tn),
        in_specs=[
            pl.BlockSpec((tm, D), lambda i, j: (i, 0)),
            pl.BlockSpec((1, D), lambda i, j: (0, 0)),
            pl.BlockSpec((D, tn), lambda i, j: (0, j)),
        ],
        out_specs=pl.BlockSpec((tm, tn), lambda i, j: (i, j)),
        out_shape=jax.ShapeDtypeStruct((T, N), out_dtype),
        scratch_shapes=[pltpu.VMEM((tm, D), BF16)],
        compiler_params=_cparams(("parallel", "arbitrary")),
        name="norm_matmul",
    )(x, g.reshape(1, D), w)


def _rope_kernel(proj_ref, pos_ref, invf_ref, o_ref, vt_ref):
    pos = pos_ref[...].astype(F32)
    ang = pos * invf_ref[...]
    cos_t = jnp.cos(ang)
    sin_t = jnp.sin(ang)
    d = lax.broadcasted_iota(I32, ang.shape, 1) & (DA_HEAD_DIM - 1)
    half = ROT_DIM // 2
    c_mul = jnp.where(d < ROT_DIM, cos_t, 1.0)
    s_mul = jnp.where(d < half, -sin_t, jnp.where(d < ROT_DIM, sin_t, 0.0))
    n_qk = 2 * DA_HEADS * DA_HEAD_DIM // LANES
    for c in range(2 * n_qk):
        x = proj_ref[:, c * LANES:(c + 1) * LANES].astype(F32)
        swapped = jnp.where(d < half, pltpu.roll(x, LANES - half, 1), pltpu.roll(x, half, 1))
        r = x * c_mul + swapped * s_mul
        if c < n_qk:
            r = r * (DA_HEAD_DIM ** -0.5 * math.log2(math.e))
        o_ref[:, c * LANES:(c + 1) * LANES] = r.astype(BF16)
    v0 = 2 * n_qk * LANES
    for hd in range(DA_HEADS):
        vt_ref[hd * DA_V_DIM:(hd + 1) * DA_V_DIM, :] = (
            proj_ref[:, v0 + hd * DA_V_DIM:v0 + (hd + 1) * DA_V_DIM].astype(F32).T.astype(BF16))


def rope_qkv(proj, positions_col, invf):
    T = proj.shape[0]
    WQK = 2 * 2 * DA_HEADS * DA_HEAD_DIM
    WV = DA_HEADS * DA_V_DIM
    tm = min(ROW_TILE, T)
    return pl.pallas_call(
        _rope_kernel,
        grid=(T // tm,),
        in_specs=[
            pl.BlockSpec((tm, WQK + WV), lambda i: (i, 0)),
            pl.BlockSpec((tm, 1), lambda i: (i, 0)),
            pl.BlockSpec((1, LANES), lambda i: (0, 0)),
        ],
        out_specs=[pl.BlockSpec((tm, WQK), lambda i: (i, 0)),
                   pl.BlockSpec((WV, tm), lambda i: (0, i))],
        out_shape=[jax.ShapeDtypeStruct((T, WQK), BF16), jax.ShapeDtypeStruct((WV, T), BF16)],
        compiler_params=_cparams(("parallel",)),
        name="rope_qkv",
    )(proj, positions_col, invf)


def _attn_kernel(qt_ref, kt_ref, q_ref, k_ref, v_ref, pq_ref, pk_ref, lam_ref, g_ref, o_ref,
                 m_ref, l_ref, acc_ref, *, tq, tk, lam0):
    p = pl.program_id(1)
    qi = qt_ref[p]
    ki = kt_ref[p]

    @pl.when(ki == 0)
    def _():
        m_ref[...] = jnp.full(m_ref.shape, NEG_BIG, F32)
        l_ref[...] = jnp.zeros(l_ref.shape, F32)
        acc_ref[...] = jnp.zeros(acc_ref.shape, F32)

    tqs = min(tq, ATT_TQ_SPLIT)
    items = [(s, j) for s in range(2) for j in range(tq // tqs)]

    def step(masked):
        q = q_ref[...]
        k = k_ref[...]
        v = v_ref[...]
        def nkeys(j):
            return (j + 1) * tqs if (masked and tq == tk) else tk

        st = {}
        for s, j in items:
            cols = slice(j * tqs, (j + 1) * tqs)
            qs = q[j * tqs:(j + 1) * tqs, s * DA_HEAD_DIM:(s + 1) * DA_HEAD_DIM]
            ks = k[:nkeys(j), s * DA_HEAD_DIM:(s + 1) * DA_HEAD_DIM]
            sc = lax.dot_general(ks, qs, (((1,), (1,)), ((), ())), preferred_element_type=F32)
            if masked:
                sc = jnp.where(pk_ref[0:nkeys(j), :] <= pq_ref[:, cols], sc, NEG_BIG)
            st[s, j] = sc
        for s, j in items:
            cols = slice(j * tqs, (j + 1) * tqs)
            sc = st[s, j]
            nk = nkeys(j)
            m_prev = m_ref[s, :, cols]
            m_new = jnp.maximum(m_prev, jnp.max(sc, axis=0, keepdims=True))
            alpha = jnp.exp2(m_prev - m_new)
            psum = jnp.zeros((1, tqs), F32)
            pchunks = []
            for r0 in range(0, nk, ATT_EXP_ROWS):
                pc = jnp.exp2(sc[r0:r0 + ATT_EXP_ROWS] - m_new)
                psum = psum + jnp.sum(pc, axis=0, keepdims=True)
                pchunks.append(pc.astype(BF16))
            pb = jnp.concatenate(pchunks, axis=0)
            l_ref[s, :, cols] = alpha * l_ref[s, :, cols] + psum
            pv = jnp.dot(v[:, :nk], pb, preferred_element_type=F32)
            acc_ref[s, :, cols] = alpha * acc_ref[s, :, cols] + pv
            m_ref[s, :, cols] = m_new

    crosses = (ki + 1) * tk - 1 > qi * tq

    @pl.when(crosses)
    def _():
        step(True)

    @pl.when(jnp.logical_not(crosses))
    def _():
        step(False)

    @pl.when((ki + 1) * tk >= (qi + 1) * tq)
    def _():
        lm = lam_ref[...]
        s1 = jnp.sum(lm[0:1] * lm[1:2], axis=-1, keepdims=True)
        s2 = jnp.sum(lm[2:3] * lm[3:4], axis=-1, keepdims=True)
        lam = jnp.exp(s1) - jnp.exp(s2) + lam0
        ot = acc_ref[0] * (1.0 / l_ref[0]) - lam * (acc_ref[1] * (1.0 / l_ref[1]))
        ms = jnp.mean(ot * ot, axis=0, keepdims=True)
        ot = ot * lax.rsqrt(ms + SUBLN_EPS) * g_ref[...] * (1.0 - lam0)
        o_ref[...] = ot.T.astype(o_ref.dtype)


def diff_attention(qk, vt, pos_col, pos_row, lam_params, subln_g, lam0):
    T = qk.shape[0]
    tq = min(ATT_TQ, T)
    tk = min(ATT_TK, T)
    nq = T // tq
    pairs = [(qi, ki) for qi in range(nq) for ki in range(-(-((qi + 1) * tq) // tk))]
    qt = jnp.asarray(np.array([a for a, _ in pairs], np.int32))
    kt = jnp.asarray(np.array([b for _, b in pairs], np.int32))
    H = DA_HEADS
    kern = functools.partial(_attn_kernel, tq=tq, tk=tk, lam0=lam0)
    grid_spec = pltpu.PrefetchScalarGridSpec(
        num_scalar_prefetch=2,
        grid=(H, len(pairs)),
        in_specs=[
            pl.BlockSpec((tq, LANES), lambda h, p, qt, kt: (qt[p], h)),
            pl.BlockSpec((tk, LANES), lambda h, p, qt, kt: (kt[p], H + h)),
            pl.BlockSpec((DA_V_DIM, tk), lambda h, p, qt, kt: (h, kt[p])),
            pl.BlockSpec((1, tq), lambda h, p, qt, kt: (0, qt[p])),
            pl.BlockSpec((tk, 1), lambda h, p, qt, kt: (kt[p], 0)),
            pl.BlockSpec((4, DA_HEAD_DIM), lambda h, p, qt, kt: (0, 0)),
            pl.BlockSpec((DA_V_DIM, 1), lambda h, p, qt, kt: (0, 0)),
        ],
        out_specs=pl.BlockSpec((tq, DA_V_DIM), lambda h, p, qt, kt: (qt[p], h)),
        scratch_shapes=[
            pltpu.VMEM((2, 1, tq), F32),
            pltpu.VMEM((2, 1, tq), F32),
            pltpu.VMEM((2, DA_V_DIM, tq), F32),
        ],
    )
    return pl.pallas_call(
        kern,
        grid_spec=grid_spec,
        out_shape=jax.ShapeDtypeStruct((T, H * DA_V_DIM), BF16),
        compiler_params=_cparams(("parallel", "arbitrary")),
        name="diff_attention",
    )(qt, kt, qk, qk, vt, pos_row, pos_col, lam_params, subln_g.reshape(DA_V_DIM, 1))


def _even_out_kernel(o_ref, b_ref, c_ref, x_ref, ch_ref, xh_ref, cw_ref, w_ref, h_ref, out_ref, u_ref):
    tm = o_ref.shape[0]
    u_prev = ch_ref[...].astype(F32) * xh_ref[...].astype(F32)
    u_prev = jnp.where(pl.program_id(0) == 0, 0.0, u_prev)
    u_ref[0:HALO, :] = u_prev
    u_ref[HALO:, :] = c_ref[...].astype(F32) * x_ref[...].astype(F32)
    cw = cw_ref[...]
    conv = (u_ref[HALO - 2:HALO - 2 + tm, :] * cw[0:1]
            + u_ref[HALO - 1:HALO - 1 + tm, :] * cw[1:2]
            + u_ref[HALO:, :] * cw[2:3])
    sc = (b_ref[...].astype(F32) * conv).astype(BF16)
    na = o_ref.shape[1]
    acc = jnp.dot(o_ref[...], w_ref[0:na, :], preferred_element_type=F32)
    acc = acc + jnp.dot(sc, w_ref[na:, :], preferred_element_type=F32)
    out_ref[...] = h_ref[...] + acc


def even_out(attn_o, proj, conv_w, w_out, h):
    T = h.shape[0]
    tm = min(ROW_TILE, T)
    W = SC_WIDTH
    cb = (proj.shape[1] - 3 * W) // W
    hb = tm // HALO
    halo = lambda col: pl.BlockSpec((HALO, W), lambda i: (jnp.maximum(i * hb - 1, 0), col))
    return pl.pallas_call(
        _even_out_kernel,
        grid=(T // tm,),
        in_specs=[
            pl.BlockSpec((tm, attn_o.shape[1]), lambda i: (i, 0)),
            pl.BlockSpec((tm, W), lambda i: (i, cb)),
            pl.BlockSpec((tm, W), lambda i: (i, cb + 1)),
            pl.BlockSpec((tm, W), lambda i: (i, cb + 2)),
            halo(cb + 1),
            halo(cb + 2),
            pl.BlockSpec(conv_w.shape, lambda i: (0, 0)),
            pl.BlockSpec(w_out.shape, lambda i: (0, 0)),
            pl.BlockSpec((tm, D_MODEL), lambda i: (i, 0)),
        ],
        out_specs=pl.BlockSpec((tm, D_MODEL), lambda i: (i, 0)),
        out_shape=jax.ShapeDtypeStruct((T, D_MODEL), F32),
        scratch_shapes=[pltpu.VMEM((tm + HALO, W), F32)],
        compiler_params=_cparams(("parallel",)),
        name="even_out",
    )(attn_o, proj, proj, proj, proj, proj, conv_w, w_out, h)


def _ffn_kernel(h_ref, g_ref, w1_ref, w3_ref, w2_ref, o_ref, xn_ref, acc_ref):
    f = pl.program_id(1)

    @pl.when(f == 0)
    def _():
        xn_ref[...] = _rms(h_ref[...], g_ref[...], RMS_EPS).astype(BF16)
        acc_ref[...] = jnp.zeros(acc_ref.shape, F32)

    xn = xn_ref[...]
    a = jnp.dot(xn, w1_ref[...], preferred_element_type=F32)
    b = jnp.dot(xn, w3_ref[...], preferred_element_type=F32)
    hm = (_silu(a) * b).astype(BF16)
    acc_ref[...] += jnp.dot(hm, w2_ref[...], preferred_element_type=F32)

    @pl.when(f == pl.num_programs(1) - 1)
    def _():
        o_ref[...] = h_ref[...] + acc_ref[...]


def ffn(h, g, w1, w3, w2, layer):
    T, D = h.shape
    F = w1.shape[2]
    tm = min(FFN_ROW_TILE, T)
    tf = FF_TILE
    return pl.pallas_call(
        _ffn_kernel,
        grid=(T // tm, F // tf),
        in_specs=[
            pl.BlockSpec((tm, D), lambda i, f: (i, 0)),
            pl.BlockSpec((1, D), lambda i, f: (0, 0)),
            pl.BlockSpec((None, D, tf), lambda i, f: (layer, 0, f)),
            pl.BlockSpec((None, D, tf), lambda i, f: (layer, 0, f)),
            pl.BlockSpec((None, tf, D), lambda i, f: (layer, f, 0)),
        ],
        out_specs=pl.BlockSpec((tm, D), lambda i, f: (i, 0)),
        out_shape=jax.ShapeDtypeStruct((T, D), F32),
        scratch_shapes=[pltpu.VMEM((tm, D), BF16), pltpu.VMEM((tm, D), F32)],
        compiler_params=_cparams(("parallel", "arbitrary")),
        name="ffn",
    )(h, g.reshape(1, D), w1, w3, w2)


def _ple_kernel(h_ref, g_ref, wg_ref, p_ref, wp_ref, gf_ref, o_ref, *, final_norm):
    h = h_ref[...]
    xn = _rms(h, g_ref[...], RMS_EPS).astype(BF16)
    gate = _sigmoid(jnp.dot(xn, wg_ref[...], preferred_element_type=F32))
    emb = jnp.dot(p_ref[...].astype(BF16), wp_ref[...], preferred_element_type=F32)
    out = h + gate * emb
    if final_norm:
        out = _rms(out, gf_ref[...], RMS_EPS)
    o_ref[...] = out


def ple(h, g, wg, p, wp, g_final, final_norm, layer):
    T, D = h.shape
    tm = min(ROW_TILE, T)
    return pl.pallas_call(
        functools.partial(_ple_kernel, final_norm=final_norm),
        grid=(T // tm,),
        in_specs=[
            pl.BlockSpec((tm, D), lambda i: (i, 0)),
            pl.BlockSpec((1, D), lambda i: (0, 0)),
            pl.BlockSpec((None, D, D), lambda i: (layer, 0, 0)),
            pl.BlockSpec((None, tm, PLE_DIM), lambda i: (layer, i, 0)),
            pl.BlockSpec((None, PLE_DIM, D), lambda i: (layer, 0, 0)),
            pl.BlockSpec((1, D), lambda i: (0, 0)),
        ],
        out_specs=pl.BlockSpec((tm, D), lambda i: (i, 0)),
        out_shape=jax.ShapeDtypeStruct((T, D), F32),
        compiler_params=_cparams(("parallel",)),
        name="ple",
    )(h, g.reshape(1, D), wg, p, wp, g_final.reshape(1, D))


def _dot_hi(a, b):
    return jnp.dot(a, b, preferred_element_type=F32, precision=lax.Precision.HIGHEST)


def _gdn_prep_kernel(q_ref, k_ref, v_ref, qh_ref, kh_ref, vh_ref, ba_ref, cw_ref, alog_ref, dtb_ref,
                     qg_ref, kg_ref, w_ref, u_ref, ai_ref, dl_ref,
                     xs_ref, gc_ref, gl_ref, sg_ref, gct_ref):
    i = pl.program_id(0)
    h = pl.program_id(1)
    R = q_ref.shape[0]
    C = DN_CHUNK
    KC = cw_ref.shape[1]

    def conv_silu(x_ref, halo_ref, j, hp):
        cs = slice(hp * LANES, (hp + 1) * LANES)
        buf = xs_ref.at[3 * hp + j]
        buf[0:HALO, :] = jnp.where(i == 0, 0.0, halo_ref[:, cs].astype(F32))
        buf[HALO:, :] = x_ref[:, cs].astype(F32)
        acc = buf[HALO:, :] * cw_ref[hp, KC - 1:KC, j * LANES:(j + 1) * LANES]
        for t in range(1, KC):
            acc = acc + (buf[HALO - t:HALO - t + R, :]
                         * cw_ref[hp, KC - 1 - t:KC - t, j * LANES:(j + 1) * LANES])
        return _silu(acc)

    def l2n(x):
        return x * lax.rsqrt(jnp.sum(x * x, axis=-1, keepdims=True) + 1e-6)

    chunks = range(R // C)
    rows = [slice(c * C, (c + 1) * C) for c in chunks]
    r64 = lax.broadcasted_iota(I32, (C, C), 0)
    c64 = lax.broadcasted_iota(I32, (C, C), 1)
    incl = r64 >= c64
    strict = r64 > c64

    @pl.when(h == 0)
    def _():
        ba = ba_ref[...]
        sp_in = ba + dtb_ref[...]
        softplus = jnp.maximum(sp_in, 0.0) + jnp.log(1.0 + jnp.exp(-jnp.abs(sp_in)))
        g_all = -jnp.exp(alog_ref[...]) * softplus
        g_wide = jnp.concatenate([g_all[r] for r in rows], axis=1)
        gc_wide = _dot_hi(jnp.where(incl, 1.0, 0.0), g_wide)
        for c in chunks:
            gc_c = gc_wide[:, c * LANES:(c + 1) * LANES]
            gc_ref[rows[c], :] = gc_c
            gl_ref[rows[c], :] = jnp.broadcast_to(gc_c[C - 1:C, :], (C, LANES))
        sg_ref[...] = _sigmoid(ba)
        gct_ref[...] = gc_ref[...].T

    lane = lax.broadcasted_iota(I32, (R, LANES), 1)

    def pick(ref, l):
        return jnp.sum(jnp.where(lane == l, ref[...], 0.0), axis=-1, keepdims=True)

    def bdot(a, b):
        return jnp.dot(a.astype(BF16), b.astype(BF16), preferred_element_type=F32)

    heads = range(qg_ref.shape[1] // LANES)
    items = [(hp, c) for hp in heads for c in chunks]
    decay, kq, rhs, edl = {}, {}, {}, {}
    for hp in heads:
        hh = h * len(heads) + hp
        cs = slice(hp * LANES, (hp + 1) * LANES)
        q = l2n(conv_silu(q_ref, qh_ref, 0, hp)) * (DN_HEAD_DIM ** -0.5)
        k = l2n(conv_silu(k_ref, kh_ref, 1, hp))
        v = conv_silu(v_ref, vh_ref, 2, hp)
        beta = pick(sg_ref, hh)
        gcol = pick(gc_ref, DN_HEADS + hh)
        glast = pick(gl_ref, DN_HEADS + hh)
        grow = gct_ref[pl.ds(DN_HEADS + hh, 1), :]
        eg = jnp.exp(gcol)
        kb = k * beta
        qg_ref[:, cs] = (q * eg).astype(BF16)
        kg_ref[:, cs] = (k * jnp.exp(glast - gcol)).astype(BF16)
        edl[hp] = jnp.broadcast_to(jnp.exp(glast), (R, LANES))
        rhs[hp] = jnp.concatenate([v * beta, kb * eg], axis=-1)
        kbf = k.astype(BF16)
        lhs = jnp.concatenate([kb.astype(BF16).reshape(R // C, C, LANES),
                               q.astype(BF16).reshape(R // C, C, LANES)], axis=1)
        for c in chunks:
            r = rows[c]
            decay[hp, c] = jnp.where(incl, jnp.exp(jnp.where(incl, gcol[r] - grow[:, r], 0.0)), 0.0)
            kq[hp, c] = lax.dot_general(lhs[c], kbf[r], (((1,), (1,)), ((), ())), preferred_element_type=F32)
    pw = {it: -jnp.where(strict, kq[it][:C] * decay[it], 0.0) for it in items}
    n = dict(pw)
    for _ in range(5):
        pw = {it: bdot(pw[it], pw[it]) for it in items}
        n = {it: n[it] + pw[it] + bdot(n[it], pw[it]) for it in items}
    for hp, c in items:
        cs = slice(hp * LANES, (hp + 1) * LANES)
        rc = rhs[hp][rows[c]]
        uw = rc + bdot(n[hp, c], rc)
        u_ref[rows[c], cs] = uw[:, :DN_HEAD_DIM]
        w_ref[rows[c], cs] = uw[:, DN_HEAD_DIM:].astype(BF16)
        ai_ref[hp, rows[c], :] = (kq[hp, c][C:] * decay[hp, c]).astype(BF16)
        dl_ref[hp, c:c + 1, :] = edl[hp][c * C:c * C + 1, :]


def gdn_prep(proj, ba, conv_w, alog_l, dtb_l):
    T = proj.shape[0]
    R = min(GDN_TILE, T)
    H = DN_HEADS
    nchunk = T // DN_CHUNK
    hb = R // HALO
    HP = GDN_PREP_HEADS
    W = HP * LANES
    col = lambda j: pl.BlockSpec((R, W), lambda i, h: (i, j * (H // HP) + h))
    halo = lambda j: pl.BlockSpec((HALO, W), lambda i, h: (jnp.maximum(i * hb - 1, 0), j * (H // HP) + h))
    KC = conv_w.shape[0]
    cw = conv_w.reshape(KC, 3, H, LANES).transpose(2, 0, 1, 3).reshape(H, KC, 3 * LANES)
    cw_spec = pl.BlockSpec((HP, KC, 3 * LANES), lambda i, h: (h, 0, 0))
    row_out = lambda dt: jax.ShapeDtypeStruct((T, DN_WIDTH), dt)
    out_col = pl.BlockSpec((R, W), lambda i, h: (i, h))
    return pl.pallas_call(
        _gdn_prep_kernel,
        grid=(T // R, H // HP),
        in_specs=[col(0), col(1), col(2), halo(0), halo(1), halo(2),
                  pl.BlockSpec((R, LANES), lambda i, h: (i, 0)),
                  cw_spec,
                  pl.BlockSpec((1, LANES), lambda i, h: (0, 0)),
                  pl.BlockSpec((1, LANES), lambda i, h: (0, 0))],
        out_specs=[out_col, out_col, out_col, out_col,
                   pl.BlockSpec((HP, R, DN_CHUNK), lambda i, h: (h, i, 0)),
                   pl.BlockSpec((HP, R // DN_CHUNK, LANES), lambda i, h: (h, i, 0))],
        out_shape=[row_out(BF16), row_out(BF16), row_out(BF16), row_out(F32),
                   jax.ShapeDtypeStruct((H, T, DN_CHUNK), BF16),
                   jax.ShapeDtypeStruct((H, nchunk, LANES), F32)],
        scratch_shapes=[pltpu.VMEM((3 * HP, R + HALO, LANES), F32),
                        pltpu.VMEM((R, LANES), F32), pltpu.VMEM((R, LANES), F32), pltpu.VMEM((R, LANES), F32),
                        pltpu.VMEM((LANES, R), F32)],
        compiler_params=_cparams(("parallel", "arbitrary")),
        name="gdn_prep",
    )(proj, proj, proj, proj, proj, proj, ba, cw, alog_l, dtb_l)


def _gdn_scan_kernel(qg_ref, kg_ref, w_ref, u_ref, ai_ref, dl_ref, o_ref, s_ref):
    @pl.when(pl.program_id(1) == 0)
    def _():
        s_ref[...] = jnp.zeros(s_ref.shape, F32)

    C = DN_CHUNK
    D = DN_HEAD_DIM
    heads = range(s_ref.shape[0])

    def chunk(c, carry):
        rows = pl.ds(pl.multiple_of(c * C, C), C)
        s = [s_ref[hh] for hh in heads]
        sb = [x.astype(BF16) for x in s]
        cols = [slice(hh * D, (hh + 1) * D) for hh in heads]
        r1 = [jnp.dot(jnp.concatenate([w_ref[rows, cols[hh]], qg_ref[rows, cols[hh]]], axis=0), sb[hh],
                      preferred_element_type=F32) for hh in heads]
        vb = [(u_ref[rows, cols[hh]] - r1[hh][:C]).astype(BF16) for hh in heads]
        for hh in heads:
            o_ref[rows, cols[hh]] = r1[hh][C:] + jnp.dot(ai_ref[hh, rows, :], vb[hh], preferred_element_type=F32)
        upd = [lax.dot_general(kg_ref[rows, cols[hh]], vb[hh], (((0,), (0,)), ((), ())),
                               preferred_element_type=F32) for hh in heads]
        for hh in heads:
            s_ref[hh] = s[hh] * dl_ref[hh, pl.ds(c, 1), :] + upd[hh]
        return carry

    lax.fori_loop(0, qg_ref.shape[0] // C, chunk, 0)


def gdn_scan(qg, kg, w, u, ai, dl):
    T = qg.shape[0]
    R = min(GDN_TILE, T)
    H = DN_HEADS
    HB = GDN_SCAN_HEADS
    col = pl.BlockSpec((R, HB * LANES), lambda h, i: (i, h))
    return pl.pallas_call(
        _gdn_scan_kernel,
        grid=(H // HB, T // R),
        in_specs=[col, col, col, col,
                  pl.BlockSpec((HB, R, DN_CHUNK), lambda h, i: (h, i, 0)),
                  pl.BlockSpec((HB, R // DN_CHUNK, LANES), lambda h, i: (h, i, 0))],
        out_specs=col,
        out_shape=jax.ShapeDtypeStruct((T, DN_WIDTH), F32),
        scratch_shapes=[pltpu.VMEM((HB, DN_HEAD_DIM, DN_HEAD_DIM), F32)],
        compiler_params=_cparams(("parallel", "arbitrary")),
        name="gdn_scan",
    )(qg, kg, w, u, ai, dl)


def _odd_out_kernel(o_ref, z_ref, g_ref, w_ref, h_ref, out_ref):
    g = g_ref[...]
    parts = []
    for hd in range(DN_HEADS):
        sl = slice(hd * DN_HEAD_DIM, (hd + 1) * DN_HEAD_DIM)
        parts.append((_rms(o_ref[:, sl], g, RMS_EPS) * _silu(z_ref[:, sl].astype(F32))).astype(BF16))
    y = jnp.concatenate(parts, axis=-1)
    out_ref[...] = h_ref[...] + jnp.dot(y, w_ref[...], preferred_element_type=F32)


def odd_out(o, proj, onorm_g, w_out, h):
    T = h.shape[0]
    tm = min(ROW_TILE, T)
    zb = 3 * DN_WIDTH // DN_WIDTH
    return pl.pallas_call(
        _odd_out_kernel,
        grid=(T // tm,),
        in_specs=[
            pl.BlockSpec((tm, DN_WIDTH), lambda i: (i, 0)),
            pl.BlockSpec((tm, DN_WIDTH), lambda i: (i, zb)),
            pl.BlockSpec((1, DN_HEAD_DIM), lambda i: (0, 0)),
            pl.BlockSpec(w_out.shape, lambda i: (0, 0)),
            pl.BlockSpec((tm, D_MODEL), lambda i: (i, 0)),
        ],
        out_specs=pl.BlockSpec((tm, D_MODEL), lambda i: (i, 0)),
        out_shape=jax.ShapeDtypeStruct((T, D_MODEL), F32),
        compiler_params=_cparams(("parallel",)),
        name="odd_out",
    )(o, proj, onorm_g.reshape(1, DN_HEAD_DIM), w_out, h)


def _route_kernel(h_ref, g_ref, wr_ref, ri_ref, gate_ref, cnt_ref, carry_ref):
    i = pl.program_id(0)

    @pl.when(i == 0)
    def _():
        carry_ref[...] = jnp.zeros(carry_ref.shape, F32)

    xn = _rms(h_ref[...], g_ref[...], RMS_EPS)
    logits = jnp.dot(xn.astype(BF16), wr_ref[...].astype(BF16), preferred_element_type=F32)
    tm = logits.shape[0]
    lane = lax.broadcasted_iota(I32, logits.shape, 1)
    logits = jnp.where(lane < N_EXPERTS, logits, NEG_BIG)
    lane_f = lane.astype(F32)
    m1 = jnp.max(logits, axis=-1, keepdims=True)
    i1 = jnp.min(jnp.where(logits == m1, lane_f, float(LANES)), axis=-1, keepdims=True)
    rest = jnp.where(lane_f == i1, NEG_BIG, logits)
    m2 = jnp.max(rest, axis=-1, keepdims=True)
    i2 = jnp.min(jnp.where(rest == m2, lane_f, float(LANES)), axis=-1, keepdims=True)
    e = jnp.exp(m2 - m1)
    g1 = 1.0 / (1.0 + e)
    g2 = e / (1.0 + e)
    oh1 = lane_f == i1
    oh2 = lane_f == i2
    i1 = i1.astype(I32)
    i2 = i2.astype(I32)
    oh = jnp.where(oh1 | oh2, 1.0, 0.0)
    ri_ = lax.broadcasted_iota(I32, (tm, tm), 0)
    ci_ = lax.broadcasted_iota(I32, (tm, tm), 1)
    below = jnp.where(ri_ > ci_, 1.0, 0.0).astype(BF16)
    ex = jnp.dot(below, oh.astype(BF16), preferred_element_type=F32) + carry_ref[0:1, :]
    r1 = jnp.sum(jnp.where(oh1, ex, 0.0), axis=-1, keepdims=True).astype(I32)
    r2 = jnp.sum(jnp.where(oh2, ex, 0.0), axis=-1, keepdims=True).astype(I32)
    ri_ref[...] = jnp.where(lane == 0, i1, jnp.where(lane == 1, i2, jnp.where(lane == 2, r1, r2)))
    gate_ref[...] = jnp.where(lane == 0, g1, g2)
    carry_ref[...] = carry_ref[...] + jnp.sum(oh, axis=0, keepdims=True)
    cnt_ref[...] = carry_ref[...]


def route(h, g, wr):
    T, D = h.shape
    tm = min(ROW_TILE, T)
    return pl.pallas_call(
        _route_kernel,
        grid=(T // tm,),
        in_specs=[
            pl.BlockSpec((tm, D), lambda i: (i, 0)),
            pl.BlockSpec((1, D), lambda i: (0, 0)),
            pl.BlockSpec((D, LANES), lambda i: (0, 0)),
        ],
        out_specs=[
            pl.BlockSpec((tm, LANES), lambda i: (i, 0)),
            pl.BlockSpec((tm, LANES), lambda i: (i, 0)),
            pl.BlockSpec((SUBLANES, LANES), lambda i: (0, 0)),
        ],
        out_shape=[
            jax.ShapeDtypeStruct((T, LANES), I32),
            jax.ShapeDtypeStruct((T, LANES), F32),
            jax.ShapeDtypeStruct((SUBLANES, LANES), F32),
        ],
        scratch_shapes=[pltpu.VMEM((SUBLANES, LANES), F32)],
        compiler_params=_cparams(("arbitrary",)),
        name="route",
    )(h, g.reshape(1, D), wr)


def _row_copy(src_ref, s, dst_ref, d, sem):
    return pltpu.make_async_copy(src_ref.at[pl.ds(s, 1), :], dst_ref.at[pl.ds(d, 1), :], sem)


def _dispatch_kernel(pz_ref, dest_ref, h_ref, g_ref, xs_ref, xn_ref, zb_ref, sem, zsem):
    tm = h_ref.shape[0]
    G = zb_ref.shape[0]

    @pl.when(pl.program_id(0) == 0)
    def _():
        zb_ref[...] = jnp.zeros(zb_ref.shape, F32)

        def zero_copy(e):
            return pltpu.make_async_copy(zb_ref, xs_ref.at[pl.ds(pl.multiple_of(pz_ref[e], G), G), :], zsem)

        for e in range(2 * N_EXPERTS):
            @pl.when(pz_ref[e] >= 0)
            def _():
                zero_copy(e).start()

        for e in range(2 * N_EXPERTS):
            @pl.when(pz_ref[e] >= 0)
            def _():
                zero_copy(e).wait()

    xn_ref[...] = _rms(h_ref[...], g_ref[...], RMS_EPS)

    def issue(r, carry):
        _row_copy(xn_ref, r, xs_ref, dest_ref[2 * r], sem).start(priority=0)
        _row_copy(xn_ref, r, xs_ref, dest_ref[2 * r + 1], sem).start(priority=1)
        return carry

    lax.fori_loop(0, tm, issue, 0, unroll=ROW_DMA_UNROLL)

    def drain(r, carry):
        _row_copy(xn_ref, 0, xs_ref, 0, sem).wait()
        _row_copy(xn_ref, 0, xs_ref, 0, sem).wait()
        return carry

    lax.fori_loop(0, tm, drain, 0, unroll=ROW_DMA_UNROLL)


def dispatch(h, g, dest_flat, zero_tiles, n_slots):
    T, D = h.shape
    tm = min(ROW_TILE, T)
    grid_spec = pltpu.PrefetchScalarGridSpec(
        num_scalar_prefetch=1,
        grid=(T // tm,),
        in_specs=[
            pl.BlockSpec((2 * tm,), lambda i, pz: (i,), memory_space=pltpu.SMEM),
            pl.BlockSpec((tm, D), lambda i, pz: (i, 0)),
            pl.BlockSpec((1, D), lambda i, pz: (0, 0)),
        ],
        out_specs=pl.BlockSpec(memory_space=pl.ANY),
        scratch_shapes=[pltpu.VMEM((tm, D), F32), pltpu.VMEM((MOE_TILE, D), F32),
                        pltpu.SemaphoreType.DMA, pltpu.SemaphoreType.DMA],
    )
    return pl.pallas_call(
        _dispatch_kernel,
        grid_spec=grid_spec,
        out_shape=jax.ShapeDtypeStruct((n_slots, D), F32),
        compiler_params=_cparams(("arbitrary",)),
        name="moe_dispatch",
    )(zero_tiles, dest_flat, h, g.reshape(1, D))


def _gmm_kernel(te_ref, na_ref, x_ref, w1_ref, w3_ref, w2_ref, y_ref, xb_ref, acc_ref):
    b = pl.program_id(0)
    f = pl.program_id(1)

    @pl.when(b < na_ref[0])
    def _():
        @pl.when(f == 0)
        def _():
            xb_ref[...] = x_ref[...].astype(BF16)
            acc_ref[...] = jnp.zeros(acc_ref.shape, F32)

        xb = xb_ref[...]
        a = jnp.dot(xb, w1_ref[...], preferred_element_type=F32)
        c = jnp.dot(xb, w3_ref[...], preferred_element_type=F32)
        hm = (_silu(a) * c).astype(BF16)
        acc_ref[...] += jnp.dot(hm, w2_ref[...], preferred_element_type=F32)

        @pl.when(f == pl.num_programs(1) - 1)
        def _():
            y_ref[...] = acc_ref[...]

    @pl.when((b >= na_ref[0]) & (f == pl.num_programs(1) - 1))
    def _():
        y_ref[...] = jnp.zeros(y_ref.shape, F32)


def gmm(xs, w1, w3, w2, tile_e, n_active, layer):
    P, D = xs.shape
    G = MOE_TILE
    F = w1.shape[3]
    tf = MOE_FF_TILE
    nf = F // tf

    def row_idx(b, f, te, na):
        return (jnp.minimum(b, na[0] - 1), 0)

    def f_idx(b, f, na):
        return jnp.where(b < na[0], f, nf - 1)

    grid_spec = pltpu.PrefetchScalarGridSpec(
        num_scalar_prefetch=2,
        grid=(P // G, nf),
        in_specs=[
            pl.BlockSpec((G, D), row_idx),
            pl.BlockSpec((None, None, D, tf), lambda b, f, te, na: (layer, te[b], 0, f_idx(b, f, na))),
            pl.BlockSpec((None, None, D, tf), lambda b, f, te, na: (layer, te[b], 0, f_idx(b, f, na))),
            pl.BlockSpec((None, None, tf, D), lambda b, f, te, na: (layer, te[b], f_idx(b, f, na), 0)),
        ],
        out_specs=pl.BlockSpec((G, D), lambda b, f, te, na: (b, 0)),
        scratch_shapes=[pltpu.VMEM((G, D), BF16), pltpu.VMEM((G, D), F32)],
    )
    return pl.pallas_call(
        _gmm_kernel,
        grid_spec=grid_spec,
        out_shape=jax.ShapeDtypeStruct((P, D), F32),
        compiler_params=_cparams(("arbitrary", "arbitrary")),
        name="moe_gmm",
    )(tile_e, n_active, xs, w1, w3, w2)


def _combine_kernel(dest_ref, h_ref, gate_ref, y_ref, o_ref, ya_ref, yb_ref, sem):
    tm = h_ref.shape[0]

    def issue(r, carry):
        _row_copy(y_ref, dest_ref[2 * r], ya_ref, r, sem).start(priority=0)
        _row_copy(y_ref, dest_ref[2 * r + 1], yb_ref, r, sem).start(priority=1)
        return carry

    lax.fori_loop(0, tm, issue, 0, unroll=ROW_DMA_UNROLL)

    def drain(r, carry):
        _row_copy(y_ref, 0, ya_ref, 0, sem).wait()
        _row_copy(y_ref, 0, yb_ref, 0, sem).wait()
        return carry

    lax.fori_loop(0, tm, drain, 0, unroll=ROW_DMA_UNROLL)
    gt = gate_ref[...]
    o_ref[...] = h_ref[...] + gt[:, 0:1] * ya_ref[...] + gt[:, 1:2] * yb_ref[...]


def combine(h, gates, y, dest_flat):
    T, D = h.shape
    tm = min(ROW_TILE, T)
    return pl.pallas_call(
        _combine_kernel,
        grid=(T // tm,),
        in_specs=[
            pl.BlockSpec((2 * tm,), lambda i: (i,), memory_space=pltpu.SMEM),
            pl.BlockSpec((tm, D), lambda i: (i, 0)),
            pl.BlockSpec((tm, LANES), lambda i: (i, 0)),
            pl.BlockSpec(memory_space=pl.ANY),
        ],
        out_specs=pl.BlockSpec((tm, D), lambda i: (i, 0)),
        out_shape=jax.ShapeDtypeStruct((T, D), F32),
        scratch_shapes=[pltpu.VMEM((tm, D), F32), pltpu.VMEM((tm, D), F32), pltpu.SemaphoreType.DMA],
        compiler_params=_cparams(("arbitrary",)),
        name="moe_combine",
    )(dest_flat, h, gates, y)


def _lambda_init(layer_idx):
    return 0.8 - 0.6 * math.exp(-0.3 * layer_idx)


def _even_layer(h, pos_col, pos_row, invf, ln_mix, w_in, w_out, lam_params, subln_g, conv_w, layer_idx):
    proj = norm_matmul(h, ln_mix, w_in.astype(BF16), tn=PROJ_COL_TILE, out_dtype=BF16)
    qk, vt = rope_qkv(proj, pos_col, invf)
    o = diff_attention(qk, vt, pos_col, pos_row, lam_params, subln_g, _lambda_init(layer_idx))
    return even_out(o, proj, conv_w, w_out.astype(BF16), h)


def _odd_mixer(h, ln_mix, w_in, conv_w, a_log, dt_bias, onorm_g, w_out):
    main_w = 4 * DN_WIDTH
    w_main = w_in[:, :main_w].astype(BF16)
    w_ba = jnp.pad(w_in[:, main_w:], ((0, 0), (0, LANES - 2 * DN_HEADS))).astype(BF16)
    proj = norm_matmul(h, ln_mix, w_main, tn=PROJ_COL_TILE, out_dtype=BF16)
    ba = norm_matmul(h, ln_mix, w_ba, tn=LANES)
    pad8 = lambda v: jnp.pad(v.astype(F32), (DN_HEADS, LANES - 2 * DN_HEADS)).reshape(1, LANES)
    qg, kg, w, u, ai, dl = gdn_prep(proj, ba, conv_w, pad8(a_log), pad8(dt_bias))
    o = gdn_scan(qg, kg, w, u, ai, dl)
    return odd_out(o, proj, onorm_g, w_out.astype(BF16), h)


def _moe(h, ln_ffn, w_router, w1, w3, w2, layer):
    T = h.shape[0]
    G = MOE_TILE
    wr = jnp.pad(w_router, ((0, 0), (0, LANES - N_EXPERTS)))
    ri, gates, cnt = route(h, ln_ffn, wr)
    counts = cnt[0, :N_EXPERTS].astype(I32)
    padded = ((counts + G - 1) // G) * G
    pends = jnp.cumsum(padded)
    pstarts = pends - padded
    dest = (jnp.take(pstarts, ri[:, 0:2]) + ri[:, 2:4]).reshape(-1)
    n_tiles = (2 * T) // G + N_EXPERTS
    tile_start = jnp.arange(n_tiles, dtype=I32) * G
    tile_e = jnp.minimum(jnp.sum(pends[None, :] <= tile_start[:, None], axis=1), N_EXPERTS - 1).astype(I32)
    n_active = (pends[-1:] // G).astype(I32)
    tail = pends[-1] + jnp.arange(N_EXPERTS, dtype=I32) * G
    zero_tiles = jnp.concatenate([jnp.where(padded > 0, pends - G, -1),
                                  jnp.where(tail < n_tiles * G, tail, -1)]).astype(I32)
    xs = dispatch(h, ln_ffn, dest, zero_tiles, n_tiles * G)
    y = gmm(xs, w1, w3, w2, tile_e, n_active, layer)
    return combine(h, gates, y, dest)


def kernel(x, p, positions, ln_mix, ln_ffn, ln_ple, ln_final, w_in_even, w_out_even, lam_q1, lam_k1, lam_q2, lam_k2, subln_gain, conv_w_short, w_in_odd, conv_w_qkv, a_log, dt_bias, onorm_gain, w_out_odd, w1_dense, w3_dense, w2_dense, w_router, w1_moe, w3_moe, w2_moe, w_ple_gate, w_ple_proj):
    B, S, D = x.shape
    T = B * S
    depth = p.shape[0]
    h = x.reshape(T, D)
    pos_col = positions.reshape(T, 1).astype(I32)
    pos_row = positions.reshape(1, T).astype(I32)
    inv_freq = ROPE_THETA ** (-jnp.arange(0, ROT_DIM, 2, dtype=F32) / ROT_DIM)
    invf = jnp.tile(inv_freq, LANES // (ROT_DIM // 2)).reshape(1, LANES)
    w1d, w3d, w2d = (w.astype(BF16) for w in (w1_dense, w3_dense, w2_dense))
    w1m, w3m, w2m = (w.astype(BF16) for w in (w1_moe, w3_moe, w2_moe))
    wpg, wpp = w_ple_gate.astype(BF16), w_ple_proj.astype(BF16)
    p_rows = p.reshape(depth, T, PLE_DIM)
    for i in range(depth):
        j = i // 2
        if i % 2 == 0:
            lam_params = jnp.stack([lam_q1[j], lam_k1[j], lam_q2[j], lam_k2[j]]).astype(F32)
            h = _even_layer(h, pos_col, pos_row, invf, ln_mix[i], w_in_even[j], w_out_even[j], lam_params,
                            subln_gain[j], conv_w_short[j], i)
            h = ffn(h, ln_ffn[i], w1d, w3d, w2d, j)
        else:
            h = _odd_mixer(h, ln_mix[i], w_in_odd[j], conv_w_qkv[j], a_log[j], dt_bias[j], onorm_gain[j],
                           w_out_odd[j])
            h = _moe(h, ln_ffn[i], w_router[j], w1m, w3m, w2m, j)
        h = ple(h, ln_ple[i], wpg, p_rows, wpp, ln_final, final_norm=(i == depth - 1), layer=i)
    return h.reshape(B, S, D)
```

```python
import functools
import math

import jax
import jax.numpy as jnp
import numpy as np
from jax import lax
from jax.experimental import pallas as pl
from jax.experimental.pallas import tpu as pltpu

F32 = jnp.float32
BF16 = jnp.bfloat16
I32 = jnp.int32

D_MODEL = 1024
DEPTH = 4
RMS_EPS = 1e-6
DA_HEADS = 4
DA_HEAD_DIM = 64
DA_V_DIM = 2 * DA_HEAD_DIM
ROPE_THETA = 500000.0
ROT_DIM = DA_HEAD_DIM // 4
SUBLN_EPS = 1e-5
SC_WIDTH = 512
DN_HEADS = 8
DN_HEAD_DIM = 128
DN_WIDTH = DN_HEADS * DN_HEAD_DIM
DN_CHUNK = 64
D_FF = 3584
N_EXPERTS = 8
PLE_DIM = 256

LANES = 128
SUBLANES = 8
HALO = 16
VMEM_LIMIT = 52 * 1024 * 1024

NEG_BIG = -1e30

ROW_TILE = 512
PROJ_COL_TILE = 1024
FFN_ROW_TILE = 512
FF_TILE = 1792
MOE_FF_TILE = 1792
ATT_TQ = 1024
ATT_TK = 1024
ATT_EXP_ROWS = 128
ATT_TQ_SPLIT = 256
GDN_TILE = 1024
GDN_PREP_HEADS = 2
GDN_SCAN_HEADS = 8
MOE_TILE = 512
ROW_DMA_UNROLL = 8


def _cparams(sem):
    return pltpu.CompilerParams(dimension_semantics=sem, vmem_limit_bytes=VMEM_LIMIT)


def _rms(x, g, eps):
    ms = jnp.mean(x * x, axis=-1, keepdims=True)
    return x * lax.rsqrt(ms + eps) * g


def _silu(x):
    return x * (1.0 / (1.0 + jnp.exp(-x)))


def _sigmoid(x):
    return 1.0 / (1.0 + jnp.exp(-x))


def _norm_matmul_kernel(x_ref, g_ref, w_ref, *rest):
    xn = _rms(x_ref[...], g_ref[...], RMS_EPS).astype(BF16)
    if len(rest) == 1:
        (o_ref,) = rest
    else:
        ws_ref, o_ref, os_ref = rest
        os_ref[...] = jnp.dot(xn, ws_ref[...], preferred_element_type=F32)
    tn = PROJ_COL_TILE
    for c in range(o_ref.shape[1] // tn):
        o_ref[:, c * tn:(c + 1) * tn] = jnp.dot(
            xn, w_ref[:, c * tn:(c + 1) * tn], preferred_element_type=F32).astype(o_ref.dtype)


def norm_matmul(x, g, w, w_side=None, out_dtype=BF16):
    T, D = x.shape
    N = w.shape[1]
    tm = min(ROW_TILE, T)
    in_specs = [
        pl.BlockSpec((tm, D), lambda i: (i, 0)),
        pl.BlockSpec((1, D), lambda i: (0, 0)),
        pl.BlockSpec((D, N), lambda i: (0, 0)),
    ]
    out_specs = [pl.BlockSpec((tm, N), lambda i: (i, 0))]
    out_shape = [jax.ShapeDtypeStruct((T, N), out_dtype)]
    args = [x, g.reshape(1, D), w]
    if w_side is not None:
        ns = w_side.shape[1]
        in_specs.append(pl.BlockSpec((D, ns), lambda i: (0, 0)))
        out_specs.append(pl.BlockSpec((tm, ns), lambda i: (i, 0)))
        out_shape.append(jax.ShapeDtypeStruct((T, ns), F32))
        args.append(w_side)
    return pl.pallas_call(
        _norm_matmul_kernel,
        grid=(T // tm,),
        in_specs=in_specs,
        out_specs=out_specs,
        out_shape=out_shape,
        compiler_params=_cparams(("parallel",)),
        name="norm_matmul",
    )(*args)


def _rope_kernel(proj_ref, pos_ref, invf_ref, o_ref, vt_ref):
    pos = pos_ref[...].astype(F32)
    ang = pos * invf_ref[...]
    cos_t = jnp.cos(ang)
    sin_t = jnp.sin(ang)
    d = lax.broadcasted_iota(I32, ang.shape, 1) & (DA_HEAD_DIM - 1)
    half = ROT_DIM // 2
    c_mul = jnp.where(d < ROT_DIM, cos_t, 1.0)
    s_mul = jnp.where(d < half, -sin_t, jnp.where(d < ROT_DIM, sin_t, 0.0))
    n_qk = 2 * DA_HEADS * DA_HEAD_DIM // LANES
    for c in range(2 * n_qk):
        x = proj_ref[:, c * LANES:(c + 1) * LANES].astype(F32)
        swapped = jnp.where(d < half, pltpu.roll(x, LANES - half, 1), pltpu.roll(x, half, 1))
        r = x * c_mul + swapped * s_mul
        if c < n_qk:
            r = r * (DA_HEAD_DIM ** -0.5 * math.log2(math.e))
        o_ref[:, c * LANES:(c + 1) * LANES] = r.astype(BF16)
    v0 = 2 * n_qk * LANES
    for hd in range(DA_HEADS):
        vt_ref[hd * DA_V_DIM:(hd + 1) * DA_V_DIM, :] = (
            proj_ref[:, v0 + hd * DA_V_DIM:v0 + (hd + 1) * DA_V_DIM].astype(F32).T.astype(BF16))


def rope_qkv(proj, positions_col, invf):
    T = proj.shape[0]
    WQK = 2 * 2 * DA_HEADS * DA_HEAD_DIM
    WV = DA_HEADS * DA_V_DIM
    tm = min(ROW_TILE, T)
    return pl.pallas_call(
        _rope_kernel,
        grid=(T // tm,),
        in_specs=[
            pl.BlockSpec((tm, WQK + WV), lambda i: (i, 0)),
            pl.BlockSpec((tm, 1), lambda i: (i, 0)),
            pl.BlockSpec((1, LANES), lambda i: (0, 0)),
        ],
        out_specs=[pl.BlockSpec((tm, WQK), lambda i: (i, 0)),
                   pl.BlockSpec((WV, tm), lambda i: (0, i))],
        out_shape=[jax.ShapeDtypeStruct((T, WQK), BF16), jax.ShapeDtypeStruct((WV, T), BF16)],
        compiler_params=_cparams(("parallel",)),
        name="rope_qkv",
    )(proj, positions_col, invf)


def _attn_kernel(qt_ref, kt_ref, q_ref, k_ref, v_ref, pq_ref, pk_ref, lam_ref, g_ref, o_ref,
                 m_ref, l_ref, acc_ref, *, tq, tk, lam0):
    p = pl.program_id(1)
    qi = qt_ref[p]
    ki = kt_ref[p]

    @pl.when(ki == 0)
    def _():
        m_ref[...] = jnp.full(m_ref.shape, NEG_BIG, F32)
        l_ref[...] = jnp.zeros(l_ref.shape, F32)
        acc_ref[...] = jnp.zeros(acc_ref.shape, F32)

    tqs = min(tq, ATT_TQ_SPLIT)
    items = [(s, j) for s in range(2) for j in range(tq // tqs)]

    def step(masked):
        q = q_ref[...]
        k = k_ref[...]
        v = v_ref[...]
        def nkeys(j):
            return (j + 1) * tqs if (masked and tq == tk) else tk

        st = {}
        for s, j in items:
            cols = slice(j * tqs, (j + 1) * tqs)
            qs = q[j * tqs:(j + 1) * tqs, s * DA_HEAD_DIM:(s + 1) * DA_HEAD_DIM]
            ks = k[:nkeys(j), s * DA_HEAD_DIM:(s + 1) * DA_HEAD_DIM]
            sc = lax.dot_general(ks, qs, (((1,), (1,)), ((), ())), preferred_element_type=F32)
            if masked:
                sc = jnp.where(pk_ref[0:nkeys(j), :] <= pq_ref[:, cols], sc, NEG_BIG)
            st[s, j] = sc
        for s, j in items:
            cols = slice(j * tqs, (j + 1) * tqs)
            sc = st[s, j]
            nk = nkeys(j)
            m_prev = m_ref[s, :, cols]
            m_new = jnp.maximum(m_prev, jnp.max(sc, axis=0, keepdims=True))
            alpha = jnp.exp2(m_prev - m_new)
            psum = jnp.zeros((1, tqs), F32)
            pchunks = []
            for r0 in range(0, nk, ATT_EXP_ROWS):
                pc = jnp.exp2(sc[r0:r0 + ATT_EXP_ROWS] - m_new)
                psum = psum + jnp.sum(pc, axis=0, keepdims=True)
                pchunks.append(pc.astype(BF16))
            pb = jnp.concatenate(pchunks, axis=0)
            l_ref[s, :, cols] = alpha * l_ref[s, :, cols] + psum
            pv = jnp.dot(v[:, :nk], pb, preferred_element_type=F32)
            acc_ref[s, :, cols] = alpha * acc_ref[s, :, cols] + pv
            m_ref[s, :, cols] = m_new

    crosses = (ki + 1) * tk - 1 > qi * tq

    @pl.when(crosses)
    def _():
        step(True)

    @pl.when(jnp.logical_not(crosses))
    def _():
        step(False)

    @pl.when((ki + 1) * tk >= (qi + 1) * tq)
    def _():
        lm = lam_ref[...]
        s1 = jnp.sum(lm[0:1] * lm[1:2], axis=-1, keepdims=True)
        s2 = jnp.sum(lm[2:3] * lm[3:4], axis=-1, keepdims=True)
        lam = jnp.exp(s1) - jnp.exp(s2) + lam0
        ot = acc_ref[0] * (1.0 / l_ref[0]) - lam * (acc_ref[1] * (1.0 / l_ref[1]))
        ms = jnp.mean(ot * ot, axis=0, keepdims=True)
        ot = ot * lax.rsqrt(ms + SUBLN_EPS) * g_ref[...] * (1.0 - lam0)
        o_ref[...] = ot.T.astype(o_ref.dtype)


def diff_attention(qk, vt, pos_col, pos_row, lam_params, subln_g, lam0):
    T = qk.shape[0]
    tq = min(ATT_TQ, T)
    tk = min(ATT_TK, T)
    nq = T // tq
    pairs = [(qi, ki) for qi in range(nq) for ki in range(-(-((qi + 1) * tq) // tk))]
    qt = jnp.asarray(np.array([a for a, _ in pairs], np.int32))
    kt = jnp.asarray(np.array([b for _, b in pairs], np.int32))
    H = DA_HEADS
    kern = functools.partial(_attn_kernel, tq=tq, tk=tk, lam0=lam0)
    grid_spec = pltpu.PrefetchScalarGridSpec(
        num_scalar_prefetch=2,
        grid=(H, len(pairs)),
        in_specs=[
            pl.BlockSpec((tq, LANES), lambda h, p, qt, kt: (qt[p], h)),
            pl.BlockSpec((tk, LANES), lambda h, p, qt, kt: (kt[p], H + h)),
            pl.BlockSpec((DA_V_DIM, tk), lambda h, p, qt, kt: (h, kt[p])),
            pl.BlockSpec((1, tq), lambda h, p, qt, kt: (0, qt[p])),
            pl.BlockSpec((tk, 1), lambda h, p, qt, kt: (kt[p], 0)),
            pl.BlockSpec((4, DA_HEAD_DIM), lambda h, p, qt, kt: (0, 0)),
            pl.BlockSpec((DA_V_DIM, 1), lambda h, p, qt, kt: (0, 0)),
        ],
        out_specs=pl.BlockSpec((tq, DA_V_DIM), lambda h, p, qt, kt: (qt[p], h)),
        scratch_shapes=[
            pltpu.VMEM((2, 1, tq), F32),
            pltpu.VMEM((2, 1, tq), F32),
            pltpu.VMEM((2, DA_V_DIM, tq), F32),
        ],
    )
    return pl.pallas_call(
        kern,
        grid_spec=grid_spec,
        out_shape=jax.ShapeDtypeStruct((T, H * DA_V_DIM), BF16),
        compiler_params=_cparams(("parallel", "arbitrary")),
        name="diff_attention",
    )(qt, kt, qk, qk, vt, pos_row, pos_col, lam_params, subln_g.reshape(DA_V_DIM, 1))


def _even_out_kernel(o_ref, b_ref, c_ref, x_ref, ch_ref, xh_ref, cw_ref, w_ref, h_ref, out_ref, u_ref):
    tm = o_ref.shape[0]
    u_prev = ch_ref[...].astype(F32) * xh_ref[...].astype(F32)
    u_prev = jnp.where(pl.program_id(0) == 0, 0.0, u_prev)
    u_ref[0:HALO, :] = u_prev
    u_ref[HALO:, :] = c_ref[...].astype(F32) * x_ref[...].astype(F32)
    cw = cw_ref[...]
    conv = (u_ref[HALO - 2:HALO - 2 + tm, :] * cw[0:1]
            + u_ref[HALO - 1:HALO - 1 + tm, :] * cw[1:2]
            + u_ref[HALO:, :] * cw[2:3])
    sc = (b_ref[...].astype(F32) * conv).astype(BF16)
    na = o_ref.shape[1]
    acc = jnp.dot(o_ref[...], w_ref[0:na, :], preferred_element_type=F32)
    acc = acc + jnp.dot(sc, w_ref[na:, :], preferred_element_type=F32)
    out_ref[...] = h_ref[...] + acc


def even_out(attn_o, proj, conv_w, w_out, h):
    T = h.shape[0]
    tm = min(ROW_TILE, T)
    W = SC_WIDTH
    cb = (proj.shape[1] - 3 * W) // W
    hb = tm // HALO
    halo = lambda col: pl.BlockSpec((HALO, W), lambda i: (jnp.maximum(i * hb - 1, 0), col))
    return pl.pallas_call(
        _even_out_kernel,
        grid=(T // tm,),
        in_specs=[
            pl.BlockSpec((tm, attn_o.shape[1]), lambda i: (i, 0)),
            pl.BlockSpec((tm, W), lambda i: (i, cb)),
            pl.BlockSpec((tm, W), lambda i: (i, cb + 1)),
            pl.BlockSpec((tm, W), lambda i: (i, cb + 2)),
            halo(cb + 1),
            halo(cb + 2),
            pl.BlockSpec(conv_w.shape, lambda i: (0, 0)),
            pl.BlockSpec(w_out.shape, lambda i: (0, 0)),
            pl.BlockSpec((tm, D_MODEL), lambda i: (i, 0)),
        ],
        out_specs=pl.BlockSpec((tm, D_MODEL), lambda i: (i, 0)),
        out_shape=jax.ShapeDtypeStruct((T, D_MODEL), F32),
        scratch_shapes=[pltpu.VMEM((tm + HALO, W), F32)],
        compiler_params=_cparams(("parallel",)),
        name="even_out",
    )(attn_o, proj, proj, proj, proj, proj, conv_w, w_out, h)


def _ffn_kernel(h_ref, g_ref, w1_ref, w3_ref, w2_ref, o_ref, xn_ref, acc_ref):
    f = pl.program_id(1)

    @pl.when(f == 0)
    def _():
        xn_ref[...] = _rms(h_ref[...], g_ref[...], RMS_EPS).astype(BF16)
        acc_ref[...] = jnp.zeros(acc_ref.shape, F32)

    xn = xn_ref[...]
    a = jnp.dot(xn, w1_ref[...], preferred_element_type=F32)
    b = jnp.dot(xn, w3_ref[...], preferred_element_type=F32)
    hm = (_silu(a) * b).astype(BF16)
    acc_ref[...] += jnp.dot(hm, w2_ref[...], preferred_element_type=F32)

    @pl.when(f == pl.num_programs(1) - 1)
    def _():
        o_ref[...] = h_ref[...] + acc_ref[...]


def ffn(h, g, w1, w3, w2, layer):
    T, D = h.shape
    F = w1.shape[2]
    tm = min(FFN_ROW_TILE, T)
    tf = FF_TILE
    return pl.pallas_call(
        _ffn_kernel,
        grid=(T // tm, F // tf),
        in_specs=[
            pl.BlockSpec((tm, D), lambda i, f: (i, 0)),
            pl.BlockSpec((1, D), lambda i, f: (0, 0)),
            pl.BlockSpec((None, D, tf), lambda i, f: (layer, 0, f)),
            pl.BlockSpec((None, D, tf), lambda i, f: (layer, 0, f)),
            pl.BlockSpec((None, tf, D), lambda i, f: (layer, f, 0)),
        ],
        out_specs=pl.BlockSpec((tm, D), lambda i, f: (i, 0)),
        out_shape=jax.ShapeDtypeStruct((T, D), F32),
        scratch_shapes=[pltpu.VMEM((tm, D), BF16), pltpu.VMEM((tm, D), F32)],
        compiler_params=_cparams(("parallel", "arbitrary")),
        name="ffn",
    )(h, g.reshape(1, D), w1, w3, w2)


def _ple_math(h, g_ref, wg_ref, p_ref, wp_ref, gf_ref, final_norm):
    xn = _rms(h, g_ref[...], RMS_EPS).astype(BF16)
    gate = _sigmoid(jnp.dot(xn, wg_ref[...], preferred_element_type=F32))
    emb = jnp.dot(p_ref[...].astype(BF16), wp_ref[...], preferred_element_type=F32)
    out = h + gate * emb
    if final_norm:
        out = _rms(out, gf_ref[...], RMS_EPS)
    return out


def _ple_kernel(h_ref, g_ref, wg_ref, p_ref, wp_ref, gf_ref, o_ref, *, final_norm):
    o_ref[...] = _ple_math(h_ref[...], g_ref, wg_ref, p_ref, wp_ref, gf_ref, final_norm)


def ple(h, g, wg, p, wp, g_final, final_norm, layer):
    T, D = h.shape
    tm = min(ROW_TILE, T)
    return pl.pallas_call(
        functools.partial(_ple_kernel, final_norm=final_norm),
        grid=(T // tm,),
        in_specs=[
            pl.BlockSpec((tm, D), lambda i: (i, 0)),
            pl.BlockSpec((1, D), lambda i: (0, 0)),
            pl.BlockSpec((None, D, D), lambda i: (layer, 0, 0)),
            pl.BlockSpec((None, tm, PLE_DIM), lambda i: (layer, i, 0)),
            pl.BlockSpec((None, PLE_DIM, D), lambda i: (layer, 0, 0)),
            pl.BlockSpec((1, D), lambda i: (0, 0)),
        ],
        out_specs=pl.BlockSpec((tm, D), lambda i: (i, 0)),
        out_shape=jax.ShapeDtypeStruct((T, D), F32),
        compiler_params=_cparams(("parallel",)),
        name="ple",
    )(h, g.reshape(1, D), wg, p, wp, g_final.reshape(1, D))


def _dot_hi(a, b):
    return jnp.dot(a, b, preferred_element_type=F32, precision=lax.Precision.HIGHEST)


def _gdn_prep_kernel(q_ref, k_ref, v_ref, qh_ref, kh_ref, vh_ref, ba_ref, cw_ref, alog_ref, dtb_ref,
                     qg_ref, kg_ref, w_ref, u_ref, ai_ref, dl_ref,
                     xs_ref, gc_ref, gl_ref, sg_ref, gct_ref):
    i = pl.program_id(0)
    h = pl.program_id(1)
    R = q_ref.shape[0]
    C = DN_CHUNK
    KC = cw_ref.shape[1]

    def conv_silu(x_ref, halo_ref, j, hp):
        cs = slice(hp * LANES, (hp + 1) * LANES)
        buf = xs_ref.at[3 * hp + j]
        buf[0:HALO, :] = jnp.where(i == 0, 0.0, halo_ref[:, cs].astype(F32))
        buf[HALO:, :] = x_ref[:, cs].astype(F32)
        acc = buf[HALO:, :] * cw_ref[hp, KC - 1:KC, j * LANES:(j + 1) * LANES]
        for t in range(1, KC):
            acc = acc + (buf[HALO - t:HALO - t + R, :]
                         * cw_ref[hp, KC - 1 - t:KC - t, j * LANES:(j + 1) * LANES])
        return _silu(acc)

    def l2n(x):
        return x * lax.rsqrt(jnp.sum(x * x, axis=-1, keepdims=True) + 1e-6)

    chunks = range(R // C)
    rows = [slice(c * C, (c + 1) * C) for c in chunks]
    r64 = lax.broadcasted_iota(I32, (C, C), 0)
    c64 = lax.broadcasted_iota(I32, (C, C), 1)
    incl = r64 >= c64
    strict = r64 > c64

    @pl.when(h == 0)
    def _():
        ba = ba_ref[...]
        sp_in = ba + dtb_ref[...]
        softplus = jnp.maximum(sp_in, 0.0) + jnp.log(1.0 + jnp.exp(-jnp.abs(sp_in)))
        g_all = -jnp.exp(alog_ref[...]) * softplus
        g_wide = jnp.concatenate([g_all[r] for r in rows], axis=1)
        gc_wide = _dot_hi(jnp.where(incl, 1.0, 0.0), g_wide)
        for c in chunks:
            gc_c = gc_wide[:, c * LANES:(c + 1) * LANES]
            gc_ref[rows[c], :] = gc_c
            gl_ref[rows[c], :] = jnp.broadcast_to(gc_c[C - 1:C, :], (C, LANES))
        sg_ref[...] = _sigmoid(ba)
        gct_ref[...] = gc_ref[...].T

    lane = lax.broadcasted_iota(I32, (R, LANES), 1)

    def pick(ref, l):
        return jnp.sum(jnp.where(lane == l, ref[...], 0.0), axis=-1, keepdims=True)

    def bdot(a, b):
        return jnp.dot(a.astype(BF16), b.astype(BF16), preferred_element_type=F32)

    heads = range(qg_ref.shape[1] // LANES)
    items = [(hp, c) for hp in heads for c in chunks]
    decay, kq, rhs, edl = {}, {}, {}, {}
    for hp in heads:
        hh = h * len(heads) + hp
        cs = slice(hp * LANES, (hp + 1) * LANES)
        q = l2n(conv_silu(q_ref, qh_ref, 0, hp)) * (DN_HEAD_DIM ** -0.5)
        k = l2n(conv_silu(k_ref, kh_ref, 1, hp))
        v = conv_silu(v_ref, vh_ref, 2, hp)
        beta = pick(sg_ref, hh)
        gcol = pick(gc_ref, DN_HEADS + hh)
        glast = pick(gl_ref, DN_HEADS + hh)
        grow = gct_ref[pl.ds(DN_HEADS + hh, 1), :]
        eg = jnp.exp(gcol)
        kb = k * beta
        qg_ref[:, cs] = (q * eg).astype(BF16)
        kg_ref[:, cs] = (k * jnp.exp(glast - gcol)).astype(BF16)
        edl[hp] = jnp.broadcast_to(jnp.exp(glast), (R, LANES))
        rhs[hp] = jnp.concatenate([v * beta, kb * eg], axis=-1)
        kbf = k.astype(BF16)
        lhs = jnp.concatenate([kb.astype(BF16).reshape(R // C, C, LANES),
                               q.astype(BF16).reshape(R // C, C, LANES)], axis=1)
        for c in chunks:
            r = rows[c]
            decay[hp, c] = jnp.where(incl, jnp.exp(jnp.where(incl, gcol[r] - grow[:, r], 0.0)), 0.0)
            kq[hp, c] = lax.dot_general(lhs[c], kbf[r], (((1,), (1,)), ((), ())), preferred_element_type=F32)
    pw = {it: -jnp.where(strict, kq[it][:C] * decay[it], 0.0) for it in items}
    n = dict(pw)
    for _ in range(5):
        pw = {it: bdot(pw[it], pw[it]) for it in items}
        n = {it: n[it] + pw[it] + bdot(n[it], pw[it]) for it in items}
    for hp, c in items:
        cs = slice(hp * LANES, (hp + 1) * LANES)
        rc = rhs[hp][rows[c]]
        uw = rc + bdot(n[hp, c], rc)
        u_ref[rows[c], cs] = uw[:, :DN_HEAD_DIM]
        w_ref[rows[c], cs] = uw[:, DN_HEAD_DIM:].astype(BF16)
        ai_ref[hp, rows[c], :] = (kq[hp, c][C:] * decay[hp, c]).astype(BF16)
        dl_ref[hp, c:c + 1, :] = edl[hp][c * C:c * C + 1, :]


def gdn_prep(proj, ba, conv_w, alog_l, dtb_l):
    T = proj.shape[0]
    R = min(GDN_TILE, T)
    H = DN_HEADS
    nchunk = T // DN_CHUNK
    hb = R // HALO
    HP = GDN_PREP_HEADS
    W = HP * LANES
    col = lambda j: pl.BlockSpec((R, W), lambda i, h: (i, j * (H // HP) + h))
    halo = lambda j: pl.BlockSpec((HALO, W), lambda i, h: (jnp.maximum(i * hb - 1, 0), j * (H // HP) + h))
    KC = conv_w.shape[0]
    cw = conv_w.reshape(KC, 3, H, LANES).transpose(2, 0, 1, 3).reshape(H, KC, 3 * LANES)
    cw_spec = pl.BlockSpec((HP, KC, 3 * LANES), lambda i, h: (h, 0, 0))
    row_out = lambda dt: jax.ShapeDtypeStruct((T, DN_WIDTH), dt)
    out_col = pl.BlockSpec((R, W), lambda i, h: (i, h))
    return pl.pallas_call(
        _gdn_prep_kernel,
        grid=(T // R, H // HP),
        in_specs=[col(0), col(1), col(2), halo(0), halo(1), halo(2),
                  pl.BlockSpec((R, LANES), lambda i, h: (i, 0)),
                  cw_spec,
                  pl.BlockSpec((1, LANES), lambda i, h: (0, 0)),
                  pl.BlockSpec((1, LANES), lambda i, h: (0, 0))],
        out_specs=[out_col, out_col, out_col, out_col,
                   pl.BlockSpec((HP, R, DN_CHUNK), lambda i, h: (h, i, 0)),
                   pl.BlockSpec((HP, R // DN_CHUNK, LANES), lambda i, h: (h, i, 0))],
        out_shape=[row_out(BF16), row_out(BF16), row_out(BF16), row_out(F32),
                   jax.ShapeDtypeStruct((H, T, DN_CHUNK), BF16),
                   jax.ShapeDtypeStruct((H, nchunk, LANES), F32)],
        scratch_shapes=[pltpu.VMEM((3 * HP, R + HALO, LANES), F32),
                        pltpu.VMEM((R, LANES), F32), pltpu.VMEM((R, LANES), F32), pltpu.VMEM((R, LANES), F32),
                        pltpu.VMEM((LANES, R), F32)],
        compiler_params=_cparams(("parallel", "arbitrary")),
        name="gdn_prep",
    )(proj, proj, proj, proj, proj, proj, ba, cw, alog_l, dtb_l)


def _gdn_scan_kernel(qg_ref, kg_ref, w_ref, u_ref, ai_ref, dl_ref, o_ref, s_ref):
    @pl.when(pl.program_id(1) == 0)
    def _():
        s_ref[...] = jnp.zeros(s_ref.shape, F32)

    C = DN_CHUNK
    D = DN_HEAD_DIM
    heads = range(s_ref.shape[0])

    def chunk(c, carry):
        rows = pl.ds(pl.multiple_of(c * C, C), C)
        s = [s_ref[hh] for hh in heads]
        sb = [x.astype(BF16) for x in s]
        cols = [slice(hh * D, (hh + 1) * D) for hh in heads]
        r1 = [jnp.dot(jnp.concatenate([w_ref[rows, cols[hh]], qg_ref[rows, cols[hh]]], axis=0), sb[hh],
                      preferred_element_type=F32) for hh in heads]
        vb = [(u_ref[rows, cols[hh]] - r1[hh][:C]).astype(BF16) for hh in heads]
        for hh in heads:
            o_ref[rows, cols[hh]] = r1[hh][C:] + jnp.dot(ai_ref[hh, rows, :], vb[hh], preferred_element_type=F32)
        upd = [lax.dot_general(kg_ref[rows, cols[hh]], vb[hh], (((0,), (0,)), ((), ())),
                               preferred_element_type=F32) for hh in heads]
        for hh in heads:
            s_ref[hh] = s[hh] * dl_ref[hh, pl.ds(c, 1), :] + upd[hh]
        return carry

    lax.fori_loop(0, qg_ref.shape[0] // C, chunk, 0)


def gdn_scan(qg, kg, w, u, ai, dl):
    T = qg.shape[0]
    R = min(GDN_TILE, T)
    H = DN_HEADS
    HB = GDN_SCAN_HEADS
    col = pl.BlockSpec((R, HB * LANES), lambda h, i: (i, h))
    return pl.pallas_call(
        _gdn_scan_kernel,
        grid=(H // HB, T // R),
        in_specs=[col, col, col, col,
                  pl.BlockSpec((HB, R, DN_CHUNK), lambda h, i: (h, i, 0)),
                  pl.BlockSpec((HB, R // DN_CHUNK, LANES), lambda h, i: (h, i, 0))],
        out_specs=col,
        out_shape=jax.ShapeDtypeStruct((T, DN_WIDTH), F32),
        scratch_shapes=[pltpu.VMEM((HB, DN_HEAD_DIM, DN_HEAD_DIM), F32)],
        compiler_params=_cparams(("parallel", "arbitrary")),
        name="gdn_scan",
    )(qg, kg, w, u, ai, dl)


def _odd_out_kernel(o_ref, z_ref, g_ref, w_ref, h_ref, out_ref):
    g = g_ref[...]
    parts = []
    for hd in range(DN_HEADS):
        sl = slice(hd * DN_HEAD_DIM, (hd + 1) * DN_HEAD_DIM)
        parts.append((_rms(o_ref[:, sl], g, RMS_EPS) * _silu(z_ref[:, sl].astype(F32))).astype(BF16))
    y = jnp.concatenate(parts, axis=-1)
    out_ref[...] = h_ref[...] + jnp.dot(y, w_ref[...], preferred_element_type=F32)


def odd_out(o, proj, onorm_g, w_out, h):
    T = h.shape[0]
    tm = min(ROW_TILE, T)
    zb = 3 * DN_WIDTH // DN_WIDTH
    return pl.pallas_call(
        _odd_out_kernel,
        grid=(T // tm,),
        in_specs=[
            pl.BlockSpec((tm, DN_WIDTH), lambda i: (i, 0)),
            pl.BlockSpec((tm, DN_WIDTH), lambda i: (i, zb)),
            pl.BlockSpec((1, DN_HEAD_DIM), lambda i: (0, 0)),
            pl.BlockSpec(w_out.shape, lambda i: (0, 0)),
            pl.BlockSpec((tm, D_MODEL), lambda i: (i, 0)),
        ],
        out_specs=pl.BlockSpec((tm, D_MODEL), lambda i: (i, 0)),
        out_shape=jax.ShapeDtypeStruct((T, D_MODEL), F32),
        compiler_params=_cparams(("parallel",)),
        name="odd_out",
    )(o, proj, onorm_g.reshape(1, DN_HEAD_DIM), w_out, h)


def _route_kernel(h_ref, g_ref, wr_ref, ri_ref, gate_ref, cnt_ref, carry_ref):
    i = pl.program_id(0)

    @pl.when(i == 0)
    def _():
        carry_ref[...] = jnp.zeros(carry_ref.shape, F32)

    xn = _rms(h_ref[...], g_ref[...], RMS_EPS)
    logits = jnp.dot(xn.astype(BF16), wr_ref[...].astype(BF16), preferred_element_type=F32)
    tm = logits.shape[0]
    lane = lax.broadcasted_iota(I32, logits.shape, 1)
    logits = jnp.where(lane < N_EXPERTS, logits, NEG_BIG)
    lane_f = lane.astype(F32)
    m1 = jnp.max(logits, axis=-1, keepdims=True)
    i1 = jnp.min(jnp.where(logits == m1, lane_f, float(LANES)), axis=-1, keepdims=True)
    rest = jnp.where(lane_f == i1, NEG_BIG, logits)
    m2 = jnp.max(rest, axis=-1, keepdims=True)
    i2 = jnp.min(jnp.where(rest == m2, lane_f, float(LANES)), axis=-1, keepdims=True)
    e = jnp.exp(m2 - m1)
    g1 = 1.0 / (1.0 + e)
    g2 = e / (1.0 + e)
    oh1 = lane_f == i1
    oh2 = lane_f == i2
    i1 = i1.astype(I32)
    i2 = i2.astype(I32)
    oh = jnp.where(oh1 | oh2, 1.0, 0.0)
    ri_ = lax.broadcasted_iota(I32, (tm, tm), 0)
    ci_ = lax.broadcasted_iota(I32, (tm, tm), 1)
    below = jnp.where(ri_ > ci_, 1.0, 0.0).astype(BF16)
    ex = jnp.dot(below, oh.astype(BF16), preferred_element_type=F32) + carry_ref[0:1, :]
    r1 = jnp.sum(jnp.where(oh1, ex, 0.0), axis=-1, keepdims=True).astype(I32)
    r2 = jnp.sum(jnp.where(oh2, ex, 0.0), axis=-1, keepdims=True).astype(I32)
    ri_ref[...] = jnp.where(lane == 0, i1, jnp.where(lane == 1, i2, jnp.where(lane == 2, r1, r2)))
    gate_ref[...] = jnp.where(lane == 0, g1, g2)
    carry_ref[...] = carry_ref[...] + jnp.sum(oh, axis=0, keepdims=True)
    cnt_ref[...] = carry_ref[...]


def route(h, g, wr):
    T, D = h.shape
    tm = min(ROW_TILE, T)
    return pl.pallas_call(
        _route_kernel,
        grid=(T // tm,),
        in_specs=[
            pl.BlockSpec((tm, D), lambda i: (i, 0)),
            pl.BlockSpec((1, D), lambda i: (0, 0)),
            pl.BlockSpec((D, LANES), lambda i: (0, 0)),
        ],
        out_specs=[
            pl.BlockSpec((tm, LANES), lambda i: (i, 0)),
            pl.BlockSpec((tm, LANES), lambda i: (i, 0)),
            pl.BlockSpec((SUBLANES, LANES), lambda i: (0, 0)),
        ],
        out_shape=[
            jax.ShapeDtypeStruct((T, LANES), I32),
            jax.ShapeDtypeStruct((T, LANES), F32),
            jax.ShapeDtypeStruct((SUBLANES, LANES), F32),
        ],
        scratch_shapes=[pltpu.VMEM((SUBLANES, LANES), F32)],
        compiler_params=_cparams(("arbitrary",)),
        name="route",
    )(h, g.reshape(1, D), wr)


def _row_copy(src_ref, s, dst_ref, d, sem):
    return pltpu.make_async_copy(src_ref.at[pl.ds(s, 1), :], dst_ref.at[pl.ds(d, 1), :], sem)


def _dispatch_kernel(pz_ref, dest_ref, h_ref, g_ref, xs_ref, xn_ref, zb_ref, sem, zsem):
    tm = h_ref.shape[0]
    G = zb_ref.shape[0]

    @pl.when(pl.program_id(0) == 0)
    def _():
        zb_ref[...] = jnp.zeros(zb_ref.shape, F32)

        def zero_copy(e):
            return pltpu.make_async_copy(zb_ref, xs_ref.at[pl.ds(pl.multiple_of(pz_ref[e], G), G), :], zsem)

        for e in range(2 * N_EXPERTS):
            @pl.when(pz_ref[e] >= 0)
            def _():
                zero_copy(e).start()

        for e in range(2 * N_EXPERTS):
            @pl.when(pz_ref[e] >= 0)
            def _():
                zero_copy(e).wait()

    xn_ref[...] = _rms(h_ref[...], g_ref[...], RMS_EPS)

    def issue(r, carry):
        _row_copy(xn_ref, r, xs_ref, dest_ref[2 * r], sem).start(priority=0)
        _row_copy(xn_ref, r, xs_ref, dest_ref[2 * r + 1], sem).start(priority=1)
        return carry

    lax.fori_loop(0, tm, issue, 0, unroll=ROW_DMA_UNROLL)

    def drain(r, carry):
        _row_copy(xn_ref, 0, xs_ref, 0, sem).wait()
        _row_copy(xn_ref, 0, xs_ref, 0, sem).wait()
        return carry

    lax.fori_loop(0, tm, drain, 0, unroll=ROW_DMA_UNROLL)


def dispatch(h, g, dest_flat, zero_tiles, n_slots):
    T, D = h.shape
    tm = min(ROW_TILE, T)
    grid_spec = pltpu.PrefetchScalarGridSpec(
        num_scalar_prefetch=1,
        grid=(T // tm,),
        in_specs=[
            pl.BlockSpec((2 * tm,), lambda i, pz: (i,), memory_space=pltpu.SMEM),
            pl.BlockSpec((tm, D), lambda i, pz: (i, 0)),
            pl.BlockSpec((1, D), lambda i, pz: (0, 0)),
        ],
        out_specs=pl.BlockSpec(memory_space=pl.ANY),
        scratch_shapes=[pltpu.VMEM((tm, D), F32), pltpu.VMEM((MOE_TILE, D), F32),
                        pltpu.SemaphoreType.DMA, pltpu.SemaphoreType.DMA],
    )
    return pl.pallas_call(
        _dispatch_kernel,
        grid_spec=grid_spec,
        out_shape=jax.ShapeDtypeStruct((n_slots, D), F32),
        compiler_params=_cparams(("arbitrary",)),
        name="moe_dispatch",
    )(zero_tiles, dest_flat, h, g.reshape(1, D))


def _gmm_kernel(te_ref, na_ref, x_ref, w1_ref, w3_ref, w2_ref, y_ref, xb_ref, acc_ref):
    b = pl.program_id(0)
    f = pl.program_id(1)

    @pl.when(b < na_ref[0])
    def _():
        @pl.when(f == 0)
        def _():
            xb_ref[...] = x_ref[...].astype(BF16)
            acc_ref[...] = jnp.zeros(acc_ref.shape, F32)

        xb = xb_ref[...]
        a = jnp.dot(xb, w1_ref[...], preferred_element_type=F32)
        c = jnp.dot(xb, w3_ref[...], preferred_element_type=F32)
        hm = (_silu(a) * c).astype(BF16)
        acc_ref[...] += jnp.dot(hm, w2_ref[...], preferred_element_type=F32)

        @pl.when(f == pl.num_programs(1) - 1)
        def _():
            y_ref[...] = acc_ref[...]

    @pl.when((b >= na_ref[0]) & (f == pl.num_programs(1) - 1))
    def _():
        y_ref[...] = jnp.zeros(y_ref.shape, F32)


def gmm(xs, w1, w3, w2, tile_e, n_active, layer):
    P, D = xs.shape
    G = MOE_TILE
    F = w1.shape[3]
    tf = MOE_FF_TILE
    nf = F // tf

    def row_idx(b, f, te, na):
        return (jnp.minimum(b, na[0] - 1), 0)

    def f_idx(b, f, na):
        return jnp.where(b < na[0], f, nf - 1)

    grid_spec = pltpu.PrefetchScalarGridSpec(
        num_scalar_prefetch=2,
        grid=(P // G, nf),
        in_specs=[
            pl.BlockSpec((G, D), row_idx),
            pl.BlockSpec((None, None, D, tf), lambda b, f, te, na: (layer, te[b], 0, f_idx(b, f, na))),
            pl.BlockSpec((None, None, D, tf), lambda b, f, te, na: (layer, te[b], 0, f_idx(b, f, na))),
            pl.BlockSpec((None, None, tf, D), lambda b, f, te, na: (layer, te[b], f_idx(b, f, na), 0)),
        ],
        out_specs=pl.BlockSpec((G, D), lambda b, f, te, na: (b, 0)),
        scratch_shapes=[pltpu.VMEM((G, D), BF16), pltpu.VMEM((G, D), F32)],
    )
    return pl.pallas_call(
        _gmm_kernel,
        grid_spec=grid_spec,
        out_shape=jax.ShapeDtypeStruct((P, D), F32),
        compiler_params=_cparams(("arbitrary", "arbitrary")),
        name="moe_gmm",
    )(tile_e, n_active, xs, w1, w3, w2)


def _combine_ple_kernel(dest_ref, dnext_ref, h_ref, gate_ref, y_ref, g_ref, wg_ref, p_ref, wp_ref, gf_ref,
                        o_ref, yab_ref, sem, *, final_norm):
    i = pl.program_id(0)
    n = pl.num_programs(0)
    tm = h_ref.shape[0]
    slot = lax.rem(i, 2)

    def gather(d_ref, sl):
        def issue(r, carry):
            _row_copy(y_ref, d_ref[2 * r], yab_ref.at[sl, 0], r, sem.at[sl]).start(priority=0)
            _row_copy(y_ref, d_ref[2 * r + 1], yab_ref.at[sl, 1], r, sem.at[sl]).start(priority=1)
            return carry

        lax.fori_loop(0, tm, issue, 0, unroll=ROW_DMA_UNROLL)

    @pl.when(i == 0)
    def _():
        gather(dest_ref, 0)

    @pl.when(i + 1 < n)
    def _():
        gather(dnext_ref, 1 - slot)

    def drain(r, carry):
        _row_copy(y_ref, 0, yab_ref.at[slot, 0], 0, sem.at[slot]).wait()
        _row_copy(y_ref, 0, yab_ref.at[slot, 1], 0, sem.at[slot]).wait()
        return carry

    lax.fori_loop(0, tm, drain, 0, unroll=ROW_DMA_UNROLL)
    gt = gate_ref[...]
    h = h_ref[...] + gt[:, 0:1] * yab_ref[slot, 0] + gt[:, 1:2] * yab_ref[slot, 1]
    o_ref[...] = _ple_math(h, g_ref, wg_ref, p_ref, wp_ref, gf_ref, final_norm)


def combine_ple(h, gates, y, dest_flat, g, wg, p, wp, g_final, final_norm, layer):
    T, D = h.shape
    tm = min(ROW_TILE, T)
    n = T // tm
    return pl.pallas_call(
        functools.partial(_combine_ple_kernel, final_norm=final_norm),
        grid=(n,),
        in_specs=[
            pl.BlockSpec((2 * tm,), lambda i: (i,), memory_space=pltpu.SMEM),
            pl.BlockSpec((2 * tm,), lambda i: (jnp.minimum(i + 1, n - 1),), memory_space=pltpu.SMEM),
            pl.BlockSpec((tm, D), lambda i: (i, 0)),
            pl.BlockSpec((tm, LANES), lambda i: (i, 0)),
            pl.BlockSpec(memory_space=pl.ANY),
            pl.BlockSpec((1, D), lambda i: (0, 0)),
            pl.BlockSpec((None, D, D), lambda i: (layer, 0, 0)),
            pl.BlockSpec((None, tm, PLE_DIM), lambda i: (layer, i, 0)),
            pl.BlockSpec((None, PLE_DIM, D), lambda i: (layer, 0, 0)),
            pl.BlockSpec((1, D), lambda i: (0, 0)),
        ],
        out_specs=pl.BlockSpec((tm, D), lambda i: (i, 0)),
        out_shape=jax.ShapeDtypeStruct((T, D), F32),
        scratch_shapes=[pltpu.VMEM((2, 2, tm, D), F32), pltpu.SemaphoreType.DMA((2,))],
        compiler_params=_cparams(("arbitrary",)),
        name="moe_combine_ple",
    )(dest_flat, dest_flat, h, gates, y, g.reshape(1, D), wg, p, wp, g_final.reshape(1, D))


def _lambda_init(layer_idx):
    return 0.8 - 0.6 * math.exp(-0.3 * layer_idx)


def _even_layer(h, pos_col, pos_row, invf, ln_mix, w_in, w_out, lam_params, subln_g, conv_w, layer_idx):
    (proj,) = norm_matmul(h, ln_mix, w_in.astype(BF16))
    qk, vt = rope_qkv(proj, pos_col, invf)
    o = diff_attention(qk, vt, pos_col, pos_row, lam_params, subln_g, _lambda_init(layer_idx))
    return even_out(o, proj, conv_w, w_out.astype(BF16), h)


def _odd_mixer(h, ln_mix, w_in, conv_w, a_log, dt_bias, onorm_g, w_out):
    main_w = 4 * DN_WIDTH
    w_main = w_in[:, :main_w].astype(BF16)
    w_ba = jnp.pad(w_in[:, main_w:], ((0, 0), (0, LANES - 2 * DN_HEADS))).astype(BF16)
    proj, ba = norm_matmul(h, ln_mix, w_main, w_side=w_ba)
    pad8 = lambda v: jnp.pad(v.astype(F32), (DN_HEADS, LANES - 2 * DN_HEADS)).reshape(1, LANES)
    qg, kg, w, u, ai, dl = gdn_prep(proj, ba, conv_w, pad8(a_log), pad8(dt_bias))
    o = gdn_scan(qg, kg, w, u, ai, dl)
    return odd_out(o, proj, onorm_g, w_out.astype(BF16), h)


def _moe(h, ln_ffn, w_router, w1, w3, w2, layer):
    T = h.shape[0]
    G = MOE_TILE
    wr = jnp.pad(w_router, ((0, 0), (0, LANES - N_EXPERTS)))
    ri, gates, cnt = route(h, ln_ffn, wr)
    counts = cnt[0, :N_EXPERTS].astype(I32)
    padded = ((counts + G - 1) // G) * G
    pends = jnp.cumsum(padded)
    pstarts = pends - padded
    dest = (jnp.take(pstarts, ri[:, 0:2]) + ri[:, 2:4]).reshape(-1)
    n_tiles = (2 * T) // G + N_EXPERTS
    tile_start = jnp.arange(n_tiles, dtype=I32) * G
    tile_e = jnp.minimum(jnp.sum(pends[None, :] <= tile_start[:, None], axis=1), N_EXPERTS - 1).astype(I32)
    n_active = (pends[-1:] // G).astype(I32)
    tail = pends[-1] + jnp.arange(N_EXPERTS, dtype=I32) * G
    zero_tiles = jnp.concatenate([jnp.where(padded > 0, pends - G, -1),
                                  jnp.where(tail < n_tiles * G, tail, -1)]).astype(I32)
    xs = dispatch(h, ln_ffn, dest, zero_tiles, n_tiles * G)
    y = gmm(xs, w1, w3, w2, tile_e, n_active, layer)
    return gates, y, dest


def kernel(x, p, positions, ln_mix, ln_ffn, ln_ple, ln_final, w_in_even, w_out_even, lam_q1, lam_k1, lam_q2, lam_k2, subln_gain, conv_w_short, w_in_odd, conv_w_qkv, a_log, dt_bias, onorm_gain, w_out_odd, w1_dense, w3_dense, w2_dense, w_router, w1_moe, w3_moe, w2_moe, w_ple_gate, w_ple_proj):
    B, S, D = x.shape
    T = B * S
    depth = p.shape[0]
    h = x.reshape(T, D)
    pos_col = positions.reshape(T, 1).astype(I32)
    pos_row = positions.reshape(1, T).astype(I32)
    inv_freq = ROPE_THETA ** (-jnp.arange(0, ROT_DIM, 2, dtype=F32) / ROT_DIM)
    invf = jnp.tile(inv_freq, LANES // (ROT_DIM // 2)).reshape(1, LANES)
    w1d, w3d, w2d = (w.astype(BF16) for w in (w1_dense, w3_dense, w2_dense))
    w1m, w3m, w2m = (w.astype(BF16) for w in (w1_moe, w3_moe, w2_moe))
    wpg, wpp = w_ple_gate.astype(BF16), w_ple_proj.astype(BF16)
    p_rows = p.reshape(depth, T, PLE_DIM)
    for i in range(depth):
        j = i // 2
        if i % 2 == 0:
            lam_params = jnp.stack([lam_q1[j], lam_k1[j], lam_q2[j], lam_k2[j]]).astype(F32)
            h = _even_layer(h, pos_col, pos_row, invf, ln_mix[i], w_in_even[j], w_out_even[j], lam_params,
                            subln_gain[j], conv_w_short[j], i)
            h = ffn(h, ln_ffn[i], w1d, w3d, w2d, j)
            h = ple(h, ln_ple[i], wpg, p_rows, wpp, ln_final, final_norm=(i == depth - 1), layer=i)
        else:
            h = _odd_mixer(h, ln_mix[i], w_in_odd[j], conv_w_qkv[j], a_log[j], dt_bias[j], onorm_gain[j],
                           w_out_odd[j])
            gates, y, dest = _moe(h, ln_ffn[i], w_router[j], w1m, w3m, w2m, j)
            h = combine_ple(h, gates, y, dest, ln_ple[i], wpg, p_rows, wpp, ln_final,
                            final_norm=(i == depth - 1), layer=i)
    return h.reshape(B, S, D)
```

```python
import functools
import math

import jax
import jax.numpy as jnp
import numpy as np
from jax import lax
from jax.experimental import pallas as pl
from jax.experimental.pallas import tpu as pltpu

F32 = jnp.float32
BF16 = jnp.bfloat16
I32 = jnp.int32

D_MODEL = 1024
DEPTH = 4
RMS_EPS = 1e-6
DA_HEADS = 4
DA_HEAD_DIM = 64
DA_V_DIM = 2 * DA_HEAD_DIM
ROPE_THETA = 500000.0
ROT_DIM = DA_HEAD_DIM // 4
SUBLN_EPS = 1e-5
SC_WIDTH = 512
DN_HEADS = 8
DN_HEAD_DIM = 128
DN_WIDTH = DN_HEADS * DN_HEAD_DIM
DN_CHUNK = 64
D_FF = 3584
N_EXPERTS = 8
PLE_DIM = 256

LANES = 128
SUBLANES = 8
HALO = 16
VMEM_LIMIT = 52 * 1024 * 1024

NEG_BIG = -1e30

ROW_TILE = 512
PROJ_COL_TILE = 1024
FFN_ROW_TILE = 512
FF_TILE = 1792
MOE_FF_TILE = 1792
ATT_TQ = 1024
ATT_TK = 1024
ATT_EXP_ROWS = 128
ATT_TQ_SPLIT = 256
GDN_TILE = 1024
GDN_PREP_HEADS = 2
GDN_SCAN_HEADS = 8
MOE_TILE = 512
ROW_DMA_UNROLL = 8


def _cparams(sem):
    return pltpu.CompilerParams(dimension_semantics=sem, vmem_limit_bytes=VMEM_LIMIT)


def _rms(x, g, eps):
    ms = jnp.mean(x * x, axis=-1, keepdims=True)
    return x * lax.rsqrt(ms + eps) * g


def _silu(x):
    return x * (1.0 / (1.0 + jnp.exp(-x)))


def _sigmoid(x):
    return 1.0 / (1.0 + jnp.exp(-x))


def _norm_matmul_kernel(x_ref, g_ref, w_ref, *rest):
    xn = _rms(x_ref[...], g_ref[...], RMS_EPS).astype(BF16)
    if len(rest) == 1:
        (o_ref,) = rest
    else:
        ws_ref, o_ref, os_ref = rest
        os_ref[...] = jnp.dot(xn, ws_ref[...], preferred_element_type=F32)
    tn = PROJ_COL_TILE
    for c in range(o_ref.shape[1] // tn):
        o_ref[:, c * tn:(c + 1) * tn] = jnp.dot(
            xn, w_ref[:, c * tn:(c + 1) * tn], preferred_element_type=F32).astype(o_ref.dtype)


def norm_matmul(x, g, w, w_side=None, out_dtype=BF16):
    T, D = x.shape
    N = w.shape[1]
    tm = min(ROW_TILE, T)
    in_specs = [
        pl.BlockSpec((tm, D), lambda i: (i, 0)),
        pl.BlockSpec((1, D), lambda i: (0, 0)),
        pl.BlockSpec((D, N), lambda i: (0, 0)),
    ]
    out_specs = [pl.BlockSpec((tm, N), lambda i: (i, 0))]
    out_shape = [jax.ShapeDtypeStruct((T, N), out_dtype)]
    args = [x, g.reshape(1, D), w]
    if w_side is not None:
        ns = w_side.shape[1]
        in_specs.append(pl.BlockSpec((D, ns), lambda i: (0, 0)))
        out_specs.append(pl.BlockSpec((tm, ns), lambda i: (i, 0)))
        out_shape.append(jax.ShapeDtypeStruct((T, ns), F32))
        args.append(w_side)
    return pl.pallas_call(
        _norm_matmul_kernel,
        grid=(T // tm,),
        in_specs=in_specs,
        out_specs=out_specs,
        out_shape=out_shape,
        compiler_params=_cparams(("parallel",)),
        name="norm_matmul",
    )(*args)


def _even_proj_kernel(x_ref, g_ref, w_ref, pos_ref, invf_ref, qk_ref, vt_ref, rest_ref):
    xn = _rms(x_ref[...], g_ref[...], RMS_EPS).astype(BF16)
    WQK = qk_ref.shape[1]
    WV = vt_ref.shape[0]

    def proj(c0, c1):
        return jnp.dot(xn, w_ref[:, c0:c1], preferred_element_type=F32)

    pos = pos_ref[...].astype(F32)
    ang = pos * invf_ref[...]
    cos_t = jnp.cos(ang)
    sin_t = jnp.sin(ang)
    d = lax.broadcasted_iota(I32, ang.shape, 1) & (DA_HEAD_DIM - 1)
    half = ROT_DIM // 2
    c_mul = jnp.where(d < ROT_DIM, cos_t, 1.0)
    s_mul = jnp.where(d < half, -sin_t, jnp.where(d < ROT_DIM, sin_t, 0.0))
    qk = proj(0, WQK)
    n_q = WQK // (2 * LANES)
    for c in range(WQK // LANES):
        x = qk[:, c * LANES:(c + 1) * LANES]
        swapped = jnp.where(d < half, pltpu.roll(x, LANES - half, 1), pltpu.roll(x, half, 1))
        r = x * c_mul + swapped * s_mul
        if c < n_q:
            r = r * (DA_HEAD_DIM ** -0.5 * math.log2(math.e))
        qk_ref[:, c * LANES:(c + 1) * LANES] = r.astype(BF16)
    v = proj(WQK, WQK + WV)
    for hd in range(DA_HEADS):
        vt_ref[hd * DA_V_DIM:(hd + 1) * DA_V_DIM, :] = v[:, hd * DA_V_DIM:(hd + 1) * DA_V_DIM].T.astype(BF16)
    WR = rest_ref.shape[1]
    for c0 in range(0, WR, PROJ_COL_TILE):
        c1 = min(c0 + PROJ_COL_TILE, WR)
        rest_ref[:, c0:c1] = proj(WQK + WV + c0, WQK + WV + c1).astype(BF16)


def even_proj(x, g, w, positions_col, invf):
    T, D = x.shape
    WQK = 2 * 2 * DA_HEADS * DA_HEAD_DIM
    WV = DA_HEADS * DA_V_DIM
    WR = w.shape[1] - WQK - WV
    tm = min(ROW_TILE, T)
    return pl.pallas_call(
        _even_proj_kernel,
        grid=(T // tm,),
        in_specs=[
            pl.BlockSpec((tm, D), lambda i: (i, 0)),
            pl.BlockSpec((1, D), lambda i: (0, 0)),
            pl.BlockSpec(w.shape, lambda i: (0, 0)),
            pl.BlockSpec((tm, 1), lambda i: (i, 0)),
            pl.BlockSpec((1, LANES), lambda i: (0, 0)),
        ],
        out_specs=[pl.BlockSpec((tm, WQK), lambda i: (i, 0)),
                   pl.BlockSpec((WV, tm), lambda i: (0, i)),
                   pl.BlockSpec((tm, WR), lambda i: (i, 0))],
        out_shape=[jax.ShapeDtypeStruct((T, WQK), BF16), jax.ShapeDtypeStruct((WV, T), BF16),
                   jax.ShapeDtypeStruct((T, WR), BF16)],
        compiler_params=_cparams(("parallel",)),
        name="even_proj",
    )(x, g.reshape(1, D), w, positions_col, invf)


def _attn_kernel(qt_ref, kt_ref, q_ref, k_ref, v_ref, pq_ref, pk_ref, lam_ref, g_ref, o_ref,
                 m_ref, l_ref, acc_ref, *, tq, tk, lam0):
    p = pl.program_id(1)
    qi = qt_ref[p]
    ki = kt_ref[p]

    @pl.when(ki == 0)
    def _():
        m_ref[...] = jnp.full(m_ref.shape, NEG_BIG, F32)
        l_ref[...] = jnp.zeros(l_ref.shape, F32)
        acc_ref[...] = jnp.zeros(acc_ref.shape, F32)

    tqs = min(tq, ATT_TQ_SPLIT)
    items = [(s, j) for s in range(2) for j in range(tq // tqs)]

    def step(masked):
        q = q_ref[...]
        k = k_ref[...]
        v = v_ref[...]
        def nkeys(j):
            return (j + 1) * tqs if (masked and tq == tk) else tk

        st = {}
        for s, j in items:
            cols = slice(j * tqs, (j + 1) * tqs)
            qs = q[j * tqs:(j + 1) * tqs, s * DA_HEAD_DIM:(s + 1) * DA_HEAD_DIM]
            ks = k[:nkeys(j), s * DA_HEAD_DIM:(s + 1) * DA_HEAD_DIM]
            sc = lax.dot_general(ks, qs, (((1,), (1,)), ((), ())), preferred_element_type=F32)
            if masked:
                sc = jnp.where(pk_ref[0:nkeys(j), :] <= pq_ref[:, cols], sc, NEG_BIG)
            st[s, j] = sc
        for s, j in items:
            cols = slice(j * tqs, (j + 1) * tqs)
            sc = st[s, j]
            nk = nkeys(j)
            m_prev = m_ref[s, :, cols]
            m_new = jnp.maximum(m_prev, jnp.max(sc, axis=0, keepdims=True))
            alpha = jnp.exp2(m_prev - m_new)
            psum = jnp.zeros((1, tqs), F32)
            pchunks = []
            for r0 in range(0, nk, ATT_EXP_ROWS):
                pc = jnp.exp2(sc[r0:r0 + ATT_EXP_ROWS] - m_new)
                psum = psum + jnp.sum(pc, axis=0, keepdims=True)
                pchunks.append(pc.astype(BF16))
            pb = jnp.concatenate(pchunks, axis=0)
            l_ref[s, :, cols] = alpha * l_ref[s, :, cols] + psum
            pv = jnp.dot(v[:, :nk], pb, preferred_element_type=F32)
            acc_ref[s, :, cols] = alpha * acc_ref[s, :, cols] + pv
            m_ref[s, :, cols] = m_new

    crosses = (ki + 1) * tk - 1 > qi * tq

    @pl.when(crosses)
    def _():
        step(True)

    @pl.when(jnp.logical_not(crosses))
    def _():
        step(False)

    @pl.when((ki + 1) * tk >= (qi + 1) * tq)
    def _():
        lm = lam_ref[...]
        s1 = jnp.sum(lm[0:1] * lm[1:2], axis=-1, keepdims=True)
        s2 = jnp.sum(lm[2:3] * lm[3:4], axis=-1, keepdims=True)
        lam = jnp.exp(s1) - jnp.exp(s2) + lam0
        ot = acc_ref[0] * (1.0 / l_ref[0]) - lam * (acc_ref[1] * (1.0 / l_ref[1]))
        ms = jnp.mean(ot * ot, axis=0, keepdims=True)
        ot = ot * lax.rsqrt(ms + SUBLN_EPS) * g_ref[...] * (1.0 - lam0)
        o_ref[...] = ot.T.astype(o_ref.dtype)


def diff_attention(qk, vt, pos_col, pos_row, lam_params, subln_g, lam0):
    T = qk.shape[0]
    tq = min(ATT_TQ, T)
    tk = min(ATT_TK, T)
    nq = T // tq
    pairs = [(qi, ki) for qi in range(nq) for ki in range(-(-((qi + 1) * tq) // tk))]
    qt = jnp.asarray(np.array([a for a, _ in pairs], np.int32))
    kt = jnp.asarray(np.array([b for _, b in pairs], np.int32))
    H = DA_HEADS
    kern = functools.partial(_attn_kernel, tq=tq, tk=tk, lam0=lam0)
    grid_spec = pltpu.PrefetchScalarGridSpec(
        num_scalar_prefetch=2,
        grid=(H, len(pairs)),
        in_specs=[
            pl.BlockSpec((tq, LANES), lambda h, p, qt, kt: (qt[p], h)),
            pl.BlockSpec((tk, LANES), lambda h, p, qt, kt: (kt[p], H + h)),
            pl.BlockSpec((DA_V_DIM, tk), lambda h, p, qt, kt: (h, kt[p])),
            pl.BlockSpec((1, tq), lambda h, p, qt, kt: (0, qt[p])),
            pl.BlockSpec((tk, 1), lambda h, p, qt, kt: (kt[p], 0)),
            pl.BlockSpec((4, DA_HEAD_DIM), lambda h, p, qt, kt: (0, 0)),
            pl.BlockSpec((DA_V_DIM, 1), lambda h, p, qt, kt: (0, 0)),
        ],
        out_specs=pl.BlockSpec((tq, DA_V_DIM), lambda h, p, qt, kt: (qt[p], h)),
        scratch_shapes=[
            pltpu.VMEM((2, 1, tq), F32),
            pltpu.VMEM((2, 1, tq), F32),
            pltpu.VMEM((2, DA_V_DIM, tq), F32),
        ],
    )
    return pl.pallas_call(
        kern,
        grid_spec=grid_spec,
        out_shape=jax.ShapeDtypeStruct((T, H * DA_V_DIM), BF16),
        compiler_params=_cparams(("parallel", "arbitrary")),
        name="diff_attention",
    )(qt, kt, qk, qk, vt, pos_row, pos_col, lam_params, subln_g.reshape(DA_V_DIM, 1))


def _even_out_kernel(o_ref, b_ref, c_ref, x_ref, ch_ref, xh_ref, cw_ref, w_ref, h_ref, out_ref, u_ref):
    tm = o_ref.shape[0]
    u_prev = ch_ref[...].astype(F32) * xh_ref[...].astype(F32)
    u_prev = jnp.where(pl.program_id(0) == 0, 0.0, u_prev)
    u_ref[0:HALO, :] = u_prev
    u_ref[HALO:, :] = c_ref[...].astype(F32) * x_ref[...].astype(F32)
    cw = cw_ref[...]
    conv = (u_ref[HALO - 2:HALO - 2 + tm, :] * cw[0:1]
            + u_ref[HALO - 1:HALO - 1 + tm, :] * cw[1:2]
            + u_ref[HALO:, :] * cw[2:3])
    sc = (b_ref[...].astype(F32) * conv).astype(BF16)
    na = o_ref.shape[1]
    acc = jnp.dot(o_ref[...], w_ref[0:na, :], preferred_element_type=F32)
    acc = acc + jnp.dot(sc, w_ref[na:, :], preferred_element_type=F32)
    out_ref[...] = h_ref[...] + acc


def even_out(attn_o, proj, conv_w, w_out, h):
    T = h.shape[0]
    tm = min(ROW_TILE, T)
    W = SC_WIDTH
    cb = (proj.shape[1] - 3 * W) // W
    hb = tm // HALO
    halo = lambda col: pl.BlockSpec((HALO, W), lambda i: (jnp.maximum(i * hb - 1, 0), col))
    return pl.pallas_call(
        _even_out_kernel,
        grid=(T // tm,),
        in_specs=[
            pl.BlockSpec((tm, attn_o.shape[1]), lambda i: (i, 0)),
            pl.BlockSpec((tm, W), lambda i: (i, cb)),
            pl.BlockSpec((tm, W), lambda i: (i, cb + 1)),
            pl.BlockSpec((tm, W), lambda i: (i, cb + 2)),
            halo(cb + 1),
            halo(cb + 2),
            pl.BlockSpec(conv_w.shape, lambda i: (0, 0)),
            pl.BlockSpec(w_out.shape, lambda i: (0, 0)),
            pl.BlockSpec((tm, D_MODEL), lambda i: (i, 0)),
        ],
        out_specs=pl.BlockSpec((tm, D_MODEL), lambda i: (i, 0)),
        out_shape=jax.ShapeDtypeStruct((T, D_MODEL), F32),
        scratch_shapes=[pltpu.VMEM((tm + HALO, W), F32)],
        compiler_params=_cparams(("parallel",)),
        name="even_out",
    )(attn_o, proj, proj, proj, proj, proj, conv_w, w_out, h)


def _ffn_kernel(h_ref, g_ref, w1_ref, w3_ref, w2_ref, gp_ref, wg_ref, p_ref, wp_ref, gf_ref, o_ref, xn_ref, acc_ref,
                *, final_norm):
    f = pl.program_id(1)

    @pl.when(f == 0)
    def _():
        xn_ref[...] = _rms(h_ref[...], g_ref[...], RMS_EPS).astype(BF16)
        acc_ref[...] = jnp.zeros(acc_ref.shape, F32)

    xn = xn_ref[...]
    a = jnp.dot(xn, w1_ref[...], preferred_element_type=F32)
    b = jnp.dot(xn, w3_ref[...], preferred_element_type=F32)
    hm = (_silu(a) * b).astype(BF16)
    acc_ref[...] += jnp.dot(hm, w2_ref[...], preferred_element_type=F32)

    @pl.when(f == pl.num_programs(1) - 1)
    def _():
        o_ref[...] = _ple_math(h_ref[...] + acc_ref[...], gp_ref, wg_ref, p_ref, wp_ref, gf_ref, final_norm)


def ffn_ple(h, g, w1, w3, w2, layer, g_ple, wg, p, wp, g_final, final_norm, ple_layer):
    T, D = h.shape
    F = w1.shape[2]
    tm = min(FFN_ROW_TILE, T)
    tf = FF_TILE
    return pl.pallas_call(
        functools.partial(_ffn_kernel, final_norm=final_norm),
        grid=(T // tm, F // tf),
        in_specs=[
            pl.BlockSpec((tm, D), lambda i, f: (i, 0)),
            pl.BlockSpec((1, D), lambda i, f: (0, 0)),
            pl.BlockSpec((None, D, tf), lambda i, f: (layer, 0, f)),
            pl.BlockSpec((None, D, tf), lambda i, f: (layer, 0, f)),
            pl.BlockSpec((None, tf, D), lambda i, f: (layer, f, 0)),
            pl.BlockSpec((1, D), lambda i, f: (0, 0)),
            pl.BlockSpec((None, D, D), lambda i, f: (ple_layer, 0, 0)),
            pl.BlockSpec((None, tm, PLE_DIM), lambda i, f: (ple_layer, i, 0)),
            pl.BlockSpec((None, PLE_DIM, D), lambda i, f: (ple_layer, 0, 0)),
            pl.BlockSpec((1, D), lambda i, f: (0, 0)),
        ],
        out_specs=pl.BlockSpec((tm, D), lambda i, f: (i, 0)),
        out_shape=jax.ShapeDtypeStruct((T, D), F32),
        scratch_shapes=[pltpu.VMEM((tm, D), BF16), pltpu.VMEM((tm, D), F32)],
        compiler_params=_cparams(("parallel", "arbitrary")),
        name="ffn_ple",
    )(h, g.reshape(1, D), w1, w3, w2, g_ple.reshape(1, D), wg, p, wp, g_final.reshape(1, D))


def _ple_math(h, g_ref, wg_ref, p_ref, wp_ref, gf_ref, final_norm):
    xn = _rms(h, g_ref[...], RMS_EPS).astype(BF16)
    gate = _sigmoid(jnp.dot(xn, wg_ref[...], preferred_element_type=F32))
    emb = jnp.dot(p_ref[...].astype(BF16), wp_ref[...], preferred_element_type=F32)
    out = h + gate * emb
    if final_norm:
        out = _rms(out, gf_ref[...], RMS_EPS)
    return out


def _dot_hi(a, b):
    return jnp.dot(a, b, preferred_element_type=F32, precision=lax.Precision.HIGHEST)


def _gdn_prep_kernel(q_ref, k_ref, v_ref, qh_ref, kh_ref, vh_ref, ba_ref, cw_ref, alog_ref, dtb_ref,
                     qg_ref, kg_ref, w_ref, u_ref, ai_ref, dl_ref,
                     xs_ref, gc_ref, gl_ref, sg_ref, gct_ref):
    i = pl.program_id(0)
    h = pl.program_id(1)
    R = q_ref.shape[0]
    C = DN_CHUNK
    KC = cw_ref.shape[1]

    def conv_silu(x_ref, halo_ref, j, hp):
        cs = slice(hp * LANES, (hp + 1) * LANES)
        buf = xs_ref.at[3 * hp + j]
        buf[0:HALO, :] = jnp.where(i == 0, 0.0, halo_ref[:, cs].astype(F32))
        buf[HALO:, :] = x_ref[:, cs].astype(F32)
        acc = buf[HALO:, :] * cw_ref[hp, KC - 1:KC, j * LANES:(j + 1) * LANES]
        for t in range(1, KC):
            acc = acc + (buf[HALO - t:HALO - t + R, :]
                         * cw_ref[hp, KC - 1 - t:KC - t, j * LANES:(j + 1) * LANES])
        return _silu(acc)

    def l2n(x):
        return x * lax.rsqrt(jnp.sum(x * x, axis=-1, keepdims=True) + 1e-6)

    chunks = range(R // C)
    rows = [slice(c * C, (c + 1) * C) for c in chunks]
    r64 = lax.broadcasted_iota(I32, (C, C), 0)
    c64 = lax.broadcasted_iota(I32, (C, C), 1)
    incl = r64 >= c64
    strict = r64 > c64

    @pl.when(h == 0)
    def _():
        ba = ba_ref[...]
        sp_in = ba + dtb_ref[...]
        softplus = jnp.maximum(sp_in, 0.0) + jnp.log(1.0 + jnp.exp(-jnp.abs(sp_in)))
        g_all = -jnp.exp(alog_ref[...]) * softplus
        g_wide = jnp.concatenate([g_all[r] for r in rows], axis=1)
        gc_wide = _dot_hi(jnp.where(incl, 1.0, 0.0), g_wide)
        for c in chunks:
            gc_c = gc_wide[:, c * LANES:(c + 1) * LANES]
            gc_ref[rows[c], :] = gc_c
            gl_ref[rows[c], :] = jnp.broadcast_to(gc_c[C - 1:C, :], (C, LANES))
        sg_ref[...] = _sigmoid(ba)
        gct_ref[...] = gc_ref[...].T

    lane = lax.broadcasted_iota(I32, (R, LANES), 1)

    def pick(ref, l):
        return jnp.sum(jnp.where(lane == l, ref[...], 0.0), axis=-1, keepdims=True)

    def bdot(a, b):
        return jnp.dot(a.astype(BF16), b.astype(BF16), preferred_element_type=F32)

    heads = range(qg_ref.shape[1] // LANES)
    items = [(hp, c) for hp in heads for c in chunks]
    decay, kq, rhs, edl = {}, {}, {}, {}
    for hp in heads:
        hh = h * len(heads) + hp
        cs = slice(hp * LANES, (hp + 1) * LANES)
        q = l2n(conv_silu(q_ref, qh_ref, 0, hp)) * (DN_HEAD_DIM ** -0.5)
        k = l2n(conv_silu(k_ref, kh_ref, 1, hp))
        v = conv_silu(v_ref, vh_ref, 2, hp)
        beta = pick(sg_ref, hh)
        gcol = pick(gc_ref, DN_HEADS + hh)
        glast = pick(gl_ref, DN_HEADS + hh)
        grow = gct_ref[pl.ds(DN_HEADS + hh, 1), :]
        eg = jnp.exp(gcol)
        kb = k * beta
        qg_ref[:, cs] = (q * eg).astype(BF16)
        kg_ref[:, cs] = (k * jnp.exp(glast - gcol)).astype(BF16)
        edl[hp] = jnp.broadcast_to(jnp.exp(glast), (R, LANES))
        rhs[hp] = jnp.concatenate([v * beta, kb * eg], axis=-1)
        kbf = k.astype(BF16)
        lhs = jnp.concatenate([kb.astype(BF16).reshape(R // C, C, LANES),
                               q.astype(BF16).reshape(R // C, C, LANES)], axis=1)
        for c in chunks:
            r = rows[c]
            decay[hp, c] = jnp.where(incl, jnp.exp(jnp.where(incl, gcol[r] - grow[:, r], 0.0)), 0.0)
            kq[hp, c] = lax.dot_general(lhs[c], kbf[r], (((1,), (1,)), ((), ())), preferred_element_type=F32)
    pw = {it: -jnp.where(strict, kq[it][:C] * decay[it], 0.0) for it in items}
    n = dict(pw)
    for _ in range(5):
        pw = {it: bdot(pw[it], pw[it]) for it in items}
        n = {it: n[it] + pw[it] + bdot(n[it], pw[it]) for it in items}
    for hp, c in items:
        cs = slice(hp * LANES, (hp + 1) * LANES)
        rc = rhs[hp][rows[c]]
        uw = rc + bdot(n[hp, c], rc)
        u_ref[rows[c], cs] = uw[:, :DN_HEAD_DIM]
        w_ref[rows[c], cs] = uw[:, DN_HEAD_DIM:].astype(BF16)
        ai_ref[hp, rows[c], :] = (kq[hp, c][C:] * decay[hp, c]).astype(BF16)
        dl_ref[hp, c:c + 1, :] = edl[hp][c * C:c * C + 1, :]


def gdn_prep(proj, ba, conv_w, alog_l, dtb_l):
    T = proj.shape[0]
    R = min(GDN_TILE, T)
    H = DN_HEADS
    nchunk = T // DN_CHUNK
    hb = R // HALO
    HP = GDN_PREP_HEADS
    W = HP * LANES
    col = lambda j: pl.BlockSpec((R, W), lambda i, h: (i, j * (H // HP) + h))
    halo = lambda j: pl.BlockSpec((HALO, W), lambda i, h: (jnp.maximum(i * hb - 1, 0), j * (H // HP) + h))
    KC = conv_w.shape[0]
    cw = conv_w.reshape(KC, 3, H, LANES).transpose(2, 0, 1, 3).reshape(H, KC, 3 * LANES)
    cw_spec = pl.BlockSpec((HP, KC, 3 * LANES), lambda i, h: (h, 0, 0))
    row_out = lambda dt: jax.ShapeDtypeStruct((T, DN_WIDTH), dt)
    out_col = pl.BlockSpec((R, W), lambda i, h: (i, h))
    return pl.pallas_call(
        _gdn_prep_kernel,
        grid=(T // R, H // HP),
        in_specs=[col(0), col(1), col(2), halo(0), halo(1), halo(2),
                  pl.BlockSpec((R, LANES), lambda i, h: (i, 0)),
                  cw_spec,
                  pl.BlockSpec((1, LANES), lambda i, h: (0, 0)),
                  pl.BlockSpec((1, LANES), lambda i, h: (0, 0))],
        out_specs=[out_col, out_col, out_col, out_col,
                   pl.BlockSpec((HP, R, DN_CHUNK), lambda i, h: (h, i, 0)),
                   pl.BlockSpec((HP, R // DN_CHUNK, LANES), lambda i, h: (h, i, 0))],
        out_shape=[row_out(BF16), row_out(BF16), row_out(BF16), row_out(F32),
                   jax.ShapeDtypeStruct((H, T, DN_CHUNK), BF16),
                   jax.ShapeDtypeStruct((H, nchunk, LANES), F32)],
        scratch_shapes=[pltpu.VMEM((3 * HP, R + HALO, LANES), F32),
                        pltpu.VMEM((R, LANES), F32), pltpu.VMEM((R, LANES), F32), pltpu.VMEM((R, LANES), F32),
                        pltpu.VMEM((LANES, R), F32)],
        compiler_params=_cparams(("parallel", "arbitrary")),
        name="gdn_prep",
    )(proj, proj, proj, proj, proj, proj, ba, cw, alog_l, dtb_l)


def _gdn_scan_kernel(qg_ref, kg_ref, w_ref, u_ref, ai_ref, dl_ref, o_ref, s_ref):
    @pl.when(pl.program_id(1) == 0)
    def _():
        s_ref[...] = jnp.zeros(s_ref.shape, F32)

    C = DN_CHUNK
    D = DN_HEAD_DIM
    heads = range(s_ref.shape[0])

    def chunk(c, carry):
        rows = pl.ds(pl.multiple_of(c * C, C), C)
        s = [s_ref[hh] for hh in heads]
        sb = [x.astype(BF16) for x in s]
        cols = [slice(hh * D, (hh + 1) * D) for hh in heads]
        r1 = [jnp.dot(jnp.concatenate([w_ref[rows, cols[hh]], qg_ref[rows, cols[hh]]], axis=0), sb[hh],
                      preferred_element_type=F32) for hh in heads]
        vb = [(u_ref[rows, cols[hh]] - r1[hh][:C]).astype(BF16) for hh in heads]
        for hh in heads:
            o_ref[rows, cols[hh]] = r1[hh][C:] + jnp.dot(ai_ref[hh, rows, :], vb[hh], preferred_element_type=F32)
        upd = [lax.dot_general(kg_ref[rows, cols[hh]], vb[hh], (((0,), (0,)), ((), ())),
                               preferred_element_type=F32) for hh in heads]
        for hh in heads:
            s_ref[hh] = s[hh] * dl_ref[hh, pl.ds(c, 1), :] + upd[hh]
        return carry

    lax.fori_loop(0, qg_ref.shape[0] // C, chunk, 0)


def gdn_scan(qg, kg, w, u, ai, dl):
    T = qg.shape[0]
    R = min(GDN_TILE, T)
    H = DN_HEADS
    HB = GDN_SCAN_HEADS
    col = pl.BlockSpec((R, HB * LANES), lambda h, i: (i, h))
    return pl.pallas_call(
        _gdn_scan_kernel,
        grid=(H // HB, T // R),
        in_specs=[col, col, col, col,
                  pl.BlockSpec((HB, R, DN_CHUNK), lambda h, i: (h, i, 0)),
                  pl.BlockSpec((HB, R // DN_CHUNK, LANES), lambda h, i: (h, i, 0))],
        out_specs=col,
        out_shape=jax.ShapeDtypeStruct((T, DN_WIDTH), F32),
        scratch_shapes=[pltpu.VMEM((HB, DN_HEAD_DIM, DN_HEAD_DIM), F32)],
        compiler_params=_cparams(("parallel", "arbitrary")),
        name="gdn_scan",
    )(qg, kg, w, u, ai, dl)


def _odd_out_kernel(o_ref, z_ref, g_ref, w_ref, h_ref, out_ref):
    g = g_ref[...]
    parts = []
    for hd in range(DN_HEADS):
        sl = slice(hd * DN_HEAD_DIM, (hd + 1) * DN_HEAD_DIM)
        parts.append((_rms(o_ref[:, sl], g, RMS_EPS) * _silu(z_ref[:, sl].astype(F32))).astype(BF16))
    y = jnp.concatenate(parts, axis=-1)
    out_ref[...] = h_ref[...] + jnp.dot(y, w_ref[...], preferred_element_type=F32)


def odd_out(o, proj, onorm_g, w_out, h):
    T = h.shape[0]
    tm = min(ROW_TILE, T)
    zb = 3 * DN_WIDTH // DN_WIDTH
    return pl.pallas_call(
        _odd_out_kernel,
        grid=(T // tm,),
        in_specs=[
            pl.BlockSpec((tm, DN_WIDTH), lambda i: (i, 0)),
            pl.BlockSpec((tm, DN_WIDTH), lambda i: (i, zb)),
            pl.BlockSpec((1, DN_HEAD_DIM), lambda i: (0, 0)),
            pl.BlockSpec(w_out.shape, lambda i: (0, 0)),
            pl.BlockSpec((tm, D_MODEL), lambda i: (i, 0)),
        ],
        out_specs=pl.BlockSpec((tm, D_MODEL), lambda i: (i, 0)),
        out_shape=jax.ShapeDtypeStruct((T, D_MODEL), F32),
        compiler_params=_cparams(("parallel",)),
        name="odd_out",
    )(o, proj, onorm_g.reshape(1, DN_HEAD_DIM), w_out, h)


def _route_kernel(h_ref, g_ref, wr_ref, ri_ref, gate_ref, cnt_ref, carry_ref):
    i = pl.program_id(0)

    @pl.when(i == 0)
    def _():
        carry_ref[...] = jnp.zeros(carry_ref.shape, F32)

    xn = _rms(h_ref[...], g_ref[...], RMS_EPS)
    logits = jnp.dot(xn.astype(BF16), wr_ref[...].astype(BF16), preferred_element_type=F32)
    tm = logits.shape[0]
    lane = lax.broadcasted_iota(I32, logits.shape, 1)
    logits = jnp.where(lane < N_EXPERTS, logits, NEG_BIG)
    lane_f = lane.astype(F32)
    m1 = jnp.max(logits, axis=-1, keepdims=True)
    i1 = jnp.min(jnp.where(logits == m1, lane_f, float(LANES)), axis=-1, keepdims=True)
    rest = jnp.where(lane_f == i1, NEG_BIG, logits)
    m2 = jnp.max(rest, axis=-1, keepdims=True)
    i2 = jnp.min(jnp.where(rest == m2, lane_f, float(LANES)), axis=-1, keepdims=True)
    e = jnp.exp(m2 - m1)
    g1 = 1.0 / (1.0 + e)
    g2 = e / (1.0 + e)
    oh1 = lane_f == i1
    oh2 = lane_f == i2
    i1 = i1.astype(I32)
    i2 = i2.astype(I32)
    oh = jnp.where(oh1 | oh2, 1.0, 0.0)
    ri_ = lax.broadcasted_iota(I32, (tm, tm), 0)
    ci_ = lax.broadcasted_iota(I32, (tm, tm), 1)
    below = jnp.where(ri_ > ci_, 1.0, 0.0).astype(BF16)
    ex = jnp.dot(below, oh.astype(BF16), preferred_element_type=F32) + carry_ref[0:1, :]
    r1 = jnp.sum(jnp.where(oh1, ex, 0.0), axis=-1, keepdims=True).astype(I32)
    r2 = jnp.sum(jnp.where(oh2, ex, 0.0), axis=-1, keepdims=True).astype(I32)
    ri_ref[...] = jnp.where(lane == 0, i1, jnp.where(lane == 1, i2, jnp.where(lane == 2, r1, r2)))
    gate_ref[...] = jnp.where(lane == 0, g1, g2)
    carry_ref[...] = carry_ref[...] + jnp.sum(oh, axis=0, keepdims=True)
    cnt_ref[...] = carry_ref[...]


def route(h, g, wr):
    T, D = h.shape
    tm = min(ROW_TILE, T)
    return pl.pallas_call(
        _route_kernel,
        grid=(T // tm,),
        in_specs=[
            pl.BlockSpec((tm, D), lambda i: (i, 0)),
            pl.BlockSpec((1, D), lambda i: (0, 0)),
            pl.BlockSpec((D, LANES), lambda i: (0, 0)),
        ],
        out_specs=[
            pl.BlockSpec((tm, LANES), lambda i: (i, 0)),
            pl.BlockSpec((tm, LANES), lambda i: (i, 0)),
            pl.BlockSpec((SUBLANES, LANES), lambda i: (0, 0)),
        ],
        out_shape=[
            jax.ShapeDtypeStruct((T, LANES), I32),
            jax.ShapeDtypeStruct((T, LANES), F32),
            jax.ShapeDtypeStruct((SUBLANES, LANES), F32),
        ],
        scratch_shapes=[pltpu.VMEM((SUBLANES, LANES), F32)],
        compiler_params=_cparams(("arbitrary",)),
        name="route",
    )(h, g.reshape(1, D), wr)


def _row_copy(src_ref, s, dst_ref, d, sem):
    return pltpu.make_async_copy(src_ref.at[pl.ds(s, 1), :], dst_ref.at[pl.ds(d, 1), :], sem)


def _dispatch_kernel(pz_ref, dest_ref, h_ref, g_ref, xs_ref, xn_ref, zb_ref, sem, zsem):
    tm = h_ref.shape[0]
    G = zb_ref.shape[0]

    @pl.when(pl.program_id(0) == 0)
    def _():
        zb_ref[...] = jnp.zeros(zb_ref.shape, F32)

        def zero_copy(e):
            return pltpu.make_async_copy(zb_ref, xs_ref.at[pl.ds(pl.multiple_of(pz_ref[e], G), G), :], zsem)

        for e in range(2 * N_EXPERTS):
            @pl.when(pz_ref[e] >= 0)
            def _():
                zero_copy(e).start()

        for e in range(2 * N_EXPERTS):
            @pl.when(pz_ref[e] >= 0)
            def _():
                zero_copy(e).wait()

    xn_ref[...] = _rms(h_ref[...], g_ref[...], RMS_EPS)

    def issue(r, carry):
        _row_copy(xn_ref, r, xs_ref, dest_ref[2 * r], sem).start(priority=0)
        _row_copy(xn_ref, r, xs_ref, dest_ref[2 * r + 1], sem).start(priority=1)
        return carry

    lax.fori_loop(0, tm, issue, 0, unroll=ROW_DMA_UNROLL)

    def drain(r, carry):
        _row_copy(xn_ref, 0, xs_ref, 0, sem).wait()
        _row_copy(xn_ref, 0, xs_ref, 0, sem).wait()
        return carry

    lax.fori_loop(0, tm, drain, 0, unroll=ROW_DMA_UNROLL)


def dispatch(h, g, dest_flat, zero_tiles, n_slots):
    T, D = h.shape
    tm = min(ROW_TILE, T)
    grid_spec = pltpu.PrefetchScalarGridSpec(
        num_scalar_prefetch=1,
        grid=(T // tm,),
        in_specs=[
            pl.BlockSpec((2 * tm,), lambda i, pz: (i,), memory_space=pltpu.SMEM),
            pl.BlockSpec((tm, D), lambda i, pz: (i, 0)),
            pl.BlockSpec((1, D), lambda i, pz: (0, 0)),
        ],
        out_specs=pl.BlockSpec(memory_space=pl.ANY),
        scratch_shapes=[pltpu.VMEM((tm, D), F32), pltpu.VMEM((MOE_TILE, D), F32),
                        pltpu.SemaphoreType.DMA, pltpu.SemaphoreType.DMA],
    )
    return pl.pallas_call(
        _dispatch_kernel,
        grid_spec=grid_spec,
        out_shape=jax.ShapeDtypeStruct((n_slots, D), F32),
        compiler_params=_cparams(("arbitrary",)),
        name="moe_dispatch",
    )(zero_tiles, dest_flat, h, g.reshape(1, D))


def _gmm_kernel(te_ref, na_ref, x_ref, w1_ref, w3_ref, w2_ref, y_ref, xb_ref, acc_ref):
    b = pl.program_id(0)
    f = pl.program_id(1)

    @pl.when(b < na_ref[0])
    def _():
        @pl.when(f == 0)
        def _():
            xb_ref[...] = x_ref[...].astype(BF16)
            acc_ref[...] = jnp.zeros(acc_ref.shape, F32)

        xb = xb_ref[...]
        a = jnp.dot(xb, w1_ref[...], preferred_element_type=F32)
        c = jnp.dot(xb, w3_ref[...], preferred_element_type=F32)
        hm = (_silu(a) * c).astype(BF16)
        acc_ref[...] += jnp.dot(hm, w2_ref[...], preferred_element_type=F32)

        @pl.when(f == pl.num_programs(1) - 1)
        def _():
            y_ref[...] = acc_ref[...]

    @pl.when((b >= na_ref[0]) & (f == pl.num_programs(1) - 1))
    def _():
        y_ref[...] = jnp.zeros(y_ref.shape, F32)


def gmm(xs, w1, w3, w2, tile_e, n_active, layer):
    P, D = xs.shape
    G = MOE_TILE
    F = w1.shape[3]
    tf = MOE_FF_TILE
    nf = F // tf

    def row_idx(b, f, te, na):
        return (jnp.minimum(b, na[0] - 1), 0)

    def f_idx(b, f, na):
        return jnp.where(b < na[0], f, nf - 1)

    grid_spec = pltpu.PrefetchScalarGridSpec(
        num_scalar_prefetch=2,
        grid=(P // G, nf),
        in_specs=[
            pl.BlockSpec((G, D), row_idx),
            pl.BlockSpec((None, None, D, tf), lambda b, f, te, na: (layer, te[b], 0, f_idx(b, f, na))),
            pl.BlockSpec((None, None, D, tf), lambda b, f, te, na: (layer, te[b], 0, f_idx(b, f, na))),
            pl.BlockSpec((None, None, tf, D), lambda b, f, te, na: (layer, te[b], f_idx(b, f, na), 0)),
        ],
        out_specs=pl.BlockSpec((G, D), lambda b, f, te, na: (b, 0)),
        scratch_shapes=[pltpu.VMEM((G, D), BF16), pltpu.VMEM((G, D), F32)],
    )
    return pl.pallas_call(
        _gmm_kernel,
        grid_spec=grid_spec,
        out_shape=jax.ShapeDtypeStruct((P, D), F32),
        compiler_params=_cparams(("arbitrary", "arbitrary")),
        name="moe_gmm",
    )(tile_e, n_active, xs, w1, w3, w2)


def _combine_ple_kernel(dest_ref, dnext_ref, h_ref, gate_ref, y_ref, g_ref, wg_ref, p_ref, wp_ref, gf_ref,
                        o_ref, yab_ref, sem, *, final_norm):
    i = pl.program_id(0)
    n = pl.num_programs(0)
    tm = h_ref.shape[0]
    slot = lax.rem(i, 2)

    def gather(d_ref, sl):
        def issue(r, carry):
            _row_copy(y_ref, d_ref[2 * r], yab_ref.at[sl, 0], r, sem.at[sl]).start(priority=0)
            _row_copy(y_ref, d_ref[2 * r + 1], yab_ref.at[sl, 1], r, sem.at[sl]).start(priority=1)
            return carry

        lax.fori_loop(0, tm, issue, 0, unroll=ROW_DMA_UNROLL)

    @pl.when(i == 0)
    def _():
        gather(dest_ref, 0)

    @pl.when(i + 1 < n)
    def _():
        gather(dnext_ref, 1 - slot)

    def drain(r, carry):
        _row_copy(y_ref, 0, yab_ref.at[slot, 0], 0, sem.at[slot]).wait()
        _row_copy(y_ref, 0, yab_ref.at[slot, 1], 0, sem.at[slot]).wait()
        return carry

    lax.fori_loop(0, tm, drain, 0, unroll=ROW_DMA_UNROLL)
    gt = gate_ref[...]
    h = h_ref[...] + gt[:, 0:1] * yab_ref[slot, 0] + gt[:, 1:2] * yab_ref[slot, 1]
    o_ref[...] = _ple_math(h, g_ref, wg_ref, p_ref, wp_ref, gf_ref, final_norm)


def combine_ple(h, gates, y, dest_flat, g, wg, p, wp, g_final, final_norm, layer):
    T, D = h.shape
    tm = min(ROW_TILE, T)
    n = T // tm
    return pl.pallas_call(
        functools.partial(_combine_ple_kernel, final_norm=final_norm),
        grid=(n,),
        in_specs=[
            pl.BlockSpec((2 * tm,), lambda i: (i,), memory_space=pltpu.SMEM),
            pl.BlockSpec((2 * tm,), lambda i: (jnp.minimum(i + 1, n - 1),), memory_space=pltpu.SMEM),
            pl.BlockSpec((tm, D), lambda i: (i, 0)),
            pl.BlockSpec((tm, LANES), lambda i: (i, 0)),
            pl.BlockSpec(memory_space=pl.ANY),
            pl.BlockSpec((1, D), lambda i: (0, 0)),
            pl.BlockSpec((None, D, D), lambda i: (layer, 0, 0)),
            pl.BlockSpec((None, tm, PLE_DIM), lambda i: (layer, i, 0)),
            pl.BlockSpec((None, PLE_DIM, D), lambda i: (layer, 0, 0)),
            pl.BlockSpec((1, D), lambda i: (0, 0)),
        ],
        out_specs=pl.BlockSpec((tm, D), lambda i: (i, 0)),
        out_shape=jax.ShapeDtypeStruct((T, D), F32),
        scratch_shapes=[pltpu.VMEM((2, 2, tm, D), F32), pltpu.SemaphoreType.DMA((2,))],
        compiler_params=_cparams(("arbitrary",)),
        name="moe_combine_ple",
    )(dest_flat, dest_flat, h, gates, y, g.reshape(1, D), wg, p, wp, g_final.reshape(1, D))


def _lambda_init(layer_idx):
    return 0.8 - 0.6 * math.exp(-0.3 * layer_idx)


def _even_layer(h, pos_col, pos_row, invf, ln_mix, w_in, w_out, lam_params, subln_g, conv_w, layer_idx):
    qk, vt, gates_conv = even_proj(h, ln_mix, w_in.astype(BF16), pos_col, invf)
    o = diff_attention(qk, vt, pos_col, pos_row, lam_params, subln_g, _lambda_init(layer_idx))
    return even_out(o, gates_conv, conv_w, w_out.astype(BF16), h)


def _odd_mixer(h, ln_mix, w_in, conv_w, a_log, dt_bias, onorm_g, w_out):
    main_w = 4 * DN_WIDTH
    w_main = w_in[:, :main_w].astype(BF16)
    w_ba = jnp.pad(w_in[:, main_w:], ((0, 0), (0, LANES - 2 * DN_HEADS))).astype(BF16)
    proj, ba = norm_matmul(h, ln_mix, w_main, w_side=w_ba)
    pad8 = lambda v: jnp.pad(v.astype(F32), (DN_HEADS, LANES - 2 * DN_HEADS)).reshape(1, LANES)
    qg, kg, w, u, ai, dl = gdn_prep(proj, ba, conv_w, pad8(a_log), pad8(dt_bias))
    o = gdn_scan(qg, kg, w, u, ai, dl)
    return odd_out(o, proj, onorm_g, w_out.astype(BF16), h)


def _moe(h, ln_ffn, w_router, w1, w3, w2, layer):
    T = h.shape[0]
    G = MOE_TILE
    wr = jnp.pad(w_router, ((0, 0), (0, LANES - N_EXPERTS)))
    ri, gates, cnt = route(h, ln_ffn, wr)
    counts = cnt[0, :N_EXPERTS].astype(I32)
    padded = ((counts + G - 1) // G) * G
    pends = jnp.cumsum(padded)
    pstarts = pends - padded
    dest = (jnp.take(pstarts, ri[:, 0:2]) + ri[:, 2:4]).reshape(-1)
    n_tiles = (2 * T) // G + N_EXPERTS
    tile_start = jnp.arange(n_tiles, dtype=I32) * G
    tile_e = jnp.minimum(jnp.sum(pends[None, :] <= tile_start[:, None], axis=1), N_EXPERTS - 1).astype(I32)
    n_active = (pends[-1:] // G).astype(I32)
    tail = pends[-1] + jnp.arange(N_EXPERTS, dtype=I32) * G
    zero_tiles = jnp.concatenate([jnp.where(padded > 0, pends - G, -1),
                                  jnp.where(tail < n_tiles * G, tail, -1)]).astype(I32)
    xs = dispatch(h, ln_ffn, dest, zero_tiles, n_tiles * G)
    y = gmm(xs, w1, w3, w2, tile_e, n_active, layer)
    return gates, y, dest


def kernel(x, p, positions, ln_mix, ln_ffn, ln_ple, ln_final, w_in_even, w_out_even, lam_q1, lam_k1, lam_q2, lam_k2, subln_gain, conv_w_short, w_in_odd, conv_w_qkv, a_log, dt_bias, onorm_gain, w_out_odd, w1_dense, w3_dense, w2_dense, w_router, w1_moe, w3_moe, w2_moe, w_ple_gate, w_ple_proj):
    B, S, D = x.shape
    T = B * S
    depth = p.shape[0]
    h = x.reshape(T, D)
    pos_col = positions.reshape(T, 1).astype(I32)
    pos_row = positions.reshape(1, T).astype(I32)
    inv_freq = ROPE_THETA ** (-jnp.arange(0, ROT_DIM, 2, dtype=F32) / ROT_DIM)
    invf = jnp.tile(inv_freq, LANES // (ROT_DIM // 2)).reshape(1, LANES)
    w1d, w3d, w2d = (w.astype(BF16) for w in (w1_dense, w3_dense, w2_dense))
    w1m, w3m, w2m = (w.astype(BF16) for w in (w1_moe, w3_moe, w2_moe))
    wpg, wpp = w_ple_gate.astype(BF16), w_ple_proj.astype(BF16)
    p_rows = p.reshape(depth, T, PLE_DIM)
    for i in range(depth):
        j = i // 2
        if i % 2 == 0:
            lam_params = jnp.stack([lam_q1[j], lam_k1[j], lam_q2[j], lam_k2[j]]).astype(F32)
            h = _even_layer(h, pos_col, pos_row, invf, ln_mix[i], w_in_even[j], w_out_even[j], lam_params,
                            subln_gain[j], conv_w_short[j], i)
            h = ffn_ple(h, ln_ffn[i], w1d, w3d, w2d, j, ln_ple[i], wpg, p_rows, wpp, ln_final,
                        final_norm=(i == depth - 1), ple_layer=i)
        else:
            h = _odd_mixer(h, ln_mix[i], w_in_odd[j], conv_w_qkv[j], a_log[j], dt_bias[j], onorm_gain[j],
                           w_out_odd[j])
            gates, y, dest = _moe(h, ln_ffn[i], w_router[j], w1m, w3m, w2m, j)
            h = combine_ple(h, gates, y, dest, ln_ple[i], wpg, p_rows, wpp, ln_final,
                            final_norm=(i == depth - 1), layer=i)
    return h.reshape(B, S, D)
```

```python
import functools
import math

import jax
import jax.numpy as jnp
import numpy as np
from jax import lax
from jax.experimental import pallas as pl
from jax.experimental.pallas import tpu as pltpu

F32 = jnp.float32
BF16 = jnp.bfloat16
I32 = jnp.int32

D_MODEL = 1024
DEPTH = 4
RMS_EPS = 1e-6
DA_HEADS = 4
DA_HEAD_DIM = 64
DA_V_DIM = 2 * DA_HEAD_DIM
ROPE_THETA = 500000.0
ROT_DIM = DA_HEAD_DIM // 4
SUBLN_EPS = 1e-5
SC_WIDTH = 512
DN_HEADS = 8
DN_HEAD_DIM = 128
DN_WIDTH = DN_HEADS * DN_HEAD_DIM
DN_CHUNK = 64
D_FF = 3584
N_EXPERTS = 8
PLE_DIM = 256

LANES = 128
SUBLANES = 8
HALO = 16
VMEM_LIMIT = 52 * 1024 * 1024

NEG_BIG = -1e30

ROW_TILE = 512
PROJ_COL_TILE = 1024
FFN_ROW_TILE = 512
FF_TILE = 1792
MOE_FF_TILE = 1792
ATT_TQ = 1024
ATT_TK = 1024
ATT_EXP_ROWS = 128
ATT_TQ_SPLIT = 256
GDN_TILE = 1024
GDN_PREP_HEADS = 2
GDN_SCAN_HEADS = 8
MOE_TILE = 512
ROW_DMA_UNROLL = 8


def _cparams(sem):
    return pltpu.CompilerParams(dimension_semantics=sem, vmem_limit_bytes=VMEM_LIMIT)


def _rms(x, g, eps):
    ms = jnp.mean(x * x, axis=-1, keepdims=True)
    return x * lax.rsqrt(ms + eps) * g


def _silu(x):
    return x * (1.0 / (1.0 + jnp.exp(-x)))


def _sigmoid(x):
    return 1.0 / (1.0 + jnp.exp(-x))


def _norm_matmul_kernel(x_ref, g_ref, w_ref, *rest):
    xn = _rms(x_ref[...], g_ref[...], RMS_EPS).astype(BF16)
    if len(rest) == 1:
        (o_ref,) = rest
    else:
        ws_ref, o_ref, os_ref = rest
        os_ref[...] = jnp.dot(xn, ws_ref[...], preferred_element_type=F32)
    tn = PROJ_COL_TILE
    for c in range(o_ref.shape[1] // tn):
        o_ref[:, c * tn:(c + 1) * tn] = jnp.dot(
            xn, w_ref[:, c * tn:(c + 1) * tn], preferred_element_type=F32).astype(o_ref.dtype)


def norm_matmul(x, g, w, w_side=None, out_dtype=BF16):
    T, D = x.shape
    N = w.shape[1]
    tm = min(ROW_TILE, T)
    in_specs = [
        pl.BlockSpec((tm, D), lambda i: (i, 0)),
        pl.BlockSpec((1, D), lambda i: (0, 0)),
        pl.BlockSpec((D, N), lambda i: (0, 0)),
    ]
    out_specs = [pl.BlockSpec((tm, N), lambda i: (i, 0))]
    out_shape = [jax.ShapeDtypeStruct((T, N), out_dtype)]
    args = [x, g.reshape(1, D), w]
    if w_side is not None:
        ns = w_side.shape[1]
        in_specs.append(pl.BlockSpec((D, ns), lambda i: (0, 0)))
        out_specs.append(pl.BlockSpec((tm, ns), lambda i: (i, 0)))
        out_shape.append(jax.ShapeDtypeStruct((T, ns), F32))
        args.append(w_side)
    return pl.pallas_call(
        _norm_matmul_kernel,
        grid=(T // tm,),
        in_specs=in_specs,
        out_specs=out_specs,
        out_shape=out_shape,
        compiler_params=_cparams(("parallel",)),
        name="norm_matmul",
    )(*args)


def _even_proj_kernel(x_ref, g_ref, w_ref, pos_ref, invf_ref, qk_ref, vt_ref, rest_ref):
    xn = _rms(x_ref[...], g_ref[...], RMS_EPS).astype(BF16)
    WQK = qk_ref.shape[1]
    WV = vt_ref.shape[0]

    def proj(c0, c1):
        return jnp.dot(xn, w_ref[:, c0:c1], preferred_element_type=F32)

    pos = pos_ref[...].astype(F32)
    ang = pos * invf_ref[...]
    cos_t = jnp.cos(ang)
    sin_t = jnp.sin(ang)
    d = lax.broadcasted_iota(I32, ang.shape, 1) & (DA_HEAD_DIM - 1)
    half = ROT_DIM // 2
    c_mul = jnp.where(d < ROT_DIM, cos_t, 1.0)
    s_mul = jnp.where(d < half, -sin_t, jnp.where(d < ROT_DIM, sin_t, 0.0))
    qk = proj(0, WQK)
    n_q = WQK // (2 * LANES)
    for c in range(WQK // LANES):
        x = qk[:, c * LANES:(c + 1) * LANES]
        swapped = jnp.where(d < half, pltpu.roll(x, LANES - half, 1), pltpu.roll(x, half, 1))
        r = x * c_mul + swapped * s_mul
        if c < n_q:
            r = r * (DA_HEAD_DIM ** -0.5 * math.log2(math.e))
        qk_ref[:, c * LANES:(c + 1) * LANES] = r.astype(BF16)
    v = proj(WQK, WQK + WV)
    for hd in range(DA_HEADS):
        vt_ref[hd * DA_V_DIM:(hd + 1) * DA_V_DIM, :] = v[:, hd * DA_V_DIM:(hd + 1) * DA_V_DIM].T.astype(BF16)
    WR = rest_ref.shape[1]
    for c0 in range(0, WR, PROJ_COL_TILE):
        c1 = min(c0 + PROJ_COL_TILE, WR)
        rest_ref[:, c0:c1] = proj(WQK + WV + c0, WQK + WV + c1).astype(BF16)


def even_proj(x, g, w, positions_col, invf):
    T, D = x.shape
    WQK = 2 * 2 * DA_HEADS * DA_HEAD_DIM
    WV = DA_HEADS * DA_V_DIM
    WR = w.shape[1] - WQK - WV
    tm = min(ROW_TILE, T)
    return pl.pallas_call(
        _even_proj_kernel,
        grid=(T // tm,),
        in_specs=[
            pl.BlockSpec((tm, D), lambda i: (i, 0)),
            pl.BlockSpec((1, D), lambda i: (0, 0)),
            pl.BlockSpec(w.shape, lambda i: (0, 0)),
            pl.BlockSpec((tm, 1), lambda i: (i, 0)),
            pl.BlockSpec((1, LANES), lambda i: (0, 0)),
        ],
        out_specs=[pl.BlockSpec((tm, WQK), lambda i: (i, 0)),
                   pl.BlockSpec((WV, tm), lambda i: (0, i)),
                   pl.BlockSpec((tm, WR), lambda i: (i, 0))],
        out_shape=[jax.ShapeDtypeStruct((T, WQK), BF16), jax.ShapeDtypeStruct((WV, T), BF16),
                   jax.ShapeDtypeStruct((T, WR), BF16)],
        compiler_params=_cparams(("parallel",)),
        name="even_proj",
    )(x, g.reshape(1, D), w, positions_col, invf)


def _attn_kernel(qt_ref, kt_ref, q_ref, k_ref, v_ref, pq_ref, pk_ref, lam_ref, g_ref, o_ref,
                 m_ref, l_ref, acc_ref, *, tq, tk, lam0):
    p = pl.program_id(1)
    qi = qt_ref[p]
    ki = kt_ref[p]

    @pl.when(ki == 0)
    def _():
        m_ref[...] = jnp.full(m_ref.shape, NEG_BIG, F32)
        l_ref[...] = jnp.zeros(l_ref.shape, F32)
        acc_ref[...] = jnp.zeros(acc_ref.shape, F32)

    tqs = min(tq, ATT_TQ_SPLIT)
    items = [(s, j) for s in range(2) for j in range(tq // tqs)]

    def step(masked):
        q = q_ref[...]
        k = k_ref[...]
        v = v_ref[...]
        def nkeys(j):
            return (j + 1) * tqs if (masked and tq == tk) else tk

        st = {}
        for s, j in items:
            cols = slice(j * tqs, (j + 1) * tqs)
            qs = q[j * tqs:(j + 1) * tqs, s * DA_HEAD_DIM:(s + 1) * DA_HEAD_DIM]
            ks = k[:nkeys(j), s * DA_HEAD_DIM:(s + 1) * DA_HEAD_DIM]
            sc = lax.dot_general(ks, qs, (((1,), (1,)), ((), ())), preferred_element_type=F32)
            if masked:
                sc = jnp.where(pk_ref[0:nkeys(j), :] <= pq_ref[:, cols], sc, NEG_BIG)
            st[s, j] = sc
        for s, j in items:
            cols = slice(j * tqs, (j + 1) * tqs)
            sc = st[s, j]
            nk = nkeys(j)
            m_prev = m_ref[s, :, cols]
            m_new = jnp.maximum(m_prev, jnp.max(sc, axis=0, keepdims=True))
            alpha = jnp.exp2(m_prev - m_new)
            psum = jnp.zeros((1, tqs), F32)
            pchunks = []
            for r0 in range(0, nk, ATT_EXP_ROWS):
                pc = jnp.exp2(sc[r0:r0 + ATT_EXP_ROWS] - m_new)
                psum = psum + jnp.sum(pc, axis=0, keepdims=True)
                pchunks.append(pc.astype(BF16))
            pb = jnp.concatenate(pchunks, axis=0)
            l_ref[s, :, cols] = alpha * l_ref[s, :, cols] + psum
            pv = jnp.dot(v[:, :nk], pb, preferred_element_type=F32)
            acc_ref[s, :, cols] = alpha * acc_ref[s, :, cols] + pv
            m_ref[s, :, cols] = m_new

    crosses = (ki + 1) * tk - 1 > qi * tq

    @pl.when(crosses)
    def _():
        step(True)

    @pl.when(jnp.logical_not(crosses))
    def _():
        step(False)

    @pl.when((ki + 1) * tk >= (qi + 1) * tq)
    def _():
        lm = lam_ref[...]
        s1 = jnp.sum(lm[0:1] * lm[1:2], axis=-1, keepdims=True)
        s2 = jnp.sum(lm[2:3] * lm[3:4], axis=-1, keepdims=True)
        lam = jnp.exp(s1) - jnp.exp(s2) + lam0
        ot = acc_ref[0] * (1.0 / l_ref[0]) - lam * (acc_ref[1] * (1.0 / l_ref[1]))
        ms = jnp.mean(ot * ot, axis=0, keepdims=True)
        ot = ot * lax.rsqrt(ms + SUBLN_EPS) * g_ref[...] * (1.0 - lam0)
        o_ref[...] = ot.T.astype(o_ref.dtype)


def diff_attention(qk, vt, pos_col, pos_row, lam_params, subln_g, lam0):
    T = qk.shape[0]
    tq = min(ATT_TQ, T)
    tk = min(ATT_TK, T)
    nq = T // tq
    pairs = [(qi, ki) for qi in range(nq) for ki in range(-(-((qi + 1) * tq) // tk))]
    qt = jnp.asarray(np.array([a for a, _ in pairs], np.int32))
    kt = jnp.asarray(np.array([b for _, b in pairs], np.int32))
    H = DA_HEADS
    kern = functools.partial(_attn_kernel, tq=tq, tk=tk, lam0=lam0)
    grid_spec = pltpu.PrefetchScalarGridSpec(
        num_scalar_prefetch=2,
        grid=(H, len(pairs)),
        in_specs=[
            pl.BlockSpec((tq, LANES), lambda h, p, qt, kt: (qt[p], h)),
            pl.BlockSpec((tk, LANES), lambda h, p, qt, kt: (kt[p], H + h)),
            pl.BlockSpec((DA_V_DIM, tk), lambda h, p, qt, kt: (h, kt[p])),
            pl.BlockSpec((1, tq), lambda h, p, qt, kt: (0, qt[p])),
            pl.BlockSpec((tk, 1), lambda h, p, qt, kt: (kt[p], 0)),
            pl.BlockSpec((4, DA_HEAD_DIM), lambda h, p, qt, kt: (0, 0)),
            pl.BlockSpec((DA_V_DIM, 1), lambda h, p, qt, kt: (0, 0)),
        ],
        out_specs=pl.BlockSpec((tq, DA_V_DIM), lambda h, p, qt, kt: (qt[p], h)),
        scratch_shapes=[
            pltpu.VMEM((2, 1, tq), F32),
            pltpu.VMEM((2, 1, tq), F32),
            pltpu.VMEM((2, DA_V_DIM, tq), F32),
        ],
    )
    return pl.pallas_call(
        kern,
        grid_spec=grid_spec,
        out_shape=jax.ShapeDtypeStruct((T, H * DA_V_DIM), BF16),
        compiler_params=_cparams(("parallel", "arbitrary")),
        name="diff_attention",
    )(qt, kt, qk, qk, vt, pos_row, pos_col, lam_params, subln_g.reshape(DA_V_DIM, 1))


def _even_out_kernel(o_ref, b_ref, c_ref, x_ref, ch_ref, xh_ref, cw_ref, w_ref, h_ref, out_ref, u_ref):
    tm = o_ref.shape[0]
    u_prev = ch_ref[...].astype(F32) * xh_ref[...].astype(F32)
    u_prev = jnp.where(pl.program_id(0) == 0, 0.0, u_prev)
    u_ref[0:HALO, :] = u_prev
    u_ref[HALO:, :] = c_ref[...].astype(F32) * x_ref[...].astype(F32)
    cw = cw_ref[...]
    conv = (u_ref[HALO - 2:HALO - 2 + tm, :] * cw[0:1]
            + u_ref[HALO - 1:HALO - 1 + tm, :] * cw[1:2]
            + u_ref[HALO:, :] * cw[2:3])
    sc = (b_ref[...].astype(F32) * conv).astype(BF16)
    na = o_ref.shape[1]
    acc = jnp.dot(o_ref[...], w_ref[0:na, :], preferred_element_type=F32)
    acc = acc + jnp.dot(sc, w_ref[na:, :], preferred_element_type=F32)
    out_ref[...] = h_ref[...] + acc


def even_out(attn_o, proj, conv_w, w_out, h):
    T = h.shape[0]
    tm = min(ROW_TILE, T)
    W = SC_WIDTH
    cb = (proj.shape[1] - 3 * W) // W
    hb = tm // HALO
    halo = lambda col: pl.BlockSpec((HALO, W), lambda i: (jnp.maximum(i * hb - 1, 0), col))
    return pl.pallas_call(
        _even_out_kernel,
        grid=(T // tm,),
        in_specs=[
            pl.BlockSpec((tm, attn_o.shape[1]), lambda i: (i, 0)),
            pl.BlockSpec((tm, W), lambda i: (i, cb)),
            pl.BlockSpec((tm, W), lambda i: (i, cb + 1)),
            pl.BlockSpec((tm, W), lambda i: (i, cb + 2)),
            halo(cb + 1),
            halo(cb + 2),
            pl.BlockSpec(conv_w.shape, lambda i: (0, 0)),
            pl.BlockSpec(w_out.shape, lambda i: (0, 0)),
            pl.BlockSpec((tm, D_MODEL), lambda i: (i, 0)),
        ],
        out_specs=pl.BlockSpec((tm, D_MODEL), lambda i: (i, 0)),
        out_shape=jax.ShapeDtypeStruct((T, D_MODEL), F32),
        scratch_shapes=[pltpu.VMEM((tm + HALO, W), F32)],
        compiler_params=_cparams(("parallel",)),
        name="even_out",
    )(attn_o, proj, proj, proj, proj, proj, conv_w, w_out, h)


def _ffn_kernel(h_ref, g_ref, w1_ref, w3_ref, w2_ref, gp_ref, wg_ref, p_ref, wp_ref, gf_ref, o_ref, xn_ref, acc_ref,
                *, final_norm):
    f = pl.program_id(1)

    @pl.when(f == 0)
    def _():
        xn_ref[...] = _rms(h_ref[...], g_ref[...], RMS_EPS).astype(BF16)
        acc_ref[...] = jnp.zeros(acc_ref.shape, F32)

    xn = xn_ref[...]
    a = jnp.dot(xn, w1_ref[...], preferred_element_type=F32)
    b = jnp.dot(xn, w3_ref[...], preferred_element_type=F32)
    hm = (_silu(a) * b).astype(BF16)
    acc_ref[...] += jnp.dot(hm, w2_ref[...], preferred_element_type=F32)

    @pl.when(f == pl.num_programs(1) - 1)
    def _():
        o_ref[...] = _ple_math(h_ref[...] + acc_ref[...], gp_ref, wg_ref, p_ref, wp_ref, gf_ref, final_norm)


def ffn_ple(h, g, w1, w3, w2, layer, g_ple, wg, p, wp, g_final, final_norm, ple_layer):
    T, D = h.shape
    F = w1.shape[2]
    tm = min(FFN_ROW_TILE, T)
    tf = FF_TILE
    return pl.pallas_call(
        functools.partial(_ffn_kernel, final_norm=final_norm),
        grid=(T // tm, F // tf),
        in_specs=[
            pl.BlockSpec((tm, D), lambda i, f: (i, 0)),
            pl.BlockSpec((1, D), lambda i, f: (0, 0)),
            pl.BlockSpec((None, D, tf), lambda i, f: (layer, 0, f)),
            pl.BlockSpec((None, D, tf), lambda i, f: (layer, 0, f)),
            pl.BlockSpec((None, tf, D), lambda i, f: (layer, f, 0)),
            pl.BlockSpec((1, D), lambda i, f: (0, 0)),
            pl.BlockSpec((None, D, D), lambda i, f: (ple_layer, 0, 0)),
            pl.BlockSpec((None, tm, PLE_DIM), lambda i, f: (ple_layer, i, 0)),
            pl.BlockSpec((None, PLE_DIM, D), lambda i, f: (ple_layer, 0, 0)),
            pl.BlockSpec((1, D), lambda i, f: (0, 0)),
        ],
        out_specs=pl.BlockSpec((tm, D), lambda i, f: (i, 0)),
        out_shape=jax.ShapeDtypeStruct((T, D), F32),
        scratch_shapes=[pltpu.VMEM((tm, D), BF16), pltpu.VMEM((tm, D), F32)],
        compiler_params=_cparams(("parallel", "arbitrary")),
        name="ffn_ple",
    )(h, g.reshape(1, D), w1, w3, w2, g_ple.reshape(1, D), wg, p, wp, g_final.reshape(1, D))


def _ple_math(h, g_ref, wg_ref, p_ref, wp_ref, gf_ref, final_norm):
    xn = _rms(h, g_ref[...], RMS_EPS).astype(BF16)
    gate = _sigmoid(jnp.dot(xn, wg_ref[...], preferred_element_type=F32))
    emb = jnp.dot(p_ref[...].astype(BF16), wp_ref[...], preferred_element_type=F32)
    out = h + gate * emb
    if final_norm:
        out = _rms(out, gf_ref[...], RMS_EPS)
    return out


def _dot_hi(a, b):
    return jnp.dot(a, b, preferred_element_type=F32, precision=lax.Precision.HIGHEST)


def _gdn_prep_kernel(q_ref, k_ref, v_ref, qh_ref, kh_ref, vh_ref, ba_ref, cw_ref, alog_ref, dtb_ref,
                     qg_ref, kg_ref, w_ref, u_ref, ai_ref, dl_ref,
                     xs_ref, gc_ref, gl_ref, sg_ref, gct_ref):
    i = pl.program_id(0)
    h = pl.program_id(1)
    R = q_ref.shape[0]
    C = DN_CHUNK
    KC = cw_ref.shape[1]

    def conv_silu(x_ref, halo_ref, j, hp):
        cs = slice(hp * LANES, (hp + 1) * LANES)
        buf = xs_ref.at[3 * hp + j]
        buf[0:HALO, :] = jnp.where(i == 0, 0.0, halo_ref[:, cs].astype(F32))
        buf[HALO:, :] = x_ref[:, cs].astype(F32)
        acc = buf[HALO:, :] * cw_ref[hp, KC - 1:KC, j * LANES:(j + 1) * LANES]
        for t in range(1, KC):
            acc = acc + (buf[HALO - t:HALO - t + R, :]
                         * cw_ref[hp, KC - 1 - t:KC - t, j * LANES:(j + 1) * LANES])
        return _silu(acc)

    def l2n(x):
        return x * lax.rsqrt(jnp.sum(x * x, axis=-1, keepdims=True) + 1e-6)

    chunks = range(R // C)
    rows = [slice(c * C, (c + 1) * C) for c in chunks]
    r64 = lax.broadcasted_iota(I32, (C, C), 0)
    c64 = lax.broadcasted_iota(I32, (C, C), 1)
    incl = r64 >= c64
    strict = r64 > c64

    @pl.when(h == 0)
    def _():
        ba = ba_ref[...]
        sp_in = ba + dtb_ref[...]
        softplus = jnp.maximum(sp_in, 0.0) + jnp.log(1.0 + jnp.exp(-jnp.abs(sp_in)))
        g_all = -jnp.exp(alog_ref[...]) * softplus
        g_wide = jnp.concatenate([g_all[r] for r in rows], axis=1)
        gc_wide = _dot_hi(jnp.where(incl, 1.0, 0.0), g_wide)
        for c in chunks:
            gc_c = gc_wide[:, c * LANES:(c + 1) * LANES]
            gc_ref[rows[c], :] = gc_c
            gl_ref[rows[c], :] = jnp.broadcast_to(gc_c[C - 1:C, :], (C, LANES))
        sg_ref[...] = _sigmoid(ba)
        gct_ref[...] = gc_ref[...].T

    lane = lax.broadcasted_iota(I32, (R, LANES), 1)

    def pick(ref, l):
        return jnp.sum(jnp.where(lane == l, ref[...], 0.0), axis=-1, keepdims=True)

    def bdot(a, b):
        return jnp.dot(a.astype(BF16), b.astype(BF16), preferred_element_type=F32)

    heads = range(qg_ref.shape[1] // LANES)
    items = [(hp, c) for hp in heads for c in chunks]
    decay, kq, rhs, edl = {}, {}, {}, {}
    for hp in heads:
        hh = h * len(heads) + hp
        cs = slice(hp * LANES, (hp + 1) * LANES)
        q = l2n(conv_silu(q_ref, qh_ref, 0, hp)) * (DN_HEAD_DIM ** -0.5)
        k = l2n(conv_silu(k_ref, kh_ref, 1, hp))
        v = conv_silu(v_ref, vh_ref, 2, hp)
        beta = pick(sg_ref, hh)
        gcol = pick(gc_ref, DN_HEADS + hh)
        glast = pick(gl_ref, DN_HEADS + hh)
        grow = gct_ref[pl.ds(DN_HEADS + hh, 1), :]
        eg = jnp.exp(gcol)
        kb = k * beta
        qg_ref[:, cs] = (q * eg).astype(BF16)
        kg_ref[:, cs] = (k * jnp.exp(glast - gcol)).astype(BF16)
        edl[hp] = jnp.broadcast_to(jnp.exp(glast), (R, LANES))
        rhs[hp] = jnp.concatenate([v * beta, kb * eg], axis=-1)
        kbf = k.astype(BF16)
        lhs = jnp.concatenate([kb.astype(BF16).reshape(R // C, C, LANES),
                               q.astype(BF16).reshape(R // C, C, LANES)], axis=1)
        for c in chunks:
            r = rows[c]
            decay[hp, c] = jnp.where(incl, jnp.exp(jnp.where(incl, gcol[r] - grow[:, r], 0.0)), 0.0)
            kq[hp, c] = lax.dot_general(lhs[c], kbf[r], (((1,), (1,)), ((), ())), preferred_element_type=F32)
    pw = {it: -jnp.where(strict, kq[it][:C] * decay[it], 0.0) for it in items}
    n = dict(pw)
    for _ in range(5):
        pw = {it: bdot(pw[it], pw[it]) for it in items}
        n = {it: n[it] + pw[it] + bdot(n[it], pw[it]) for it in items}
    for hp, c in items:
        cs = slice(hp * LANES, (hp + 1) * LANES)
        rc = rhs[hp][rows[c]]
        uw = rc + bdot(n[hp, c], rc)
        u_ref[rows[c], cs] = uw[:, :DN_HEAD_DIM]
        w_ref[rows[c], cs] = uw[:, DN_HEAD_DIM:].astype(BF16)
        ai_ref[hp, rows[c], :] = (kq[hp, c][C:] * decay[hp, c]).astype(BF16)
        dl_ref[hp, c:c + 1, :] = edl[hp][c * C:c * C + 1, :]


def gdn_prep(proj, ba, conv_w, alog_l, dtb_l):
    T = proj.shape[0]
    R = min(GDN_TILE, T)
    H = DN_HEADS
    nchunk = T // DN_CHUNK
    hb = R // HALO
    HP = GDN_PREP_HEADS
    W = HP * LANES
    col = lambda j: pl.BlockSpec((R, W), lambda i, h: (i, j * (H // HP) + h))
    halo = lambda j: pl.BlockSpec((HALO, W), lambda i, h: (jnp.maximum(i * hb - 1, 0), j * (H // HP) + h))
    KC = conv_w.shape[0]
    cw = conv_w.reshape(KC, 3, H, LANES).transpose(2, 0, 1, 3).reshape(H, KC, 3 * LANES)
    cw_spec = pl.BlockSpec((HP, KC, 3 * LANES), lambda i, h: (h, 0, 0))
    row_out = lambda dt: jax.ShapeDtypeStruct((T, DN_WIDTH), dt)
    out_col = pl.BlockSpec((R, W), lambda i, h: (i, h))
    return pl.pallas_call(
        _gdn_prep_kernel,
        grid=(T // R, H // HP),
        in_specs=[col(0), col(1), col(2), halo(0), halo(1), halo(2),
                  pl.BlockSpec((R, LANES), lambda i, h: (i, 0)),
                  cw_spec,
                  pl.BlockSpec((1, LANES), lambda i, h: (0, 0)),
                  pl.BlockSpec((1, LANES), lambda i, h: (0, 0))],
        out_specs=[out_col, out_col, out_col, out_col,
                   pl.BlockSpec((HP, R, DN_CHUNK), lambda i, h: (h, i, 0)),
                   pl.BlockSpec((HP, R // DN_CHUNK, LANES), lambda i, h: (h, i, 0))],
        out_shape=[row_out(BF16), row_out(BF16), row_out(BF16), row_out(F32),
                   jax.ShapeDtypeStruct((H, T, DN_CHUNK), BF16),
                   jax.ShapeDtypeStruct((H, nchunk, LANES), F32)],
        scratch_shapes=[pltpu.VMEM((3 * HP, R + HALO, LANES), F32),
                        pltpu.VMEM((R, LANES), F32), pltpu.VMEM((R, LANES), F32), pltpu.VMEM((R, LANES), F32),
                        pltpu.VMEM((LANES, R), F32)],
        compiler_params=_cparams(("parallel", "arbitrary")),
        name="gdn_prep",
    )(proj, proj, proj, proj, proj, proj, ba, cw, alog_l, dtb_l)


def _gdn_scan_kernel(qg_ref, kg_ref, w_ref, u_ref, ai_ref, dl_ref, o_ref, s_ref):
    @pl.when(pl.program_id(1) == 0)
    def _():
        s_ref[...] = jnp.zeros(s_ref.shape, F32)

    C = DN_CHUNK
    D = DN_HEAD_DIM
    heads = range(s_ref.shape[0])

    def chunk(c, carry):
        rows = pl.ds(pl.multiple_of(c * C, C), C)
        s = [s_ref[hh] for hh in heads]
        sb = [x.astype(BF16) for x in s]
        cols = [slice(hh * D, (hh + 1) * D) for hh in heads]
        r1 = [jnp.dot(jnp.concatenate([w_ref[rows, cols[hh]], qg_ref[rows, cols[hh]]], axis=0), sb[hh],
                      preferred_element_type=F32) for hh in heads]
        vb = [(u_ref[rows, cols[hh]] - r1[hh][:C]).astype(BF16) for hh in heads]
        for hh in heads:
            o_ref[rows, cols[hh]] = r1[hh][C:] + jnp.dot(ai_ref[hh, rows, :], vb[hh], preferred_element_type=F32)
        upd = [lax.dot_general(kg_ref[rows, cols[hh]], vb[hh], (((0,), (0,)), ((), ())),
                               preferred_element_type=F32) for hh in heads]
        for hh in heads:
            s_ref[hh] = s[hh] * dl_ref[hh, pl.ds(c, 1), :] + upd[hh]
        return carry

    lax.fori_loop(0, qg_ref.shape[0] // C, chunk, 0)


def gdn_scan(qg, kg, w, u, ai, dl):
    T = qg.shape[0]
    R = min(GDN_TILE, T)
    H = DN_HEADS
    HB = GDN_SCAN_HEADS
    col = pl.BlockSpec((R, HB * LANES), lambda h, i: (i, h))
    return pl.pallas_call(
        _gdn_scan_kernel,
        grid=(H // HB, T // R),
        in_specs=[col, col, col, col,
                  pl.BlockSpec((HB, R, DN_CHUNK), lambda h, i: (h, i, 0)),
                  pl.BlockSpec((HB, R // DN_CHUNK, LANES), lambda h, i: (h, i, 0))],
        out_specs=col,
        out_shape=jax.ShapeDtypeStruct((T, DN_WIDTH), F32),
        scratch_shapes=[pltpu.VMEM((HB, DN_HEAD_DIM, DN_HEAD_DIM), F32)],
        compiler_params=_cparams(("parallel", "arbitrary")),
        name="gdn_scan",
    )(qg, kg, w, u, ai, dl)


def _odd_out_math(o_ref, z_ref, g_ref, w_ref, h_ref):
    g = g_ref[...]
    parts = []
    for hd in range(DN_HEADS):
        sl = slice(hd * DN_HEAD_DIM, (hd + 1) * DN_HEAD_DIM)
        parts.append((_rms(o_ref[:, sl], g, RMS_EPS) * _silu(z_ref[:, sl].astype(F32))).astype(BF16))
    y = jnp.concatenate(parts, axis=-1)
    return h_ref[...] + jnp.dot(y, w_ref[...], preferred_element_type=F32)


def _route_math(h, g_ref, wr_ref, ri_ref, gate_ref, cnt_ref, carry_ref):
    i = pl.program_id(0)

    @pl.when(i == 0)
    def _():
        carry_ref[...] = jnp.zeros(carry_ref.shape, F32)

    xn = _rms(h, g_ref[...], RMS_EPS)
    logits = jnp.dot(xn.astype(BF16), wr_ref[...].astype(BF16), preferred_element_type=F32)
    tm = logits.shape[0]
    lane = lax.broadcasted_iota(I32, logits.shape, 1)
    logits = jnp.where(lane < N_EXPERTS, logits, NEG_BIG)
    lane_f = lane.astype(F32)
    m1 = jnp.max(logits, axis=-1, keepdims=True)
    i1 = jnp.min(jnp.where(logits == m1, lane_f, float(LANES)), axis=-1, keepdims=True)
    rest = jnp.where(lane_f == i1, NEG_BIG, logits)
    m2 = jnp.max(rest, axis=-1, keepdims=True)
    i2 = jnp.min(jnp.where(rest == m2, lane_f, float(LANES)), axis=-1, keepdims=True)
    e = jnp.exp(m2 - m1)
    g1 = 1.0 / (1.0 + e)
    g2 = e / (1.0 + e)
    oh1 = lane_f == i1
    oh2 = lane_f == i2
    i1 = i1.astype(I32)
    i2 = i2.astype(I32)
    oh = jnp.where(oh1 | oh2, 1.0, 0.0)
    ri_ = lax.broadcasted_iota(I32, (tm, tm), 0)
    ci_ = lax.broadcasted_iota(I32, (tm, tm), 1)
    below = jnp.where(ri_ > ci_, 1.0, 0.0).astype(BF16)
    ex = jnp.dot(below, oh.astype(BF16), preferred_element_type=F32) + carry_ref[0:1, :]
    r1 = jnp.sum(jnp.where(oh1, ex, 0.0), axis=-1, keepdims=True).astype(I32)
    r2 = jnp.sum(jnp.where(oh2, ex, 0.0), axis=-1, keepdims=True).astype(I32)
    packed = jnp.where(lane == 0, i1, jnp.where(lane == 1, i2, jnp.where(lane == 2, r1, r2)))
    ri_ref[...] = packed[:, 0:ri_ref.shape[1]]
    gate_ref[...] = jnp.where(lane == 0, g1, g2)
    carry_ref[...] = carry_ref[...] + jnp.sum(oh, axis=0, keepdims=True)
    cnt_ref[...] = carry_ref[...]


def _odd_out_route_kernel(o_ref, z_ref, g_ref, w_ref, h_ref, gr_ref, wr_ref,
                          out_ref, ri_ref, gate_ref, cnt_ref, carry_ref):
    h = _odd_out_math(o_ref, z_ref, g_ref, w_ref, h_ref)
    out_ref[...] = h
    _route_math(h, gr_ref, wr_ref, ri_ref, gate_ref, cnt_ref, carry_ref)


def odd_out_route(o, proj, onorm_g, w_out, h, g_route, wr):
    T, D = h.shape
    tm = min(ROW_TILE, T)
    zb = 3 * DN_WIDTH // DN_WIDTH
    return pl.pallas_call(
        _odd_out_route_kernel,
        grid=(T // tm,),
        in_specs=[
            pl.BlockSpec((tm, DN_WIDTH), lambda i: (i, 0)),
            pl.BlockSpec((tm, DN_WIDTH), lambda i: (i, zb)),
            pl.BlockSpec((1, DN_HEAD_DIM), lambda i: (0, 0)),
            pl.BlockSpec(w_out.shape, lambda i: (0, 0)),
            pl.BlockSpec((tm, D), lambda i: (i, 0)),
            pl.BlockSpec((1, D), lambda i: (0, 0)),
            pl.BlockSpec((D, LANES), lambda i: (0, 0)),
        ],
        out_specs=[
            pl.BlockSpec((tm, D), lambda i: (i, 0)),
            pl.BlockSpec((tm, 4), lambda i: (i, 0)),
            pl.BlockSpec((tm, LANES), lambda i: (i, 0)),
            pl.BlockSpec((SUBLANES, LANES), lambda i: (0, 0)),
        ],
        out_shape=[
            jax.ShapeDtypeStruct((T, D), F32),
            jax.ShapeDtypeStruct((T, 4), I32),
            jax.ShapeDtypeStruct((T, LANES), F32),
            jax.ShapeDtypeStruct((SUBLANES, LANES), F32),
        ],
        scratch_shapes=[pltpu.VMEM((SUBLANES, LANES), F32)],
        compiler_params=_cparams(("arbitrary",)),
        name="odd_out_route",
    )(o, proj, onorm_g.reshape(1, DN_HEAD_DIM), w_out, h, g_route.reshape(1, D), wr)


def _row_copy(src_ref, s, dst_ref, d, sem):
    return pltpu.make_async_copy(src_ref.at[pl.ds(s, 1), :], dst_ref.at[pl.ds(d, 1), :], sem)


def _dispatch_kernel(pz_ref, dest_ref, h_ref, g_ref, xs_ref, xn_ref, zb_ref, sem, zsem):
    tm = h_ref.shape[0]
    G = zb_ref.shape[0]

    @pl.when(pl.program_id(0) == 0)
    def _():
        zb_ref[...] = jnp.zeros(zb_ref.shape, F32)

        def zero_copy(e):
            return pltpu.make_async_copy(zb_ref, xs_ref.at[pl.ds(pl.multiple_of(pz_ref[e], G), G), :], zsem)

        for e in range(2 * N_EXPERTS):
            @pl.when(pz_ref[e] >= 0)
            def _():
                zero_copy(e).start()

        for e in range(2 * N_EXPERTS):
            @pl.when(pz_ref[e] >= 0)
            def _():
                zero_copy(e).wait()

    xn_ref[...] = _rms(h_ref[...], g_ref[...], RMS_EPS)

    def issue(r, carry):
        _row_copy(xn_ref, r, xs_ref, dest_ref[2 * r], sem).start(priority=0)
        _row_copy(xn_ref, r, xs_ref, dest_ref[2 * r + 1], sem).start(priority=1)
        return carry

    lax.fori_loop(0, tm, issue, 0, unroll=ROW_DMA_UNROLL)

    def drain(r, carry):
        _row_copy(xn_ref, 0, xs_ref, 0, sem).wait()
        _row_copy(xn_ref, 0, xs_ref, 0, sem).wait()
        return carry

    lax.fori_loop(0, tm, drain, 0, unroll=ROW_DMA_UNROLL)


def dispatch(h, g, dest_flat, zero_tiles, n_slots):
    T, D = h.shape
    tm = min(ROW_TILE, T)
    grid_spec = pltpu.PrefetchScalarGridSpec(
        num_scalar_prefetch=1,
        grid=(T // tm,),
        in_specs=[
            pl.BlockSpec((2 * tm,), lambda i, pz: (i,), memory_space=pltpu.SMEM),
            pl.BlockSpec((tm, D), lambda i, pz: (i, 0)),
            pl.BlockSpec((1, D), lambda i, pz: (0, 0)),
        ],
        out_specs=pl.BlockSpec(memory_space=pl.ANY),
        scratch_shapes=[pltpu.VMEM((tm, D), F32), pltpu.VMEM((MOE_TILE, D), F32),
                        pltpu.SemaphoreType.DMA, pltpu.SemaphoreType.DMA],
    )
    return pl.pallas_call(
        _dispatch_kernel,
        grid_spec=grid_spec,
        out_shape=jax.ShapeDtypeStruct((n_slots, D), F32),
        compiler_params=_cparams(("arbitrary",)),
        name="moe_dispatch",
    )(zero_tiles, dest_flat, h, g.reshape(1, D))


def _gmm_kernel(te_ref, na_ref, x_ref, w1_ref, w3_ref, w2_ref, y_ref, xb_ref, acc_ref):
    b = pl.program_id(0)
    f = pl.program_id(1)

    @pl.when(b < na_ref[0])
    def _():
        @pl.when(f == 0)
        def _():
            xb_ref[...] = x_ref[...].astype(BF16)
            acc_ref[...] = jnp.zeros(acc_ref.shape, F32)

        xb = xb_ref[...]
        a = jnp.dot(xb, w1_ref[...], preferred_element_type=F32)
        c = jnp.dot(xb, w3_ref[...], preferred_element_type=F32)
        hm = (_silu(a) * c).astype(BF16)
        acc_ref[...] += jnp.dot(hm, w2_ref[...], preferred_element_type=F32)

        @pl.when(f == pl.num_programs(1) - 1)
        def _():
            y_ref[...] = acc_ref[...]

    @pl.when((b >= na_ref[0]) & (f == pl.num_programs(1) - 1))
    def _():
        y_ref[...] = jnp.zeros(y_ref.shape, F32)


def gmm(xs, w1, w3, w2, tile_e, n_active, layer):
    P, D = xs.shape
    G = MOE_TILE
    F = w1.shape[3]
    tf = MOE_FF_TILE
    nf = F // tf

    def row_idx(b, f, te, na):
        return (jnp.minimum(b, na[0] - 1), 0)

    def f_idx(b, f, na):
        return jnp.where(b < na[0], f, nf - 1)

    grid_spec = pltpu.PrefetchScalarGridSpec(
        num_scalar_prefetch=2,
        grid=(P // G, nf),
        in_specs=[
            pl.BlockSpec((G, D), row_idx),
            pl.BlockSpec((None, None, D, tf), lambda b, f, te, na: (layer, te[b], 0, f_idx(b, f, na))),
            pl.BlockSpec((None, None, D, tf), lambda b, f, te, na: (layer, te[b], 0, f_idx(b, f, na))),
            pl.BlockSpec((None, None, tf, D), lambda b, f, te, na: (layer, te[b], f_idx(b, f, na), 0)),
        ],
        out_specs=pl.BlockSpec((G, D), lambda b, f, te, na: (b, 0)),
        scratch_shapes=[pltpu.VMEM((G, D), BF16), pltpu.VMEM((G, D), F32)],
    )
    return pl.pallas_call(
        _gmm_kernel,
        grid_spec=grid_spec,
        out_shape=jax.ShapeDtypeStruct((P, D), F32),
        compiler_params=_cparams(("arbitrary", "arbitrary")),
        name="moe_gmm",
    )(tile_e, n_active, xs, w1, w3, w2)


def _combine_ple_kernel(dest_ref, dnext_ref, h_ref, gate_ref, y_ref, g_ref, wg_ref, p_ref, wp_ref, gf_ref,
                        o_ref, yab_ref, sem, *, final_norm):
    i = pl.program_id(0)
    n = pl.num_programs(0)
    tm = h_ref.shape[0]
    slot = lax.rem(i, 2)

    def gather(d_ref, sl):
        def issue(r, carry):
            _row_copy(y_ref, d_ref[2 * r], yab_ref.at[sl, 0], r, sem.at[sl]).start(priority=0)
            _row_copy(y_ref, d_ref[2 * r + 1], yab_ref.at[sl, 1], r, sem.at[sl]).start(priority=1)
            return carry

        lax.fori_loop(0, tm, issue, 0, unroll=ROW_DMA_UNROLL)

    @pl.when(i == 0)
    def _():
        gather(dest_ref, 0)

    @pl.when(i + 1 < n)
    def _():
        gather(dnext_ref, 1 - slot)

    def drain(r, carry):
        _row_copy(y_ref, 0, yab_ref.at[slot, 0], 0, sem.at[slot]).wait()
        _row_copy(y_ref, 0, yab_ref.at[slot, 1], 0, sem.at[slot]).wait()
        return carry

    lax.fori_loop(0, tm, drain, 0, unroll=ROW_DMA_UNROLL)
    gt = gate_ref[...]
    h = h_ref[...] + gt[:, 0:1] * yab_ref[slot, 0] + gt[:, 1:2] * yab_ref[slot, 1]
    o_ref[...] = _ple_math(h, g_ref, wg_ref, p_ref, wp_ref, gf_ref, final_norm)


def combine_ple(h, gates, y, dest_flat, g, wg, p, wp, g_final, final_norm, layer):
    T, D = h.shape
    tm = min(ROW_TILE, T)
    n = T // tm
    return pl.pallas_call(
        functools.partial(_combine_ple_kernel, final_norm=final_norm),
        grid=(n,),
        in_specs=[
            pl.BlockSpec((2 * tm,), lambda i: (i,), memory_space=pltpu.SMEM),
            pl.BlockSpec((2 * tm,), lambda i: (jnp.minimum(i + 1, n - 1),), memory_space=pltpu.SMEM),
            pl.BlockSpec((tm, D), lambda i: (i, 0)),
            pl.BlockSpec((tm, LANES), lambda i: (i, 0)),
            pl.BlockSpec(memory_space=pl.ANY),
            pl.BlockSpec((1, D), lambda i: (0, 0)),
            pl.BlockSpec((None, D, D), lambda i: (layer, 0, 0)),
            pl.BlockSpec((None, tm, PLE_DIM), lambda i: (layer, i, 0)),
            pl.BlockSpec((None, PLE_DIM, D), lambda i: (layer, 0, 0)),
            pl.BlockSpec((1, D), lambda i: (0, 0)),
        ],
        out_specs=pl.BlockSpec((tm, D), lambda i: (i, 0)),
        out_shape=jax.ShapeDtypeStruct((T, D), F32),
        scratch_shapes=[pltpu.VMEM((2, 2, tm, D), F32), pltpu.SemaphoreType.DMA((2,))],
        compiler_params=_cparams(("arbitrary",)),
        name="moe_combine_ple",
    )(dest_flat, dest_flat, h, gates, y, g.reshape(1, D), wg, p, wp, g_final.reshape(1, D))


def _lambda_init(layer_idx):
    return 0.8 - 0.6 * math.exp(-0.3 * layer_idx)


def _even_layer(h, pos_col, pos_row, invf, ln_mix, w_in, w_out, lam_params, subln_g, conv_w, layer_idx):
    qk, vt, gates_conv = even_proj(h, ln_mix, w_in.astype(BF16), pos_col, invf)
    o = diff_attention(qk, vt, pos_col, pos_row, lam_params, subln_g, _lambda_init(layer_idx))
    return even_out(o, gates_conv, conv_w, w_out.astype(BF16), h)


def _odd_mixer_route(h, ln_mix, w_in, conv_w, a_log, dt_bias, onorm_g, w_out, ln_ffn, w_router):
    main_w = 4 * DN_WIDTH
    w_main = w_in[:, :main_w].astype(BF16)
    w_ba = jnp.pad(w_in[:, main_w:], ((0, 0), (0, LANES - 2 * DN_HEADS))).astype(BF16)
    proj, ba = norm_matmul(h, ln_mix, w_main, w_side=w_ba)
    pad8 = lambda v: jnp.pad(v.astype(F32), (DN_HEADS, LANES - 2 * DN_HEADS)).reshape(1, LANES)
    qg, kg, w, u, ai, dl = gdn_prep(proj, ba, conv_w, pad8(a_log), pad8(dt_bias))
    o = gdn_scan(qg, kg, w, u, ai, dl)
    wr = jnp.pad(w_router, ((0, 0), (0, LANES - N_EXPERTS)))
    return odd_out_route(o, proj, onorm_g, w_out.astype(BF16), h, ln_ffn, wr)


def _moe(h, ln_ffn, ri, gates, cnt, w1, w3, w2, layer):
    T = h.shape[0]
    G = MOE_TILE
    counts = cnt[0, :N_EXPERTS].astype(I32)
    padded = ((counts + G - 1) // G) * G
    pends = jnp.cumsum(padded)
    pstarts = pends - padded
    dest = (jnp.take(pstarts, ri[:, 0:2]) + ri[:, 2:4]).reshape(-1)
    n_tiles = (2 * T) // G + N_EXPERTS
    tile_start = jnp.arange(n_tiles, dtype=I32) * G
    tile_e = jnp.minimum(jnp.sum(pends[None, :] <= tile_start[:, None], axis=1), N_EXPERTS - 1).astype(I32)
    n_active = (pends[-1:] // G).astype(I32)
    tail = pends[-1] + jnp.arange(N_EXPERTS, dtype=I32) * G
    zero_tiles = jnp.concatenate([jnp.where(padded > 0, pends - G, -1),
                                  jnp.where(tail < n_tiles * G, tail, -1)]).astype(I32)
    xs = dispatch(h, ln_ffn, dest, zero_tiles, n_tiles * G)
    y = gmm(xs, w1, w3, w2, tile_e, n_active, layer)
    return gates, y, dest


def kernel(x, p, positions, ln_mix, ln_ffn, ln_ple, ln_final, w_in_even, w_out_even, lam_q1, lam_k1, lam_q2, lam_k2, subln_gain, conv_w_short, w_in_odd, conv_w_qkv, a_log, dt_bias, onorm_gain, w_out_odd, w1_dense, w3_dense, w2_dense, w_router, w1_moe, w3_moe, w2_moe, w_ple_gate, w_ple_proj):
    B, S, D = x.shape
    T = B * S
    depth = p.shape[0]
    h = x.reshape(T, D)
    pos_col = positions.reshape(T, 1).astype(I32)
    pos_row = positions.reshape(1, T).astype(I32)
    inv_freq = ROPE_THETA ** (-jnp.arange(0, ROT_DIM, 2, dtype=F32) / ROT_DIM)
    invf = jnp.tile(inv_freq, LANES // (ROT_DIM // 2)).reshape(1, LANES)
    w1d, w3d, w2d = (w.astype(BF16) for w in (w1_dense, w3_dense, w2_dense))
    w1m, w3m, w2m = (w.astype(BF16) for w in (w1_moe, w3_moe, w2_moe))
    wpg, wpp = w_ple_gate.astype(BF16), w_ple_proj.astype(BF16)
    p_rows = p.reshape(depth, T, PLE_DIM)
    for i in range(depth):
        j = i // 2
        if i % 2 == 0:
            lam_params = jnp.stack([lam_q1[j], lam_k1[j], lam_q2[j], lam_k2[j]]).astype(F32)
            h = _even_layer(h, pos_col, pos_row, invf, ln_mix[i], w_in_even[j], w_out_even[j], lam_params,
                            subln_gain[j], conv_w_short[j], i)
            h = ffn_ple(h, ln_ffn[i], w1d, w3d, w2d, j, ln_ple[i], wpg, p_rows, wpp, ln_final,
                        final_norm=(i == depth - 1), ple_layer=i)
        else:
            h, ri, gates, cnt = _odd_mixer_route(h, ln_mix[i], w_in_odd[j], conv_w_qkv[j], a_log[j], dt_bias[j],
                                                 onorm_gain[j], w_out_odd[j], ln_ffn[i], w_router[j])
            gates, y, dest = _moe(h, ln_ffn[i], ri, gates, cnt, w1m, w3m, w2m, j)
            h = combine_ple(h, gates, y, dest, ln_ple[i], wpg, p_rows, wpp, ln_final,
                            final_norm=(i == depth - 1), layer=i)
    return h.reshape(B, S, D)
```

```python
import functools
import math

import jax
import jax.numpy as jnp
import numpy as np
from jax import lax
from jax.experimental import pallas as pl
from jax.experimental.pallas import tpu as pltpu

F32 = jnp.float32
BF16 = jnp.bfloat16
I32 = jnp.int32

D_MODEL = 1024
DEPTH = 4
RMS_EPS = 1e-6
DA_HEADS = 4
DA_HEAD_DIM = 64
DA_V_DIM = 2 * DA_HEAD_DIM
ROPE_THETA = 500000.0
ROT_DIM = DA_HEAD_DIM // 4
SUBLN_EPS = 1e-5
SC_WIDTH = 512
DN_HEADS = 8
DN_HEAD_DIM = 128
DN_WIDTH = DN_HEADS * DN_HEAD_DIM
DN_CHUNK = 64
D_FF = 3584
N_EXPERTS = 8
PLE_DIM = 256

LANES = 128
SUBLANES = 8
HALO = 16
VMEM_LIMIT = 52 * 1024 * 1024

NEG_BIG = -1e30

ROW_TILE = 512
PROJ_COL_TILE = 1024
FFN_ROW_TILE = 512
FF_TILE = 1792
MOE_FF_TILE = 1792
ATT_TQ = 1024
ATT_TK = 1024
ATT_EXP_ROWS = 128
ATT_TQ_SPLIT = 256
GDN_TILE = 1024
GDN_PREP_HEADS = 2
GDN_SCAN_HEADS = 8
MOE_TILE = 512
ROW_DMA_UNROLL = 8


def _cparams(sem):
    return pltpu.CompilerParams(dimension_semantics=sem, vmem_limit_bytes=VMEM_LIMIT)


def _rms(x, g, eps):
    ms = jnp.mean(x * x, axis=-1, keepdims=True)
    return x * lax.rsqrt(ms + eps) * g


def _silu(x):
    return x * (1.0 / (1.0 + jnp.exp(-x)))


def _sigmoid(x):
    return 1.0 / (1.0 + jnp.exp(-x))


def _norm_matmul_kernel(x_ref, g_ref, w_ref, *rest):
    xn = _rms(x_ref[...], g_ref[...], RMS_EPS).astype(BF16)
    if len(rest) == 1:
        (o_ref,) = rest
    else:
        ws_ref, o_ref, os_ref = rest
        os_ref[...] = jnp.dot(xn, ws_ref[...], preferred_element_type=F32)
    tn = PROJ_COL_TILE
    for c in range(o_ref.shape[1] // tn):
        o_ref[:, c * tn:(c + 1) * tn] = jnp.dot(
            xn, w_ref[:, c * tn:(c + 1) * tn], preferred_element_type=F32).astype(o_ref.dtype)


def norm_matmul(x, g, w, w_side=None, out_dtype=BF16):
    T, D = x.shape
    N = w.shape[1]
    tm = min(ROW_TILE, T)
    in_specs = [
        pl.BlockSpec((tm, D), lambda i: (i, 0)),
        pl.BlockSpec((1, D), lambda i: (0, 0)),
        pl.BlockSpec((D, N), lambda i: (0, 0)),
    ]
    out_specs = [pl.BlockSpec((tm, N), lambda i: (i, 0))]
    out_shape = [jax.ShapeDtypeStruct((T, N), out_dtype)]
    args = [x, g.reshape(1, D), w]
    if w_side is not None:
        ns = w_side.shape[1]
        in_specs.append(pl.BlockSpec((D, ns), lambda i: (0, 0)))
        out_specs.append(pl.BlockSpec((tm, ns), lambda i: (i, 0)))
        out_shape.append(jax.ShapeDtypeStruct((T, ns), F32))
        args.append(w_side)
    return pl.pallas_call(
        _norm_matmul_kernel,
        grid=(T // tm,),
        in_specs=in_specs,
        out_specs=out_specs,
        out_shape=out_shape,
        compiler_params=_cparams(("parallel",)),
        name="norm_matmul",
    )(*args)


def _even_proj_kernel(x_ref, g_ref, w_ref, pos_ref, invf_ref, qk_ref, vt_ref, rest_ref):
    xn = _rms(x_ref[...], g_ref[...], RMS_EPS).astype(BF16)
    WQK = qk_ref.shape[1]
    WV = vt_ref.shape[0]

    def proj(c0, c1):
        return jnp.dot(xn, w_ref[:, c0:c1], preferred_element_type=F32)

    pos = pos_ref[...].astype(F32)
    ang = pos * invf_ref[...]
    cos_t = jnp.cos(ang)
    sin_t = jnp.sin(ang)
    d = lax.broadcasted_iota(I32, ang.shape, 1) & (DA_HEAD_DIM - 1)
    half = ROT_DIM // 2
    c_mul = jnp.where(d < ROT_DIM, cos_t, 1.0)
    s_mul = jnp.where(d < half, -sin_t, jnp.where(d < ROT_DIM, sin_t, 0.0))
    qk = proj(0, WQK)
    n_q = WQK // (2 * LANES)
    for c in range(WQK // LANES):
        x = qk[:, c * LANES:(c + 1) * LANES]
        swapped = jnp.where(d < half, pltpu.roll(x, LANES - half, 1), pltpu.roll(x, half, 1))
        r = x * c_mul + swapped * s_mul
        if c < n_q:
            r = r * (DA_HEAD_DIM ** -0.5 * math.log2(math.e))
        qk_ref[:, c * LANES:(c + 1) * LANES] = r.astype(BF16)
    v = proj(WQK, WQK + WV)
    for hd in range(DA_HEADS):
        vt_ref[hd * DA_V_DIM:(hd + 1) * DA_V_DIM, :] = v[:, hd * DA_V_DIM:(hd + 1) * DA_V_DIM].T.astype(BF16)
    WR = rest_ref.shape[1]
    for c0 in range(0, WR, PROJ_COL_TILE):
        c1 = min(c0 + PROJ_COL_TILE, WR)
        rest_ref[:, c0:c1] = proj(WQK + WV + c0, WQK + WV + c1).astype(BF16)


def even_proj(x, g, w, positions_col, invf):
    T, D = x.shape
    WQK = 2 * 2 * DA_HEADS * DA_HEAD_DIM
    WV = DA_HEADS * DA_V_DIM
    WR = w.shape[1] - WQK - WV
    tm = min(ROW_TILE, T)
    return pl.pallas_call(
        _even_proj_kernel,
        grid=(T // tm,),
        in_specs=[
            pl.BlockSpec((tm, D), lambda i: (i, 0)),
            pl.BlockSpec((1, D), lambda i: (0, 0)),
            pl.BlockSpec(w.shape, lambda i: (0, 0)),
            pl.BlockSpec((tm, 1), lambda i: (i, 0)),
            pl.BlockSpec((1, LANES), lambda i: (0, 0)),
        ],
        out_specs=[pl.BlockSpec((tm, WQK), lambda i: (i, 0)),
                   pl.BlockSpec((WV, tm), lambda i: (0, i)),
                   pl.BlockSpec((tm, WR), lambda i: (i, 0))],
        out_shape=[jax.ShapeDtypeStruct((T, WQK), BF16), jax.ShapeDtypeStruct((WV, T), BF16),
                   jax.ShapeDtypeStruct((T, WR), BF16)],
        compiler_params=_cparams(("parallel",)),
        name="even_proj",
    )(x, g.reshape(1, D), w, positions_col, invf)


def _attn_kernel(qt_ref, kt_ref, q_ref, k_ref, v_ref, pq_ref, pk_ref, lam_ref, g_ref, o_ref,
                 m_ref, l_ref, acc_ref, *, tq, tk, lam0):
    p = pl.program_id(1)
    qi = qt_ref[p]
    ki = kt_ref[p]

    @pl.when(ki == 0)
    def _():
        m_ref[...] = jnp.full(m_ref.shape, NEG_BIG, F32)
        l_ref[...] = jnp.zeros(l_ref.shape, F32)
        acc_ref[...] = jnp.zeros(acc_ref.shape, F32)

    tqs = min(tq, ATT_TQ_SPLIT)
    items = [(s, j) for s in range(2) for j in range(tq // tqs)]

    def step(masked):
        q = q_ref[...]
        k = k_ref[...]
        v = v_ref[...]
        def nkeys(j):
            return (j + 1) * tqs if (masked and tq == tk) else tk

        st = {}
        for s, j in items:
            cols = slice(j * tqs, (j + 1) * tqs)
            qs = q[j * tqs:(j + 1) * tqs, s * DA_HEAD_DIM:(s + 1) * DA_HEAD_DIM]
            ks = k[:nkeys(j), s * DA_HEAD_DIM:(s + 1) * DA_HEAD_DIM]
            sc = lax.dot_general(ks, qs, (((1,), (1,)), ((), ())), preferred_element_type=F32)
            if masked:
                sc = jnp.where(pk_ref[0:nkeys(j), :] <= pq_ref[:, cols], sc, NEG_BIG)
            st[s, j] = sc
        for s, j in items:
            cols = slice(j * tqs, (j + 1) * tqs)
            sc = st[s, j]
            nk = nkeys(j)
            m_prev = m_ref[s, :, cols]
            m_new = jnp.maximum(m_prev, jnp.max(sc, axis=0, keepdims=True))
            alpha = jnp.exp2(m_prev - m_new)
            psum = jnp.zeros((1, tqs), F32)
            pchunks = []
            for r0 in range(0, nk, ATT_EXP_ROWS):
                pc = jnp.exp2(sc[r0:r0 + ATT_EXP_ROWS] - m_new)
                psum = psum + jnp.sum(pc, axis=0, keepdims=True)
                pchunks.append(pc.astype(BF16))
            pb = jnp.concatenate(pchunks, axis=0)
            l_ref[s, :, cols] = alpha * l_ref[s, :, cols] + psum
            pv = jnp.dot(v[:, :nk], pb, preferred_element_type=F32)
            acc_ref[s, :, cols] = alpha * acc_ref[s, :, cols] + pv
            m_ref[s, :, cols] = m_new

    crosses = (ki + 1) * tk - 1 > qi * tq

    @pl.when(crosses)
    def _():
        step(True)

    @pl.when(jnp.logical_not(crosses))
    def _():
        step(False)

    @pl.when((ki + 1) * tk >= (qi + 1) * tq)
    def _():
        lm = lam_ref[...]
        s1 = jnp.sum(lm[0:1] * lm[1:2], axis=-1, keepdims=True)
        s2 = jnp.sum(lm[2:3] * lm[3:4], axis=-1, keepdims=True)
        lam = jnp.exp(s1) - jnp.exp(s2) + lam0
        ot = acc_ref[0] * (1.0 / l_ref[0]) - lam * (acc_ref[1] * (1.0 / l_ref[1]))
        ms = jnp.mean(ot * ot, axis=0, keepdims=True)
        ot = ot * lax.rsqrt(ms + SUBLN_EPS) * g_ref[...] * (1.0 - lam0)
        o_ref[...] = ot.T.astype(o_ref.dtype)


def diff_attention(qk, vt, pos_col, pos_row, lam_params, subln_g, lam0):
    T = qk.shape[0]
    tq = min(ATT_TQ, T)
    tk = min(ATT_TK, T)
    nq = T // tq
    pairs = [(qi, ki) for qi in range(nq) for ki in range(-(-((qi + 1) * tq) // tk))]
    qt = jnp.asarray(np.array([a for a, _ in pairs], np.int32))
    kt = jnp.asarray(np.array([b for _, b in pairs], np.int32))
    H = DA_HEADS
    kern = functools.partial(_attn_kernel, tq=tq, tk=tk, lam0=lam0)
    grid_spec = pltpu.PrefetchScalarGridSpec(
        num_scalar_prefetch=2,
        grid=(H, len(pairs)),
        in_specs=[
            pl.BlockSpec((tq, LANES), lambda h, p, qt, kt: (qt[p], h)),
            pl.BlockSpec((tk, LANES), lambda h, p, qt, kt: (kt[p], H + h)),
            pl.BlockSpec((DA_V_DIM, tk), lambda h, p, qt, kt: (h, kt[p])),
            pl.BlockSpec((1, tq), lambda h, p, qt, kt: (0, qt[p])),
            pl.BlockSpec((tk, 1), lambda h, p, qt, kt: (kt[p], 0)),
            pl.BlockSpec((4, DA_HEAD_DIM), lambda h, p, qt, kt: (0, 0)),
            pl.BlockSpec((DA_V_DIM, 1), lambda h, p, qt, kt: (0, 0)),
        ],
        out_specs=pl.BlockSpec((tq, DA_V_DIM), lambda h, p, qt, kt: (qt[p], h)),
        scratch_shapes=[
            pltpu.VMEM((2, 1, tq), F32),
            pltpu.VMEM((2, 1, tq), F32),
            pltpu.VMEM((2, DA_V_DIM, tq), F32),
        ],
    )
    return pl.pallas_call(
        kern,
        grid_spec=grid_spec,
        out_shape=jax.ShapeDtypeStruct((T, H * DA_V_DIM), BF16),
        compiler_params=_cparams(("parallel", "arbitrary")),
        name="diff_attention",
    )(qt, kt, qk, qk, vt, pos_row, pos_col, lam_params, subln_g.reshape(DA_V_DIM, 1))


def _even_out_kernel(o_ref, b_ref, c_ref, x_ref, ch_ref, xh_ref, cw_ref, w_ref, h_ref, out_ref, u_ref):
    tm = o_ref.shape[0]
    u_prev = ch_ref[...].astype(F32) * xh_ref[...].astype(F32)
    u_prev = jnp.where(pl.program_id(0) == 0, 0.0, u_prev)
    u_ref[0:HALO, :] = u_prev
    u_ref[HALO:, :] = c_ref[...].astype(F32) * x_ref[...].astype(F32)
    cw = cw_ref[...]
    conv = (u_ref[HALO - 2:HALO - 2 + tm, :] * cw[0:1]
            + u_ref[HALO - 1:HALO - 1 + tm, :] * cw[1:2]
            + u_ref[HALO:, :] * cw[2:3])
    sc = (b_ref[...].astype(F32) * conv).astype(BF16)
    na = o_ref.shape[1]
    acc = jnp.dot(o_ref[...], w_ref[0:na, :], preferred_element_type=F32)
    acc = acc + jnp.dot(sc, w_ref[na:, :], preferred_element_type=F32)
    out_ref[...] = h_ref[...] + acc


def even_out(attn_o, proj, conv_w, w_out, h):
    T = h.shape[0]
    tm = min(ROW_TILE, T)
    W = SC_WIDTH
    cb = (proj.shape[1] - 3 * W) // W
    hb = tm // HALO
    halo = lambda col: pl.BlockSpec((HALO, W), lambda i: (jnp.maximum(i * hb - 1, 0), col))
    return pl.pallas_call(
        _even_out_kernel,
        grid=(T // tm,),
        in_specs=[
            pl.BlockSpec((tm, attn_o.shape[1]), lambda i: (i, 0)),
            pl.BlockSpec((tm, W), lambda i: (i, cb)),
            pl.BlockSpec((tm, W), lambda i: (i, cb + 1)),
            pl.BlockSpec((tm, W), lambda i: (i, cb + 2)),
            halo(cb + 1),
            halo(cb + 2),
            pl.BlockSpec(conv_w.shape, lambda i: (0, 0)),
            pl.BlockSpec(w_out.shape, lambda i: (0, 0)),
            pl.BlockSpec((tm, D_MODEL), lambda i: (i, 0)),
        ],
        out_specs=pl.BlockSpec((tm, D_MODEL), lambda i: (i, 0)),
        out_shape=jax.ShapeDtypeStruct((T, D_MODEL), F32),
        scratch_shapes=[pltpu.VMEM((tm + HALO, W), F32)],
        compiler_params=_cparams(("parallel",)),
        name="even_out",
    )(attn_o, proj, proj, proj, proj, proj, conv_w, w_out, h)


def _ffn_kernel(h_ref, g_ref, w1_ref, w3_ref, w2_ref, gp_ref, wg_ref, p_ref, wp_ref, gf_ref, o_ref, xn_ref, acc_ref,
                *, final_norm):
    f = pl.program_id(1)

    @pl.when(f == 0)
    def _():
        xn_ref[...] = _rms(h_ref[...], g_ref[...], RMS_EPS).astype(BF16)
        acc_ref[...] = jnp.zeros(acc_ref.shape, F32)

    xn = xn_ref[...]
    a = jnp.dot(xn, w1_ref[...], preferred_element_type=F32)
    b = jnp.dot(xn, w3_ref[...], preferred_element_type=F32)
    hm = (_silu(a) * b).astype(BF16)
    acc_ref[...] += jnp.dot(hm, w2_ref[...], preferred_element_type=F32)

    @pl.when(f == pl.num_programs(1) - 1)
    def _():
        o_ref[...] = _ple_math(h_ref[...] + acc_ref[...], gp_ref, wg_ref, p_ref, wp_ref, gf_ref, final_norm)


def ffn_ple(h, g, w1, w3, w2, layer, g_ple, wg, p, wp, g_final, final_norm, ple_layer):
    T, D = h.shape
    F = w1.shape[2]
    tm = min(FFN_ROW_TILE, T)
    tf = FF_TILE
    return pl.pallas_call(
        functools.partial(_ffn_kernel, final_norm=final_norm),
        grid=(T // tm, F // tf),
        in_specs=[
            pl.BlockSpec((tm, D), lambda i, f: (i, 0)),
            pl.BlockSpec((1, D), lambda i, f: (0, 0)),
            pl.BlockSpec((None, D, tf), lambda i, f: (layer, 0, f)),
            pl.BlockSpec((None, D, tf), lambda i, f: (layer, 0, f)),
            pl.BlockSpec((None, tf, D), lambda i, f: (layer, f, 0)),
            pl.BlockSpec((1, D), lambda i, f: (0, 0)),
            pl.BlockSpec((None, D, D), lambda i, f: (ple_layer, 0, 0)),
            pl.BlockSpec((None, tm, PLE_DIM), lambda i, f: (ple_layer, i, 0)),
            pl.BlockSpec((None, PLE_DIM, D), lambda i, f: (ple_layer, 0, 0)),
            pl.BlockSpec((1, D), lambda i, f: (0, 0)),
        ],
        out_specs=pl.BlockSpec((tm, D), lambda i, f: (i, 0)),
        out_shape=jax.ShapeDtypeStruct((T, D), F32),
        scratch_shapes=[pltpu.VMEM((tm, D), BF16), pltpu.VMEM((tm, D), F32)],
        compiler_params=_cparams(("parallel", "arbitrary")),
        name="ffn_ple",
    )(h, g.reshape(1, D), w1, w3, w2, g_ple.reshape(1, D), wg, p, wp, g_final.reshape(1, D))


def _ple_math(h, g_ref, wg_ref, p_ref, wp_ref, gf_ref, final_norm):
    xn = _rms(h, g_ref[...], RMS_EPS).astype(BF16)
    gate = _sigmoid(jnp.dot(xn, wg_ref[...], preferred_element_type=F32))
    emb = jnp.dot(p_ref[...].astype(BF16), wp_ref[...], preferred_element_type=F32)
    out = h + gate * emb
    if final_norm:
        out = _rms(out, gf_ref[...], RMS_EPS)
    return out


def _dot_hi(a, b):
    return jnp.dot(a, b, preferred_element_type=F32, precision=lax.Precision.HIGHEST)


def _gdn_prep_kernel(q_ref, k_ref, v_ref, qh_ref, kh_ref, vh_ref, ba_ref, cw_ref, alog_ref, dtb_ref,
                     qg_ref, kg_ref, w_ref, u_ref, ai_ref, dl_ref,
                     xs_ref, gc_ref, gl_ref, sg_ref, gct_ref):
    i = pl.program_id(0)
    h = pl.program_id(1)
    R = q_ref.shape[0]
    C = DN_CHUNK
    KC = cw_ref.shape[1]

    def conv_silu(x_ref, halo_ref, j, hp):
        cs = slice(hp * LANES, (hp + 1) * LANES)
        buf = xs_ref.at[3 * hp + j]
        buf[0:HALO, :] = jnp.where(i == 0, 0.0, halo_ref[:, cs].astype(F32))
        buf[HALO:, :] = x_ref[:, cs].astype(F32)
        acc = buf[HALO:, :] * cw_ref[hp, KC - 1:KC, j * LANES:(j + 1) * LANES]
        for t in range(1, KC):
            acc = acc + (buf[HALO - t:HALO - t + R, :]
                         * cw_ref[hp, KC - 1 - t:KC - t, j * LANES:(j + 1) * LANES])
        return _silu(acc)

    def l2n(x):
        return x * lax.rsqrt(jnp.sum(x * x, axis=-1, keepdims=True) + 1e-6)

    chunks = range(R // C)
    rows = [slice(c * C, (c + 1) * C) for c in chunks]
    r64 = lax.broadcasted_iota(I32, (C, C), 0)
    c64 = lax.broadcasted_iota(I32, (C, C), 1)
    incl = r64 >= c64
    strict = r64 > c64

    @pl.when(h == 0)
    def _():
        ba = ba_ref[...]
        sp_in = ba + dtb_ref[...]
        softplus = jnp.maximum(sp_in, 0.0) + jnp.log(1.0 + jnp.exp(-jnp.abs(sp_in)))
        g_all = -jnp.exp(alog_ref[...]) * softplus
        g_wide = jnp.concatenate([g_all[r] for r in rows], axis=1)
        gc_wide = _dot_hi(jnp.where(incl, 1.0, 0.0), g_wide)
        for c in chunks:
            gc_c = gc_wide[:, c * LANES:(c + 1) * LANES]
            gc_ref[rows[c], :] = gc_c
            gl_ref[rows[c], :] = jnp.broadcast_to(gc_c[C - 1:C, :], (C, LANES))
        sg_ref[...] = _sigmoid(ba)
        gct_ref[...] = gc_ref[...].T

    lane = lax.broadcasted_iota(I32, (R, LANES), 1)

    def pick(ref, l):
        return jnp.sum(jnp.where(lane == l, ref[...], 0.0), axis=-1, keepdims=True)

    def bdot(a, b):
        return jnp.dot(a.astype(BF16), b.astype(BF16), preferred_element_type=F32)

    heads = range(qg_ref.shape[1] // LANES)
    items = [(hp, c) for hp in heads for c in chunks]
    decay, kq, rhs, edl = {}, {}, {}, {}
    for hp in heads:
        hh = h * len(heads) + hp
        cs = slice(hp * LANES, (hp + 1) * LANES)
        q = l2n(conv_silu(q_ref, qh_ref, 0, hp)) * (DN_HEAD_DIM ** -0.5)
        k = l2n(conv_silu(k_ref, kh_ref, 1, hp))
        v = conv_silu(v_ref, vh_ref, 2, hp)
        beta = pick(sg_ref, hh)
        gcol = pick(gc_ref, DN_HEADS + hh)
        glast = pick(gl_ref, DN_HEADS + hh)
        grow = gct_ref[pl.ds(DN_HEADS + hh, 1), :]
        eg = jnp.exp(gcol)
        kb = k * beta
        qg_ref[:, cs] = (q * eg).astype(BF16)
        kg_ref[:, cs] = (k * jnp.exp(glast - gcol)).astype(BF16)
        edl[hp] = jnp.broadcast_to(jnp.exp(glast), (R, LANES))
        rhs[hp] = jnp.concatenate([v * beta, kb * eg], axis=-1)
        kbf = k.astype(BF16)
        lhs = jnp.concatenate([kb.astype(BF16).reshape(R // C, C, LANES),
                               q.astype(BF16).reshape(R // C, C, LANES)], axis=1)
        for c in chunks:
            r = rows[c]
            decay[hp, c] = jnp.where(incl, jnp.exp(jnp.where(incl, gcol[r] - grow[:, r], 0.0)), 0.0)
            kq[hp, c] = lax.dot_general(lhs[c], kbf[r], (((1,), (1,)), ((), ())), preferred_element_type=F32)
    pw = {it: -jnp.where(strict, kq[it][:C] * decay[it], 0.0) for it in items}
    n = dict(pw)
    for _ in range(5):
        pw = {it: bdot(pw[it], pw[it]) for it in items}
        n = {it: n[it] + pw[it] + bdot(n[it], pw[it]) for it in items}
    for hp, c in items:
        cs = slice(hp * LANES, (hp + 1) * LANES)
        rc = rhs[hp][rows[c]]
        uw = rc + bdot(n[hp, c], rc)
        u_ref[rows[c], cs] = uw[:, :DN_HEAD_DIM].astype(BF16)
        w_ref[rows[c], cs] = uw[:, DN_HEAD_DIM:].astype(BF16)
        ai_ref[hp, rows[c], :] = (kq[hp, c][C:] * decay[hp, c]).astype(BF16)
        dl_ref[hp, c:c + 1, :] = edl[hp][c * C:c * C + 1, :]


def gdn_prep(proj, ba, conv_w, alog_l, dtb_l):
    T = proj.shape[0]
    R = min(GDN_TILE, T)
    H = DN_HEADS
    nchunk = T // DN_CHUNK
    hb = R // HALO
    HP = GDN_PREP_HEADS
    W = HP * LANES
    col = lambda j: pl.BlockSpec((R, W), lambda i, h: (i, j * (H // HP) + h))
    halo = lambda j: pl.BlockSpec((HALO, W), lambda i, h: (jnp.maximum(i * hb - 1, 0), j * (H // HP) + h))
    KC = conv_w.shape[0]
    cw = conv_w.reshape(KC, 3, H, LANES).transpose(2, 0, 1, 3).reshape(H, KC, 3 * LANES)
    cw_spec = pl.BlockSpec((HP, KC, 3 * LANES), lambda i, h: (h, 0, 0))
    row_out = lambda dt: jax.ShapeDtypeStruct((T, DN_WIDTH), dt)
    out_col = pl.BlockSpec((R, W), lambda i, h: (i, h))
    return pl.pallas_call(
        _gdn_prep_kernel,
        grid=(T // R, H // HP),
        in_specs=[col(0), col(1), col(2), halo(0), halo(1), halo(2),
                  pl.BlockSpec((R, LANES), lambda i, h: (i, 0)),
                  cw_spec,
                  pl.BlockSpec((1, LANES), lambda i, h: (0, 0)),
                  pl.BlockSpec((1, LANES), lambda i, h: (0, 0))],
        out_specs=[out_col, out_col, out_col, out_col,
                   pl.BlockSpec((HP, R, DN_CHUNK), lambda i, h: (h, i, 0)),
                   pl.BlockSpec((HP, R // DN_CHUNK, LANES), lambda i, h: (h, i, 0))],
        out_shape=[row_out(BF16), row_out(BF16), row_out(BF16), row_out(BF16),
                   jax.ShapeDtypeStruct((H, T, DN_CHUNK), BF16),
                   jax.ShapeDtypeStruct((H, nchunk, LANES), F32)],
        scratch_shapes=[pltpu.VMEM((3 * HP, R + HALO, LANES), F32),
                        pltpu.VMEM((R, LANES), F32), pltpu.VMEM((R, LANES), F32), pltpu.VMEM((R, LANES), F32),
                        pltpu.VMEM((LANES, R), F32)],
        compiler_params=_cparams(("parallel", "arbitrary")),
        name="gdn_prep",
    )(proj, proj, proj, proj, proj, proj, ba, cw, alog_l, dtb_l)


def _gdn_scan_kernel(qg_ref, kg_ref, w_ref, u_ref, ai_ref, dl_ref, o_ref, s_ref):
    @pl.when(pl.program_id(1) == 0)
    def _():
        s_ref[...] = jnp.zeros(s_ref.shape, F32)

    C = DN_CHUNK
    D = DN_HEAD_DIM
    heads = range(s_ref.shape[0])

    def chunk(c, carry):
        rows = pl.ds(pl.multiple_of(c * C, C), C)
        s = [s_ref[hh] for hh in heads]
        sb = [x.astype(BF16) for x in s]
        cols = [slice(hh * D, (hh + 1) * D) for hh in heads]
        r1 = [jnp.dot(jnp.concatenate([w_ref[rows, cols[hh]], qg_ref[rows, cols[hh]]], axis=0), sb[hh],
                      preferred_element_type=F32) for hh in heads]
        vb = [(u_ref[rows, cols[hh]].astype(F32) - r1[hh][:C]).astype(BF16) for hh in heads]
        for hh in heads:
            o_ref[rows, cols[hh]] = (r1[hh][C:] + jnp.dot(ai_ref[hh, rows, :], vb[hh],
                                                          preferred_element_type=F32)).astype(o_ref.dtype)
        upd = [lax.dot_general(kg_ref[rows, cols[hh]], vb[hh], (((0,), (0,)), ((), ())),
                               preferred_element_type=F32) for hh in heads]
        for hh in heads:
            s_ref[hh] = s[hh] * dl_ref[hh, pl.ds(c, 1), :] + upd[hh]
        return carry

    lax.fori_loop(0, qg_ref.shape[0] // C, chunk, 0)


def gdn_scan(qg, kg, w, u, ai, dl):
    T = qg.shape[0]
    R = min(GDN_TILE, T)
    H = DN_HEADS
    HB = GDN_SCAN_HEADS
    col = pl.BlockSpec((R, HB * LANES), lambda h, i: (i, h))
    return pl.pallas_call(
        _gdn_scan_kernel,
        grid=(H // HB, T // R),
        in_specs=[col, col, col, col,
                  pl.BlockSpec((HB, R, DN_CHUNK), lambda h, i: (h, i, 0)),
                  pl.BlockSpec((HB, R // DN_CHUNK, LANES), lambda h, i: (h, i, 0))],
        out_specs=col,
        out_shape=jax.ShapeDtypeStruct((T, DN_WIDTH), BF16),
        scratch_shapes=[pltpu.VMEM((HB, DN_HEAD_DIM, DN_HEAD_DIM), F32)],
        compiler_params=_cparams(("parallel", "arbitrary")),
        name="gdn_scan",
    )(qg, kg, w, u, ai, dl)


def _odd_out_math(o_ref, z_ref, g_ref, w_ref, h_ref):
    g = g_ref[...]
    parts = []
    for hd in range(DN_HEADS):
        sl = slice(hd * DN_HEAD_DIM, (hd + 1) * DN_HEAD_DIM)
        parts.append((_rms(o_ref[:, sl].astype(F32), g, RMS_EPS) * _silu(z_ref[:, sl].astype(F32))).astype(BF16))
    y = jnp.concatenate(parts, axis=-1)
    return h_ref[...] + jnp.dot(y, w_ref[...], preferred_element_type=F32)


def _route_math(h, g_ref, wr_ref, ri_ref, gate_ref, cnt_ref, carry_ref):
    i = pl.program_id(0)

    @pl.when(i == 0)
    def _():
        carry_ref[...] = jnp.zeros(carry_ref.shape, F32)

    xn = _rms(h, g_ref[...], RMS_EPS)
    logits = jnp.dot(xn.astype(BF16), wr_ref[...].astype(BF16), preferred_element_type=F32)
    tm = logits.shape[0]
    lane = lax.broadcasted_iota(I32, logits.shape, 1)
    logits = jnp.where(lane < N_EXPERTS, logits, NEG_BIG)
    lane_f = lane.astype(F32)
    m1 = jnp.max(logits, axis=-1, keepdims=True)
    i1 = jnp.min(jnp.where(logits == m1, lane_f, float(LANES)), axis=-1, keepdims=True)
    rest = jnp.where(lane_f == i1, NEG_BIG, logits)
    m2 = jnp.max(rest, axis=-1, keepdims=True)
    i2 = jnp.min(jnp.where(rest == m2, lane_f, float(LANES)), axis=-1, keepdims=True)
    e = jnp.exp(m2 - m1)
    g1 = 1.0 / (1.0 + e)
    g2 = e / (1.0 + e)
    oh1 = lane_f == i1
    oh2 = lane_f == i2
    i1 = i1.astype(I32)
    i2 = i2.astype(I32)
    oh = jnp.where(oh1 | oh2, 1.0, 0.0)
    ri_ = lax.broadcasted_iota(I32, (tm, tm), 0)
    ci_ = lax.broadcasted_iota(I32, (tm, tm), 1)
    below = jnp.where(ri_ > ci_, 1.0, 0.0).astype(BF16)
    ex = jnp.dot(below, oh.astype(BF16), preferred_element_type=F32) + carry_ref[0:1, :]
    r1 = jnp.sum(jnp.where(oh1, ex, 0.0), axis=-1, keepdims=True).astype(I32)
    r2 = jnp.sum(jnp.where(oh2, ex, 0.0), axis=-1, keepdims=True).astype(I32)
    packed = jnp.where(lane == 0, i1, jnp.where(lane == 1, i2, jnp.where(lane == 2, r1, r2)))
    ri_ref[...] = packed[:, 0:ri_ref.shape[1]]
    gate_ref[...] = jnp.where(lane == 0, g1, g2)
    carry_ref[...] = carry_ref[...] + jnp.sum(oh, axis=0, keepdims=True)
    cnt_ref[...] = carry_ref[...]


def _odd_out_route_kernel(o_ref, z_ref, g_ref, w_ref, h_ref, gr_ref, wr_ref,
                          out_ref, ri_ref, gate_ref, cnt_ref, carry_ref):
    h = _odd_out_math(o_ref, z_ref, g_ref, w_ref, h_ref)
    out_ref[...] = h
    _route_math(h, gr_ref, wr_ref, ri_ref, gate_ref, cnt_ref, carry_ref)


def odd_out_route(o, proj, onorm_g, w_out, h, g_route, wr):
    T, D = h.shape
    tm = min(ROW_TILE, T)
    zb = 3 * DN_WIDTH // DN_WIDTH
    return pl.pallas_call(
        _odd_out_route_kernel,
        grid=(T // tm,),
        in_specs=[
            pl.BlockSpec((tm, DN_WIDTH), lambda i: (i, 0)),
            pl.BlockSpec((tm, DN_WIDTH), lambda i: (i, zb)),
            pl.BlockSpec((1, DN_HEAD_DIM), lambda i: (0, 0)),
            pl.BlockSpec(w_out.shape, lambda i: (0, 0)),
            pl.BlockSpec((tm, D), lambda i: (i, 0)),
            pl.BlockSpec((1, D), lambda i: (0, 0)),
            pl.BlockSpec((D, LANES), lambda i: (0, 0)),
        ],
        out_specs=[
            pl.BlockSpec((tm, D), lambda i: (i, 0)),
            pl.BlockSpec((tm, 4), lambda i: (i, 0)),
            pl.BlockSpec((tm, LANES), lambda i: (i, 0)),
            pl.BlockSpec((SUBLANES, LANES), lambda i: (0, 0)),
        ],
        out_shape=[
            jax.ShapeDtypeStruct((T, D), F32),
            jax.ShapeDtypeStruct((T, 4), I32),
            jax.ShapeDtypeStruct((T, LANES), F32),
            jax.ShapeDtypeStruct((SUBLANES, LANES), F32),
        ],
        scratch_shapes=[pltpu.VMEM((SUBLANES, LANES), F32)],
        compiler_params=_cparams(("arbitrary",)),
        name="odd_out_route",
    )(o, proj, onorm_g.reshape(1, DN_HEAD_DIM), w_out, h, g_route.reshape(1, D), wr)


def _row_copy(src_ref, s, dst_ref, d, sem):
    return pltpu.make_async_copy(src_ref.at[pl.ds(s, 1), :], dst_ref.at[pl.ds(d, 1), :], sem)


def _dispatch_kernel(pz_ref, dest_ref, h_ref, g_ref, xs_ref, xn_ref, zb_ref, sem, zsem):
    tm = h_ref.shape[0]
    G = zb_ref.shape[0]

    @pl.when(pl.program_id(0) == 0)
    def _():
        zb_ref[...] = jnp.zeros(zb_ref.shape, F32)

        def zero_copy(e):
            return pltpu.make_async_copy(zb_ref, xs_ref.at[pl.ds(pl.multiple_of(pz_ref[e], G), G), :], zsem)

        for e in range(2 * N_EXPERTS):
            @pl.when(pz_ref[e] >= 0)
            def _():
                zero_copy(e).start()

        for e in range(2 * N_EXPERTS):
            @pl.when(pz_ref[e] >= 0)
            def _():
                zero_copy(e).wait()

    xn_ref[...] = _rms(h_ref[...], g_ref[...], RMS_EPS)

    def issue(r, carry):
        _row_copy(xn_ref, r, xs_ref, dest_ref[2 * r], sem).start(priority=0)
        _row_copy(xn_ref, r, xs_ref, dest_ref[2 * r + 1], sem).start(priority=1)
        return carry

    lax.fori_loop(0, tm, issue, 0, unroll=ROW_DMA_UNROLL)

    def drain(r, carry):
        _row_copy(xn_ref, 0, xs_ref, 0, sem).wait()
        _row_copy(xn_ref, 0, xs_ref, 0, sem).wait()
        return carry

    lax.fori_loop(0, tm, drain, 0, unroll=ROW_DMA_UNROLL)


def dispatch(h, g, dest_flat, zero_tiles, n_slots):
    T, D = h.shape
    tm = min(ROW_TILE, T)
    grid_spec = pltpu.PrefetchScalarGridSpec(
        num_scalar_prefetch=1,
        grid=(T // tm,),
        in_specs=[
            pl.BlockSpec((2 * tm,), lambda i, pz: (i,), memory_space=pltpu.SMEM),
            pl.BlockSpec((tm, D), lambda i, pz: (i, 0)),
            pl.BlockSpec((1, D), lambda i, pz: (0, 0)),
        ],
        out_specs=pl.BlockSpec(memory_space=pl.ANY),
        scratch_shapes=[pltpu.VMEM((tm, D), F32), pltpu.VMEM((MOE_TILE, D), F32),
                        pltpu.SemaphoreType.DMA, pltpu.SemaphoreType.DMA],
    )
    return pl.pallas_call(
        _dispatch_kernel,
        grid_spec=grid_spec,
        out_shape=jax.ShapeDtypeStruct((n_slots, D), F32),
        compiler_params=_cparams(("arbitrary",)),
        name="moe_dispatch",
    )(zero_tiles, dest_flat, h, g.reshape(1, D))


def _gmm_kernel(te_ref, na_ref, x_ref, w1_ref, w3_ref, w2_ref, y_ref, xb_ref, acc_ref):
    b = pl.program_id(0)
    f = pl.program_id(1)

    @pl.when(b < na_ref[0])
    def _():
        @pl.when(f == 0)
        def _():
            xb_ref[...] = x_ref[...].astype(BF16)
            acc_ref[...] = jnp.zeros(acc_ref.shape, F32)

        xb = xb_ref[...]
        a = jnp.dot(xb, w1_ref[...], preferred_element_type=F32)
        c = jnp.dot(xb, w3_ref[...], preferred_element_type=F32)
        hm = (_silu(a) * c).astype(BF16)
        acc_ref[...] += jnp.dot(hm, w2_ref[...], preferred_element_type=F32)

        @pl.when(f == pl.num_programs(1) - 1)
        def _():
            y_ref[...] = acc_ref[...]

    @pl.when((b >= na_ref[0]) & (f == pl.num_programs(1) - 1))
    def _():
        y_ref[...] = jnp.zeros(y_ref.shape, F32)


def gmm(xs, w1, w3, w2, tile_e, n_active, layer):
    P, D = xs.shape
    G = MOE_TILE
    F = w1.shape[3]
    tf = MOE_FF_TILE
    nf = F // tf

    def row_idx(b, f, te, na):
        return (jnp.minimum(b, na[0] - 1), 0)

    def f_idx(b, f, na):
        return jnp.where(b < na[0], f, nf - 1)

    grid_spec = pltpu.PrefetchScalarGridSpec(
        num_scalar_prefetch=2,
        grid=(P // G, nf),
        in_specs=[
            pl.BlockSpec((G, D), row_idx),
            pl.BlockSpec((None, None, D, tf), lambda b, f, te, na: (layer, te[b], 0, f_idx(b, f, na))),
            pl.BlockSpec((None, None, D, tf), lambda b, f, te, na: (layer, te[b], 0, f_idx(b, f, na))),
            pl.BlockSpec((None, None, tf, D), lambda b, f, te, na: (layer, te[b], f_idx(b, f, na), 0)),
        ],
        out_specs=pl.BlockSpec((G, D), lambda b, f, te, na: (b, 0)),
        scratch_shapes=[pltpu.VMEM((G, D), BF16), pltpu.VMEM((G, D), F32)],
    )
    return pl.pallas_call(
        _gmm_kernel,
        grid_spec=grid_spec,
        out_shape=jax.ShapeDtypeStruct((P, D), F32),
        compiler_params=_cparams(("arbitrary", "arbitrary")),
        name="moe_gmm",
    )(tile_e, n_active, xs, w1, w3, w2)


def _combine_ple_kernel(dest_ref, dnext_ref, h_ref, gate_ref, y_ref, g_ref, wg_ref, p_ref, wp_ref, gf_ref,
                        o_ref, yab_ref, sem, *, final_norm):
    i = pl.program_id(0)
    n = pl.num_programs(0)
    tm = h_ref.shape[0]
    slot = lax.rem(i, 2)

    def gather(d_ref, sl):
        def issue(r, carry):
            _row_copy(y_ref, d_ref[2 * r], yab_ref.at[sl, 0], r, sem.at[sl]).start(priority=0)
            _row_copy(y_ref, d_ref[2 * r + 1], yab_ref.at[sl, 1], r, sem.at[sl]).start(priority=1)
            return carry

        lax.fori_loop(0, tm, issue, 0, unroll=ROW_DMA_UNROLL)

    @pl.when(i == 0)
    def _():
        gather(dest_ref, 0)

    @pl.when(i + 1 < n)
    def _():
        gather(dnext_ref, 1 - slot)

    def drain(r, carry):
        _row_copy(y_ref, 0, yab_ref.at[slot, 0], 0, sem.at[slot]).wait()
        _row_copy(y_ref, 0, yab_ref.at[slot, 1], 0, sem.at[slot]).wait()
        return carry

    lax.fori_loop(0, tm, drain, 0, unroll=ROW_DMA_UNROLL)
    gt = gate_ref[...]
    h = h_ref[...] + gt[:, 0:1] * yab_ref[slot, 0] + gt[:, 1:2] * yab_ref[slot, 1]
    o_ref[...] = _ple_math(h, g_ref, wg_ref, p_ref, wp_ref, gf_ref, final_norm)


def combine_ple(h, gates, y, dest_flat, g, wg, p, wp, g_final, final_norm, layer):
    T, D = h.shape
    tm = min(ROW_TILE, T)
    n = T // tm
    return pl.pallas_call(
        functools.partial(_combine_ple_kernel, final_norm=final_norm),
        grid=(n,),
        in_specs=[
            pl.BlockSpec((2 * tm,), lambda i: (i,), memory_space=pltpu.SMEM),
            pl.BlockSpec((2 * tm,), lambda i: (jnp.minimum(i + 1, n - 1),), memory_space=pltpu.SMEM),
            pl.BlockSpec((tm, D), lambda i: (i, 0)),
            pl.BlockSpec((tm, LANES), lambda i: (i, 0)),
            pl.BlockSpec(memory_space=pl.ANY),
            pl.BlockSpec((1, D), lambda i: (0, 0)),
            pl.BlockSpec((None, D, D), lambda i: (layer, 0, 0)),
            pl.BlockSpec((None, tm, PLE_DIM), lambda i: (layer, i, 0)),
            pl.BlockSpec((None, PLE_DIM, D), lambda i: (layer, 0, 0)),
            pl.BlockSpec((1, D), lambda i: (0, 0)),
        ],
        out_specs=pl.BlockSpec((tm, D), lambda i: (i, 0)),
        out_shape=jax.ShapeDtypeStruct((T, D), F32),
        scratch_shapes=[pltpu.VMEM((2, 2, tm, D), F32), pltpu.SemaphoreType.DMA((2,))],
        compiler_params=_cparams(("arbitrary",)),
        name="moe_combine_ple",
    )(dest_flat, dest_flat, h, gates, y, g.reshape(1, D), wg, p, wp, g_final.reshape(1, D))


def _lambda_init(layer_idx):
    return 0.8 - 0.6 * math.exp(-0.3 * layer_idx)


def _even_layer(h, pos_col, pos_row, invf, ln_mix, w_in, w_out, lam_params, subln_g, conv_w, layer_idx):
    qk, vt, gates_conv = even_proj(h, ln_mix, w_in.astype(BF16), pos_col, invf)
    o = diff_attention(qk, vt, pos_col, pos_row, lam_params, subln_g, _lambda_init(layer_idx))
    return even_out(o, gates_conv, conv_w, w_out.astype(BF16), h)


def _odd_mixer_route(h, ln_mix, w_in, conv_w, a_log, dt_bias, onorm_g, w_out, ln_ffn, w_router):
    main_w = 4 * DN_WIDTH
    w_main = w_in[:, :main_w].astype(BF16)
    w_ba = jnp.pad(w_in[:, main_w:], ((0, 0), (0, LANES - 2 * DN_HEADS))).astype(BF16)
    proj, ba = norm_matmul(h, ln_mix, w_main, w_side=w_ba)
    pad8 = lambda v: jnp.pad(v.astype(F32), (DN_HEADS, LANES - 2 * DN_HEADS)).reshape(1, LANES)
    qg, kg, w, u, ai, dl = gdn_prep(proj, ba, conv_w, pad8(a_log), pad8(dt_bias))
    o = gdn_scan(qg, kg, w, u, ai, dl)
    wr = jnp.pad(w_router, ((0, 0), (0, LANES - N_EXPERTS)))
    return odd_out_route(o, proj, onorm_g, w_out.astype(BF16), h, ln_ffn, wr)


def _moe(h, ln_ffn, ri, gates, cnt, w1, w3, w2, layer):
    T = h.shape[0]
    G = MOE_TILE
    counts = cnt[0, :N_EXPERTS].astype(I32)
    padded = ((counts + G - 1) // G) * G
    pends = jnp.cumsum(padded)
    pstarts = pends - padded
    dest = (jnp.take(pstarts, ri[:, 0:2]) + ri[:, 2:4]).reshape(-1)
    n_tiles = (2 * T) // G + N_EXPERTS
    tile_start = jnp.arange(n_tiles, dtype=I32) * G
    tile_e = jnp.minimum(jnp.sum(pends[None, :] <= tile_start[:, None], axis=1), N_EXPERTS - 1).astype(I32)
    n_active = (pends[-1:] // G).astype(I32)
    tail = pends[-1] + jnp.arange(N_EXPERTS, dtype=I32) * G
    zero_tiles = jnp.concatenate([jnp.where(padded > 0, pends - G, -1),
                                  jnp.where(tail < n_tiles * G, tail, -1)]).astype(I32)
    xs = dispatch(h, ln_ffn, dest, zero_tiles, n_tiles * G)
    y = gmm(xs, w1, w3, w2, tile_e, n_active, layer)
    return gates, y, dest


def kernel(x, p, positions, ln_mix, ln_ffn, ln_ple, ln_final, w_in_even, w_out_even, lam_q1, lam_k1, lam_q2, lam_k2, subln_gain, conv_w_short, w_in_odd, conv_w_qkv, a_log, dt_bias, onorm_gain, w_out_odd, w1_dense, w3_dense, w2_dense, w_router, w1_moe, w3_moe, w2_moe, w_ple_gate, w_ple_proj):
    B, S, D = x.shape
    T = B * S
    depth = p.shape[0]
    h = x.reshape(T, D)
    pos_col = positions.reshape(T, 1).astype(I32)
    pos_row = positions.reshape(1, T).astype(I32)
    inv_freq = ROPE_THETA ** (-jnp.arange(0, ROT_DIM, 2, dtype=F32) / ROT_DIM)
    invf = jnp.tile(inv_freq, LANES // (ROT_DIM // 2)).reshape(1, LANES)
    w1d, w3d, w2d = (w.astype(BF16) for w in (w1_dense, w3_dense, w2_dense))
    w1m, w3m, w2m = (w.astype(BF16) for w in (w1_moe, w3_moe, w2_moe))
    wpg, wpp = w_ple_gate.astype(BF16), w_ple_proj.astype(BF16)
    p_rows = p.reshape(depth, T, PLE_DIM)
    for i in range(depth):
        j = i // 2
        if i % 2 == 0:
            lam_params = jnp.stack([lam_q1[j], lam_k1[j], lam_q2[j], lam_k2[j]]).astype(F32)
            h = _even_layer(h, pos_col, pos_row, invf, ln_mix[i], w_in_even[j], w_out_even[j], lam_params,
                            subln_gain[j], conv_w_short[j], i)
            h = ffn_ple(h, ln_ffn[i], w1d, w3d, w2d, j, ln_ple[i], wpg, p_rows, wpp, ln_final,
                        final_norm=(i == depth - 1), ple_layer=i)
        else:
            h, ri, gates, cnt = _odd_mixer_route(h, ln_mix[i], w_in_odd[j], conv_w_qkv[j], a_log[j], dt_bias[j],
                                                 onorm_gain[j], w_out_odd[j], ln_ffn[i], w_router[j])
            gates, y, dest = _moe(h, ln_ffn[i], ri, gates, cnt, w1m, w3m, w2m, j)
            h = combine_ple(h, gates, y, dest, ln_ple[i], wpg, p_rows, wpp, ln_final,
                            final_norm=(i == depth - 1), layer=i)
    return h.reshape(B, S, D)
```

```python
import functools
import math

import jax
import jax.numpy as jnp
import numpy as np
from jax import lax
from jax.experimental import pallas as pl
from jax.experimental.pallas import tpu as pltpu

F32 = jnp.float32
BF16 = jnp.bfloat16
I32 = jnp.int32

D_MODEL = 1024
DEPTH = 4
RMS_EPS = 1e-6
DA_HEADS = 4
DA_HEAD_DIM = 64
DA_V_DIM = 2 * DA_HEAD_DIM
ROPE_THETA = 500000.0
ROT_DIM = DA_HEAD_DIM // 4
SUBLN_EPS = 1e-5
SC_WIDTH = 512
DN_HEADS = 8
DN_HEAD_DIM = 128
DN_WIDTH = DN_HEADS * DN_HEAD_DIM
DN_CHUNK = 64
D_FF = 3584
N_EXPERTS = 8
PLE_DIM = 256

LANES = 128
SUBLANES = 8
HALO = 16
VMEM_LIMIT = 52 * 1024 * 1024

NEG_BIG = -1e30

ROW_TILE = 512
PROJ_COL_TILE = 1024
FFN_ROW_TILE = 512
FF_TILE = 1792
MOE_FF_TILE = 1792
ATT_TQ = 1024
ATT_TK = 1024
ATT_EXP_ROWS = 128
ATT_TQ_SPLIT = 256
GDN_TILE = 1024
GDN_PREP_HEADS = 2
GDN_SCAN_HEADS = 8
MOE_TILE = 512
ROW_DMA_UNROLL = 8


def _cparams(sem):
    return pltpu.CompilerParams(dimension_semantics=sem, vmem_limit_bytes=VMEM_LIMIT)


def _rms(x, g, eps):
    ms = jnp.mean(x * x, axis=-1, keepdims=True)
    return x * lax.rsqrt(ms + eps) * g


def _silu(x):
    return x * (1.0 / (1.0 + jnp.exp(-x)))


def _sigmoid(x):
    return 1.0 / (1.0 + jnp.exp(-x))


def _norm_matmul_kernel(x_ref, g_ref, w_ref, *rest):
    xn = _rms(x_ref[...], g_ref[...], RMS_EPS).astype(BF16)
    if len(rest) == 1:
        (o_ref,) = rest
    else:
        ws_ref, o_ref, os_ref = rest
        os_ref[...] = jnp.dot(xn, ws_ref[...], preferred_element_type=F32)
    tn = PROJ_COL_TILE
    for c in range(o_ref.shape[1] // tn):
        o_ref[:, c * tn:(c + 1) * tn] = jnp.dot(
            xn, w_ref[:, c * tn:(c + 1) * tn], preferred_element_type=F32).astype(o_ref.dtype)


def norm_matmul(x, g, w, w_side=None, out_dtype=BF16):
    T, D = x.shape
    N = w.shape[1]
    tm = min(ROW_TILE, T)
    in_specs = [
        pl.BlockSpec((tm, D), lambda i: (i, 0)),
        pl.BlockSpec((1, D), lambda i: (0, 0)),
        pl.BlockSpec((D, N), lambda i: (0, 0)),
    ]
    out_specs = [pl.BlockSpec((tm, N), lambda i: (i, 0))]
    out_shape = [jax.ShapeDtypeStruct((T, N), out_dtype)]
    args = [x, g.reshape(1, D), w]
    if w_side is not None:
        ns = w_side.shape[1]
        in_specs.append(pl.BlockSpec((D, ns), lambda i: (0, 0)))
        out_specs.append(pl.BlockSpec((tm, ns), lambda i: (i, 0)))
        out_shape.append(jax.ShapeDtypeStruct((T, ns), F32))
        args.append(w_side)
    return pl.pallas_call(
        _norm_matmul_kernel,
        grid=(T // tm,),
        in_specs=in_specs,
        out_specs=out_specs,
        out_shape=out_shape,
        compiler_params=_cparams(("parallel",)),
        name="norm_matmul",
    )(*args)


def _even_proj_kernel(x_ref, g_ref, w_ref, pos_ref, invf_ref, qk_ref, vt_ref, rest_ref):
    xn = _rms(x_ref[...], g_ref[...], RMS_EPS).astype(BF16)
    WQK = qk_ref.shape[1]
    WV = vt_ref.shape[0]

    def proj(c0, c1):
        return jnp.dot(xn, w_ref[:, c0:c1], preferred_element_type=F32)

    pos = pos_ref[...].astype(F32)
    ang = pos * invf_ref[...]
    cos_t = jnp.cos(ang)
    sin_t = jnp.sin(ang)
    d = lax.broadcasted_iota(I32, ang.shape, 1) & (DA_HEAD_DIM - 1)
    half = ROT_DIM // 2
    c_mul = jnp.where(d < ROT_DIM, cos_t, 1.0)
    s_mul = jnp.where(d < half, -sin_t, jnp.where(d < ROT_DIM, sin_t, 0.0))
    qk = proj(0, WQK)
    n_q = WQK // (2 * LANES)
    for c in range(WQK // LANES):
        x = qk[:, c * LANES:(c + 1) * LANES]
        swapped = jnp.where(d < half, pltpu.roll(x, LANES - half, 1), pltpu.roll(x, half, 1))
        r = x * c_mul + swapped * s_mul
        if c < n_q:
            r = r * (DA_HEAD_DIM ** -0.5 * math.log2(math.e))
        qk_ref[:, c * LANES:(c + 1) * LANES] = r.astype(BF16)
    v = proj(WQK, WQK + WV)
    for hd in range(DA_HEADS):
        vt_ref[hd * DA_V_DIM:(hd + 1) * DA_V_DIM, :] = v[:, hd * DA_V_DIM:(hd + 1) * DA_V_DIM].T.astype(BF16)
    WR = rest_ref.shape[1]
    for c0 in range(0, WR, PROJ_COL_TILE):
        c1 = min(c0 + PROJ_COL_TILE, WR)
        rest_ref[:, c0:c1] = proj(WQK + WV + c0, WQK + WV + c1).astype(BF16)


def even_proj(x, g, w, positions_col, invf):
    T, D = x.shape
    WQK = 2 * 2 * DA_HEADS * DA_HEAD_DIM
    WV = DA_HEADS * DA_V_DIM
    WR = w.shape[1] - WQK - WV
    tm = min(ROW_TILE, T)
    return pl.pallas_call(
        _even_proj_kernel,
        grid=(T // tm,),
        in_specs=[
            pl.BlockSpec((tm, D), lambda i: (i, 0)),
            pl.BlockSpec((1, D), lambda i: (0, 0)),
            pl.BlockSpec(w.shape, lambda i: (0, 0)),
            pl.BlockSpec((tm, 1), lambda i: (i, 0)),
            pl.BlockSpec((1, LANES), lambda i: (0, 0)),
        ],
        out_specs=[pl.BlockSpec((tm, WQK), lambda i: (i, 0)),
                   pl.BlockSpec((WV, tm), lambda i: (0, i)),
                   pl.BlockSpec((tm, WR), lambda i: (i, 0))],
        out_shape=[jax.ShapeDtypeStruct((T, WQK), BF16), jax.ShapeDtypeStruct((WV, T), BF16),
                   jax.ShapeDtypeStruct((T, WR), BF16)],
        compiler_params=_cparams(("parallel",)),
        name="even_proj",
    )(x, g.reshape(1, D), w, positions_col, invf)


def _attn_kernel(qt_ref, kt_ref, q_ref, k_ref, v_ref, pq_ref, pk_ref, lam_ref, g_ref, *rest, tq, tk, lam0, n_cast):
    cast_in = rest[:n_cast]
    o_ref = rest[n_cast]
    cast_out = rest[n_cast + 1:2 * n_cast + 1]
    m_ref, l_ref, acc_ref = rest[2 * n_cast + 1:]
    for wi_ref, wo_ref in zip(cast_in, cast_out):
        wo_ref[...] = wi_ref[...].astype(BF16)

    p = pl.program_id(1)
    qi = qt_ref[p]
    ki = kt_ref[p]

    @pl.when(ki == 0)
    def _():
        m_ref[...] = jnp.full(m_ref.shape, NEG_BIG, F32)
        l_ref[...] = jnp.zeros(l_ref.shape, F32)
        acc_ref[...] = jnp.zeros(acc_ref.shape, F32)

    tqs = min(tq, ATT_TQ_SPLIT)
    items = [(s, j) for s in range(2) for j in range(tq // tqs)]

    def step(masked):
        q = q_ref[...]
        k = k_ref[...]
        v = v_ref[...]
        def nkeys(j):
            return (j + 1) * tqs if (masked and tq == tk) else tk

        st = {}
        for s, j in items:
            cols = slice(j * tqs, (j + 1) * tqs)
            qs = q[j * tqs:(j + 1) * tqs, s * DA_HEAD_DIM:(s + 1) * DA_HEAD_DIM]
            ks = k[:nkeys(j), s * DA_HEAD_DIM:(s + 1) * DA_HEAD_DIM]
            sc = lax.dot_general(ks, qs, (((1,), (1,)), ((), ())), preferred_element_type=F32)
            if masked:
                sc = jnp.where(pk_ref[0:nkeys(j), :] <= pq_ref[:, cols], sc, NEG_BIG)
            st[s, j] = sc
        for s, j in items:
            cols = slice(j * tqs, (j + 1) * tqs)
            sc = st[s, j]
            nk = nkeys(j)
            m_prev = m_ref[s, :, cols]
            m_new = jnp.maximum(m_prev, jnp.max(sc, axis=0, keepdims=True))
            alpha = jnp.exp2(m_prev - m_new)
            psum = jnp.zeros((1, tqs), F32)
            pchunks = []
            for r0 in range(0, nk, ATT_EXP_ROWS):
                pc = jnp.exp2(sc[r0:r0 + ATT_EXP_ROWS] - m_new)
                psum = psum + jnp.sum(pc, axis=0, keepdims=True)
                pchunks.append(pc.astype(BF16))
            pb = jnp.concatenate(pchunks, axis=0)
            l_ref[s, :, cols] = alpha * l_ref[s, :, cols] + psum
            pv = jnp.dot(v[:, :nk], pb, preferred_element_type=F32)
            acc_ref[s, :, cols] = alpha * acc_ref[s, :, cols] + pv
            m_ref[s, :, cols] = m_new

    crosses = (ki + 1) * tk - 1 > qi * tq

    @pl.when(crosses)
    def _():
        step(True)

    @pl.when(jnp.logical_not(crosses))
    def _():
        step(False)

    @pl.when((ki + 1) * tk >= (qi + 1) * tq)
    def _():
        lm = lam_ref[...]
        s1 = jnp.sum(lm[0:1] * lm[1:2], axis=-1, keepdims=True)
        s2 = jnp.sum(lm[2:3] * lm[3:4], axis=-1, keepdims=True)
        lam = jnp.exp(s1) - jnp.exp(s2) + lam0
        ot = acc_ref[0] * (1.0 / l_ref[0]) - lam * (acc_ref[1] * (1.0 / l_ref[1]))
        ms = jnp.mean(ot * ot, axis=0, keepdims=True)
        ot = ot * lax.rsqrt(ms + SUBLN_EPS) * g_ref[...] * (1.0 - lam0)
        o_ref[...] = ot.T.astype(o_ref.dtype)


def _cast_blocks(rows, steps):
    for nb in range(min(steps, rows), 0, -1):
        if rows % nb == 0 and (rows // nb) % HALO == 0:
            return nb
    raise ValueError((rows, steps))


def diff_attention(qk, vt, pos_col, pos_row, lam_params, subln_g, lam0, cast=()):
    T = qk.shape[0]
    tq = min(ATT_TQ, T)
    tk = min(ATT_TK, T)
    nq = T // tq
    pairs = [(qi, ki) for qi in range(nq) for ki in range(-(-((qi + 1) * tq) // tk))]
    qt = jnp.asarray(np.array([a for a, _ in pairs], np.int32))
    kt = jnp.asarray(np.array([b for _, b in pairs], np.int32))
    H = DA_HEADS
    npairs = len(pairs)
    kern = functools.partial(_attn_kernel, tq=tq, tk=tk, lam0=lam0, n_cast=len(cast))
    cast_in_specs, cast_out_specs, cast_out_shapes = [], [], []
    for w, layer in cast:
        _, rows, cols = w.shape
        nb = _cast_blocks(rows, H * npairs)
        br = rows // nb

        def blk(h, p, nb=nb):
            return jnp.minimum(h * npairs + p, nb - 1)

        cast_in_specs.append(pl.BlockSpec((None, br, cols), lambda h, p, qt, kt, layer=layer, blk=blk: (layer, blk(h, p), 0)))
        cast_out_specs.append(pl.BlockSpec((br, cols), lambda h, p, qt, kt, blk=blk: (blk(h, p), 0)))
        cast_out_shapes.append(jax.ShapeDtypeStruct((rows, cols), BF16))
    grid_spec = pltpu.PrefetchScalarGridSpec(
        num_scalar_prefetch=2,
        grid=(H, npairs),
        in_specs=[
            pl.BlockSpec((tq, LANES), lambda h, p, qt, kt: (qt[p], h)),
            pl.BlockSpec((tk, LANES), lambda h, p, qt, kt: (kt[p], H + h)),
            pl.BlockSpec((DA_V_DIM, tk), lambda h, p, qt, kt: (h, kt[p])),
            pl.BlockSpec((1, tq), lambda h, p, qt, kt: (0, qt[p])),
            pl.BlockSpec((tk, 1), lambda h, p, qt, kt: (kt[p], 0)),
            pl.BlockSpec((4, DA_HEAD_DIM), lambda h, p, qt, kt: (0, 0)),
            pl.BlockSpec((DA_V_DIM, 1), lambda h, p, qt, kt: (0, 0)),
        ] + cast_in_specs,
        out_specs=[pl.BlockSpec((tq, DA_V_DIM), lambda h, p, qt, kt: (qt[p], h))] + cast_out_specs,
        scratch_shapes=[
            pltpu.VMEM((2, 1, tq), F32),
            pltpu.VMEM((2, 1, tq), F32),
            pltpu.VMEM((2, DA_V_DIM, tq), F32),
        ],
    )
    outs = pl.pallas_call(
        kern,
        grid_spec=grid_spec,
        out_shape=[jax.ShapeDtypeStruct((T, H * DA_V_DIM), BF16)] + cast_out_shapes,
        compiler_params=_cparams(("arbitrary", "arbitrary")),
        name="diff_attention",
    )(qt, kt, qk, qk, vt, pos_row, pos_col, lam_params, subln_g.reshape(DA_V_DIM, 1), *[w for w, _ in cast])
    return outs[0], outs[1:]


def _even_out_kernel(o_ref, b_ref, c_ref, x_ref, ch_ref, xh_ref, cw_ref, w_ref, h_ref, out_ref, u_ref):
    tm = o_ref.shape[0]
    u_prev = ch_ref[...].astype(F32) * xh_ref[...].astype(F32)
    u_prev = jnp.where(pl.program_id(0) == 0, 0.0, u_prev)
    u_ref[0:HALO, :] = u_prev
    u_ref[HALO:, :] = c_ref[...].astype(F32) * x_ref[...].astype(F32)
    cw = cw_ref[...]
    conv = (u_ref[HALO - 2:HALO - 2 + tm, :] * cw[0:1]
            + u_ref[HALO - 1:HALO - 1 + tm, :] * cw[1:2]
            + u_ref[HALO:, :] * cw[2:3])
    sc = (b_ref[...].astype(F32) * conv).astype(BF16)
    na = o_ref.shape[1]
    acc = jnp.dot(o_ref[...], w_ref[0:na, :], preferred_element_type=F32)
    acc = acc + jnp.dot(sc, w_ref[na:, :], preferred_element_type=F32)
    out_ref[...] = h_ref[...] + acc


def even_out(attn_o, proj, conv_w, w_out, h):
    T = h.shape[0]
    tm = min(ROW_TILE, T)
    W = SC_WIDTH
    cb = (proj.shape[1] - 3 * W) // W
    hb = tm // HALO
    halo = lambda col: pl.BlockSpec((HALO, W), lambda i: (jnp.maximum(i * hb - 1, 0), col))
    return pl.pallas_call(
        _even_out_kernel,
        grid=(T // tm,),
        in_specs=[
            pl.BlockSpec((tm, attn_o.shape[1]), lambda i: (i, 0)),
            pl.BlockSpec((tm, W), lambda i: (i, cb)),
            pl.BlockSpec((tm, W), lambda i: (i, cb + 1)),
            pl.BlockSpec((tm, W), lambda i: (i, cb + 2)),
            halo(cb + 1),
            halo(cb + 2),
            pl.BlockSpec(conv_w.shape, lambda i: (0, 0)),
            pl.BlockSpec(w_out.shape, lambda i: (0, 0)),
            pl.BlockSpec((tm, D_MODEL), lambda i: (i, 0)),
        ],
        out_specs=pl.BlockSpec((tm, D_MODEL), lambda i: (i, 0)),
        out_shape=jax.ShapeDtypeStruct((T, D_MODEL), F32),
        scratch_shapes=[pltpu.VMEM((tm + HALO, W), F32)],
        compiler_params=_cparams(("parallel",)),
        name="even_out",
    )(attn_o, proj, proj, proj, proj, proj, conv_w, w_out, h)


def _ffn_kernel(h_ref, g_ref, w1_ref, w3_ref, w2_ref, gp_ref, wg_ref, p_ref, wp_ref, gf_ref, o_ref, xn_ref, acc_ref,
                *, final_norm):
    f = pl.program_id(1)

    @pl.when(f == 0)
    def _():
        xn_ref[...] = _rms(h_ref[...], g_ref[...], RMS_EPS).astype(BF16)
        acc_ref[...] = jnp.zeros(acc_ref.shape, F32)

    xn = xn_ref[...]
    a = jnp.dot(xn, w1_ref[...], preferred_element_type=F32)
    b = jnp.dot(xn, w3_ref[...], preferred_element_type=F32)
    hm = (_silu(a) * b).astype(BF16)
    acc_ref[...] += jnp.dot(hm, w2_ref[...], preferred_element_type=F32)

    @pl.when(f == pl.num_programs(1) - 1)
    def _():
        o_ref[...] = _ple_math(h_ref[...] + acc_ref[...], gp_ref, wg_ref, p_ref, wp_ref, gf_ref, final_norm)


def ffn_ple(h, g, w1, w3, w2, layer, g_ple, wg, p, wp, g_final, final_norm, ple_layer):
    T, D = h.shape
    F = w1.shape[2]
    tm = min(FFN_ROW_TILE, T)
    tf = FF_TILE
    return pl.pallas_call(
        functools.partial(_ffn_kernel, final_norm=final_norm),
        grid=(T // tm, F // tf),
        in_specs=[
            pl.BlockSpec((tm, D), lambda i, f: (i, 0)),
            pl.BlockSpec((1, D), lambda i, f: (0, 0)),
            pl.BlockSpec((None, D, tf), lambda i, f: (layer, 0, f)),
            pl.BlockSpec((None, D, tf), lambda i, f: (layer, 0, f)),
            pl.BlockSpec((None, tf, D), lambda i, f: (layer, f, 0)),
            pl.BlockSpec((1, D), lambda i, f: (0, 0)),
            pl.BlockSpec((None, D, D), lambda i, f: (ple_layer, 0, 0)),
            pl.BlockSpec((None, tm, PLE_DIM), lambda i, f: (ple_layer, i, 0)),
            pl.BlockSpec((None, PLE_DIM, D), lambda i, f: (ple_layer, 0, 0)),
            pl.BlockSpec((1, D), lambda i, f: (0, 0)),
        ],
        out_specs=pl.BlockSpec((tm, D), lambda i, f: (i, 0)),
        out_shape=jax.ShapeDtypeStruct((T, D), F32),
        scratch_shapes=[pltpu.VMEM((tm, D), BF16), pltpu.VMEM((tm, D), F32)],
        compiler_params=_cparams(("parallel", "arbitrary")),
        name="ffn_ple",
    )(h, g.reshape(1, D), w1, w3, w2, g_ple.reshape(1, D), wg, p, wp, g_final.reshape(1, D))


def _ple_math(h, g_ref, wg_ref, p_ref, wp_ref, gf_ref, final_norm):
    xn = _rms(h, g_ref[...], RMS_EPS).astype(BF16)
    gate = _sigmoid(jnp.dot(xn, wg_ref[...], preferred_element_type=F32))
    emb = jnp.dot(p_ref[...].astype(BF16), wp_ref[...], preferred_element_type=F32)
    out = h + gate * emb
    if final_norm:
        out = _rms(out, gf_ref[...], RMS_EPS)
    return out


def _dot_hi(a, b):
    return jnp.dot(a, b, preferred_element_type=F32, precision=lax.Precision.HIGHEST)


def _gdn_prep_kernel(q_ref, k_ref, v_ref, qh_ref, kh_ref, vh_ref, ba_ref, cw_ref, alog_ref, dtb_ref,
                     qg_ref, kg_ref, w_ref, u_ref, ai_ref, dl_ref,
                     xs_ref, gc_ref, gl_ref, sg_ref, gct_ref):
    i = pl.program_id(0)
    h = pl.program_id(1)
    R = q_ref.shape[0]
    C = DN_CHUNK
    KC = cw_ref.shape[1]

    def conv_silu(x_ref, halo_ref, j, hp):
        cs = slice(hp * LANES, (hp + 1) * LANES)
        buf = xs_ref.at[3 * hp + j]
        buf[0:HALO, :] = jnp.where(i == 0, 0.0, halo_ref[:, cs].astype(F32))
        buf[HALO:, :] = x_ref[:, cs].astype(F32)
        acc = buf[HALO:, :] * cw_ref[hp, KC - 1:KC, j * LANES:(j + 1) * LANES]
        for t in range(1, KC):
            acc = acc + (buf[HALO - t:HALO - t + R, :]
                         * cw_ref[hp, KC - 1 - t:KC - t, j * LANES:(j + 1) * LANES])
        return _silu(acc)

    def l2n(x):
        return x * lax.rsqrt(jnp.sum(x * x, axis=-1, keepdims=True) + 1e-6)

    chunks = range(R // C)
    rows = [slice(c * C, (c + 1) * C) for c in chunks]
    r64 = lax.broadcasted_iota(I32, (C, C), 0)
    c64 = lax.broadcasted_iota(I32, (C, C), 1)
    incl = r64 >= c64
    strict = r64 > c64

    @pl.when(h == 0)
    def _():
        ba = ba_ref[...]
        sp_in = ba + dtb_ref[...]
        softplus = jnp.maximum(sp_in, 0.0) + jnp.log(1.0 + jnp.exp(-jnp.abs(sp_in)))
        g_all = -jnp.exp(alog_ref[...]) * softplus
        g_wide = jnp.concatenate([g_all[r] for r in rows], axis=1)
        gc_wide = _dot_hi(jnp.where(incl, 1.0, 0.0), g_wide)
        for c in chunks:
            gc_c = gc_wide[:, c * LANES:(c + 1) * LANES]
            gc_ref[rows[c], :] = gc_c
            gl_ref[rows[c], :] = jnp.broadcast_to(gc_c[C - 1:C, :], (C, LANES))
        sg_ref[...] = _sigmoid(ba)
        gct_ref[...] = gc_ref[...].T

    lane = lax.broadcasted_iota(I32, (R, LANES), 1)

    def pick(ref, l):
        return jnp.sum(jnp.where(lane == l, ref[...], 0.0), axis=-1, keepdims=True)

    def bdot(a, b):
        return jnp.dot(a.astype(BF16), b.astype(BF16), preferred_element_type=F32)

    heads = range(qg_ref.shape[1] // LANES)
    items = [(hp, c) for hp in heads for c in chunks]
    decay, kq, rhs, edl = {}, {}, {}, {}
    for hp in heads:
        hh = h * len(heads) + hp
        cs = slice(hp * LANES, (hp + 1) * LANES)
        q = l2n(conv_silu(q_ref, qh_ref, 0, hp)) * (DN_HEAD_DIM ** -0.5)
        k = l2n(conv_silu(k_ref, kh_ref, 1, hp))
        v = conv_silu(v_ref, vh_ref, 2, hp)
        beta = pick(sg_ref, hh)
        gcol = pick(gc_ref, DN_HEADS + hh)
        glast = pick(gl_ref, DN_HEADS + hh)
        grow = gct_ref[pl.ds(DN_HEADS + hh, 1), :]
        eg = jnp.exp(gcol)
        kb = k * beta
        qg_ref[:, cs] = (q * eg).astype(BF16)
        kg_ref[:, cs] = (k * jnp.exp(glast - gcol)).astype(BF16)
        edl[hp] = jnp.broadcast_to(jnp.exp(glast), (R, LANES))
        rhs[hp] = jnp.concatenate([v * beta, kb * eg], axis=-1)
        kbf = k.astype(BF16)
        lhs = jnp.concatenate([kb.astype(BF16).reshape(R // C, C, LANES),
                               q.astype(BF16).reshape(R // C, C, LANES)], axis=1)
        for c in chunks:
            r = rows[c]
            decay[hp, c] = jnp.where(incl, jnp.exp(jnp.where(incl, gcol[r] - grow[:, r], 0.0)), 0.0)
            kq[hp, c] = lax.dot_general(lhs[c], kbf[r], (((1,), (1,)), ((), ())), preferred_element_type=F32)
    pw = {it: -jnp.where(strict, kq[it][:C] * decay[it], 0.0) for it in items}
    n = dict(pw)
    for _ in range(5):
        pw = {it: bdot(pw[it], pw[it]) for it in items}
        n = {it: n[it] + pw[it] + bdot(n[it], pw[it]) for it in items}
    for hp, c in items:
        cs = slice(hp * LANES, (hp + 1) * LANES)
        rc = rhs[hp][rows[c]]
        uw = rc + bdot(n[hp, c], rc)
        u_ref[rows[c], cs] = uw[:, :DN_HEAD_DIM]
        w_ref[rows[c], cs] = uw[:, DN_HEAD_DIM:].astype(BF16)
        ai_ref[hp, rows[c], :] = (kq[hp, c][C:] * decay[hp, c]).astype(BF16)
        dl_ref[hp, c:c + 1, :] = edl[hp][c * C:c * C + 1, :]


def gdn_prep(proj, ba, conv_w, alog_l, dtb_l):
    T = proj.shape[0]
    R = min(GDN_TILE, T)
    H = DN_HEADS
    nchunk = T // DN_CHUNK
    hb = R // HALO
    HP = GDN_PREP_HEADS
    W = HP * LANES
    col = lambda j: pl.BlockSpec((R, W), lambda i, h: (i, j * (H // HP) + h))
    halo = lambda j: pl.BlockSpec((HALO, W), lambda i, h: (jnp.maximum(i * hb - 1, 0), j * (H // HP) + h))
    KC = conv_w.shape[0]
    cw = conv_w.reshape(KC, 3, H, LANES).transpose(2, 0, 1, 3).reshape(H, KC, 3 * LANES)
    cw_spec = pl.BlockSpec((HP, KC, 3 * LANES), lambda i, h: (h, 0, 0))
    row_out = lambda dt: jax.ShapeDtypeStruct((T, DN_WIDTH), dt)
    out_col = pl.BlockSpec((R, W), lambda i, h: (i, h))
    return pl.pallas_call(
        _gdn_prep_kernel,
        grid=(T // R, H // HP),
        in_specs=[col(0), col(1), col(2), halo(0), halo(1), halo(2),
                  pl.BlockSpec((R, LANES), lambda i, h: (i, 0)),
                  cw_spec,
                  pl.BlockSpec((1, LANES), lambda i, h: (0, 0)),
                  pl.BlockSpec((1, LANES), lambda i, h: (0, 0))],
        out_specs=[out_col, out_col, out_col, out_col,
                   pl.BlockSpec((HP, R, DN_CHUNK), lambda i, h: (h, i, 0)),
                   pl.BlockSpec((HP, R // DN_CHUNK, LANES), lambda i, h: (h, i, 0))],
        out_shape=[row_out(BF16), row_out(BF16), row_out(BF16), row_out(F32),
                   jax.ShapeDtypeStruct((H, T, DN_CHUNK), BF16),
                   jax.ShapeDtypeStruct((H, nchunk, LANES), F32)],
        scratch_shapes=[pltpu.VMEM((3 * HP, R + HALO, LANES), F32),
                        pltpu.VMEM((R, LANES), F32), pltpu.VMEM((R, LANES), F32), pltpu.VMEM((R, LANES), F32),
                        pltpu.VMEM((LANES, R), F32)],
        compiler_params=_cparams(("parallel", "arbitrary")),
        name="gdn_prep",
    )(proj, proj, proj, proj, proj, proj, ba, cw, alog_l, dtb_l)


def _gdn_scan_kernel(qg_ref, kg_ref, w_ref, u_ref, ai_ref, dl_ref, o_ref, s_ref):
    @pl.when(pl.program_id(1) == 0)
    def _():
        s_ref[...] = jnp.zeros(s_ref.shape, F32)

    C = DN_CHUNK
    D = DN_HEAD_DIM
    heads = range(s_ref.shape[0])

    def chunk(c, carry):
        rows = pl.ds(pl.multiple_of(c * C, C), C)
        s = [s_ref[hh] for hh in heads]
        sb = [x.astype(BF16) for x in s]
        cols = [slice(hh * D, (hh + 1) * D) for hh in heads]
        r1 = [jnp.dot(jnp.concatenate([w_ref[rows, cols[hh]], qg_ref[rows, cols[hh]]], axis=0), sb[hh],
                      preferred_element_type=F32) for hh in heads]
        vb = [(u_ref[rows, cols[hh]] - r1[hh][:C]).astype(BF16) for hh in heads]
        for hh in heads:
            o_ref[rows, cols[hh]] = r1[hh][C:] + jnp.dot(ai_ref[hh, rows, :], vb[hh], preferred_element_type=F32)
        upd = [lax.dot_general(kg_ref[rows, cols[hh]], vb[hh], (((0,), (0,)), ((), ())),
                               preferred_element_type=F32) for hh in heads]
        for hh in heads:
            s_ref[hh] = s[hh] * dl_ref[hh, pl.ds(c, 1), :] + upd[hh]
        return carry

    lax.fori_loop(0, qg_ref.shape[0] // C, chunk, 0)


def gdn_scan(qg, kg, w, u, ai, dl):
    T = qg.shape[0]
    R = min(GDN_TILE, T)
    H = DN_HEADS
    HB = GDN_SCAN_HEADS
    col = pl.BlockSpec((R, HB * LANES), lambda h, i: (i, h))
    return pl.pallas_call(
        _gdn_scan_kernel,
        grid=(H // HB, T // R),
        in_specs=[col, col, col, col,
                  pl.BlockSpec((HB, R, DN_CHUNK), lambda h, i: (h, i, 0)),
                  pl.BlockSpec((HB, R // DN_CHUNK, LANES), lambda h, i: (h, i, 0))],
        out_specs=col,
        out_shape=jax.ShapeDtypeStruct((T, DN_WIDTH), F32),
        scratch_shapes=[pltpu.VMEM((HB, DN_HEAD_DIM, DN_HEAD_DIM), F32)],
        compiler_params=_cparams(("parallel", "arbitrary")),
        name="gdn_scan",
    )(qg, kg, w, u, ai, dl)


def _odd_out_math(o_ref, z_ref, g_ref, w_ref, h_ref):
    g = g_ref[...]
    parts = []
    for hd in range(DN_HEADS):
        sl = slice(hd * DN_HEAD_DIM, (hd + 1) * DN_HEAD_DIM)
        parts.append((_rms(o_ref[:, sl], g, RMS_EPS) * _silu(z_ref[:, sl].astype(F32))).astype(BF16))
    y = jnp.concatenate(parts, axis=-1)
    return h_ref[...] + jnp.dot(y, w_ref[...], preferred_element_type=F32)


def _route_math(h, g_ref, wr_ref, ri_ref, gate_ref, cnt_ref, carry_ref):
    i = pl.program_id(0)

    @pl.when(i == 0)
    def _():
        carry_ref[...] = jnp.zeros(carry_ref.shape, F32)

    xn = _rms(h, g_ref[...], RMS_EPS)
    logits = jnp.dot(xn.astype(BF16), wr_ref[...].astype(BF16), preferred_element_type=F32)
    tm = logits.shape[0]
    lane = lax.broadcasted_iota(I32, logits.shape, 1)
    logits = jnp.where(lane < N_EXPERTS, logits, NEG_BIG)
    lane_f = lane.astype(F32)
    m1 = jnp.max(logits, axis=-1, keepdims=True)
    i1 = jnp.min(jnp.where(logits == m1, lane_f, float(LANES)), axis=-1, keepdims=True)
    rest = jnp.where(lane_f == i1, NEG_BIG, logits)
    m2 = jnp.max(rest, axis=-1, keepdims=True)
    i2 = jnp.min(jnp.where(rest == m2, lane_f, float(LANES)), axis=-1, keepdims=True)
    e = jnp.exp(m2 - m1)
    g1 = 1.0 / (1.0 + e)
    g2 = e / (1.0 + e)
    oh1 = lane_f == i1
    oh2 = lane_f == i2
    i1 = i1.astype(I32)
    i2 = i2.astype(I32)
    oh = jnp.where(oh1 | oh2, 1.0, 0.0)
    ri_ = lax.broadcasted_iota(I32, (tm, tm), 0)
    ci_ = lax.broadcasted_iota(I32, (tm, tm), 1)
    below = jnp.where(ri_ > ci_, 1.0, 0.0).astype(BF16)
    ex = jnp.dot(below, oh.astype(BF16), preferred_element_type=F32) + carry_ref[0:1, :]
    r1 = jnp.sum(jnp.where(oh1, ex, 0.0), axis=-1, keepdims=True).astype(I32)
    r2 = jnp.sum(jnp.where(oh2, ex, 0.0), axis=-1, keepdims=True).astype(I32)
    packed = jnp.where(lane == 0, i1, jnp.where(lane == 1, i2, jnp.where(lane == 2, r1, r2)))
    ri_ref[...] = packed[:, 0:ri_ref.shape[1]]
    gate_ref[...] = jnp.where(lane == 0, g1, g2)
    carry_ref[...] = carry_ref[...] + jnp.sum(oh, axis=0, keepdims=True)
    cnt_ref[...] = carry_ref[...]


def _odd_out_route_kernel(o_ref, z_ref, g_ref, w_ref, h_ref, gr_ref, wr_ref,
                          out_ref, ri_ref, gate_ref, cnt_ref, carry_ref):
    h = _odd_out_math(o_ref, z_ref, g_ref, w_ref, h_ref)
    out_ref[...] = h
    _route_math(h, gr_ref, wr_ref, ri_ref, gate_ref, cnt_ref, carry_ref)


def odd_out_route(o, proj, onorm_g, w_out, h, g_route, wr):
    T, D = h.shape
    tm = min(ROW_TILE, T)
    zb = 3 * DN_WIDTH // DN_WIDTH
    return pl.pallas_call(
        _odd_out_route_kernel,
        grid=(T // tm,),
        in_specs=[
            pl.BlockSpec((tm, DN_WIDTH), lambda i: (i, 0)),
            pl.BlockSpec((tm, DN_WIDTH), lambda i: (i, zb)),
            pl.BlockSpec((1, DN_HEAD_DIM), lambda i: (0, 0)),
            pl.BlockSpec(w_out.shape, lambda i: (0, 0)),
            pl.BlockSpec((tm, D), lambda i: (i, 0)),
            pl.BlockSpec((1, D), lambda i: (0, 0)),
            pl.BlockSpec((D, LANES), lambda i: (0, 0)),
        ],
        out_specs=[
            pl.BlockSpec((tm, D), lambda i: (i, 0)),
            pl.BlockSpec((tm, 4), lambda i: (i, 0)),
            pl.BlockSpec((tm, LANES), lambda i: (i, 0)),
            pl.BlockSpec((SUBLANES, LANES), lambda i: (0, 0)),
        ],
        out_shape=[
            jax.ShapeDtypeStruct((T, D), F32),
            jax.ShapeDtypeStruct((T, 4), I32),
            jax.ShapeDtypeStruct((T, LANES), F32),
            jax.ShapeDtypeStruct((SUBLANES, LANES), F32),
        ],
        scratch_shapes=[pltpu.VMEM((SUBLANES, LANES), F32)],
        compiler_params=_cparams(("arbitrary",)),
        name="odd_out_route",
    )(o, proj, onorm_g.reshape(1, DN_HEAD_DIM), w_out, h, g_route.reshape(1, D), wr)


def _row_copy(src_ref, s, dst_ref, d, sem):
    return pltpu.make_async_copy(src_ref.at[pl.ds(s, 1), :], dst_ref.at[pl.ds(d, 1), :], sem)


def _dispatch_kernel(pz_ref, dest_ref, h_ref, g_ref, xs_ref, xn_ref, zb_ref, sem, zsem):
    tm = h_ref.shape[0]
    G = zb_ref.shape[0]

    @pl.when(pl.program_id(0) == 0)
    def _():
        zb_ref[...] = jnp.zeros(zb_ref.shape, F32)

        def zero_copy(e):
            return pltpu.make_async_copy(zb_ref, xs_ref.at[pl.ds(pl.multiple_of(pz_ref[e], G), G), :], zsem)

        for e in range(2 * N_EXPERTS):
            @pl.when(pz_ref[e] >= 0)
            def _():
                zero_copy(e).start()

        for e in range(2 * N_EXPERTS):
            @pl.when(pz_ref[e] >= 0)
            def _():
                zero_copy(e).wait()

    xn_ref[...] = _rms(h_ref[...], g_ref[...], RMS_EPS)

    def issue(r, carry):
        _row_copy(xn_ref, r, xs_ref, dest_ref[2 * r], sem).start(priority=0)
        _row_copy(xn_ref, r, xs_ref, dest_ref[2 * r + 1], sem).start(priority=1)
        return carry

    lax.fori_loop(0, tm, issue, 0, unroll=ROW_DMA_UNROLL)

    def drain(r, carry):
        _row_copy(xn_ref, 0, xs_ref, 0, sem).wait()
        _row_copy(xn_ref, 0, xs_ref, 0, sem).wait()
        return carry

    lax.fori_loop(0, tm, drain, 0, unroll=ROW_DMA_UNROLL)


def dispatch(h, g, dest_flat, zero_tiles, n_slots):
    T, D = h.shape
    tm = min(ROW_TILE, T)
    grid_spec = pltpu.PrefetchScalarGridSpec(
        num_scalar_prefetch=1,
        grid=(T // tm,),
        in_specs=[
            pl.BlockSpec((2 * tm,), lambda i, pz: (i,), memory_space=pltpu.SMEM),
            pl.BlockSpec((tm, D), lambda i, pz: (i, 0)),
            pl.BlockSpec((1, D), lambda i, pz: (0, 0)),
        ],
        out_specs=pl.BlockSpec(memory_space=pl.ANY),
        scratch_shapes=[pltpu.VMEM((tm, D), F32), pltpu.VMEM((MOE_TILE, D), F32),
                        pltpu.SemaphoreType.DMA, pltpu.SemaphoreType.DMA],
    )
    return pl.pallas_call(
        _dispatch_kernel,
        grid_spec=grid_spec,
        out_shape=jax.ShapeDtypeStruct((n_slots, D), F32),
        compiler_params=_cparams(("arbitrary",)),
        name="moe_dispatch",
    )(zero_tiles, dest_flat, h, g.reshape(1, D))


def _gmm_kernel(te_ref, na_ref, x_ref, w1_ref, w3_ref, w2_ref, y_ref, xb_ref, acc_ref):
    b = pl.program_id(0)
    f = pl.program_id(1)

    @pl.when(b < na_ref[0])
    def _():
        @pl.when(f == 0)
        def _():
            xb_ref[...] = x_ref[...].astype(BF16)
            acc_ref[...] = jnp.zeros(acc_ref.shape, F32)

        xb = xb_ref[...]
        a = jnp.dot(xb, w1_ref[...], preferred_element_type=F32)
        c = jnp.dot(xb, w3_ref[...], preferred_element_type=F32)
        hm = (_silu(a) * c).astype(BF16)
        acc_ref[...] += jnp.dot(hm, w2_ref[...], preferred_element_type=F32)

        @pl.when(f == pl.num_programs(1) - 1)
        def _():
            y_ref[...] = acc_ref[...]

    @pl.when((b >= na_ref[0]) & (f == pl.num_programs(1) - 1))
    def _():
        y_ref[...] = jnp.zeros(y_ref.shape, F32)


def gmm(xs, w1, w3, w2, tile_e, n_active):
    P, D = xs.shape
    G = MOE_TILE
    F = w1.shape[2]
    tf = MOE_FF_TILE
    nf = F // tf

    def row_idx(b, f, te, na):
        return (jnp.minimum(b, na[0] - 1), 0)

    def f_idx(b, f, na):
        return jnp.where(b < na[0], f, nf - 1)

    grid_spec = pltpu.PrefetchScalarGridSpec(
        num_scalar_prefetch=2,
        grid=(P // G, nf),
        in_specs=[
            pl.BlockSpec((G, D), row_idx),
            pl.BlockSpec((None, D, tf), lambda b, f, te, na: (te[b], 0, f_idx(b, f, na))),
            pl.BlockSpec((None, D, tf), lambda b, f, te, na: (te[b], 0, f_idx(b, f, na))),
            pl.BlockSpec((None, tf, D), lambda b, f, te, na: (te[b], f_idx(b, f, na), 0)),
        ],
        out_specs=pl.BlockSpec((G, D), lambda b, f, te, na: (b, 0)),
        scratch_shapes=[pltpu.VMEM((G, D), BF16), pltpu.VMEM((G, D), F32)],
    )
    return pl.pallas_call(
        _gmm_kernel,
        grid_spec=grid_spec,
        out_shape=jax.ShapeDtypeStruct((P, D), F32),
        compiler_params=_cparams(("arbitrary", "arbitrary")),
        name="moe_gmm",
    )(tile_e, n_active, xs, w1, w3, w2)


def _combine_ple_kernel(dest_ref, dnext_ref, h_ref, gate_ref, y_ref, g_ref, wg_ref, p_ref, wp_ref, gf_ref,
                        o_ref, yab_ref, sem, *, final_norm):
    i = pl.program_id(0)
    n = pl.num_programs(0)
    tm = h_ref.shape[0]
    slot = lax.rem(i, 2)

    def gather(d_ref, sl):
        def issue(r, carry):
            _row_copy(y_ref, d_ref[2 * r], yab_ref.at[sl, 0], r, sem.at[sl]).start(priority=0)
            _row_copy(y_ref, d_ref[2 * r + 1], yab_ref.at[sl, 1], r, sem.at[sl]).start(priority=1)
            return carry

        lax.fori_loop(0, tm, issue, 0, unroll=ROW_DMA_UNROLL)

    @pl.when(i == 0)
    def _():
        gather(dest_ref, 0)

    @pl.when(i + 1 < n)
    def _():
        gather(dnext_ref, 1 - slot)

    def drain(r, carry):
        _row_copy(y_ref, 0, yab_ref.at[slot, 0], 0, sem.at[slot]).wait()
        _row_copy(y_ref, 0, yab_ref.at[slot, 1], 0, sem.at[slot]).wait()
        return carry

    lax.fori_loop(0, tm, drain, 0, unroll=ROW_DMA_UNROLL)
    gt = gate_ref[...]
    h = h_ref[...] + gt[:, 0:1] * yab_ref[slot, 0] + gt[:, 1:2] * yab_ref[slot, 1]
    o_ref[...] = _ple_math(h, g_ref, wg_ref, p_ref, wp_ref, gf_ref, final_norm)


def combine_ple(h, gates, y, dest_flat, g, wg, p, wp, g_final, final_norm, layer):
    T, D = h.shape
    tm = min(ROW_TILE, T)
    n = T // tm
    return pl.pallas_call(
        functools.partial(_combine_ple_kernel, final_norm=final_norm),
        grid=(n,),
        in_specs=[
            pl.BlockSpec((2 * tm,), lambda i: (i,), memory_space=pltpu.SMEM),
            pl.BlockSpec((2 * tm,), lambda i: (jnp.minimum(i + 1, n - 1),), memory_space=pltpu.SMEM),
            pl.BlockSpec((tm, D), lambda i: (i, 0)),
            pl.BlockSpec((tm, LANES), lambda i: (i, 0)),
            pl.BlockSpec(memory_space=pl.ANY),
            pl.BlockSpec((1, D), lambda i: (0, 0)),
            pl.BlockSpec((None, D, D), lambda i: (layer, 0, 0)),
            pl.BlockSpec((None, tm, PLE_DIM), lambda i: (layer, i, 0)),
            pl.BlockSpec((None, PLE_DIM, D), lambda i: (layer, 0, 0)),
            pl.BlockSpec((1, D), lambda i: (0, 0)),
        ],
        out_specs=pl.BlockSpec((tm, D), lambda i: (i, 0)),
        out_shape=jax.ShapeDtypeStruct((T, D), F32),
        scratch_shapes=[pltpu.VMEM((2, 2, tm, D), F32), pltpu.SemaphoreType.DMA((2,))],
        compiler_params=_cparams(("arbitrary",)),
        name="moe_combine_ple",
    )(dest_flat, dest_flat, h, gates, y, g.reshape(1, D), wg, p, wp, g_final.reshape(1, D))


def _lambda_init(layer_idx):
    return 0.8 - 0.6 * math.exp(-0.3 * layer_idx)


def _even_layer(h, pos_col, pos_row, invf, ln_mix, w_in, w_out, lam_params, subln_g, conv_w, layer_idx, cast):
    qk, vt, gates_conv = even_proj(h, ln_mix, w_in.astype(BF16), pos_col, invf)
    o, narrowed = diff_attention(qk, vt, pos_col, pos_row, lam_params, subln_g, _lambda_init(layer_idx), cast)
    return even_out(o, gates_conv, conv_w, w_out.astype(BF16), h), narrowed


def _odd_mixer_route(h, ln_mix, w_in, conv_w, a_log, dt_bias, onorm_g, w_out, ln_ffn, w_router):
    main_w = 4 * DN_WIDTH
    w_main = w_in[:, :main_w].astype(BF16)
    w_ba = jnp.pad(w_in[:, main_w:], ((0, 0), (0, LANES - 2 * DN_HEADS))).astype(BF16)
    proj, ba = norm_matmul(h, ln_mix, w_main, w_side=w_ba)
    pad8 = lambda v: jnp.pad(v.astype(F32), (DN_HEADS, LANES - 2 * DN_HEADS)).reshape(1, LANES)
    qg, kg, w, u, ai, dl = gdn_prep(proj, ba, conv_w, pad8(a_log), pad8(dt_bias))
    o = gdn_scan(qg, kg, w, u, ai, dl)
    wr = jnp.pad(w_router, ((0, 0), (0, LANES - N_EXPERTS)))
    return odd_out_route(o, proj, onorm_g, w_out.astype(BF16), h, ln_ffn, wr)


def _moe(h, ln_ffn, ri, gates, cnt, w1, w3, w2):
    T = h.shape[0]
    G = MOE_TILE
    counts = cnt[0, :N_EXPERTS].astype(I32)
    padded = ((counts + G - 1) // G) * G
    pends = jnp.cumsum(padded)
    pstarts = pends - padded
    dest = (jnp.take(pstarts, ri[:, 0:2]) + ri[:, 2:4]).reshape(-1)
    n_tiles = (2 * T) // G + N_EXPERTS
    tile_start = jnp.arange(n_tiles, dtype=I32) * G
    tile_e = jnp.minimum(jnp.sum(pends[None, :] <= tile_start[:, None], axis=1), N_EXPERTS - 1).astype(I32)
    n_active = (pends[-1:] // G).astype(I32)
    tail = pends[-1] + jnp.arange(N_EXPERTS, dtype=I32) * G
    zero_tiles = jnp.concatenate([jnp.where(padded > 0, pends - G, -1),
                                  jnp.where(tail < n_tiles * G, tail, -1)]).astype(I32)
    xs = dispatch(h, ln_ffn, dest, zero_tiles, n_tiles * G)
    y = gmm(xs, w1, w3, w2, tile_e, n_active)
    return gates, y, dest


def kernel(x, p, positions, ln_mix, ln_ffn, ln_ple, ln_final, w_in_even, w_out_even, lam_q1, lam_k1, lam_q2, lam_k2, subln_gain, conv_w_short, w_in_odd, conv_w_qkv, a_log, dt_bias, onorm_gain, w_out_odd, w1_dense, w3_dense, w2_dense, w_router, w1_moe, w3_moe, w2_moe, w_ple_gate, w_ple_proj):
    B, S, D = x.shape
    T = B * S
    depth = p.shape[0]
    h = x.reshape(T, D)
    pos_col = positions.reshape(T, 1).astype(I32)
    pos_row = positions.reshape(1, T).astype(I32)
    inv_freq = ROPE_THETA ** (-jnp.arange(0, ROT_DIM, 2, dtype=F32) / ROT_DIM)
    invf = jnp.tile(inv_freq, LANES // (ROT_DIM // 2)).reshape(1, LANES)
    w1d, w3d, w2d = (w.astype(BF16) for w in (w1_dense, w3_dense, w2_dense))
    wpg, wpp = w_ple_gate.astype(BF16), w_ple_proj.astype(BF16)
    p_rows = p.reshape(depth, T, PLE_DIM)
    n_odd, n_exp, _, ff = w1_moe.shape
    moe_f32 = [w.reshape(n_odd, -1, w.shape[-1]) for w in (w1_moe, w3_moe, w2_moe)]
    moe_bf16 = None
    for i in range(depth):
        j = i // 2
        if i % 2 == 0:
            lam_params = jnp.stack([lam_q1[j], lam_k1[j], lam_q2[j], lam_k2[j]]).astype(F32)
            cast = tuple((w, j) for w in moe_f32) if i + 1 < depth else ()
            h, narrowed = _even_layer(h, pos_col, pos_row, invf, ln_mix[i], w_in_even[j], w_out_even[j], lam_params,
                                      subln_gain[j], conv_w_short[j], i, cast)
            if narrowed:
                moe_bf16 = (narrowed[0].reshape(n_exp, D, ff), narrowed[1].reshape(n_exp, D, ff),
                            narrowed[2].reshape(n_exp, ff, D))
            h = ffn_ple(h, ln_ffn[i], w1d, w3d, w2d, j, ln_ple[i], wpg, p_rows, wpp, ln_final,
                        final_norm=(i == depth - 1), ple_layer=i)
        else:
            h, ri, gates, cnt = _odd_mixer_route(h, ln_mix[i], w_in_odd[j], conv_w_qkv[j], a_log[j], dt_bias[j],
                                                 onorm_gain[j], w_out_odd[j], ln_ffn[i], w_router[j])
            gates, y, dest = _moe(h, ln_ffn[i], ri, gates, cnt, *moe_bf16)
            h = combine_ple(h, gates, y, dest, ln_ple[i], wpg, p_rows, wpp, ln_final,
                            final_norm=(i == depth - 1), layer=i)
    return h.reshape(B, S, D)
```

```python
import functools
import math

import jax
import jax.numpy as jnp
import numpy as np
from jax import lax
from jax.experimental import pallas as pl
from jax.experimental.pallas import tpu as pltpu

F32 = jnp.float32
BF16 = jnp.bfloat16
I32 = jnp.int32

D_MODEL = 1024
DEPTH = 4
RMS_EPS = 1e-6
DA_HEADS = 4
DA_HEAD_DIM = 64
DA_V_DIM = 2 * DA_HEAD_DIM
ROPE_THETA = 500000.0
ROT_DIM = DA_HEAD_DIM // 4
SUBLN_EPS = 1e-5
SC_WIDTH = 512
DN_HEADS = 8
DN_HEAD_DIM = 128
DN_WIDTH = DN_HEADS * DN_HEAD_DIM
DN_CHUNK = 64
D_FF = 3584
N_EXPERTS = 8
PLE_DIM = 256

LANES = 128
SUBLANES = 8
HALO = 16
VMEM_LIMIT = 52 * 1024 * 1024

NEG_BIG = -1e30

ROW_TILE = 512
PROJ_COL_TILE = 1024
FFN_ROW_TILE = 512
FF_TILE = 1792
MOE_FF_TILE = 1792
ATT_TQ = 1024
ATT_TK = 1024
ATT_EXP_ROWS = 128
ATT_TQ_SPLIT = 256
GDN_TILE = 1024
GDN_PREP_HEADS = 2
GDN_SCAN_HEADS = 8
MOE_TILE = 512
ROW_DMA_UNROLL = 8


def _cparams(sem):
    return pltpu.CompilerParams(dimension_semantics=sem, vmem_limit_bytes=VMEM_LIMIT)


def _rms(x, g, eps):
    ms = jnp.mean(x * x, axis=-1, keepdims=True)
    return x * lax.rsqrt(ms + eps) * g


def _silu(x):
    return x * (1.0 / (1.0 + jnp.exp(-x)))


def _sigmoid(x):
    return 1.0 / (1.0 + jnp.exp(-x))


def _norm_matmul_kernel(x_ref, g_ref, w_ref, *rest):
    xn = _rms(x_ref[...], g_ref[...], RMS_EPS).astype(BF16)
    if len(rest) == 1:
        (o_ref,) = rest
    else:
        ws_ref, o_ref, os_ref = rest
        os_ref[...] = jnp.dot(xn, ws_ref[...], preferred_element_type=F32)
    tn = PROJ_COL_TILE
    for c in range(o_ref.shape[1] // tn):
        o_ref[:, c * tn:(c + 1) * tn] = jnp.dot(
            xn, w_ref[:, c * tn:(c + 1) * tn], preferred_element_type=F32).astype(o_ref.dtype)


def norm_matmul(x, g, w, w_side=None, out_dtype=BF16):
    T, D = x.shape
    N = w.shape[1]
    tm = min(ROW_TILE, T)
    in_specs = [
        pl.BlockSpec((tm, D), lambda i: (i, 0)),
        pl.BlockSpec((1, D), lambda i: (0, 0)),
        pl.BlockSpec((D, N), lambda i: (0, 0)),
    ]
    out_specs = [pl.BlockSpec((tm, N), lambda i: (i, 0))]
    out_shape = [jax.ShapeDtypeStruct((T, N), out_dtype)]
    args = [x, g.reshape(1, D), w]
    if w_side is not None:
        ns = w_side.shape[1]
        in_specs.append(pl.BlockSpec((D, ns), lambda i: (0, 0)))
        out_specs.append(pl.BlockSpec((tm, ns), lambda i: (i, 0)))
        out_shape.append(jax.ShapeDtypeStruct((T, ns), F32))
        args.append(w_side)
    return pl.pallas_call(
        _norm_matmul_kernel,
        grid=(T // tm,),
        in_specs=in_specs,
        out_specs=out_specs,
        out_shape=out_shape,
        compiler_params=_cparams(("parallel",)),
        name="norm_matmul",
    )(*args)


def _even_proj_kernel(x_ref, g_ref, w_ref, pos_ref, invf_ref, qk_ref, vt_ref, rest_ref):
    xn = _rms(x_ref[...], g_ref[...], RMS_EPS).astype(BF16)
    WQK = qk_ref.shape[1]
    WV = vt_ref.shape[0]

    def proj(c0, c1):
        return jnp.dot(xn, w_ref[:, c0:c1], preferred_element_type=F32)

    pos = pos_ref[...].astype(F32)
    ang = pos * invf_ref[...]
    cos_t = jnp.cos(ang)
    sin_t = jnp.sin(ang)
    d = lax.broadcasted_iota(I32, ang.shape, 1) & (DA_HEAD_DIM - 1)
    half = ROT_DIM // 2
    c_mul = jnp.where(d < ROT_DIM, cos_t, 1.0)
    s_mul = jnp.where(d < half, -sin_t, jnp.where(d < ROT_DIM, sin_t, 0.0))
    qk = proj(0, WQK)
    n_q = WQK // (2 * LANES)
    for c in range(WQK // LANES):
        x = qk[:, c * LANES:(c + 1) * LANES]
        swapped = jnp.where(d < half, pltpu.roll(x, LANES - half, 1), pltpu.roll(x, half, 1))
        r = x * c_mul + swapped * s_mul
        if c < n_q:
            r = r * (DA_HEAD_DIM ** -0.5 * math.log2(math.e))
        qk_ref[:, c * LANES:(c + 1) * LANES] = r.astype(BF16)
    v = proj(WQK, WQK + WV)
    for hd in range(DA_HEADS):
        vt_ref[hd * DA_V_DIM:(hd + 1) * DA_V_DIM, :] = v[:, hd * DA_V_DIM:(hd + 1) * DA_V_DIM].T.astype(BF16)
    WR = rest_ref.shape[1]
    for c0 in range(0, WR, PROJ_COL_TILE):
        c1 = min(c0 + PROJ_COL_TILE, WR)
        rest_ref[:, c0:c1] = proj(WQK + WV + c0, WQK + WV + c1).astype(BF16)


def even_proj(x, g, w, positions_col, invf):
    T, D = x.shape
    WQK = 2 * 2 * DA_HEADS * DA_HEAD_DIM
    WV = DA_HEADS * DA_V_DIM
    WR = w.shape[1] - WQK - WV
    tm = min(ROW_TILE, T)
    return pl.pallas_call(
        _even_proj_kernel,
        grid=(T // tm,),
        in_specs=[
            pl.BlockSpec((tm, D), lambda i: (i, 0)),
            pl.BlockSpec((1, D), lambda i: (0, 0)),
            pl.BlockSpec(w.shape, lambda i: (0, 0)),
            pl.BlockSpec((tm, 1), lambda i: (i, 0)),
            pl.BlockSpec((1, LANES), lambda i: (0, 0)),
        ],
        out_specs=[pl.BlockSpec((tm, WQK), lambda i: (i, 0)),
                   pl.BlockSpec((WV, tm), lambda i: (0, i)),
                   pl.BlockSpec((tm, WR), lambda i: (i, 0))],
        out_shape=[jax.ShapeDtypeStruct((T, WQK), BF16), jax.ShapeDtypeStruct((WV, T), BF16),
                   jax.ShapeDtypeStruct((T, WR), BF16)],
        compiler_params=_cparams(("parallel",)),
        name="even_proj",
    )(x, g.reshape(1, D), w, positions_col, invf)


def _attn_kernel(qt_ref, kt_ref, q_ref, k_ref, v_ref, pq_ref, pk_ref, lam_ref, g_ref, *rest, tq, tk, lam0, n_cast):
    cast_in = rest[:n_cast]
    o_ref = rest[n_cast]
    cast_out = rest[n_cast + 1:2 * n_cast + 1]
    m_ref, l_ref, acc_ref = rest[2 * n_cast + 1:]
    for wi_ref, wo_ref in zip(cast_in, cast_out):
        wo_ref[...] = wi_ref[...].astype(BF16)

    p = pl.program_id(1)
    qi = qt_ref[p]
    ki = kt_ref[p]

    @pl.when(ki == 0)
    def _():
        m_ref[...] = jnp.full(m_ref.shape, NEG_BIG, F32)
        l_ref[...] = jnp.zeros(l_ref.shape, F32)
        acc_ref[...] = jnp.zeros(acc_ref.shape, F32)

    tqs = min(tq, ATT_TQ_SPLIT)
    items = [(s, j) for s in range(2) for j in range(tq // tqs)]

    def step(masked):
        q = q_ref[...]
        k = k_ref[...]
        v = v_ref[...]
        def nkeys(j):
            return (j + 1) * tqs if (masked and tq == tk) else tk

        st = {}
        for s, j in items:
            cols = slice(j * tqs, (j + 1) * tqs)
            qs = q[j * tqs:(j + 1) * tqs, s * DA_HEAD_DIM:(s + 1) * DA_HEAD_DIM]
            ks = k[:nkeys(j), s * DA_HEAD_DIM:(s + 1) * DA_HEAD_DIM]
            sc = lax.dot_general(ks, qs, (((1,), (1,)), ((), ())), preferred_element_type=F32)
            if masked:
                sc = jnp.where(pk_ref[0:nkeys(j), :] <= pq_ref[:, cols], sc, NEG_BIG)
            st[s, j] = sc
        for s, j in items:
            cols = slice(j * tqs, (j + 1) * tqs)
            sc = st[s, j]
            nk = nkeys(j)
            m_prev = m_ref[s, :, cols]
            m_new = jnp.maximum(m_prev, jnp.max(sc, axis=0, keepdims=True))
            alpha = jnp.exp2(m_prev - m_new)
            psum = jnp.zeros((1, tqs), F32)
            pchunks = []
            for r0 in range(0, nk, ATT_EXP_ROWS):
                pc = jnp.exp2(sc[r0:r0 + ATT_EXP_ROWS] - m_new)
                psum = psum + jnp.sum(pc, axis=0, keepdims=True)
                pchunks.append(pc.astype(BF16))
            pb = jnp.concatenate(pchunks, axis=0)
            l_ref[s, :, cols] = alpha * l_ref[s, :, cols] + psum
            pv = jnp.dot(v[:, :nk], pb, preferred_element_type=F32)
            acc_ref[s, :, cols] = alpha * acc_ref[s, :, cols] + pv
            m_ref[s, :, cols] = m_new

    crosses = (ki + 1) * tk - 1 > qi * tq

    @pl.when(crosses)
    def _():
        step(True)

    @pl.when(jnp.logical_not(crosses))
    def _():
        step(False)

    @pl.when((ki + 1) * tk >= (qi + 1) * tq)
    def _():
        lm = lam_ref[...]
        s1 = jnp.sum(lm[0:1] * lm[1:2], axis=-1, keepdims=True)
        s2 = jnp.sum(lm[2:3] * lm[3:4], axis=-1, keepdims=True)
        lam = jnp.exp(s1) - jnp.exp(s2) + lam0
        ot = acc_ref[0] * (1.0 / l_ref[0]) - lam * (acc_ref[1] * (1.0 / l_ref[1]))
        ms = jnp.mean(ot * ot, axis=0, keepdims=True)
        ot = ot * lax.rsqrt(ms + SUBLN_EPS) * g_ref[...] * (1.0 - lam0)
        o_ref[...] = ot.T.astype(o_ref.dtype)


def _cast_blocks(rows, steps):
    for nb in range(min(steps, rows), 0, -1):
        if rows % nb == 0 and (rows // nb) % HALO == 0:
            return nb
    raise ValueError((rows, steps))


def diff_attention(qk, vt, pos_col, pos_row, lam_params, subln_g, lam0, cast=()):
    T = qk.shape[0]
    tq = min(ATT_TQ, T)
    tk = min(ATT_TK, T)
    nq = T // tq
    pairs = [(qi, ki) for qi in range(nq) for ki in range(-(-((qi + 1) * tq) // tk))]
    qt = jnp.asarray(np.array([a for a, _ in pairs], np.int32))
    kt = jnp.asarray(np.array([b for _, b in pairs], np.int32))
    H = DA_HEADS
    npairs = len(pairs)
    kern = functools.partial(_attn_kernel, tq=tq, tk=tk, lam0=lam0, n_cast=len(cast))
    cast_in_specs, cast_out_specs, cast_out_shapes = [], [], []
    for w, layer in cast:
        _, rows, cols = w.shape
        nb = _cast_blocks(rows, H * npairs)
        br = rows // nb

        def blk(h, p, nb=nb):
            return jnp.minimum(h * npairs + p, nb - 1)

        cast_in_specs.append(pl.BlockSpec((None, br, cols), lambda h, p, qt, kt, layer=layer, blk=blk: (layer, blk(h, p), 0)))
        cast_out_specs.append(pl.BlockSpec((br, cols), lambda h, p, qt, kt, blk=blk: (blk(h, p), 0)))
        cast_out_shapes.append(jax.ShapeDtypeStruct((rows, cols), BF16))
    grid_spec = pltpu.PrefetchScalarGridSpec(
        num_scalar_prefetch=2,
        grid=(H, npairs),
        in_specs=[
            pl.BlockSpec((tq, LANES), lambda h, p, qt, kt: (qt[p], h)),
            pl.BlockSpec((tk, LANES), lambda h, p, qt, kt: (kt[p], H + h)),
            pl.BlockSpec((DA_V_DIM, tk), lambda h, p, qt, kt: (h, kt[p])),
            pl.BlockSpec((1, tq), lambda h, p, qt, kt: (0, qt[p])),
            pl.BlockSpec((tk, 1), lambda h, p, qt, kt: (kt[p], 0)),
            pl.BlockSpec((4, DA_HEAD_DIM), lambda h, p, qt, kt: (0, 0)),
            pl.BlockSpec((DA_V_DIM, 1), lambda h, p, qt, kt: (0, 0)),
        ] + cast_in_specs,
        out_specs=[pl.BlockSpec((tq, DA_V_DIM), lambda h, p, qt, kt: (qt[p], h))] + cast_out_specs,
        scratch_shapes=[
            pltpu.VMEM((2, 1, tq), F32),
            pltpu.VMEM((2, 1, tq), F32),
            pltpu.VMEM((2, DA_V_DIM, tq), F32),
        ],
    )
    outs = pl.pallas_call(
        kern,
        grid_spec=grid_spec,
        out_shape=[jax.ShapeDtypeStruct((T, H * DA_V_DIM), BF16)] + cast_out_shapes,
        compiler_params=_cparams(("arbitrary", "arbitrary")),
        name="diff_attention",
    )(qt, kt, qk, qk, vt, pos_row, pos_col, lam_params, subln_g.reshape(DA_V_DIM, 1), *[w for w, _ in cast])
    return outs[0], outs[1:]


def _even_out_kernel(o_ref, b_ref, c_ref, x_ref, ch_ref, xh_ref, cw_ref, w_ref, h_ref, out_ref, u_ref):
    tm = o_ref.shape[0]
    u_prev = ch_ref[...].astype(F32) * xh_ref[...].astype(F32)
    u_prev = jnp.where(pl.program_id(0) == 0, 0.0, u_prev)
    u_ref[0:HALO, :] = u_prev
    u_ref[HALO:, :] = c_ref[...].astype(F32) * x_ref[...].astype(F32)
    cw = cw_ref[...]
    conv = (u_ref[HALO - 2:HALO - 2 + tm, :] * cw[0:1]
            + u_ref[HALO - 1:HALO - 1 + tm, :] * cw[1:2]
            + u_ref[HALO:, :] * cw[2:3])
    sc = (b_ref[...].astype(F32) * conv).astype(BF16)
    na = o_ref.shape[1]
    acc = jnp.dot(o_ref[...], w_ref[0:na, :], preferred_element_type=F32)
    acc = acc + jnp.dot(sc, w_ref[na:, :], preferred_element_type=F32)
    out_ref[...] = h_ref[...] + acc


def even_out(attn_o, proj, conv_w, w_out, h):
    T = h.shape[0]
    tm = min(ROW_TILE, T)
    W = SC_WIDTH
    cb = (proj.shape[1] - 3 * W) // W
    hb = tm // HALO
    halo = lambda col: pl.BlockSpec((HALO, W), lambda i: (jnp.maximum(i * hb - 1, 0), col))
    return pl.pallas_call(
        _even_out_kernel,
        grid=(T // tm,),
        in_specs=[
            pl.BlockSpec((tm, attn_o.shape[1]), lambda i: (i, 0)),
            pl.BlockSpec((tm, W), lambda i: (i, cb)),
            pl.BlockSpec((tm, W), lambda i: (i, cb + 1)),
            pl.BlockSpec((tm, W), lambda i: (i, cb + 2)),
            halo(cb + 1),
            halo(cb + 2),
            pl.BlockSpec(conv_w.shape, lambda i: (0, 0)),
            pl.BlockSpec(w_out.shape, lambda i: (0, 0)),
            pl.BlockSpec((tm, D_MODEL), lambda i: (i, 0)),
        ],
        out_specs=pl.BlockSpec((tm, D_MODEL), lambda i: (i, 0)),
        out_shape=jax.ShapeDtypeStruct((T, D_MODEL), F32),
        scratch_shapes=[pltpu.VMEM((tm + HALO, W), F32)],
        compiler_params=_cparams(("parallel",)),
        name="even_out",
    )(attn_o, proj, proj, proj, proj, proj, conv_w, w_out, h)


def _ffn_kernel(h_ref, g_ref, w1_ref, w3_ref, w2_ref, gp_ref, wg_ref, p_ref, wp_ref, gf_ref, o_ref, xn_ref, acc_ref,
                *, final_norm):
    f = pl.program_id(1)

    @pl.when(f == 0)
    def _():
        xn_ref[...] = _rms(h_ref[...], g_ref[...], RMS_EPS).astype(BF16)
        acc_ref[...] = jnp.zeros(acc_ref.shape, F32)

    xn = xn_ref[...]
    a = jnp.dot(xn, w1_ref[...], preferred_element_type=F32)
    b = jnp.dot(xn, w3_ref[...], preferred_element_type=F32)
    hm = (_silu(a) * b).astype(BF16)
    acc_ref[...] += jnp.dot(hm, w2_ref[...], preferred_element_type=F32)

    @pl.when(f == pl.num_programs(1) - 1)
    def _():
        o_ref[...] = _ple_math(h_ref[...] + acc_ref[...], gp_ref, wg_ref, p_ref, wp_ref, gf_ref, final_norm)


def ffn_ple(h, g, w1, w3, w2, g_ple, wg, p, wp, g_final, final_norm, ple_layer):
    T, D = h.shape
    F = w1.shape[1]
    tm = min(FFN_ROW_TILE, T)
    tf = FF_TILE
    return pl.pallas_call(
        functools.partial(_ffn_kernel, final_norm=final_norm),
        grid=(T // tm, F // tf),
        in_specs=[
            pl.BlockSpec((tm, D), lambda i, f: (i, 0)),
            pl.BlockSpec((1, D), lambda i, f: (0, 0)),
            pl.BlockSpec((D, tf), lambda i, f: (0, f)),
            pl.BlockSpec((D, tf), lambda i, f: (0, f)),
            pl.BlockSpec((tf, D), lambda i, f: (f, 0)),
            pl.BlockSpec((1, D), lambda i, f: (0, 0)),
            pl.BlockSpec((None, D, D), lambda i, f: (ple_layer, 0, 0)),
            pl.BlockSpec((None, tm, PLE_DIM), lambda i, f: (ple_layer, i, 0)),
            pl.BlockSpec((None, PLE_DIM, D), lambda i, f: (ple_layer, 0, 0)),
            pl.BlockSpec((1, D), lambda i, f: (0, 0)),
        ],
        out_specs=pl.BlockSpec((tm, D), lambda i, f: (i, 0)),
        out_shape=jax.ShapeDtypeStruct((T, D), F32),
        scratch_shapes=[pltpu.VMEM((tm, D), BF16), pltpu.VMEM((tm, D), F32)],
        compiler_params=_cparams(("parallel", "arbitrary")),
        name="ffn_ple",
    )(h, g.reshape(1, D), w1, w3, w2, g_ple.reshape(1, D), wg, p, wp, g_final.reshape(1, D))


def _ple_math(h, g_ref, wg_ref, p_ref, wp_ref, gf_ref, final_norm):
    xn = _rms(h, g_ref[...], RMS_EPS).astype(BF16)
    gate = _sigmoid(jnp.dot(xn, wg_ref[...], preferred_element_type=F32))
    emb = jnp.dot(p_ref[...].astype(BF16), wp_ref[...], preferred_element_type=F32)
    out = h + gate * emb
    if final_norm:
        out = _rms(out, gf_ref[...], RMS_EPS)
    return out


def _dot_hi(a, b):
    return jnp.dot(a, b, preferred_element_type=F32, precision=lax.Precision.HIGHEST)


def _gdn_prep_kernel(q_ref, k_ref, v_ref, qh_ref, kh_ref, vh_ref, ba_ref, cw_ref, alog_ref, dtb_ref,
                     qg_ref, kg_ref, w_ref, u_ref, ai_ref, dl_ref,
                     xs_ref, gc_ref, gl_ref, sg_ref, gct_ref):
    i = pl.program_id(0)
    h = pl.program_id(1)
    R = q_ref.shape[0]
    C = DN_CHUNK
    KC = cw_ref.shape[1]

    def conv_silu(x_ref, halo_ref, j, hp):
        cs = slice(hp * LANES, (hp + 1) * LANES)
        buf = xs_ref.at[3 * hp + j]
        buf[0:HALO, :] = jnp.where(i == 0, 0.0, halo_ref[:, cs].astype(F32))
        buf[HALO:, :] = x_ref[:, cs].astype(F32)
        acc = buf[HALO:, :] * cw_ref[hp, KC - 1:KC, j * LANES:(j + 1) * LANES]
        for t in range(1, KC):
            acc = acc + (buf[HALO - t:HALO - t + R, :]
                         * cw_ref[hp, KC - 1 - t:KC - t, j * LANES:(j + 1) * LANES])
        return _silu(acc)

    def l2n(x):
        return x * lax.rsqrt(jnp.sum(x * x, axis=-1, keepdims=True) + 1e-6)

    chunks = range(R // C)
    rows = [slice(c * C, (c + 1) * C) for c in chunks]
    r64 = lax.broadcasted_iota(I32, (C, C), 0)
    c64 = lax.broadcasted_iota(I32, (C, C), 1)
    incl = r64 >= c64
    strict = r64 > c64

    @pl.when(h == 0)
    def _():
        ba = ba_ref[...]
        sp_in = ba + dtb_ref[...]
        softplus = jnp.maximum(sp_in, 0.0) + jnp.log(1.0 + jnp.exp(-jnp.abs(sp_in)))
        g_all = -jnp.exp(alog_ref[...]) * softplus
        g_wide = jnp.concatenate([g_all[r] for r in rows], axis=1)
        gc_wide = _dot_hi(jnp.where(incl, 1.0, 0.0), g_wide)
        for c in chunks:
            gc_c = gc_wide[:, c * LANES:(c + 1) * LANES]
            gc_ref[rows[c], :] = gc_c
            gl_ref[rows[c], :] = jnp.broadcast_to(gc_c[C - 1:C, :], (C, LANES))
        sg_ref[...] = _sigmoid(ba)
        gct_ref[...] = gc_ref[...].T

    lane = lax.broadcasted_iota(I32, (R, LANES), 1)

    def pick(ref, l):
        return jnp.sum(jnp.where(lane == l, ref[...], 0.0), axis=-1, keepdims=True)

    def bdot(a, b):
        return jnp.dot(a.astype(BF16), b.astype(BF16), preferred_element_type=F32)

    heads = range(qg_ref.shape[1] // LANES)
    items = [(hp, c) for hp in heads for c in chunks]
    decay, kq, rhs, edl = {}, {}, {}, {}
    for hp in heads:
        hh = h * len(heads) + hp
        cs = slice(hp * LANES, (hp + 1) * LANES)
        q = l2n(conv_silu(q_ref, qh_ref, 0, hp)) * (DN_HEAD_DIM ** -0.5)
        k = l2n(conv_silu(k_ref, kh_ref, 1, hp))
        v = conv_silu(v_ref, vh_ref, 2, hp)
        beta = pick(sg_ref, hh)
        gcol = pick(gc_ref, DN_HEADS + hh)
        glast = pick(gl_ref, DN_HEADS + hh)
        grow = gct_ref[pl.ds(DN_HEADS + hh, 1), :]
        eg = jnp.exp(gcol)
        kb = k * beta
        qg_ref[:, cs] = (q * eg).astype(BF16)
        kg_ref[:, cs] = (k * jnp.exp(glast - gcol)).astype(BF16)
        edl[hp] = jnp.broadcast_to(jnp.exp(glast), (R, LANES))
        rhs[hp] = jnp.concatenate([v * beta, kb * eg], axis=-1)
        kbf = k.astype(BF16)
        lhs = jnp.concatenate([kb.astype(BF16).reshape(R // C, C, LANES),
                               q.astype(BF16).reshape(R // C, C, LANES)], axis=1)
        for c in chunks:
            r = rows[c]
            decay[hp, c] = jnp.where(incl, jnp.exp(jnp.where(incl, gcol[r] - grow[:, r], 0.0)), 0.0)
            kq[hp, c] = lax.dot_general(lhs[c], kbf[r], (((1,), (1,)), ((), ())), preferred_element_type=F32)
    pw = {it: -jnp.where(strict, kq[it][:C] * decay[it], 0.0) for it in items}
    n = dict(pw)
    for _ in range(5):
        pw = {it: bdot(pw[it], pw[it]) for it in items}
        n = {it: n[it] + pw[it] + bdot(n[it], pw[it]) for it in items}
    for hp, c in items:
        cs = slice(hp * LANES, (hp + 1) * LANES)
        rc = rhs[hp][rows[c]]
        uw = rc + bdot(n[hp, c], rc)
        u_ref[rows[c], cs] = uw[:, :DN_HEAD_DIM]
        w_ref[rows[c], cs] = uw[:, DN_HEAD_DIM:].astype(BF16)
        ai_ref[hp, rows[c], :] = (kq[hp, c][C:] * decay[hp, c]).astype(BF16)
        dl_ref[hp, c:c + 1, :] = edl[hp][c * C:c * C + 1, :]


def gdn_prep(proj, ba, conv_w, alog_l, dtb_l):
    T = proj.shape[0]
    R = min(GDN_TILE, T)
    H = DN_HEADS
    nchunk = T // DN_CHUNK
    hb = R // HALO
    HP = GDN_PREP_HEADS
    W = HP * LANES
    col = lambda j: pl.BlockSpec((R, W), lambda i, h: (i, j * (H // HP) + h))
    halo = lambda j: pl.BlockSpec((HALO, W), lambda i, h: (jnp.maximum(i * hb - 1, 0), j * (H // HP) + h))
    KC = conv_w.shape[0]
    cw = conv_w.reshape(KC, 3, H, LANES).transpose(2, 0, 1, 3).reshape(H, KC, 3 * LANES)
    cw_spec = pl.BlockSpec((HP, KC, 3 * LANES), lambda i, h: (h, 0, 0))
    row_out = lambda dt: jax.ShapeDtypeStruct((T, DN_WIDTH), dt)
    out_col = pl.BlockSpec((R, W), lambda i, h: (i, h))
    return pl.pallas_call(
        _gdn_prep_kernel,
        grid=(T // R, H // HP),
        in_specs=[col(0), col(1), col(2), halo(0), halo(1), halo(2),
                  pl.BlockSpec((R, LANES), lambda i, h: (i, 0)),
                  cw_spec,
                  pl.BlockSpec((1, LANES), lambda i, h: (0, 0)),
                  pl.BlockSpec((1, LANES), lambda i, h: (0, 0))],
        out_specs=[out_col, out_col, out_col, out_col,
                   pl.BlockSpec((HP, R, DN_CHUNK), lambda i, h: (h, i, 0)),
                   pl.BlockSpec((HP, R // DN_CHUNK, LANES), lambda i, h: (h, i, 0))],
        out_shape=[row_out(BF16), row_out(BF16), row_out(BF16), row_out(F32),
                   jax.ShapeDtypeStruct((H, T, DN_CHUNK), BF16),
                   jax.ShapeDtypeStruct((H, nchunk, LANES), F32)],
        scratch_shapes=[pltpu.VMEM((3 * HP, R + HALO, LANES), F32),
                        pltpu.VMEM((R, LANES), F32), pltpu.VMEM((R, LANES), F32), pltpu.VMEM((R, LANES), F32),
                        pltpu.VMEM((LANES, R), F32)],
        compiler_params=_cparams(("parallel", "arbitrary")),
        name="gdn_prep",
    )(proj, proj, proj, proj, proj, proj, ba, cw, alog_l, dtb_l)


def _gdn_scan_kernel(qg_ref, kg_ref, w_ref, u_ref, ai_ref, dl_ref, o_ref, s_ref):
    @pl.when(pl.program_id(1) == 0)
    def _():
        s_ref[...] = jnp.zeros(s_ref.shape, F32)

    C = DN_CHUNK
    D = DN_HEAD_DIM
    heads = range(s_ref.shape[0])

    def chunk(c, carry):
        rows = pl.ds(pl.multiple_of(c * C, C), C)
        s = [s_ref[hh] for hh in heads]
        sb = [x.astype(BF16) for x in s]
        cols = [slice(hh * D, (hh + 1) * D) for hh in heads]
        r1 = [jnp.dot(jnp.concatenate([w_ref[rows, cols[hh]], qg_ref[rows, cols[hh]]], axis=0), sb[hh],
                      preferred_element_type=F32) for hh in heads]
        vb = [(u_ref[rows, cols[hh]] - r1[hh][:C]).astype(BF16) for hh in heads]
        for hh in heads:
            o_ref[rows, cols[hh]] = r1[hh][C:] + jnp.dot(ai_ref[hh, rows, :], vb[hh], preferred_element_type=F32)
        upd = [lax.dot_general(kg_ref[rows, cols[hh]], vb[hh], (((0,), (0,)), ((), ())),
                               preferred_element_type=F32) for hh in heads]
        for hh in heads:
            s_ref[hh] = s[hh] * dl_ref[hh, pl.ds(c, 1), :] + upd[hh]
        return carry

    lax.fori_loop(0, qg_ref.shape[0] // C, chunk, 0)


def gdn_scan(qg, kg, w, u, ai, dl):
    T = qg.shape[0]
    R = min(GDN_TILE, T)
    H = DN_HEADS
    HB = GDN_SCAN_HEADS
    col = pl.BlockSpec((R, HB * LANES), lambda h, i: (i, h))
    return pl.pallas_call(
        _gdn_scan_kernel,
        grid=(H // HB, T // R),
        in_specs=[col, col, col, col,
                  pl.BlockSpec((HB, R, DN_CHUNK), lambda h, i: (h, i, 0)),
                  pl.BlockSpec((HB, R // DN_CHUNK, LANES), lambda h, i: (h, i, 0))],
        out_specs=col,
        out_shape=jax.ShapeDtypeStruct((T, DN_WIDTH), F32),
        scratch_shapes=[pltpu.VMEM((HB, DN_HEAD_DIM, DN_HEAD_DIM), F32)],
        compiler_params=_cparams(("parallel", "arbitrary")),
        name="gdn_scan",
    )(qg, kg, w, u, ai, dl)


def _odd_out_math(o_ref, z_ref, g_ref, w_ref, h_ref):
    g = g_ref[...]
    parts = []
    for hd in range(DN_HEADS):
        sl = slice(hd * DN_HEAD_DIM, (hd + 1) * DN_HEAD_DIM)
        parts.append((_rms(o_ref[:, sl], g, RMS_EPS) * _silu(z_ref[:, sl].astype(F32))).astype(BF16))
    y = jnp.concatenate(parts, axis=-1)
    return h_ref[...] + jnp.dot(y, w_ref[...], preferred_element_type=F32)


def _route_math(h, g_ref, wr_ref, ri_ref, gate_ref, cnt_ref, carry_ref):
    i = pl.program_id(0)

    @pl.when(i == 0)
    def _():
        carry_ref[...] = jnp.zeros(carry_ref.shape, F32)

    xn = _rms(h, g_ref[...], RMS_EPS)
    logits = jnp.dot(xn.astype(BF16), wr_ref[...].astype(BF16), preferred_element_type=F32)
    tm = logits.shape[0]
    lane = lax.broadcasted_iota(I32, logits.shape, 1)
    logits = jnp.where(lane < N_EXPERTS, logits, NEG_BIG)
    lane_f = lane.astype(F32)
    m1 = jnp.max(logits, axis=-1, keepdims=True)
    i1 = jnp.min(jnp.where(logits == m1, lane_f, float(LANES)), axis=-1, keepdims=True)
    rest = jnp.where(lane_f == i1, NEG_BIG, logits)
    m2 = jnp.max(rest, axis=-1, keepdims=True)
    i2 = jnp.min(jnp.where(rest == m2, lane_f, float(LANES)), axis=-1, keepdims=True)
    e = jnp.exp(m2 - m1)
    g1 = 1.0 / (1.0 + e)
    g2 = e / (1.0 + e)
    oh1 = lane_f == i1
    oh2 = lane_f == i2
    i1 = i1.astype(I32)
    i2 = i2.astype(I32)
    oh = jnp.where(oh1 | oh2, 1.0, 0.0)
    ri_ = lax.broadcasted_iota(I32, (tm, tm), 0)
    ci_ = lax.broadcasted_iota(I32, (tm, tm), 1)
    below = jnp.where(ri_ > ci_, 1.0, 0.0).astype(BF16)
    ex = jnp.dot(below, oh.astype(BF16), preferred_element_type=F32) + carry_ref[0:1, :]
    r1 = jnp.sum(jnp.where(oh1, ex, 0.0), axis=-1, keepdims=True).astype(I32)
    r2 = jnp.sum(jnp.where(oh2, ex, 0.0), axis=-1, keepdims=True).astype(I32)
    packed = jnp.where(lane == 0, i1, jnp.where(lane == 1, i2, jnp.where(lane == 2, r1, r2)))
    ri_ref[...] = packed[:, 0:ri_ref.shape[1]]
    gate_ref[...] = jnp.where(lane == 0, g1, g2)
    carry_ref[...] = carry_ref[...] + jnp.sum(oh, axis=0, keepdims=True)
    cnt_ref[...] = carry_ref[...]


def _odd_out_route_kernel(o_ref, z_ref, g_ref, w_ref, h_ref, gr_ref, wr_ref,
                          out_ref, ri_ref, gate_ref, cnt_ref, carry_ref):
    h = _odd_out_math(o_ref, z_ref, g_ref, w_ref, h_ref)
    out_ref[...] = h
    _route_math(h, gr_ref, wr_ref, ri_ref, gate_ref, cnt_ref, carry_ref)


def odd_out_route(o, proj, onorm_g, w_out, h, g_route, wr):
    T, D = h.shape
    tm = min(ROW_TILE, T)
    zb = 3 * DN_WIDTH // DN_WIDTH
    return pl.pallas_call(
        _odd_out_route_kernel,
        grid=(T // tm,),
        in_specs=[
            pl.BlockSpec((tm, DN_WIDTH), lambda i: (i, 0)),
            pl.BlockSpec((tm, DN_WIDTH), lambda i: (i, zb)),
            pl.BlockSpec((1, DN_HEAD_DIM), lambda i: (0, 0)),
            pl.BlockSpec(w_out.shape, lambda i: (0, 0)),
            pl.BlockSpec((tm, D), lambda i: (i, 0)),
            pl.BlockSpec((1, D), lambda i: (0, 0)),
            pl.BlockSpec((D, LANES), lambda i: (0, 0)),
        ],
        out_specs=[
            pl.BlockSpec((tm, D), lambda i: (i, 0)),
            pl.BlockSpec((tm, 4), lambda i: (i, 0)),
            pl.BlockSpec((tm, LANES), lambda i: (i, 0)),
            pl.BlockSpec((SUBLANES, LANES), lambda i: (0, 0)),
        ],
        out_shape=[
            jax.ShapeDtypeStruct((T, D), F32),
            jax.ShapeDtypeStruct((T, 4), I32),
            jax.ShapeDtypeStruct((T, LANES), F32),
            jax.ShapeDtypeStruct((SUBLANES, LANES), F32),
        ],
        scratch_shapes=[pltpu.VMEM((SUBLANES, LANES), F32)],
        compiler_params=_cparams(("arbitrary",)),
        name="odd_out_route",
    )(o, proj, onorm_g.reshape(1, DN_HEAD_DIM), w_out, h, g_route.reshape(1, D), wr)


def _row_copy(src_ref, s, dst_ref, d, sem):
    return pltpu.make_async_copy(src_ref.at[pl.ds(s, 1), :], dst_ref.at[pl.ds(d, 1), :], sem)


def _dispatch_kernel(pz_ref, dest_ref, h_ref, g_ref, xs_ref, xn_ref, zb_ref, sem, zsem):
    tm = h_ref.shape[0]
    G = zb_ref.shape[0]

    @pl.when(pl.program_id(0) == 0)
    def _():
        zb_ref[...] = jnp.zeros(zb_ref.shape, F32)

        def zero_copy(e):
            return pltpu.make_async_copy(zb_ref, xs_ref.at[pl.ds(pl.multiple_of(pz_ref[e], G), G), :], zsem)

        for e in range(2 * N_EXPERTS):
            @pl.when(pz_ref[e] >= 0)
            def _():
                zero_copy(e).start()

        for e in range(2 * N_EXPERTS):
            @pl.when(pz_ref[e] >= 0)
            def _():
                zero_copy(e).wait()

    xn_ref[...] = _rms(h_ref[...], g_ref[...], RMS_EPS)

    def issue(r, carry):
        _row_copy(xn_ref, r, xs_ref, dest_ref[2 * r], sem).start(priority=0)
        _row_copy(xn_ref, r, xs_ref, dest_ref[2 * r + 1], sem).start(priority=1)
        return carry

    lax.fori_loop(0, tm, issue, 0, unroll=ROW_DMA_UNROLL)

    def drain(r, carry):
        _row_copy(xn_ref, 0, xs_ref, 0, sem).wait()
        _row_copy(xn_ref, 0, xs_ref, 0, sem).wait()
        return carry

    lax.fori_loop(0, tm, drain, 0, unroll=ROW_DMA_UNROLL)


def dispatch(h, g, dest_flat, zero_tiles, n_slots):
    T, D = h.shape
    tm = min(ROW_TILE, T)
    grid_spec = pltpu.PrefetchScalarGridSpec(
        num_scalar_prefetch=1,
        grid=(T // tm,),
        in_specs=[
            pl.BlockSpec((2 * tm,), lambda i, pz: (i,), memory_space=pltpu.SMEM),
            pl.BlockSpec((tm, D), lambda i, pz: (i, 0)),
            pl.BlockSpec((1, D), lambda i, pz: (0, 0)),
        ],
        out_specs=pl.BlockSpec(memory_space=pl.ANY),
        scratch_shapes=[pltpu.VMEM((tm, D), F32), pltpu.VMEM((MOE_TILE, D), F32),
                        pltpu.SemaphoreType.DMA, pltpu.SemaphoreType.DMA],
    )
    return pl.pallas_call(
        _dispatch_kernel,
        grid_spec=grid_spec,
        out_shape=jax.ShapeDtypeStruct((n_slots, D), F32),
        compiler_params=_cparams(("arbitrary",)),
        name="moe_dispatch",
    )(zero_tiles, dest_flat, h, g.reshape(1, D))


def _gmm_kernel(te_ref, na_ref, x_ref, w1_ref, w3_ref, w2_ref, y_ref, xb_ref, acc_ref):
    b = pl.program_id(0)
    f = pl.program_id(1)

    @pl.when(b < na_ref[0])
    def _():
        @pl.when(f == 0)
        def _():
            xb_ref[...] = x_ref[...].astype(BF16)
            acc_ref[...] = jnp.zeros(acc_ref.shape, F32)

        xb = xb_ref[...]
        a = jnp.dot(xb, w1_ref[...], preferred_element_type=F32)
        c = jnp.dot(xb, w3_ref[...], preferred_element_type=F32)
        hm = (_silu(a) * c).astype(BF16)
        acc_ref[...] += jnp.dot(hm, w2_ref[...], preferred_element_type=F32)

        @pl.when(f == pl.num_programs(1) - 1)
        def _():
            y_ref[...] = acc_ref[...]

    @pl.when((b >= na_ref[0]) & (f == pl.num_programs(1) - 1))
    def _():
        y_ref[...] = jnp.zeros(y_ref.shape, F32)


def gmm(xs, w1, w3, w2, tile_e, n_active):
    P, D = xs.shape
    G = MOE_TILE
    F = w1.shape[2]
    tf = MOE_FF_TILE
    nf = F // tf

    def row_idx(b, f, te, na):
        return (jnp.minimum(b, na[0] - 1), 0)

    def f_idx(b, f, na):
        return jnp.where(b < na[0], f, nf - 1)

    grid_spec = pltpu.PrefetchScalarGridSpec(
        num_scalar_prefetch=2,
        grid=(P // G, nf),
        in_specs=[
            pl.BlockSpec((G, D), row_idx),
            pl.BlockSpec((None, D, tf), lambda b, f, te, na: (te[b], 0, f_idx(b, f, na))),
            pl.BlockSpec((None, D, tf), lambda b, f, te, na: (te[b], 0, f_idx(b, f, na))),
            pl.BlockSpec((None, tf, D), lambda b, f, te, na: (te[b], f_idx(b, f, na), 0)),
        ],
        out_specs=pl.BlockSpec((G, D), lambda b, f, te, na: (b, 0)),
        scratch_shapes=[pltpu.VMEM((G, D), BF16), pltpu.VMEM((G, D), F32)],
    )
    return pl.pallas_call(
        _gmm_kernel,
        grid_spec=grid_spec,
        out_shape=jax.ShapeDtypeStruct((P, D), F32),
        compiler_params=_cparams(("arbitrary", "arbitrary")),
        name="moe_gmm",
    )(tile_e, n_active, xs, w1, w3, w2)


def _combine_ple_kernel(dest_ref, dnext_ref, h_ref, gate_ref, y_ref, g_ref, wg_ref, p_ref, wp_ref, gf_ref,
                        o_ref, yab_ref, sem, *, final_norm):
    i = pl.program_id(0)
    n = pl.num_programs(0)
    tm = h_ref.shape[0]
    slot = lax.rem(i, 2)

    def gather(d_ref, sl):
        def issue(r, carry):
            _row_copy(y_ref, d_ref[2 * r], yab_ref.at[sl, 0], r, sem.at[sl]).start(priority=0)
            _row_copy(y_ref, d_ref[2 * r + 1], yab_ref.at[sl, 1], r, sem.at[sl]).start(priority=1)
            return carry

        lax.fori_loop(0, tm, issue, 0, unroll=ROW_DMA_UNROLL)

    @pl.when(i == 0)
    def _():
        gather(dest_ref, 0)

    @pl.when(i + 1 < n)
    def _():
        gather(dnext_ref, 1 - slot)

    def drain(r, carry):
        _row_copy(y_ref, 0, yab_ref.at[slot, 0], 0, sem.at[slot]).wait()
        _row_copy(y_ref, 0, yab_ref.at[slot, 1], 0, sem.at[slot]).wait()
        return carry

    lax.fori_loop(0, tm, drain, 0, unroll=ROW_DMA_UNROLL)
    gt = gate_ref[...]
    h = h_ref[...] + gt[:, 0:1] * yab_ref[slot, 0] + gt[:, 1:2] * yab_ref[slot, 1]
    o_ref[...] = _ple_math(h, g_ref, wg_ref, p_ref, wp_ref, gf_ref, final_norm)


def combine_ple(h, gates, y, dest_flat, g, wg, p, wp, g_final, final_norm, layer):
    T, D = h.shape
    tm = min(ROW_TILE, T)
    n = T // tm
    return pl.pallas_call(
        functools.partial(_combine_ple_kernel, final_norm=final_norm),
        grid=(n,),
        in_specs=[
            pl.BlockSpec((2 * tm,), lambda i: (i,), memory_space=pltpu.SMEM),
            pl.BlockSpec((2 * tm,), lambda i: (jnp.minimum(i + 1, n - 1),), memory_space=pltpu.SMEM),
            pl.BlockSpec((tm, D), lambda i: (i, 0)),
            pl.BlockSpec((tm, LANES), lambda i: (i, 0)),
            pl.BlockSpec(memory_space=pl.ANY),
            pl.BlockSpec((1, D), lambda i: (0, 0)),
            pl.BlockSpec((None, D, D), lambda i: (layer, 0, 0)),
            pl.BlockSpec((None, tm, PLE_DIM), lambda i: (layer, i, 0)),
            pl.BlockSpec((None, PLE_DIM, D), lambda i: (layer, 0, 0)),
            pl.BlockSpec((1, D), lambda i: (0, 0)),
        ],
        out_specs=pl.BlockSpec((tm, D), lambda i: (i, 0)),
        out_shape=jax.ShapeDtypeStruct((T, D), F32),
        scratch_shapes=[pltpu.VMEM((2, 2, tm, D), F32), pltpu.SemaphoreType.DMA((2,))],
        compiler_params=_cparams(("arbitrary",)),
        name="moe_combine_ple",
    )(dest_flat, dest_flat, h, gates, y, g.reshape(1, D), wg, p, wp, g_final.reshape(1, D))


def _lambda_init(layer_idx):
    return 0.8 - 0.6 * math.exp(-0.3 * layer_idx)


def _even_layer(h, pos_col, pos_row, invf, ln_mix, w_in, w_out, lam_params, subln_g, conv_w, layer_idx, cast):
    qk, vt, gates_conv = even_proj(h, ln_mix, w_in.astype(BF16), pos_col, invf)
    o, narrowed = diff_attention(qk, vt, pos_col, pos_row, lam_params, subln_g, _lambda_init(layer_idx), cast)
    return even_out(o, gates_conv, conv_w, w_out.astype(BF16), h), narrowed


def _odd_mixer_route(h, ln_mix, w_in, conv_w, a_log, dt_bias, onorm_g, w_out, ln_ffn, w_router):
    main_w = 4 * DN_WIDTH
    w_main = w_in[:, :main_w].astype(BF16)
    w_ba = jnp.pad(w_in[:, main_w:], ((0, 0), (0, LANES - 2 * DN_HEADS))).astype(BF16)
    proj, ba = norm_matmul(h, ln_mix, w_main, w_side=w_ba)
    pad8 = lambda v: jnp.pad(v.astype(F32), (DN_HEADS, LANES - 2 * DN_HEADS)).reshape(1, LANES)
    qg, kg, w, u, ai, dl = gdn_prep(proj, ba, conv_w, pad8(a_log), pad8(dt_bias))
    o = gdn_scan(qg, kg, w, u, ai, dl)
    wr = jnp.pad(w_router, ((0, 0), (0, LANES - N_EXPERTS)))
    return odd_out_route(o, proj, onorm_g, w_out.astype(BF16), h, ln_ffn, wr)


def _moe(h, ln_ffn, ri, gates, cnt, w1, w3, w2):
    T = h.shape[0]
    G = MOE_TILE
    counts = cnt[0, :N_EXPERTS].astype(I32)
    padded = ((counts + G - 1) // G) * G
    pends = jnp.cumsum(padded)
    pstarts = pends - padded
    dest = (jnp.take(pstarts, ri[:, 0:2]) + ri[:, 2:4]).reshape(-1)
    n_tiles = (2 * T) // G + N_EXPERTS
    tile_start = jnp.arange(n_tiles, dtype=I32) * G
    tile_e = jnp.minimum(jnp.sum(pends[None, :] <= tile_start[:, None], axis=1), N_EXPERTS - 1).astype(I32)
    n_active = (pends[-1:] // G).astype(I32)
    tail = pends[-1] + jnp.arange(N_EXPERTS, dtype=I32) * G
    zero_tiles = jnp.concatenate([jnp.where(padded > 0, pends - G, -1),
                                  jnp.where(tail < n_tiles * G, tail, -1)]).astype(I32)
    xs = dispatch(h, ln_ffn, dest, zero_tiles, n_tiles * G)
    y = gmm(xs, w1, w3, w2, tile_e, n_active)
    return gates, y, dest


def kernel(x, p, positions, ln_mix, ln_ffn, ln_ple, ln_final, w_in_even, w_out_even, lam_q1, lam_k1, lam_q2, lam_k2, subln_gain, conv_w_short, w_in_odd, conv_w_qkv, a_log, dt_bias, onorm_gain, w_out_odd, w1_dense, w3_dense, w2_dense, w_router, w1_moe, w3_moe, w2_moe, w_ple_gate, w_ple_proj):
    B, S, D = x.shape
    T = B * S
    depth = p.shape[0]
    h = x.reshape(T, D)
    pos_col = positions.reshape(T, 1).astype(I32)
    pos_row = positions.reshape(1, T).astype(I32)
    inv_freq = ROPE_THETA ** (-jnp.arange(0, ROT_DIM, 2, dtype=F32) / ROT_DIM)
    invf = jnp.tile(inv_freq, LANES // (ROT_DIM // 2)).reshape(1, LANES)
    p_rows = p.reshape(depth, T, PLE_DIM)
    n_odd, n_exp, _, ff = w1_moe.shape
    moe_f32 = [w.reshape(n_odd, -1, w.shape[-1]) for w in (w1_moe, w3_moe, w2_moe)]
    ple_f32 = [w.reshape(1, -1, w.shape[-1]) for w in (w_ple_gate, w_ple_proj)]
    moe_bf16 = wpg = wpp = None
    for i in range(depth):
        j = i // 2
        if i % 2 == 0:
            lam_params = jnp.stack([lam_q1[j], lam_k1[j], lam_q2[j], lam_k2[j]]).astype(F32)
            cast = [(w, j) for w in (w1_dense, w3_dense, w2_dense)]
            if i + 1 < depth:
                cast += [(w, j) for w in moe_f32]
            if i == 0:
                cast += [(w, 0) for w in ple_f32]
            h, narrowed = _even_layer(h, pos_col, pos_row, invf, ln_mix[i], w_in_even[j], w_out_even[j], lam_params,
                                      subln_gain[j], conv_w_short[j], i, tuple(cast))
            narrowed = list(narrowed)
            w1d, w3d, w2d = narrowed[:3]
            del narrowed[:3]
            if i + 1 < depth:
                moe_bf16 = (narrowed[0].reshape(n_exp, D, ff), narrowed[1].reshape(n_exp, D, ff),
                            narrowed[2].reshape(n_exp, ff, D))
                del narrowed[:3]
            if i == 0:
                wpg = narrowed[0].reshape(w_ple_gate.shape)
                wpp = narrowed[1].reshape(w_ple_proj.shape)
            h = ffn_ple(h, ln_ffn[i], w1d, w3d, w2d, ln_ple[i], wpg, p_rows, wpp, ln_final,
                        final_norm=(i == depth - 1), ple_layer=i)
        else:
            h, ri, gates, cnt = _odd_mixer_route(h, ln_mix[i], w_in_odd[j], conv_w_qkv[j], a_log[j], dt_bias[j],
                                                 onorm_gain[j], w_out_odd[j], ln_ffn[i], w_router[j])
            gates, y, dest = _moe(h, ln_ffn[i], ri, gates, cnt, *moe_bf16)
            h = combine_ple(h, gates, y, dest, ln_ple[i], wpg, p_rows, wpp, ln_final,
                            final_norm=(i == depth - 1), layer=i)
    return h.reshape(B, S, D)
```

```python
import functools
import math

import jax
import jax.numpy as jnp
import numpy as np
from jax import lax
from jax.experimental import pallas as pl
from jax.experimental.pallas import tpu as pltpu

F32 = jnp.float32
BF16 = jnp.bfloat16
I32 = jnp.int32

D_MODEL = 1024
DEPTH = 4
RMS_EPS = 1e-6
DA_HEADS = 4
DA_HEAD_DIM = 64
DA_V_DIM = 2 * DA_HEAD_DIM
ROPE_THETA = 500000.0
ROT_DIM = DA_HEAD_DIM // 4
SUBLN_EPS = 1e-5
SC_WIDTH = 512
DN_HEADS = 8
DN_HEAD_DIM = 128
DN_WIDTH = DN_HEADS * DN_HEAD_DIM
DN_CHUNK = 64
D_FF = 3584
N_EXPERTS = 8
PLE_DIM = 256

LANES = 128
SUBLANES = 8
HALO = 16
VMEM_LIMIT = 52 * 1024 * 1024

NEG_BIG = -1e30

ROW_TILE = 512
PROJ_COL_TILE = 1024
FFN_ROW_TILE = 512
FF_TILE = 1792
MOE_FF_TILE = 1792
ATT_TQ = 1024
ATT_TK = 1024
ATT_EXP_ROWS = 128
CAST_STEPS_PER_BLOCK = 4
ATT_TQ_SPLIT = 256
GDN_TILE = 1024
GDN_PREP_HEADS = 2
GDN_SCAN_HEADS = 8
MOE_TILE = 512
ROW_DMA_UNROLL = 8


def _cparams(sem):
    return pltpu.CompilerParams(dimension_semantics=sem, vmem_limit_bytes=VMEM_LIMIT)


def _rms(x, g, eps):
    ms = jnp.mean(x * x, axis=-1, keepdims=True)
    return x * lax.rsqrt(ms + eps) * g


def _silu(x):
    return x * (1.0 / (1.0 + jnp.exp(-x)))


def _sigmoid(x):
    return 1.0 / (1.0 + jnp.exp(-x))


def _norm_matmul_kernel(x_ref, g_ref, w_ref, *rest):
    xn = _rms(x_ref[...], g_ref[...], RMS_EPS).astype(BF16)
    if len(rest) == 1:
        (o_ref,) = rest
    else:
        ws_ref, o_ref, os_ref = rest
        os_ref[...] = jnp.dot(xn, ws_ref[...], preferred_element_type=F32)
    tn = PROJ_COL_TILE
    for c in range(o_ref.shape[1] // tn):
        o_ref[:, c * tn:(c + 1) * tn] = jnp.dot(
            xn, w_ref[:, c * tn:(c + 1) * tn], preferred_element_type=F32).astype(o_ref.dtype)


def norm_matmul(x, g, w, w_side=None, out_dtype=BF16):
    T, D = x.shape
    N = w.shape[1]
    tm = min(ROW_TILE, T)
    in_specs = [
        pl.BlockSpec((tm, D), lambda i: (i, 0)),
        pl.BlockSpec((1, D), lambda i: (0, 0)),
        pl.BlockSpec((D, N), lambda i: (0, 0)),
    ]
    out_specs = [pl.BlockSpec((tm, N), lambda i: (i, 0))]
    out_shape = [jax.ShapeDtypeStruct((T, N), out_dtype)]
    args = [x, g.reshape(1, D), w]
    if w_side is not None:
        ns = w_side.shape[1]
        in_specs.append(pl.BlockSpec((D, ns), lambda i: (0, 0)))
        out_specs.append(pl.BlockSpec((tm, ns), lambda i: (i, 0)))
        out_shape.append(jax.ShapeDtypeStruct((T, ns), F32))
        args.append(w_side)
    return pl.pallas_call(
        _norm_matmul_kernel,
        grid=(T // tm,),
        in_specs=in_specs,
        out_specs=out_specs,
        out_shape=out_shape,
        compiler_params=_cparams(("parallel",)),
        name="norm_matmul",
    )(*args)


def _even_proj_kernel(x_ref, g_ref, w_ref, pos_ref, invf_ref, qk_ref, vt_ref, rest_ref):
    xn = _rms(x_ref[...], g_ref[...], RMS_EPS).astype(BF16)
    WQK = qk_ref.shape[1]
    WV = vt_ref.shape[0]

    def proj(c0, c1):
        return jnp.dot(xn, w_ref[:, c0:c1], preferred_element_type=F32)

    pos = pos_ref[...].astype(F32)
    ang = pos * invf_ref[...]
    cos_t = jnp.cos(ang)
    sin_t = jnp.sin(ang)
    d = lax.broadcasted_iota(I32, ang.shape, 1) & (DA_HEAD_DIM - 1)
    half = ROT_DIM // 2
    c_mul = jnp.where(d < ROT_DIM, cos_t, 1.0)
    s_mul = jnp.where(d < half, -sin_t, jnp.where(d < ROT_DIM, sin_t, 0.0))
    qk = proj(0, WQK)
    n_q = WQK // (2 * LANES)
    for c in range(WQK // LANES):
        x = qk[:, c * LANES:(c + 1) * LANES]
        swapped = jnp.where(d < half, pltpu.roll(x, LANES - half, 1), pltpu.roll(x, half, 1))
        r = x * c_mul + swapped * s_mul
        if c < n_q:
            r = r * (DA_HEAD_DIM ** -0.5 * math.log2(math.e))
        qk_ref[:, c * LANES:(c + 1) * LANES] = r.astype(BF16)
    v = proj(WQK, WQK + WV)
    for hd in range(DA_HEADS):
        vt_ref[hd * DA_V_DIM:(hd + 1) * DA_V_DIM, :] = v[:, hd * DA_V_DIM:(hd + 1) * DA_V_DIM].T.astype(BF16)
    WR = rest_ref.shape[1]
    for c0 in range(0, WR, PROJ_COL_TILE):
        c1 = min(c0 + PROJ_COL_TILE, WR)
        rest_ref[:, c0:c1] = proj(WQK + WV + c0, WQK + WV + c1).astype(BF16)


def even_proj(x, g, w, positions_col, invf):
    T, D = x.shape
    WQK = 2 * 2 * DA_HEADS * DA_HEAD_DIM
    WV = DA_HEADS * DA_V_DIM
    WR = w.shape[1] - WQK - WV
    tm = min(ROW_TILE, T)
    return pl.pallas_call(
        _even_proj_kernel,
        grid=(T // tm,),
        in_specs=[
            pl.BlockSpec((tm, D), lambda i: (i, 0)),
            pl.BlockSpec((1, D), lambda i: (0, 0)),
            pl.BlockSpec(w.shape, lambda i: (0, 0)),
            pl.BlockSpec((tm, 1), lambda i: (i, 0)),
            pl.BlockSpec((1, LANES), lambda i: (0, 0)),
        ],
        out_specs=[pl.BlockSpec((tm, WQK), lambda i: (i, 0)),
                   pl.BlockSpec((WV, tm), lambda i: (0, i)),
                   pl.BlockSpec((tm, WR), lambda i: (i, 0))],
        out_shape=[jax.ShapeDtypeStruct((T, WQK), BF16), jax.ShapeDtypeStruct((WV, T), BF16),
                   jax.ShapeDtypeStruct((T, WR), BF16)],
        compiler_params=_cparams(("parallel",)),
        name="even_proj",
    )(x, g.reshape(1, D), w, positions_col, invf)


def _attn_kernel(qt_ref, kt_ref, q_ref, k_ref, v_ref, pq_ref, pk_ref, lam_ref, g_ref, *rest, tq, tk, lam0, n_cast):
    cast_in = rest[:n_cast]
    o_ref = rest[n_cast]
    cast_out = rest[n_cast + 1:2 * n_cast + 1]
    m_ref, l_ref, acc_ref = rest[2 * n_cast + 1:]
    for wi_ref, wo_ref in zip(cast_in, cast_out):
        wo_ref[...] = wi_ref[...].astype(BF16)

    p = pl.program_id(1)
    qi = qt_ref[p]
    ki = kt_ref[p]

    @pl.when(ki == 0)
    def _():
        m_ref[...] = jnp.full(m_ref.shape, NEG_BIG, F32)
        l_ref[...] = jnp.zeros(l_ref.shape, F32)
        acc_ref[...] = jnp.zeros(acc_ref.shape, F32)

    tqs = min(tq, ATT_TQ_SPLIT)
    items = [(s, j) for s in range(2) for j in range(tq // tqs)]

    def step(masked):
        q = q_ref[...]
        k = k_ref[...]
        v = v_ref[...]
        def nkeys(j):
            return (j + 1) * tqs if (masked and tq == tk) else tk

        st = {}
        for s, j in items:
            cols = slice(j * tqs, (j + 1) * tqs)
            qs = q[j * tqs:(j + 1) * tqs, s * DA_HEAD_DIM:(s + 1) * DA_HEAD_DIM]
            ks = k[:nkeys(j), s * DA_HEAD_DIM:(s + 1) * DA_HEAD_DIM]
            sc = lax.dot_general(ks, qs, (((1,), (1,)), ((), ())), preferred_element_type=F32)
            if masked:
                sc = jnp.where(pk_ref[0:nkeys(j), :] <= pq_ref[:, cols], sc, NEG_BIG)
            st[s, j] = sc
        for s, j in items:
            cols = slice(j * tqs, (j + 1) * tqs)
            sc = st[s, j]
            nk = nkeys(j)
            m_prev = m_ref[s, :, cols]
            m_new = jnp.maximum(m_prev, jnp.max(sc, axis=0, keepdims=True))
            alpha = jnp.exp2(m_prev - m_new)
            psum = jnp.zeros((1, tqs), F32)
            pchunks = []
            for r0 in range(0, nk, ATT_EXP_ROWS):
                pc = jnp.exp2(sc[r0:r0 + ATT_EXP_ROWS] - m_new)
                psum = psum + jnp.sum(pc, axis=0, keepdims=True)
                pchunks.append(pc.astype(BF16))
            pb = jnp.concatenate(pchunks, axis=0)
            l_ref[s, :, cols] = alpha * l_ref[s, :, cols] + psum
            pv = jnp.dot(v[:, :nk], pb, preferred_element_type=F32)
            acc_ref[s, :, cols] = alpha * acc_ref[s, :, cols] + pv
            m_ref[s, :, cols] = m_new

    crosses = (ki + 1) * tk - 1 > qi * tq

    @pl.when(crosses)
    def _():
        step(True)

    @pl.when(jnp.logical_not(crosses))
    def _():
        step(False)

    @pl.when((ki + 1) * tk >= (qi + 1) * tq)
    def _():
        lm = lam_ref[...]
        s1 = jnp.sum(lm[0:1] * lm[1:2], axis=-1, keepdims=True)
        s2 = jnp.sum(lm[2:3] * lm[3:4], axis=-1, keepdims=True)
        lam = jnp.exp(s1) - jnp.exp(s2) + lam0
        ot = acc_ref[0] * (1.0 / l_ref[0]) - lam * (acc_ref[1] * (1.0 / l_ref[1]))
        ms = jnp.mean(ot * ot, axis=0, keepdims=True)
        ot = ot * lax.rsqrt(ms + SUBLN_EPS) * g_ref[...] * (1.0 - lam0)
        o_ref[...] = ot.T.astype(o_ref.dtype)


def _cast_blocks(rows, steps):
    for nb in range(min(steps, rows), 0, -1):
        if rows % nb == 0 and (rows // nb) % HALO == 0:
            return nb
    raise ValueError((rows, steps))


def diff_attention(qk, vt, pos_col, pos_row, lam_params, subln_g, lam0, cast=()):
    T = qk.shape[0]
    tq = min(ATT_TQ, T)
    tk = min(ATT_TK, T)
    nq = T // tq
    pairs = [(qi, ki) for qi in range(nq) for ki in range(-(-((qi + 1) * tq) // tk))]
    qt = jnp.asarray(np.array([a for a, _ in pairs], np.int32))
    kt = jnp.asarray(np.array([b for _, b in pairs], np.int32))
    H = DA_HEADS
    npairs = len(pairs)
    kern = functools.partial(_attn_kernel, tq=tq, tk=tk, lam0=lam0, n_cast=len(cast))
    cast_in_specs, cast_out_specs, cast_out_shapes = [], [], []
    for w, layer in cast:
        _, rows, cols = w.shape
        nb = _cast_blocks(rows, max(1, H * npairs // CAST_STEPS_PER_BLOCK))
        br = rows // nb

        def blk(h, p, nb=nb):
            return jnp.minimum((h * npairs + p) // CAST_STEPS_PER_BLOCK, nb - 1)

        cast_in_specs.append(pl.BlockSpec((None, br, cols), lambda h, p, qt, kt, layer=layer, blk=blk: (layer, blk(h, p), 0)))
        cast_out_specs.append(pl.BlockSpec((br, cols), lambda h, p, qt, kt, blk=blk: (blk(h, p), 0)))
        cast_out_shapes.append(jax.ShapeDtypeStruct((rows, cols), BF16))
    grid_spec = pltpu.PrefetchScalarGridSpec(
        num_scalar_prefetch=2,
        grid=(H, npairs),
        in_specs=[
            pl.BlockSpec((tq, LANES), lambda h, p, qt, kt: (qt[p], h)),
            pl.BlockSpec((tk, LANES), lambda h, p, qt, kt: (kt[p], H + h)),
            pl.BlockSpec((DA_V_DIM, tk), lambda h, p, qt, kt: (h, kt[p])),
            pl.BlockSpec((1, tq), lambda h, p, qt, kt: (0, qt[p])),
            pl.BlockSpec((tk, 1), lambda h, p, qt, kt: (kt[p], 0)),
            pl.BlockSpec((4, DA_HEAD_DIM), lambda h, p, qt, kt: (0, 0)),
            pl.BlockSpec((DA_V_DIM, 1), lambda h, p, qt, kt: (0, 0)),
        ] + cast_in_specs,
        out_specs=[pl.BlockSpec((tq, DA_V_DIM), lambda h, p, qt, kt: (qt[p], h))] + cast_out_specs,
        scratch_shapes=[
            pltpu.VMEM((2, 1, tq), F32),
            pltpu.VMEM((2, 1, tq), F32),
            pltpu.VMEM((2, DA_V_DIM, tq), F32),
        ],
    )
    outs = pl.pallas_call(
        kern,
        grid_spec=grid_spec,
        out_shape=[jax.ShapeDtypeStruct((T, H * DA_V_DIM), BF16)] + cast_out_shapes,
        compiler_params=_cparams(("arbitrary", "arbitrary")),
        name="diff_attention",
    )(qt, kt, qk, qk, vt, pos_row, pos_col, lam_params, subln_g.reshape(DA_V_DIM, 1), *[w for w, _ in cast])
    return outs[0], outs[1:]


def _even_out_kernel(o_ref, b_ref, c_ref, x_ref, ch_ref, xh_ref, cw_ref, w_ref, h_ref, out_ref, u_ref):
    tm = o_ref.shape[0]
    u_prev = ch_ref[...].astype(F32) * xh_ref[...].astype(F32)
    u_prev = jnp.where(pl.program_id(0) == 0, 0.0, u_prev)
    u_ref[0:HALO, :] = u_prev
    u_ref[HALO:, :] = c_ref[...].astype(F32) * x_ref[...].astype(F32)
    cw = cw_ref[...]
    conv = (u_ref[HALO - 2:HALO - 2 + tm, :] * cw[0:1]
            + u_ref[HALO - 1:HALO - 1 + tm, :] * cw[1:2]
            + u_ref[HALO:, :] * cw[2:3])
    sc = (b_ref[...].astype(F32) * conv).astype(BF16)
    na = o_ref.shape[1]
    acc = jnp.dot(o_ref[...], w_ref[0:na, :], preferred_element_type=F32)
    acc = acc + jnp.dot(sc, w_ref[na:, :], preferred_element_type=F32)
    out_ref[...] = h_ref[...] + acc


def even_out(attn_o, proj, conv_w, w_out, h):
    T = h.shape[0]
    tm = min(ROW_TILE, T)
    W = SC_WIDTH
    cb = (proj.shape[1] - 3 * W) // W
    hb = tm // HALO
    halo = lambda col: pl.BlockSpec((HALO, W), lambda i: (jnp.maximum(i * hb - 1, 0), col))
    return pl.pallas_call(
        _even_out_kernel,
        grid=(T // tm,),
        in_specs=[
            pl.BlockSpec((tm, attn_o.shape[1]), lambda i: (i, 0)),
            pl.BlockSpec((tm, W), lambda i: (i, cb)),
            pl.BlockSpec((tm, W), lambda i: (i, cb + 1)),
            pl.BlockSpec((tm, W), lambda i: (i, cb + 2)),
            halo(cb + 1),
            halo(cb + 2),
            pl.BlockSpec(conv_w.shape, lambda i: (0, 0)),
            pl.BlockSpec(w_out.shape, lambda i: (0, 0)),
            pl.BlockSpec((tm, D_MODEL), lambda i: (i, 0)),
        ],
        out_specs=pl.BlockSpec((tm, D_MODEL), lambda i: (i, 0)),
        out_shape=jax.ShapeDtypeStruct((T, D_MODEL), F32),
        scratch_shapes=[pltpu.VMEM((tm + HALO, W), F32)],
        compiler_params=_cparams(("parallel",)),
        name="even_out",
    )(attn_o, proj, proj, proj, proj, proj, conv_w, w_out, h)


def _ffn_kernel(h_ref, g_ref, w1_ref, w3_ref, w2_ref, gp_ref, wg_ref, p_ref, wp_ref, gf_ref, o_ref, xn_ref, acc_ref,
                *, final_norm):
    f = pl.program_id(1)

    @pl.when(f == 0)
    def _():
        xn_ref[...] = _rms(h_ref[...], g_ref[...], RMS_EPS).astype(BF16)
        acc_ref[...] = jnp.zeros(acc_ref.shape, F32)

    xn = xn_ref[...]
    a = jnp.dot(xn, w1_ref[...], preferred_element_type=F32)
    b = jnp.dot(xn, w3_ref[...], preferred_element_type=F32)
    hm = (_silu(a) * b).astype(BF16)
    acc_ref[...] += jnp.dot(hm, w2_ref[...], preferred_element_type=F32)

    @pl.when(f == pl.num_programs(1) - 1)
    def _():
        o_ref[...] = _ple_math(h_ref[...] + acc_ref[...], gp_ref, wg_ref, p_ref, wp_ref, gf_ref, final_norm)


def ffn_ple(h, g, w1, w3, w2, layer, g_ple, wg, p, wp, g_final, final_norm, ple_layer):
    T, D = h.shape
    F = w1.shape[2]
    tm = min(FFN_ROW_TILE, T)
    tf = FF_TILE
    return pl.pallas_call(
        functools.partial(_ffn_kernel, final_norm=final_norm),
        grid=(T // tm, F // tf),
        in_specs=[
            pl.BlockSpec((tm, D), lambda i, f: (i, 0)),
            pl.BlockSpec((1, D), lambda i, f: (0, 0)),
            pl.BlockSpec((None, D, tf), lambda i, f: (layer, 0, f)),
            pl.BlockSpec((None, D, tf), lambda i, f: (layer, 0, f)),
            pl.BlockSpec((None, tf, D), lambda i, f: (layer, f, 0)),
            pl.BlockSpec((1, D), lambda i, f: (0, 0)),
            pl.BlockSpec((None, D, D), lambda i, f: (ple_layer, 0, 0)),
            pl.BlockSpec((None, tm, PLE_DIM), lambda i, f: (ple_layer, i, 0)),
            pl.BlockSpec((None, PLE_DIM, D), lambda i, f: (ple_layer, 0, 0)),
            pl.BlockSpec((1, D), lambda i, f: (0, 0)),
        ],
        out_specs=pl.BlockSpec((tm, D), lambda i, f: (i, 0)),
        out_shape=jax.ShapeDtypeStruct((T, D), F32),
        scratch_shapes=[pltpu.VMEM((tm, D), BF16), pltpu.VMEM((tm, D), F32)],
        compiler_params=_cparams(("parallel", "arbitrary")),
        name="ffn_ple",
    )(h, g.reshape(1, D), w1, w3, w2, g_ple.reshape(1, D), wg, p, wp, g_final.reshape(1, D))


def _ple_math(h, g_ref, wg_ref, p_ref, wp_ref, gf_ref, final_norm):
    xn = _rms(h, g_ref[...], RMS_EPS).astype(BF16)
    gate = _sigmoid(jnp.dot(xn, wg_ref[...], preferred_element_type=F32))
    emb = jnp.dot(p_ref[...].astype(BF16), wp_ref[...], preferred_element_type=F32)
    out = h + gate * emb
    if final_norm:
        out = _rms(out, gf_ref[...], RMS_EPS)
    return out


def _dot_hi(a, b):
    return jnp.dot(a, b, preferred_element_type=F32, precision=lax.Precision.HIGHEST)


def _gdn_prep_kernel(q_ref, k_ref, v_ref, qh_ref, kh_ref, vh_ref, ba_ref, cw_ref, alog_ref, dtb_ref,
                     qg_ref, kg_ref, w_ref, u_ref, ai_ref, dl_ref,
                     xs_ref, gc_ref, gl_ref, sg_ref, gct_ref):
    i = pl.program_id(0)
    h = pl.program_id(1)
    R = q_ref.shape[0]
    C = DN_CHUNK
    KC = cw_ref.shape[1]

    def conv_silu(x_ref, halo_ref, j, hp):
        cs = slice(hp * LANES, (hp + 1) * LANES)
        buf = xs_ref.at[3 * hp + j]
        buf[0:HALO, :] = jnp.where(i == 0, 0.0, halo_ref[:, cs].astype(F32))
        buf[HALO:, :] = x_ref[:, cs].astype(F32)
        acc = buf[HALO:, :] * cw_ref[hp, KC - 1:KC, j * LANES:(j + 1) * LANES]
        for t in range(1, KC):
            acc = acc + (buf[HALO - t:HALO - t + R, :]
                         * cw_ref[hp, KC - 1 - t:KC - t, j * LANES:(j + 1) * LANES])
        return _silu(acc)

    def l2n(x):
        return x * lax.rsqrt(jnp.sum(x * x, axis=-1, keepdims=True) + 1e-6)

    chunks = range(R // C)
    rows = [slice(c * C, (c + 1) * C) for c in chunks]
    r64 = lax.broadcasted_iota(I32, (C, C), 0)
    c64 = lax.broadcasted_iota(I32, (C, C), 1)
    incl = r64 >= c64
    strict = r64 > c64

    @pl.when(h == 0)
    def _():
        ba = ba_ref[...]
        sp_in = ba + dtb_ref[...]
        softplus = jnp.maximum(sp_in, 0.0) + jnp.log(1.0 + jnp.exp(-jnp.abs(sp_in)))
        g_all = -jnp.exp(alog_ref[...]) * softplus
        g_wide = jnp.concatenate([g_all[r] for r in rows], axis=1)
        gc_wide = _dot_hi(jnp.where(incl, 1.0, 0.0), g_wide)
        for c in chunks:
            gc_c = gc_wide[:, c * LANES:(c + 1) * LANES]
            gc_ref[rows[c], :] = gc_c
            gl_ref[rows[c], :] = jnp.broadcast_to(gc_c[C - 1:C, :], (C, LANES))
        sg_ref[...] = _sigmoid(ba)
        gct_ref[...] = gc_ref[...].T

    lane = lax.broadcasted_iota(I32, (R, LANES), 1)

    def pick(ref, l):
        return jnp.sum(jnp.where(lane == l, ref[...], 0.0), axis=-1, keepdims=True)

    def bdot(a, b):
        return jnp.dot(a.astype(BF16), b.astype(BF16), preferred_element_type=F32)

    heads = range(qg_ref.shape[1] // LANES)
    items = [(hp, c) for hp in heads for c in chunks]
    decay, kq, rhs, edl = {}, {}, {}, {}
    for hp in heads:
        hh = h * len(heads) + hp
        cs = slice(hp * LANES, (hp + 1) * LANES)
        q = l2n(conv_silu(q_ref, qh_ref, 0, hp)) * (DN_HEAD_DIM ** -0.5)
        k = l2n(conv_silu(k_ref, kh_ref, 1, hp))
        v = conv_silu(v_ref, vh_ref, 2, hp)
        beta = pick(sg_ref, hh)
        gcol = pick(gc_ref, DN_HEADS + hh)
        glast = pick(gl_ref, DN_HEADS + hh)
        grow = gct_ref[pl.ds(DN_HEADS + hh, 1), :]
        eg = jnp.exp(gcol)
        kb = k * beta
        qg_ref[:, cs] = (q * eg).astype(BF16)
        kg_ref[:, cs] = (k * jnp.exp(glast - gcol)).astype(BF16)
        edl[hp] = jnp.broadcast_to(jnp.exp(glast), (R, LANES))
        rhs[hp] = jnp.concatenate([v * beta, kb * eg], axis=-1)
        kbf = k.astype(BF16)
        lhs = jnp.concatenate([kb.astype(BF16).reshape(R // C, C, LANES),
                               q.astype(BF16).reshape(R // C, C, LANES)], axis=1)
        for c in chunks:
            r = rows[c]
            decay[hp, c] = jnp.where(incl, jnp.exp(jnp.where(incl, gcol[r] - grow[:, r], 0.0)), 0.0)
            kq[hp, c] = lax.dot_general(lhs[c], kbf[r], (((1,), (1,)), ((), ())), preferred_element_type=F32)
    pw = {it: -jnp.where(strict, kq[it][:C] * decay[it], 0.0) for it in items}
    n = dict(pw)
    for _ in range(5):
        pw = {it: bdot(pw[it], pw[it]) for it in items}
        n = {it: n[it] + pw[it] + bdot(n[it], pw[it]) for it in items}
    for hp, c in items:
        cs = slice(hp * LANES, (hp + 1) * LANES)
        rc = rhs[hp][rows[c]]
        uw = rc + bdot(n[hp, c], rc)
        u_ref[rows[c], cs] = uw[:, :DN_HEAD_DIM]
        w_ref[rows[c], cs] = uw[:, DN_HEAD_DIM:].astype(BF16)
        ai_ref[hp, rows[c], :] = (kq[hp, c][C:] * decay[hp, c]).astype(BF16)
        dl_ref[hp, c:c + 1, :] = edl[hp][c * C:c * C + 1, :]


def gdn_prep(proj, ba, conv_w, alog_l, dtb_l):
    T = proj.shape[0]
    R = min(GDN_TILE, T)
    H = DN_HEADS
    nchunk = T // DN_CHUNK
    hb = R // HALO
    HP = GDN_PREP_HEADS
    W = HP * LANES
    col = lambda j: pl.BlockSpec((R, W), lambda i, h: (i, j * (H // HP) + h))
    halo = lambda j: pl.BlockSpec((HALO, W), lambda i, h: (jnp.maximum(i * hb - 1, 0), j * (H // HP) + h))
    KC = conv_w.shape[0]
    cw = conv_w.reshape(KC, 3, H, LANES).transpose(2, 0, 1, 3).reshape(H, KC, 3 * LANES)
    cw_spec = pl.BlockSpec((HP, KC, 3 * LANES), lambda i, h: (h, 0, 0))
    row_out = lambda dt: jax.ShapeDtypeStruct((T, DN_WIDTH), dt)
    out_col = pl.BlockSpec((R, W), lambda i, h: (i, h))
    return pl.pallas_call(
        _gdn_prep_kernel,
        grid=(T // R, H // HP),
        in_specs=[col(0), col(1), col(2), halo(0), halo(1), halo(2),
                  pl.BlockSpec((R, LANES), lambda i, h: (i, 0)),
                  cw_spec,
                  pl.BlockSpec((1, LANES), lambda i, h: (0, 0)),
                  pl.BlockSpec((1, LANES), lambda i, h: (0, 0))],
        out_specs=[out_col, out_col, out_col, out_col,
                   pl.BlockSpec((HP, R, DN_CHUNK), lambda i, h: (h, i, 0)),
                   pl.BlockSpec((HP, R // DN_CHUNK, LANES), lambda i, h: (h, i, 0))],
        out_shape=[row_out(BF16), row_out(BF16), row_out(BF16), row_out(F32),
                   jax.ShapeDtypeStruct((H, T, DN_CHUNK), BF16),
                   jax.ShapeDtypeStruct((H, nchunk, LANES), F32)],
        scratch_shapes=[pltpu.VMEM((3 * HP, R + HALO, LANES), F32),
                        pltpu.VMEM((R, LANES), F32), pltpu.VMEM((R, LANES), F32), pltpu.VMEM((R, LANES), F32),
                        pltpu.VMEM((LANES, R), F32)],
        compiler_params=_cparams(("parallel", "arbitrary")),
        name="gdn_prep",
    )(proj, proj, proj, proj, proj, proj, ba, cw, alog_l, dtb_l)


def _gdn_scan_kernel(qg_ref, kg_ref, w_ref, u_ref, ai_ref, dl_ref, o_ref, s_ref):
    @pl.when(pl.program_id(1) == 0)
    def _():
        s_ref[...] = jnp.zeros(s_ref.shape, F32)

    C = DN_CHUNK
    D = DN_HEAD_DIM
    heads = range(s_ref.shape[0])

    def chunk(c, carry):
        rows = pl.ds(pl.multiple_of(c * C, C), C)
        s = [s_ref[hh] for hh in heads]
        sb = [x.astype(BF16) for x in s]
        cols = [slice(hh * D, (hh + 1) * D) for hh in heads]
        r1 = [jnp.dot(jnp.concatenate([w_ref[rows, cols[hh]], qg_ref[rows, cols[hh]]], axis=0), sb[hh],
                      preferred_element_type=F32) for hh in heads]
        vb = [(u_ref[rows, cols[hh]] - r1[hh][:C]).astype(BF16) for hh in heads]
        for hh in heads:
            o_ref[rows, cols[hh]] = r1[hh][C:] + jnp.dot(ai_ref[hh, rows, :], vb[hh], preferred_element_type=F32)
        upd = [lax.dot_general(kg_ref[rows, cols[hh]], vb[hh], (((0,), (0,)), ((), ())),
                               preferred_element_type=F32) for hh in heads]
        for hh in heads:
            s_ref[hh] = s[hh] * dl_ref[hh, pl.ds(c, 1), :] + upd[hh]
        return carry

    lax.fori_loop(0, qg_ref.shape[0] // C, chunk, 0)


def gdn_scan(qg, kg, w, u, ai, dl):
    T = qg.shape[0]
    R = min(GDN_TILE, T)
    H = DN_HEADS
    HB = GDN_SCAN_HEADS
    col = pl.BlockSpec((R, HB * LANES), lambda h, i: (i, h))
    return pl.pallas_call(
        _gdn_scan_kernel,
        grid=(H // HB, T // R),
        in_specs=[col, col, col, col,
                  pl.BlockSpec((HB, R, DN_CHUNK), lambda h, i: (h, i, 0)),
                  pl.BlockSpec((HB, R // DN_CHUNK, LANES), lambda h, i: (h, i, 0))],
        out_specs=col,
        out_shape=jax.ShapeDtypeStruct((T, DN_WIDTH), F32),
        scratch_shapes=[pltpu.VMEM((HB, DN_HEAD_DIM, DN_HEAD_DIM), F32)],
        compiler_params=_cparams(("parallel", "arbitrary")),
        name="gdn_scan",
    )(qg, kg, w, u, ai, dl)


def _odd_out_math(o_ref, z_ref, g_ref, w_ref, h_ref):
    g = g_ref[...]
    parts = []
    for hd in range(DN_HEADS):
        sl = slice(hd * DN_HEAD_DIM, (hd + 1) * DN_HEAD_DIM)
        parts.append((_rms(o_ref[:, sl], g, RMS_EPS) * _silu(z_ref[:, sl].astype(F32))).astype(BF16))
    y = jnp.concatenate(parts, axis=-1)
    return h_ref[...] + jnp.dot(y, w_ref[...], preferred_element_type=F32)


def _route_math(h, g_ref, wr_ref, ri_ref, gate_ref, cnt_ref, carry_ref):
    i = pl.program_id(0)

    @pl.when(i == 0)
    def _():
        carry_ref[...] = jnp.zeros(carry_ref.shape, F32)

    xn = _rms(h, g_ref[...], RMS_EPS)
    logits = jnp.dot(xn.astype(BF16), wr_ref[...].astype(BF16), preferred_element_type=F32)
    tm = logits.shape[0]
    lane = lax.broadcasted_iota(I32, logits.shape, 1)
    logits = jnp.where(lane < N_EXPERTS, logits, NEG_BIG)
    lane_f = lane.astype(F32)
    m1 = jnp.max(logits, axis=-1, keepdims=True)
    i1 = jnp.min(jnp.where(logits == m1, lane_f, float(LANES)), axis=-1, keepdims=True)
    rest = jnp.where(lane_f == i1, NEG_BIG, logits)
    m2 = jnp.max(rest, axis=-1, keepdims=True)
    i2 = jnp.min(jnp.where(rest == m2, lane_f, float(LANES)), axis=-1, keepdims=True)
    e = jnp.exp(m2 - m1)
    g1 = 1.0 / (1.0 + e)
    g2 = e / (1.0 + e)
    oh1 = lane_f == i1
    oh2 = lane_f == i2
    i1 = i1.astype(I32)
    i2 = i2.astype(I32)
    oh = jnp.where(oh1 | oh2, 1.0, 0.0)
    ri_ = lax.broadcasted_iota(I32, (tm, tm), 0)
    ci_ = lax.broadcasted_iota(I32, (tm, tm), 1)
    below = jnp.where(ri_ > ci_, 1.0, 0.0).astype(BF16)
    ex = jnp.dot(below, oh.astype(BF16), preferred_element_type=F32) + carry_ref[0:1, :]
    r1 = jnp.sum(jnp.where(oh1, ex, 0.0), axis=-1, keepdims=True).astype(I32)
    r2 = jnp.sum(jnp.where(oh2, ex, 0.0), axis=-1, keepdims=True).astype(I32)
    packed = jnp.where(lane == 0, i1, jnp.where(lane == 1, i2, jnp.where(lane == 2, r1, r2)))
    ri_ref[...] = packed[:, 0:ri_ref.shape[1]]
    gate_ref[...] = jnp.where(lane == 0, g1, g2)
    carry_ref[...] = carry_ref[...] + jnp.sum(oh, axis=0, keepdims=True)
    cnt_ref[...] = carry_ref[...]


def _odd_out_route_kernel(o_ref, z_ref, g_ref, w_ref, h_ref, gr_ref, wr_ref,
                          out_ref, ri_ref, gate_ref, cnt_ref, carry_ref):
    h = _odd_out_math(o_ref, z_ref, g_ref, w_ref, h_ref)
    out_ref[...] = h
    _route_math(h, gr_ref, wr_ref, ri_ref, gate_ref, cnt_ref, carry_ref)


def odd_out_route(o, proj, onorm_g, w_out, h, g_route, wr):
    T, D = h.shape
    tm = min(ROW_TILE, T)
    zb = 3 * DN_WIDTH // DN_WIDTH
    return pl.pallas_call(
        _odd_out_route_kernel,
        grid=(T // tm,),
        in_specs=[
            pl.BlockSpec((tm, DN_WIDTH), lambda i: (i, 0)),
            pl.BlockSpec((tm, DN_WIDTH), lambda i: (i, zb)),
            pl.BlockSpec((1, DN_HEAD_DIM), lambda i: (0, 0)),
            pl.BlockSpec(w_out.shape, lambda i: (0, 0)),
            pl.BlockSpec((tm, D), lambda i: (i, 0)),
            pl.BlockSpec((1, D), lambda i: (0, 0)),
            pl.BlockSpec((D, LANES), lambda i: (0, 0)),
        ],
        out_specs=[
            pl.BlockSpec((tm, D), lambda i: (i, 0)),
            pl.BlockSpec((tm, 4), lambda i: (i, 0)),
            pl.BlockSpec((tm, LANES), lambda i: (i, 0)),
            pl.BlockSpec((SUBLANES, LANES), lambda i: (0, 0)),
        ],
        out_shape=[
            jax.ShapeDtypeStruct((T, D), F32),
            jax.ShapeDtypeStruct((T, 4), I32),
            jax.ShapeDtypeStruct((T, LANES), F32),
            jax.ShapeDtypeStruct((SUBLANES, LANES), F32),
        ],
        scratch_shapes=[pltpu.VMEM((SUBLANES, LANES), F32)],
        compiler_params=_cparams(("arbitrary",)),
        name="odd_out_route",
    )(o, proj, onorm_g.reshape(1, DN_HEAD_DIM), w_out, h, g_route.reshape(1, D), wr)


def _row_copy(src_ref, s, dst_ref, d, sem):
    return pltpu.make_async_copy(src_ref.at[pl.ds(s, 1), :], dst_ref.at[pl.ds(d, 1), :], sem)


def _dispatch_kernel(pz_ref, dest_ref, h_ref, g_ref, xs_ref, xn_ref, zb_ref, sem, zsem):
    tm = h_ref.shape[0]
    G = zb_ref.shape[0]

    @pl.when(pl.program_id(0) == 0)
    def _():
        zb_ref[...] = jnp.zeros(zb_ref.shape, F32)

        def zero_copy(e):
            return pltpu.make_async_copy(zb_ref, xs_ref.at[pl.ds(pl.multiple_of(pz_ref[e], G), G), :], zsem)

        for e in range(2 * N_EXPERTS):
            @pl.when(pz_ref[e] >= 0)
            def _():
                zero_copy(e).start()

        for e in range(2 * N_EXPERTS):
            @pl.when(pz_ref[e] >= 0)
            def _():
                zero_copy(e).wait()

    xn_ref[...] = _rms(h_ref[...], g_ref[...], RMS_EPS)

    def issue(r, carry):
        _row_copy(xn_ref, r, xs_ref, dest_ref[2 * r], sem).start(priority=0)
        _row_copy(xn_ref, r, xs_ref, dest_ref[2 * r + 1], sem).start(priority=1)
        return carry

    lax.fori_loop(0, tm, issue, 0, unroll=ROW_DMA_UNROLL)

    def drain(r, carry):
        _row_copy(xn_ref, 0, xs_ref, 0, sem).wait()
        _row_copy(xn_ref, 0, xs_ref, 0, sem).wait()
        return carry

    lax.fori_loop(0, tm, drain, 0, unroll=ROW_DMA_UNROLL)


def dispatch(h, g, dest_flat, zero_tiles, n_slots):
    T, D = h.shape
    tm = min(ROW_TILE, T)
    grid_spec = pltpu.PrefetchScalarGridSpec(
        num_scalar_prefetch=1,
        grid=(T // tm,),
        in_specs=[
            pl.BlockSpec((2 * tm,), lambda i, pz: (i,), memory_space=pltpu.SMEM),
            pl.BlockSpec((tm, D), lambda i, pz: (i, 0)),
            pl.BlockSpec((1, D), lambda i, pz: (0, 0)),
        ],
        out_specs=pl.BlockSpec(memory_space=pl.ANY),
        scratch_shapes=[pltpu.VMEM((tm, D), F32), pltpu.VMEM((MOE_TILE, D), F32),
                        pltpu.SemaphoreType.DMA, pltpu.SemaphoreType.DMA],
    )
    return pl.pallas_call(
        _dispatch_kernel,
        grid_spec=grid_spec,
        out_shape=jax.ShapeDtypeStruct((n_slots, D), F32),
        compiler_params=_cparams(("arbitrary",)),
        name="moe_dispatch",
    )(zero_tiles, dest_flat, h, g.reshape(1, D))


def _gmm_kernel(te_ref, na_ref, x_ref, w1_ref, w3_ref, w2_ref, y_ref, xb_ref, acc_ref):
    b = pl.program_id(0)
    f = pl.program_id(1)

    @pl.when(b < na_ref[0])
    def _():
        @pl.when(f == 0)
        def _():
            xb_ref[...] = x_ref[...].astype(BF16)
            acc_ref[...] = jnp.zeros(acc_ref.shape, F32)

        xb = xb_ref[...]
        a = jnp.dot(xb, w1_ref[...], preferred_element_type=F32)
        c = jnp.dot(xb, w3_ref[...], preferred_element_type=F32)
        hm = (_silu(a) * c).astype(BF16)
        acc_ref[...] += jnp.dot(hm, w2_ref[...], preferred_element_type=F32)

        @pl.when(f == pl.num_programs(1) - 1)
        def _():
            y_ref[...] = acc_ref[...]

    @pl.when((b >= na_ref[0]) & (f == pl.num_programs(1) - 1))
    def _():
        y_ref[...] = jnp.zeros(y_ref.shape, F32)


def gmm(xs, w1, w3, w2, tile_e, n_active):
    P, D = xs.shape
    G = MOE_TILE
    F = w1.shape[2]
    tf = MOE_FF_TILE
    nf = F // tf

    def row_idx(b, f, te, na):
        return (jnp.minimum(b, na[0] - 1), 0)

    def f_idx(b, f, na):
        return jnp.where(b < na[0], f, nf - 1)

    grid_spec = pltpu.PrefetchScalarGridSpec(
        num_scalar_prefetch=2,
        grid=(P // G, nf),
        in_specs=[
            pl.BlockSpec((G, D), row_idx),
            pl.BlockSpec((None, D, tf), lambda b, f, te, na: (te[b], 0, f_idx(b, f, na))),
            pl.BlockSpec((None, D, tf), lambda b, f, te, na: (te[b], 0, f_idx(b, f, na))),
            pl.BlockSpec((None, tf, D), lambda b, f, te, na: (te[b], f_idx(b, f, na), 0)),
        ],
        out_specs=pl.BlockSpec((G, D), lambda b, f, te, na: (b, 0)),
        scratch_shapes=[pltpu.VMEM((G, D), BF16), pltpu.VMEM((G, D), F32)],
    )
    return pl.pallas_call(
        _gmm_kernel,
        grid_spec=grid_spec,
        out_shape=jax.ShapeDtypeStruct((P, D), F32),
        compiler_params=_cparams(("arbitrary", "arbitrary")),
        name="moe_gmm",
    )(tile_e, n_active, xs, w1, w3, w2)


def _combine_ple_kernel(dest_ref, dnext_ref, h_ref, gate_ref, y_ref, g_ref, wg_ref, p_ref, wp_ref, gf_ref,
                        o_ref, yab_ref, sem, *, final_norm):
    i = pl.program_id(0)
    n = pl.num_programs(0)
    tm = h_ref.shape[0]
    slot = lax.rem(i, 2)

    def gather(d_ref, sl):
        def issue(r, carry):
            _row_copy(y_ref, d_ref[2 * r], yab_ref.at[sl, 0], r, sem.at[sl]).start(priority=0)
            _row_copy(y_ref, d_ref[2 * r + 1], yab_ref.at[sl, 1], r, sem.at[sl]).start(priority=1)
            return carry

        lax.fori_loop(0, tm, issue, 0, unroll=ROW_DMA_UNROLL)

    @pl.when(i == 0)
    def _():
        gather(dest_ref, 0)

    @pl.when(i + 1 < n)
    def _():
        gather(dnext_ref, 1 - slot)

    def drain(r, carry):
        _row_copy(y_ref, 0, yab_ref.at[slot, 0], 0, sem.at[slot]).wait()
        _row_copy(y_ref, 0, yab_ref.at[slot, 1], 0, sem.at[slot]).wait()
        return carry

    lax.fori_loop(0, tm, drain, 0, unroll=ROW_DMA_UNROLL)
    gt = gate_ref[...]
    h = h_ref[...] + gt[:, 0:1] * yab_ref[slot, 0] + gt[:, 1:2] * yab_ref[slot, 1]
    o_ref[...] = _ple_math(h, g_ref, wg_ref, p_ref, wp_ref, gf_ref, final_norm)


def combine_ple(h, gates, y, dest_flat, g, wg, p, wp, g_final, final_norm, layer):
    T, D = h.shape
    tm = min(ROW_TILE, T)
    n = T // tm
    return pl.pallas_call(
        functools.partial(_combine_ple_kernel, final_norm=final_norm),
        grid=(n,),
        in_specs=[
            pl.BlockSpec((2 * tm,), lambda i: (i,), memory_space=pltpu.SMEM),
            pl.BlockSpec((2 * tm,), lambda i: (jnp.minimum(i + 1, n - 1),), memory_space=pltpu.SMEM),
            pl.BlockSpec((tm, D), lambda i: (i, 0)),
            pl.BlockSpec((tm, LANES), lambda i: (i, 0)),
            pl.BlockSpec(memory_space=pl.ANY),
            pl.BlockSpec((1, D), lambda i: (0, 0)),
            pl.BlockSpec((None, D, D), lambda i: (layer, 0, 0)),
            pl.BlockSpec((None, tm, PLE_DIM), lambda i: (layer, i, 0)),
            pl.BlockSpec((None, PLE_DIM, D), lambda i: (layer, 0, 0)),
            pl.BlockSpec((1, D), lambda i: (0, 0)),
        ],
        out_specs=pl.BlockSpec((tm, D), lambda i: (i, 0)),
        out_shape=jax.ShapeDtypeStruct((T, D), F32),
        scratch_shapes=[pltpu.VMEM((2, 2, tm, D), F32), pltpu.SemaphoreType.DMA((2,))],
        compiler_params=_cparams(("arbitrary",)),
        name="moe_combine_ple",
    )(dest_flat, dest_flat, h, gates, y, g.reshape(1, D), wg, p, wp, g_final.reshape(1, D))


def _lambda_init(layer_idx):
    return 0.8 - 0.6 * math.exp(-0.3 * layer_idx)


def _even_layer(h, pos_col, pos_row, invf, ln_mix, w_in, w_out, lam_params, subln_g, conv_w, layer_idx, cast):
    qk, vt, gates_conv = even_proj(h, ln_mix, w_in.astype(BF16), pos_col, invf)
    o, narrowed = diff_attention(qk, vt, pos_col, pos_row, lam_params, subln_g, _lambda_init(layer_idx), cast)
    return even_out(o, gates_conv, conv_w, w_out.astype(BF16), h), narrowed


def _odd_mixer_route(h, ln_mix, w_in, conv_w, a_log, dt_bias, onorm_g, w_out, ln_ffn, w_router):
    main_w = 4 * DN_WIDTH
    w_main = w_in[:, :main_w].astype(BF16)
    w_ba = jnp.pad(w_in[:, main_w:], ((0, 0), (0, LANES - 2 * DN_HEADS))).astype(BF16)
    proj, ba = norm_matmul(h, ln_mix, w_main, w_side=w_ba)
    pad8 = lambda v: jnp.pad(v.astype(F32), (DN_HEADS, LANES - 2 * DN_HEADS)).reshape(1, LANES)
    qg, kg, w, u, ai, dl = gdn_prep(proj, ba, conv_w, pad8(a_log), pad8(dt_bias))
    o = gdn_scan(qg, kg, w, u, ai, dl)
    wr = jnp.pad(w_router, ((0, 0), (0, LANES - N_EXPERTS)))
    return odd_out_route(o, proj, onorm_g, w_out.astype(BF16), h, ln_ffn, wr)


def _moe(h, ln_ffn, ri, gates, cnt, w1, w3, w2):
    T = h.shape[0]
    G = MOE_TILE
    counts = cnt[0, :N_EXPERTS].astype(I32)
    padded = ((counts + G - 1) // G) * G
    pends = jnp.cumsum(padded)
    pstarts = pends - padded
    dest = (jnp.take(pstarts, ri[:, 0:2]) + ri[:, 2:4]).reshape(-1)
    n_tiles = (2 * T) // G + N_EXPERTS
    tile_start = jnp.arange(n_tiles, dtype=I32) * G
    tile_e = jnp.minimum(jnp.sum(pends[None, :] <= tile_start[:, None], axis=1), N_EXPERTS - 1).astype(I32)
    n_active = (pends[-1:] // G).astype(I32)
    tail = pends[-1] + jnp.arange(N_EXPERTS, dtype=I32) * G
    zero_tiles = jnp.concatenate([jnp.where(padded > 0, pends - G, -1),
                                  jnp.where(tail < n_tiles * G, tail, -1)]).astype(I32)
    xs = dispatch(h, ln_ffn, dest, zero_tiles, n_tiles * G)
    y = gmm(xs, w1, w3, w2, tile_e, n_active)
    return gates, y, dest


def kernel(x, p, positions, ln_mix, ln_ffn, ln_ple, ln_final, w_in_even, w_out_even, lam_q1, lam_k1, lam_q2, lam_k2, subln_gain, conv_w_short, w_in_odd, conv_w_qkv, a_log, dt_bias, onorm_gain, w_out_odd, w1_dense, w3_dense, w2_dense, w_router, w1_moe, w3_moe, w2_moe, w_ple_gate, w_ple_proj):
    B, S, D = x.shape
    T = B * S
    depth = p.shape[0]
    h = x.reshape(T, D)
    pos_col = positions.reshape(T, 1).astype(I32)
    pos_row = positions.reshape(1, T).astype(I32)
    inv_freq = ROPE_THETA ** (-jnp.arange(0, ROT_DIM, 2, dtype=F32) / ROT_DIM)
    invf = jnp.tile(inv_freq, LANES // (ROT_DIM // 2)).reshape(1, LANES)
    w1d, w3d, w2d = (w.astype(BF16) for w in (w1_dense, w3_dense, w2_dense))
    wpg, wpp = w_ple_gate.astype(BF16), w_ple_proj.astype(BF16)
    p_rows = p.reshape(depth, T, PLE_DIM)
    n_odd, n_exp, _, ff = w1_moe.shape
    moe_f32 = [w.reshape(n_odd, -1, w.shape[-1]) for w in (w1_moe, w3_moe, w2_moe)]
    moe_bf16 = None
    for i in range(depth):
        j = i // 2
        if i % 2 == 0:
            lam_params = jnp.stack([lam_q1[j], lam_k1[j], lam_q2[j], lam_k2[j]]).astype(F32)
            cast = tuple((w, j) for w in moe_f32) if i + 1 < depth else ()
            h, narrowed = _even_layer(h, pos_col, pos_row, invf, ln_mix[i], w_in_even[j], w_out_even[j], lam_params,
                                      subln_gain[j], conv_w_short[j], i, cast)
            if narrowed:
                moe_bf16 = (narrowed[0].reshape(n_exp, D, ff), narrowed[1].reshape(n_exp, D, ff),
                            narrowed[2].reshape(n_exp, ff, D))
            h = ffn_ple(h, ln_ffn[i], w1d, w3d, w2d, j, ln_ple[i], wpg, p_rows, wpp, ln_final,
                        final_norm=(i == depth - 1), ple_layer=i)
        else:
            h, ri, gates, cnt = _odd_mixer_route(h, ln_mix[i], w_in_odd[j], conv_w_qkv[j], a_log[j], dt_bias[j],
                                                 onorm_gain[j], w_out_odd[j], ln_ffn[i], w_router[j])
            gates, y, dest = _moe(h, ln_ffn[i], ri, gates, cnt, *moe_bf16)
            h = combine_ple(h, gates, y, dest, ln_ple[i], wpg, p_rows, wpp, ln_final,
                            final_norm=(i == depth - 1), layer=i)
    return h.reshape(B, S, D)
```

```python
import functools
import math

import jax
import jax.numpy as jnp
import numpy as np
from jax import lax
from jax.experimental import pallas as pl
from jax.experimental.pallas import tpu as pltpu

F32 = jnp.float32
BF16 = jnp.bfloat16
I32 = jnp.int32

D_MODEL = 1024
DEPTH = 4
RMS_EPS = 1e-6
DA_HEADS = 4
DA_HEAD_DIM = 64
DA_V_DIM = 2 * DA_HEAD_DIM
ROPE_THETA = 500000.0
ROT_DIM = DA_HEAD_DIM // 4
SUBLN_EPS = 1e-5
SC_WIDTH = 512
DN_HEADS = 8
DN_HEAD_DIM = 128
DN_WIDTH = DN_HEADS * DN_HEAD_DIM
DN_CHUNK = 64
D_FF = 3584
N_EXPERTS = 8
PLE_DIM = 256

LANES = 128
SUBLANES = 8
HALO = 16
VMEM_LIMIT = 52 * 1024 * 1024

NEG_BIG = -1e30

ROW_TILE = 512
PROJ_COL_TILE = 1024
FFN_ROW_TILE = 512
FF_TILE = 1792
MOE_FF_TILE = 1792
ATT_TQ = 1024
ATT_TK = 1024
ATT_EXP_ROWS = 128
ATT_TQ_SPLIT = 256
GDN_TILE = 1024
GDN_PREP_HEADS = 2
GDN_SCAN_HEADS = 8
MOE_TILE = 512
ROW_DMA_UNROLL = 8


def _cparams(sem):
    return pltpu.CompilerParams(dimension_semantics=sem, vmem_limit_bytes=VMEM_LIMIT)


def _rms(x, g, eps):
    ms = jnp.mean(x * x, axis=-1, keepdims=True)
    return x * lax.rsqrt(ms + eps) * g


def _silu(x):
    return x * (1.0 / (1.0 + jnp.exp(-x)))


def _sigmoid(x):
    return 1.0 / (1.0 + jnp.exp(-x))


def _norm_matmul_kernel(x_ref, g_ref, w_ref, *rest):
    xn = _rms(x_ref[...], g_ref[...], RMS_EPS).astype(BF16)
    if len(rest) == 1:
        (o_ref,) = rest
    else:
        ws_ref, o_ref, os_ref = rest
        os_ref[...] = jnp.dot(xn, ws_ref[...], preferred_element_type=F32)
    tn = PROJ_COL_TILE
    for c in range(o_ref.shape[1] // tn):
        o_ref[:, c * tn:(c + 1) * tn] = jnp.dot(
            xn, w_ref[:, c * tn:(c + 1) * tn], preferred_element_type=F32).astype(o_ref.dtype)


def norm_matmul(x, g, w, w_side=None, out_dtype=BF16):
    T, D = x.shape
    N = w.shape[1]
    tm = min(ROW_TILE, T)
    in_specs = [
        pl.BlockSpec((tm, D), lambda i: (i, 0)),
        pl.BlockSpec((1, D), lambda i: (0, 0)),
        pl.BlockSpec((D, N), lambda i: (0, 0)),
    ]
    out_specs = [pl.BlockSpec((tm, N), lambda i: (i, 0))]
    out_shape = [jax.ShapeDtypeStruct((T, N), out_dtype)]
    args = [x, g.reshape(1, D), w]
    if w_side is not None:
        ns = w_side.shape[1]
        in_specs.append(pl.BlockSpec((D, ns), lambda i: (0, 0)))
        out_specs.append(pl.BlockSpec((tm, ns), lambda i: (i, 0)))
        out_shape.append(jax.ShapeDtypeStruct((T, ns), F32))
        args.append(w_side)
    return pl.pallas_call(
        _norm_matmul_kernel,
        grid=(T // tm,),
        in_specs=in_specs,
        out_specs=out_specs,
        out_shape=out_shape,
        compiler_params=_cparams(("parallel",)),
        name="norm_matmul",
    )(*args)


def _even_proj_kernel(x_ref, g_ref, w_ref, pos_ref, invf_ref, qk_ref, vt_ref, rest_ref):
    xn = _rms(x_ref[...], g_ref[...], RMS_EPS).astype(BF16)
    WQK = qk_ref.shape[1]
    WV = vt_ref.shape[0]

    def proj(c0, c1):
        return jnp.dot(xn, w_ref[:, c0:c1], preferred_element_type=F32)

    pos = pos_ref[...].astype(F32)
    ang = pos * invf_ref[...]
    cos_t = jnp.cos(ang)
    sin_t = jnp.sin(ang)
    d = lax.broadcasted_iota(I32, ang.shape, 1) & (DA_HEAD_DIM - 1)
    half = ROT_DIM // 2
    c_mul = jnp.where(d < ROT_DIM, cos_t, 1.0)
    s_mul = jnp.where(d < half, -sin_t, jnp.where(d < ROT_DIM, sin_t, 0.0))
    qk = proj(0, WQK)
    n_q = WQK // (2 * LANES)
    for c in range(WQK // LANES):
        x = qk[:, c * LANES:(c + 1) * LANES]
        swapped = jnp.where(d < half, pltpu.roll(x, LANES - half, 1), pltpu.roll(x, half, 1))
        r = x * c_mul + swapped * s_mul
        if c < n_q:
            r = r * (DA_HEAD_DIM ** -0.5 * math.log2(math.e))
        qk_ref[:, c * LANES:(c + 1) * LANES] = r.astype(BF16)
    v = proj(WQK, WQK + WV)
    for hd in range(DA_HEADS):
        vt_ref[hd * DA_V_DIM:(hd + 1) * DA_V_DIM, :] = v[:, hd * DA_V_DIM:(hd + 1) * DA_V_DIM].T.astype(BF16)
    WR = rest_ref.shape[1]
    for c0 in range(0, WR, PROJ_COL_TILE):
        c1 = min(c0 + PROJ_COL_TILE, WR)
        rest_ref[:, c0:c1] = proj(WQK + WV + c0, WQK + WV + c1).astype(BF16)


def even_proj(x, g, w, positions_col, invf):
    T, D = x.shape
    WQK = 2 * 2 * DA_HEADS * DA_HEAD_DIM
    WV = DA_HEADS * DA_V_DIM
    WR = w.shape[1] - WQK - WV
    tm = min(ROW_TILE, T)
    return pl.pallas_call(
        _even_proj_kernel,
        grid=(T // tm,),
        in_specs=[
            pl.BlockSpec((tm, D), lambda i: (i, 0)),
            pl.BlockSpec((1, D), lambda i: (0, 0)),
            pl.BlockSpec(w.shape, lambda i: (0, 0)),
            pl.BlockSpec((tm, 1), lambda i: (i, 0)),
            pl.BlockSpec((1, LANES), lambda i: (0, 0)),
        ],
        out_specs=[pl.BlockSpec((tm, WQK), lambda i: (i, 0)),
                   pl.BlockSpec((WV, tm), lambda i: (0, i)),
                   pl.BlockSpec((tm, WR), lambda i: (i, 0))],
        out_shape=[jax.ShapeDtypeStruct((T, WQK), BF16), jax.ShapeDtypeStruct((WV, T), BF16),
                   jax.ShapeDtypeStruct((T, WR), BF16)],
        compiler_params=_cparams(("parallel",)),
        name="even_proj",
    )(x, g.reshape(1, D), w, positions_col, invf)


def _attn_kernel(qt_ref, kt_ref, q_ref, k_ref, v_ref, pq_ref, pk_ref, lam_ref, g_ref, *rest, tq, tk, lam0, n_cast):
    cast_in = rest[:n_cast]
    o_ref = rest[n_cast]
    cast_out = rest[n_cast + 1:2 * n_cast + 1]
    m_ref, l_ref, acc_ref = rest[2 * n_cast + 1:]
    for wi_ref, wo_ref in zip(cast_in, cast_out):
        wo_ref[...] = wi_ref[...].astype(BF16)

    p = pl.program_id(1)
    qi = qt_ref[p]
    ki = kt_ref[p]

    @pl.when(ki == 0)
    def _():
        m_ref[...] = jnp.full(m_ref.shape, NEG_BIG, F32)
        l_ref[...] = jnp.zeros(l_ref.shape, F32)
        acc_ref[...] = jnp.zeros(acc_ref.shape, F32)

    tqs = min(tq, ATT_TQ_SPLIT)
    items = [(s, j) for s in range(2) for j in range(tq // tqs)]

    def step(masked):
        q = q_ref[...]
        k = k_ref[...]
        v = v_ref[...]
        def nkeys(j):
            return (j + 1) * tqs if (masked and tq == tk) else tk

        st = {}
        for s, j in items:
            cols = slice(j * tqs, (j + 1) * tqs)
            qs = q[j * tqs:(j + 1) * tqs, s * DA_HEAD_DIM:(s + 1) * DA_HEAD_DIM]
            ks = k[:nkeys(j), s * DA_HEAD_DIM:(s + 1) * DA_HEAD_DIM]
            sc = lax.dot_general(ks, qs, (((1,), (1,)), ((), ())), preferred_element_type=F32)
            if masked:
                sc = jnp.where(pk_ref[0:nkeys(j), :] <= pq_ref[:, cols], sc, NEG_BIG)
            st[s, j] = sc
        for s, j in items:
            cols = slice(j * tqs, (j + 1) * tqs)
            sc = st[s, j]
            nk = nkeys(j)
            m_prev = m_ref[s, :, cols]
            m_new = jnp.maximum(m_prev, jnp.max(sc, axis=0, keepdims=True))
            alpha = jnp.exp2(m_prev - m_new)
            psum = jnp.zeros((1, tqs), F32)
            pchunks = []
            for r0 in range(0, nk, ATT_EXP_ROWS):
                pc = jnp.exp2(sc[r0:r0 + ATT_EXP_ROWS] - m_new)
                psum = psum + jnp.sum(pc, axis=0, keepdims=True)
                pchunks.append(pc.astype(BF16))
            pb = jnp.concatenate(pchunks, axis=0)
            l_ref[s, :, cols] = alpha * l_ref[s, :, cols] + psum
            pv = jnp.dot(v[:, :nk], pb, preferred_element_type=F32)
            acc_ref[s, :, cols] = alpha * acc_ref[s, :, cols] + pv
            m_ref[s, :, cols] = m_new

    crosses = (ki + 1) * tk - 1 > qi * tq

    @pl.when(crosses)
    def _():
        step(True)

    @pl.when(jnp.logical_not(crosses))
    def _():
        step(False)

    @pl.when((ki + 1) * tk >= (qi + 1) * tq)
    def _():
        lm = lam_ref[...]
        s1 = jnp.sum(lm[0:1] * lm[1:2], axis=-1, keepdims=True)
        s2 = jnp.sum(lm[2:3] * lm[3:4], axis=-1, keepdims=True)
        lam = jnp.exp(s1) - jnp.exp(s2) + lam0
        ot = acc_ref[0] * (1.0 / l_ref[0]) - lam * (acc_ref[1] * (1.0 / l_ref[1]))
        ms = jnp.mean(ot * ot, axis=0, keepdims=True)
        ot = ot * lax.rsqrt(ms + SUBLN_EPS) * g_ref[...] * (1.0 - lam0)
        o_ref[...] = ot.T.astype(o_ref.dtype)


def _cast_blocks(rows, steps):
    for nb in range(min(steps, rows), 0, -1):
        if rows % nb == 0 and (rows // nb) % HALO == 0:
            return nb
    raise ValueError((rows, steps))


def diff_attention(qk, vt, pos_col, pos_row, lam_params, subln_g, lam0, cast=()):
    T = qk.shape[0]
    tq = min(ATT_TQ, T)
    tk = min(ATT_TK, T)
    nq = T // tq
    pairs = [(qi, ki) for qi in range(nq) for ki in range(-(-((qi + 1) * tq) // tk))]
    qt = jnp.asarray(np.array([a for a, _ in pairs], np.int32))
    kt = jnp.asarray(np.array([b for _, b in pairs], np.int32))
    H = DA_HEADS
    npairs = len(pairs)
    kern = functools.partial(_attn_kernel, tq=tq, tk=tk, lam0=lam0, n_cast=len(cast))
    cast_in_specs, cast_out_specs, cast_out_shapes = [], [], []
    for w, layer in cast:
        _, rows, cols = w.shape
        nb = _cast_blocks(rows, H * npairs)
        br = rows // nb

        def blk(h, p, nb=nb):
            return jnp.minimum(h * npairs + p, nb - 1)

        cast_in_specs.append(pl.BlockSpec((None, br, cols), lambda h, p, qt, kt, layer=layer, blk=blk: (layer, blk(h, p), 0)))
        cast_out_specs.append(pl.BlockSpec((br, cols), lambda h, p, qt, kt, blk=blk: (blk(h, p), 0)))
        cast_out_shapes.append(jax.ShapeDtypeStruct((rows, cols), BF16))
    grid_spec = pltpu.PrefetchScalarGridSpec(
        num_scalar_prefetch=2,
        grid=(H, npairs),
        in_specs=[
            pl.BlockSpec((tq, LANES), lambda h, p, qt, kt: (qt[p], h)),
            pl.BlockSpec((tk, LANES), lambda h, p, qt, kt: (kt[p], H + h)),
            pl.BlockSpec((DA_V_DIM, tk), lambda h, p, qt, kt: (h, kt[p])),
            pl.BlockSpec((1, tq), lambda h, p, qt, kt: (0, qt[p])),
            pl.BlockSpec((tk, 1), lambda h, p, qt, kt: (kt[p], 0)),
            pl.BlockSpec((4, DA_HEAD_DIM), lambda h, p, qt, kt: (0, 0)),
            pl.BlockSpec((DA_V_DIM, 1), lambda h, p, qt, kt: (0, 0)),
        ] + cast_in_specs,
        out_specs=[pl.BlockSpec((tq, DA_V_DIM), lambda h, p, qt, kt: (qt[p], h))] + cast_out_specs,
        scratch_shapes=[
            pltpu.VMEM((2, 1, tq), F32),
            pltpu.VMEM((2, 1, tq), F32),
            pltpu.VMEM((2, DA_V_DIM, tq), F32),
        ],
    )
    outs = pl.pallas_call(
        kern,
        grid_spec=grid_spec,
        out_shape=[jax.ShapeDtypeStruct((T, H * DA_V_DIM), BF16)] + cast_out_shapes,
        compiler_params=_cparams(("arbitrary", "arbitrary")),
        name="diff_attention",
    )(qt, kt, qk, qk, vt, pos_row, pos_col, lam_params, subln_g.reshape(DA_V_DIM, 1), *[w for w, _ in cast])
    return outs[0], outs[1:]


def _even_out_kernel(o_ref, b_ref, c_ref, x_ref, ch_ref, xh_ref, cw_ref, w_ref, h_ref, out_ref, u_ref):
    tm = o_ref.shape[0]
    u_prev = ch_ref[...].astype(F32) * xh_ref[...].astype(F32)
    u_prev = jnp.where(pl.program_id(0) == 0, 0.0, u_prev)
    u_ref[0:HALO, :] = u_prev
    u_ref[HALO:, :] = c_ref[...].astype(F32) * x_ref[...].astype(F32)
    cw = cw_ref[...]
    conv = (u_ref[HALO - 2:HALO - 2 + tm, :] * cw[0:1]
            + u_ref[HALO - 1:HALO - 1 + tm, :] * cw[1:2]
            + u_ref[HALO:, :] * cw[2:3])
    sc = (b_ref[...].astype(F32) * conv).astype(BF16)
    na = o_ref.shape[1]
    acc = jnp.dot(o_ref[...], w_ref[0:na, :], preferred_element_type=F32)
    acc = acc + jnp.dot(sc, w_ref[na:, :], preferred_element_type=F32)
    out_ref[...] = h_ref[...] + acc


def even_out(attn_o, proj, conv_w, w_out, h):
    T = h.shape[0]
    tm = min(ROW_TILE, T)
    W = SC_WIDTH
    cb = (proj.shape[1] - 3 * W) // W
    hb = tm // HALO
    halo = lambda col: pl.BlockSpec((HALO, W), lambda i: (jnp.maximum(i * hb - 1, 0), col))
    return pl.pallas_call(
        _even_out_kernel,
        grid=(T // tm,),
        in_specs=[
            pl.BlockSpec((tm, attn_o.shape[1]), lambda i: (i, 0)),
            pl.BlockSpec((tm, W), lambda i: (i, cb)),
            pl.BlockSpec((tm, W), lambda i: (i, cb + 1)),
            pl.BlockSpec((tm, W), lambda i: (i, cb + 2)),
            halo(cb + 1),
            halo(cb + 2),
            pl.BlockSpec(conv_w.shape, lambda i: (0, 0)),
            pl.BlockSpec(w_out.shape, lambda i: (0, 0)),
            pl.BlockSpec((tm, D_MODEL), lambda i: (i, 0)),
        ],
        out_specs=pl.BlockSpec((tm, D_MODEL), lambda i: (i, 0)),
        out_shape=jax.ShapeDtypeStruct((T, D_MODEL), F32),
        scratch_shapes=[pltpu.VMEM((tm + HALO, W), F32)],
        compiler_params=_cparams(("parallel",)),
        name="even_out",
    )(attn_o, proj, proj, proj, proj, proj, conv_w, w_out, h)


def _ffn_kernel(h_ref, g_ref, w1_ref, w3_ref, w2_ref, gp_ref, wg_ref, p_ref, wp_ref, gf_ref, o_ref, xn_ref, acc_ref,
                *, final_norm):
    f = pl.program_id(1)

    @pl.when(f == 0)
    def _():
        xn_ref[...] = _rms(h_ref[...], g_ref[...], RMS_EPS).astype(BF16)
        acc_ref[...] = jnp.zeros(acc_ref.shape, F32)

    xn = xn_ref[...]
    a = jnp.dot(xn, w1_ref[...], preferred_element_type=F32)
    b = jnp.dot(xn, w3_ref[...], preferred_element_type=F32)
    hm = (_silu(a) * b).astype(BF16)
    acc_ref[...] += jnp.dot(hm, w2_ref[...], preferred_element_type=F32)

    @pl.when(f == pl.num_programs(1) - 1)
    def _():
        o_ref[...] = _ple_math(h_ref[...] + acc_ref[...], gp_ref, wg_ref, p_ref, wp_ref, gf_ref, final_norm)


def ffn_ple(h, g, w1, w3, w2, layer, g_ple, wg, p, wp, g_final, final_norm, ple_layer):
    T, D = h.shape
    F = w1.shape[2]
    tm = min(FFN_ROW_TILE, T)
    tf = FF_TILE
    return pl.pallas_call(
        functools.partial(_ffn_kernel, final_norm=final_norm),
        grid=(T // tm, F // tf),
        in_specs=[
            pl.BlockSpec((tm, D), lambda i, f: (i, 0)),
            pl.BlockSpec((1, D), lambda i, f: (0, 0)),
            pl.BlockSpec((None, D, tf), lambda i, f: (layer, 0, f)),
            pl.BlockSpec((None, D, tf), lambda i, f: (layer, 0, f)),
            pl.BlockSpec((None, tf, D), lambda i, f: (layer, f, 0)),
            pl.BlockSpec((1, D), lambda i, f: (0, 0)),
            pl.BlockSpec((None, D, D), lambda i, f: (ple_layer, 0, 0)),
            pl.BlockSpec((None, tm, PLE_DIM), lambda i, f: (ple_layer, i, 0)),
            pl.BlockSpec((None, PLE_DIM, D), lambda i, f: (ple_layer, 0, 0)),
            pl.BlockSpec((1, D), lambda i, f: (0, 0)),
        ],
        out_specs=pl.BlockSpec((tm, D), lambda i, f: (i, 0)),
        out_shape=jax.ShapeDtypeStruct((T, D), F32),
        scratch_shapes=[pltpu.VMEM((tm, D), BF16), pltpu.VMEM((tm, D), F32)],
        compiler_params=_cparams(("parallel", "arbitrary")),
        name="ffn_ple",
    )(h, g.reshape(1, D), w1, w3, w2, g_ple.reshape(1, D), wg, p, wp, g_final.reshape(1, D))


def _ple_math(h, g_ref, wg_ref, p_ref, wp_ref, gf_ref, final_norm):
    xn = _rms(h, g_ref[...], RMS_EPS).astype(BF16)
    gate = _sigmoid(jnp.dot(xn, wg_ref[...], preferred_element_type=F32))
    emb = jnp.dot(p_ref[...].astype(BF16), wp_ref[...], preferred_element_type=F32)
    out = h + gate * emb
    if final_norm:
        out = _rms(out, gf_ref[...], RMS_EPS)
    return out


def _dot_hi(a, b):
    return jnp.dot(a, b, preferred_element_type=F32, precision=lax.Precision.HIGHEST)


def _gdn_prep_kernel(q_ref, k_ref, v_ref, qh_ref, kh_ref, vh_ref, ba_ref, cw_ref, alog_ref, dtb_ref,
                     qg_ref, kg_ref, w_ref, u_ref, ai_ref, dl_ref,
                     xs_ref, gc_ref, gl_ref, sg_ref, gct_ref):
    i = pl.program_id(0)
    h = pl.program_id(1)
    R = q_ref.shape[0]
    C = DN_CHUNK
    KC = cw_ref.shape[1]

    def conv_silu(x_ref, halo_ref, j, hp):
        cs = slice(hp * LANES, (hp + 1) * LANES)
        buf = xs_ref.at[3 * hp + j]
        buf[0:HALO, :] = jnp.where(i == 0, 0.0, halo_ref[:, cs].astype(F32))
        buf[HALO:, :] = x_ref[:, cs].astype(F32)
        acc = buf[HALO:, :] * cw_ref[hp, KC - 1:KC, j * LANES:(j + 1) * LANES]
        for t in range(1, KC):
            acc = acc + (buf[HALO - t:HALO - t + R, :]
                         * cw_ref[hp, KC - 1 - t:KC - t, j * LANES:(j + 1) * LANES])
        return _silu(acc)

    def l2n(x):
        return x * lax.rsqrt(jnp.sum(x * x, axis=-1, keepdims=True) + 1e-6)

    chunks = range(R // C)
    rows = [slice(c * C, (c + 1) * C) for c in chunks]
    r64 = lax.broadcasted_iota(I32, (C, C), 0)
    c64 = lax.broadcasted_iota(I32, (C, C), 1)
    incl = r64 >= c64
    strict = r64 > c64

    @pl.when(h == 0)
    def _():
        ba = ba_ref[...]
        sp_in = ba + dtb_ref[...]
        softplus = jnp.maximum(sp_in, 0.0) + jnp.log(1.0 + jnp.exp(-jnp.abs(sp_in)))
        g_all = -jnp.exp(alog_ref[...]) * softplus
        g_wide = jnp.concatenate([g_all[r] for r in rows], axis=1)
        gc_wide = _dot_hi(jnp.where(incl, 1.0, 0.0), g_wide)
        for c in chunks:
            gc_c = gc_wide[:, c * LANES:(c + 1) * LANES]
            gc_ref[rows[c], :] = gc_c
            gl_ref[rows[c], :] = jnp.broadcast_to(gc_c[C - 1:C, :], (C, LANES))
        sg_ref[...] = _sigmoid(ba)
        gct_ref[...] = gc_ref[...].T

    lane = lax.broadcasted_iota(I32, (R, LANES), 1)

    def pick(ref, l):
        return jnp.sum(jnp.where(lane == l, ref[...], 0.0), axis=-1, keepdims=True)

    def bdot(a, b):
        return jnp.dot(a.astype(BF16), b.astype(BF16), preferred_element_type=F32)

    heads = range(qg_ref.shape[1] // LANES)
    items = [(hp, c) for hp in heads for c in chunks]
    decay, kq, rhs, edl = {}, {}, {}, {}
    for hp in heads:
        hh = h * len(heads) + hp
        cs = slice(hp * LANES, (hp + 1) * LANES)
        q = l2n(conv_silu(q_ref, qh_ref, 0, hp)) * (DN_HEAD_DIM ** -0.5)
        k = l2n(conv_silu(k_ref, kh_ref, 1, hp))
        v = conv_silu(v_ref, vh_ref, 2, hp)
        beta = pick(sg_ref, hh)
        gcol = pick(gc_ref, DN_HEADS + hh)
        glast = pick(gl_ref, DN_HEADS + hh)
        grow = gct_ref[pl.ds(DN_HEADS + hh, 1), :]
        eg = jnp.exp(gcol)
        kb = k * beta
        qg_ref[:, cs] = (q * eg).astype(BF16)
        kg_ref[:, cs] = (k * jnp.exp(glast - gcol)).astype(BF16)
        edl[hp] = jnp.broadcast_to(jnp.exp(glast), (R, LANES))
        rhs[hp] = jnp.concatenate([v * beta, kb * eg], axis=-1)
        kbf = k.astype(BF16)
        lhs = jnp.concatenate([kb.astype(BF16).reshape(R // C, C, LANES),
                               q.astype(BF16).reshape(R // C, C, LANES)], axis=1)
        for c in chunks:
            r = rows[c]
            decay[hp, c] = jnp.where(incl, jnp.exp(jnp.where(incl, gcol[r] - grow[:, r], 0.0)), 0.0)
            kq[hp, c] = lax.dot_general(lhs[c], kbf[r], (((1,), (1,)), ((), ())), preferred_element_type=F32)
    pw = {it: -jnp.where(strict, kq[it][:C] * decay[it], 0.0) for it in items}
    n = dict(pw)
    for _ in range(5):
        pw = {it: bdot(pw[it], pw[it]) for it in items}
        n = {it: n[it] + pw[it] + bdot(n[it], pw[it]) for it in items}
    for hp, c in items:
        cs = slice(hp * LANES, (hp + 1) * LANES)
        rc = rhs[hp][rows[c]]
        uw = rc + bdot(n[hp, c], rc)
        u_ref[rows[c], cs] = uw[:, :DN_HEAD_DIM]
        w_ref[rows[c], cs] = uw[:, DN_HEAD_DIM:].astype(BF16)
        ai_ref[hp, rows[c], :] = (kq[hp, c][C:] * decay[hp, c]).astype(BF16)
        dl_ref[hp, c:c + 1, :] = edl[hp][c * C:c * C + 1, :]


def gdn_prep(proj, ba, conv_w, alog_l, dtb_l):
    T = proj.shape[0]
    R = min(GDN_TILE, T)
    H = DN_HEADS
    nchunk = T // DN_CHUNK
    hb = R // HALO
    HP = GDN_PREP_HEADS
    W = HP * LANES
    col = lambda j: pl.BlockSpec((R, W), lambda i, h: (i, j * (H // HP) + h))
    halo = lambda j: pl.BlockSpec((HALO, W), lambda i, h: (jnp.maximum(i * hb - 1, 0), j * (H // HP) + h))
    KC = conv_w.shape[0]
    cw = conv_w.reshape(KC, 3, H, LANES).transpose(2, 0, 1, 3).reshape(H, KC, 3 * LANES)
    cw_spec = pl.BlockSpec((HP, KC, 3 * LANES), lambda i, h: (h, 0, 0))
    row_out = lambda dt: jax.ShapeDtypeStruct((T, DN_WIDTH), dt)
    out_col = pl.BlockSpec((R, W), lambda i, h: (i, h))
    return pl.pallas_call(
        _gdn_prep_kernel,
        grid=(T // R, H // HP),
        in_specs=[col(0), col(1), col(2), halo(0), halo(1), halo(2),
                  pl.BlockSpec((R, LANES), lambda i, h: (i, 0)),
                  cw_spec,
                  pl.BlockSpec((1, LANES), lambda i, h: (0, 0)),
                  pl.BlockSpec((1, LANES), lambda i, h: (0, 0))],
        out_specs=[out_col, out_col, out_col, out_col,
                   pl.BlockSpec((HP, R, DN_CHUNK), lambda i, h: (h, i, 0)),
                   pl.BlockSpec((HP, R // DN_CHUNK, LANES), lambda i, h: (h, i, 0))],
        out_shape=[row_out(BF16), row_out(BF16), row_out(BF16), row_out(F32),
                   jax.ShapeDtypeStruct((H, T, DN_CHUNK), BF16),
                   jax.ShapeDtypeStruct((H, nchunk, LANES), F32)],
        scratch_shapes=[pltpu.VMEM((3 * HP, R + HALO, LANES), F32),
                        pltpu.VMEM((R, LANES), F32), pltpu.VMEM((R, LANES), F32), pltpu.VMEM((R, LANES), F32),
                        pltpu.VMEM((LANES, R), F32)],
        compiler_params=_cparams(("parallel", "arbitrary")),
        name="gdn_prep",
    )(proj, proj, proj, proj, proj, proj, ba, cw, alog_l, dtb_l)


def _gdn_scan_kernel(qg_ref, kg_ref, w_ref, u_ref, ai_ref, dl_ref, o_ref, s_ref):
    @pl.when(pl.program_id(1) == 0)
    def _():
        s_ref[...] = jnp.zeros(s_ref.shape, F32)

    C = DN_CHUNK
    D = DN_HEAD_DIM
    heads = range(s_ref.shape[0])

    def chunk(c, carry):
        rows = pl.ds(pl.multiple_of(c * C, C), C)
        s = [s_ref[hh] for hh in heads]
        sb = [x.astype(BF16) for x in s]
        cols = [slice(hh * D, (hh + 1) * D) for hh in heads]
        r1 = [jnp.dot(jnp.concatenate([w_ref[rows, cols[hh]], qg_ref[rows, cols[hh]]], axis=0), sb[hh],
                      preferred_element_type=F32) for hh in heads]
        vb = [(u_ref[rows, cols[hh]] - r1[hh][:C]).astype(BF16) for hh in heads]
        for hh in heads:
            o_ref[rows, cols[hh]] = r1[hh][C:] + jnp.dot(ai_ref[hh, rows, :], vb[hh], preferred_element_type=F32)
        upd = [lax.dot_general(kg_ref[rows, cols[hh]], vb[hh], (((0,), (0,)), ((), ())),
                               preferred_element_type=F32) for hh in heads]
        for hh in heads:
            s_ref[hh] = s[hh] * dl_ref[hh, pl.ds(c, 1), :] + upd[hh]
        return carry

    lax.fori_loop(0, qg_ref.shape[0] // C, chunk, 0)


def gdn_scan(qg, kg, w, u, ai, dl):
    T = qg.shape[0]
    R = min(GDN_TILE, T)
    H = DN_HEADS
    HB = GDN_SCAN_HEADS
    col = pl.BlockSpec((R, HB * LANES), lambda h, i: (i, h))
    return pl.pallas_call(
        _gdn_scan_kernel,
        grid=(H // HB, T // R),
        in_specs=[col, col, col, col,
                  pl.BlockSpec((HB, R, DN_CHUNK), lambda h, i: (h, i, 0)),
                  pl.BlockSpec((HB, R // DN_CHUNK, LANES), lambda h, i: (h, i, 0))],
        out_specs=col,
        out_shape=jax.ShapeDtypeStruct((T, DN_WIDTH), F32),
        scratch_shapes=[pltpu.VMEM((HB, DN_HEAD_DIM, DN_HEAD_DIM), F32)],
        compiler_params=_cparams(("parallel", "arbitrary")),
        name="gdn_scan",
    )(qg, kg, w, u, ai, dl)


def _odd_out_math(o_ref, z_ref, g_ref, w_ref, h_ref):
    g = g_ref[...]
    parts = []
    for hd in range(DN_HEADS):
        sl = slice(hd * DN_HEAD_DIM, (hd + 1) * DN_HEAD_DIM)
        parts.append((_rms(o_ref[:, sl], g, RMS_EPS) * _silu(z_ref[:, sl].astype(F32))).astype(BF16))
    y = jnp.concatenate(parts, axis=-1)
    return h_ref[...] + jnp.dot(y, w_ref[...], preferred_element_type=F32)


def _route_math(h, g_ref, wr_ref, ri_ref, gate_ref, cnt_ref, carry_ref):
    i = pl.program_id(0)

    @pl.when(i == 0)
    def _():
        carry_ref[...] = jnp.zeros(carry_ref.shape, F32)

    xn = _rms(h, g_ref[...], RMS_EPS)
    logits = jnp.dot(xn.astype(BF16), wr_ref[...].astype(BF16), preferred_element_type=F32)
    tm = logits.shape[0]
    lane = lax.broadcasted_iota(I32, logits.shape, 1)
    logits = jnp.where(lane < N_EXPERTS, logits, NEG_BIG)
    lane_f = lane.astype(F32)
    m1 = jnp.max(logits, axis=-1, keepdims=True)
    i1 = jnp.min(jnp.where(logits == m1, lane_f, float(LANES)), axis=-1, keepdims=True)
    rest = jnp.where(lane_f == i1, NEG_BIG, logits)
    m2 = jnp.max(rest, axis=-1, keepdims=True)
    i2 = jnp.min(jnp.where(rest == m2, lane_f, float(LANES)), axis=-1, keepdims=True)
    e = jnp.exp(m2 - m1)
    g1 = 1.0 / (1.0 + e)
    g2 = e / (1.0 + e)
    oh1 = lane_f == i1
    oh2 = lane_f == i2
    i1 = i1.astype(I32)
    i2 = i2.astype(I32)
    oh = jnp.where(oh1 | oh2, 1.0, 0.0)
    ri_ = lax.broadcasted_iota(I32, (tm, tm), 0)
    ci_ = lax.broadcasted_iota(I32, (tm, tm), 1)
    below = jnp.where(ri_ > ci_, 1.0, 0.0).astype(BF16)
    ex = jnp.dot(below, oh.astype(BF16), preferred_element_type=F32) + carry_ref[0:1, :]
    r1 = jnp.sum(jnp.where(oh1, ex, 0.0), axis=-1, keepdims=True).astype(I32)
    r2 = jnp.sum(jnp.where(oh2, ex, 0.0), axis=-1, keepdims=True).astype(I32)
    packed = jnp.where(lane == 0, i1, jnp.where(lane == 1, i2, jnp.where(lane == 2, r1, r2))).astype(F32)
    pick_rows = jnp.where(lax.broadcasted_iota(I32, (SUBLANES, LANES), 0)
                          == lax.broadcasted_iota(I32, (SUBLANES, LANES), 1), 1.0, 0.0)
    ri_ref[...] = lax.dot_general(pick_rows, packed, (((1,), (1,)), ((), ())), preferred_element_type=F32,
                                  precision=lax.Precision.HIGHEST).astype(I32)
    gate_ref[...] = jnp.where(lane == 0, g1, g2)
    carry_ref[...] = carry_ref[...] + jnp.sum(oh, axis=0, keepdims=True)
    cnt_ref[...] = carry_ref[...]


def _odd_out_route_kernel(o_ref, z_ref, g_ref, w_ref, h_ref, gr_ref, wr_ref,
                          out_ref, ri_ref, gate_ref, cnt_ref, carry_ref):
    h = _odd_out_math(o_ref, z_ref, g_ref, w_ref, h_ref)
    out_ref[...] = h
    _route_math(h, gr_ref, wr_ref, ri_ref, gate_ref, cnt_ref, carry_ref)


def odd_out_route(o, proj, onorm_g, w_out, h, g_route, wr):
    T, D = h.shape
    tm = min(ROW_TILE, T)
    zb = 3 * DN_WIDTH // DN_WIDTH
    return pl.pallas_call(
        _odd_out_route_kernel,
        grid=(T // tm,),
        in_specs=[
            pl.BlockSpec((tm, DN_WIDTH), lambda i: (i, 0)),
            pl.BlockSpec((tm, DN_WIDTH), lambda i: (i, zb)),
            pl.BlockSpec((1, DN_HEAD_DIM), lambda i: (0, 0)),
            pl.BlockSpec(w_out.shape, lambda i: (0, 0)),
            pl.BlockSpec((tm, D), lambda i: (i, 0)),
            pl.BlockSpec((1, D), lambda i: (0, 0)),
            pl.BlockSpec((D, LANES), lambda i: (0, 0)),
        ],
        out_specs=[
            pl.BlockSpec((tm, D), lambda i: (i, 0)),
            pl.BlockSpec((SUBLANES, tm), lambda i: (0, i)),
            pl.BlockSpec((tm, LANES), lambda i: (i, 0)),
            pl.BlockSpec((SUBLANES, LANES), lambda i: (0, 0)),
        ],
        out_shape=[
            jax.ShapeDtypeStruct((T, D), F32),
            jax.ShapeDtypeStruct((SUBLANES, T), I32),
            jax.ShapeDtypeStruct((T, LANES), F32),
            jax.ShapeDtypeStruct((SUBLANES, LANES), F32),
        ],
        scratch_shapes=[pltpu.VMEM((SUBLANES, LANES), F32)],
        compiler_params=_cparams(("arbitrary",)),
        name="odd_out_route",
    )(o, proj, onorm_g.reshape(1, DN_HEAD_DIM), w_out, h, g_route.reshape(1, D), wr)


def _row_copy(src_ref, s, dst_ref, d, sem):
    return pltpu.make_async_copy(src_ref.at[pl.ds(s, 1), :], dst_ref.at[pl.ds(d, 1), :], sem)


def _dispatch_kernel(pz_ref, dest_ref, h_ref, g_ref, xs_ref, xn_ref, zb_ref, sem, zsem):
    tm = h_ref.shape[0]
    G = zb_ref.shape[0]

    @pl.when(pl.program_id(0) == 0)
    def _():
        zb_ref[...] = jnp.zeros(zb_ref.shape, F32)

        def zero_copy(e):
            return pltpu.make_async_copy(zb_ref, xs_ref.at[pl.ds(pl.multiple_of(pz_ref[e], G), G), :], zsem)

        for e in range(2 * N_EXPERTS):
            @pl.when(pz_ref[e] >= 0)
            def _():
                zero_copy(e).start()

        for e in range(2 * N_EXPERTS):
            @pl.when(pz_ref[e] >= 0)
            def _():
                zero_copy(e).wait()

    xn_ref[...] = _rms(h_ref[...], g_ref[...], RMS_EPS)

    def issue(r, carry):
        _row_copy(xn_ref, r, xs_ref, dest_ref[2 * r], sem).start(priority=0)
        _row_copy(xn_ref, r, xs_ref, dest_ref[2 * r + 1], sem).start(priority=1)
        return carry

    lax.fori_loop(0, tm, issue, 0, unroll=ROW_DMA_UNROLL)

    def drain(r, carry):
        _row_copy(xn_ref, 0, xs_ref, 0, sem).wait()
        _row_copy(xn_ref, 0, xs_ref, 0, sem).wait()
        return carry

    lax.fori_loop(0, tm, drain, 0, unroll=ROW_DMA_UNROLL)


def dispatch(h, g, dest_flat, zero_tiles, n_slots):
    T, D = h.shape
    tm = min(ROW_TILE, T)
    grid_spec = pltpu.PrefetchScalarGridSpec(
        num_scalar_prefetch=1,
        grid=(T // tm,),
        in_specs=[
            pl.BlockSpec((2 * tm,), lambda i, pz: (i,), memory_space=pltpu.SMEM),
            pl.BlockSpec((tm, D), lambda i, pz: (i, 0)),
            pl.BlockSpec((1, D), lambda i, pz: (0, 0)),
        ],
        out_specs=pl.BlockSpec(memory_space=pl.ANY),
        scratch_shapes=[pltpu.VMEM((tm, D), F32), pltpu.VMEM((MOE_TILE, D), F32),
                        pltpu.SemaphoreType.DMA, pltpu.SemaphoreType.DMA],
    )
    return pl.pallas_call(
        _dispatch_kernel,
        grid_spec=grid_spec,
        out_shape=jax.ShapeDtypeStruct((n_slots, D), F32),
        compiler_params=_cparams(("arbitrary",)),
        name="moe_dispatch",
    )(zero_tiles, dest_flat, h, g.reshape(1, D))


def _gmm_kernel(te_ref, na_ref, x_ref, w1_ref, w3_ref, w2_ref, y_ref, xb_ref, acc_ref):
    b = pl.program_id(0)
    f = pl.program_id(1)

    @pl.when(b < na_ref[0])
    def _():
        @pl.when(f == 0)
        def _():
            xb_ref[...] = x_ref[...].astype(BF16)
            acc_ref[...] = jnp.zeros(acc_ref.shape, F32)

        xb = xb_ref[...]
        a = jnp.dot(xb, w1_ref[...], preferred_element_type=F32)
        c = jnp.dot(xb, w3_ref[...], preferred_element_type=F32)
        hm = (_silu(a) * c).astype(BF16)
        acc_ref[...] += jnp.dot(hm, w2_ref[...], preferred_element_type=F32)

        @pl.when(f == pl.num_programs(1) - 1)
        def _():
            y_ref[...] = acc_ref[...]

    @pl.when((b >= na_ref[0]) & (f == pl.num_programs(1) - 1))
    def _():
        y_ref[...] = jnp.zeros(y_ref.shape, F32)


def gmm(xs, w1, w3, w2, tile_e, n_active):
    P, D = xs.shape
    G = MOE_TILE
    F = w1.shape[2]
    tf = MOE_FF_TILE
    nf = F // tf

    def row_idx(b, f, te, na):
        return (jnp.minimum(b, na[0] - 1), 0)

    def f_idx(b, f, na):
        return jnp.where(b < na[0], f, nf - 1)

    grid_spec = pltpu.PrefetchScalarGridSpec(
        num_scalar_prefetch=2,
        grid=(P // G, nf),
        in_specs=[
            pl.BlockSpec((G, D), row_idx),
            pl.BlockSpec((None, D, tf), lambda b, f, te, na: (te[b], 0, f_idx(b, f, na))),
            pl.BlockSpec((None, D, tf), lambda b, f, te, na: (te[b], 0, f_idx(b, f, na))),
            pl.BlockSpec((None, tf, D), lambda b, f, te, na: (te[b], f_idx(b, f, na), 0)),
        ],
        out_specs=pl.BlockSpec((G, D), lambda b, f, te, na: (b, 0)),
        scratch_shapes=[pltpu.VMEM((G, D), BF16), pltpu.VMEM((G, D), F32)],
    )
    return pl.pallas_call(
        _gmm_kernel,
        grid_spec=grid_spec,
        out_shape=jax.ShapeDtypeStruct((P, D), F32),
        compiler_params=_cparams(("arbitrary", "arbitrary")),
        name="moe_gmm",
    )(tile_e, n_active, xs, w1, w3, w2)


def _combine_ple_kernel(dest_ref, dnext_ref, h_ref, gate_ref, y_ref, g_ref, wg_ref, p_ref, wp_ref, gf_ref,
                        o_ref, yab_ref, sem, *, final_norm):
    i = pl.program_id(0)
    n = pl.num_programs(0)
    tm = h_ref.shape[0]
    slot = lax.rem(i, 2)

    def gather(d_ref, sl):
        def issue(r, carry):
            _row_copy(y_ref, d_ref[2 * r], yab_ref.at[sl, 0], r, sem.at[sl]).start(priority=0)
            _row_copy(y_ref, d_ref[2 * r + 1], yab_ref.at[sl, 1], r, sem.at[sl]).start(priority=1)
            return carry

        lax.fori_loop(0, tm, issue, 0, unroll=ROW_DMA_UNROLL)

    @pl.when(i == 0)
    def _():
        gather(dest_ref, 0)

    @pl.when(i + 1 < n)
    def _():
        gather(dnext_ref, 1 - slot)

    def drain(r, carry):
        _row_copy(y_ref, 0, yab_ref.at[slot, 0], 0, sem.at[slot]).wait()
        _row_copy(y_ref, 0, yab_ref.at[slot, 1], 0, sem.at[slot]).wait()
        return carry

    lax.fori_loop(0, tm, drain, 0, unroll=ROW_DMA_UNROLL)
    gt = gate_ref[...]
    h = h_ref[...] + gt[:, 0:1] * yab_ref[slot, 0] + gt[:, 1:2] * yab_ref[slot, 1]
    o_ref[...] = _ple_math(h, g_ref, wg_ref, p_ref, wp_ref, gf_ref, final_norm)


def combine_ple(h, gates, y, dest_flat, g, wg, p, wp, g_final, final_norm, layer):
    T, D = h.shape
    tm = min(ROW_TILE, T)
    n = T // tm
    return pl.pallas_call(
        functools.partial(_combine_ple_kernel, final_norm=final_norm),
        grid=(n,),
        in_specs=[
            pl.BlockSpec((2 * tm,), lambda i: (i,), memory_space=pltpu.SMEM),
            pl.BlockSpec((2 * tm,), lambda i: (jnp.minimum(i + 1, n - 1),), memory_space=pltpu.SMEM),
            pl.BlockSpec((tm, D), lambda i: (i, 0)),
            pl.BlockSpec((tm, LANES), lambda i: (i, 0)),
            pl.BlockSpec(memory_space=pl.ANY),
            pl.BlockSpec((1, D), lambda i: (0, 0)),
            pl.BlockSpec((None, D, D), lambda i: (layer, 0, 0)),
            pl.BlockSpec((None, tm, PLE_DIM), lambda i: (layer, i, 0)),
            pl.BlockSpec((None, PLE_DIM, D), lambda i: (layer, 0, 0)),
            pl.BlockSpec((1, D), lambda i: (0, 0)),
        ],
        out_specs=pl.BlockSpec((tm, D), lambda i: (i, 0)),
        out_shape=jax.ShapeDtypeStruct((T, D), F32),
        scratch_shapes=[pltpu.VMEM((2, 2, tm, D), F32), pltpu.SemaphoreType.DMA((2,))],
        compiler_params=_cparams(("arbitrary",)),
        name="moe_combine_ple",
    )(dest_flat, dest_flat, h, gates, y, g.reshape(1, D), wg, p, wp, g_final.reshape(1, D))


def _lambda_init(layer_idx):
    return 0.8 - 0.6 * math.exp(-0.3 * layer_idx)


def _even_layer(h, pos_col, pos_row, invf, ln_mix, w_in, w_out, lam_params, subln_g, conv_w, layer_idx, cast):
    qk, vt, gates_conv = even_proj(h, ln_mix, w_in.astype(BF16), pos_col, invf)
    o, narrowed = diff_attention(qk, vt, pos_col, pos_row, lam_params, subln_g, _lambda_init(layer_idx), cast)
    return even_out(o, gates_conv, conv_w, w_out.astype(BF16), h), narrowed


def _odd_mixer_route(h, ln_mix, w_in, conv_w, a_log, dt_bias, onorm_g, w_out, ln_ffn, w_router):
    main_w = 4 * DN_WIDTH
    w_main = w_in[:, :main_w].astype(BF16)
    w_ba = jnp.pad(w_in[:, main_w:], ((0, 0), (0, LANES - 2 * DN_HEADS))).astype(BF16)
    proj, ba = norm_matmul(h, ln_mix, w_main, w_side=w_ba)
    pad8 = lambda v: jnp.pad(v.astype(F32), (DN_HEADS, LANES - 2 * DN_HEADS)).reshape(1, LANES)
    qg, kg, w, u, ai, dl = gdn_prep(proj, ba, conv_w, pad8(a_log), pad8(dt_bias))
    o = gdn_scan(qg, kg, w, u, ai, dl)
    wr = jnp.pad(w_router, ((0, 0), (0, LANES - N_EXPERTS)))
    return odd_out_route(o, proj, onorm_g, w_out.astype(BF16), h, ln_ffn, wr)


def _moe(h, ln_ffn, ri, gates, cnt, w1, w3, w2):
    T = h.shape[0]
    G = MOE_TILE
    counts = cnt[0, :N_EXPERTS].astype(I32)
    padded = ((counts + G - 1) // G) * G
    pends = jnp.cumsum(padded)
    pstarts = pends - padded
    dest = (jnp.take(pstarts, ri[0:2]) + ri[2:4]).T.reshape(-1)
    n_tiles = (2 * T) // G + N_EXPERTS
    tile_start = jnp.arange(n_tiles, dtype=I32) * G
    tile_e = jnp.minimum(jnp.sum(pends[None, :] <= tile_start[:, None], axis=1), N_EXPERTS - 1).astype(I32)
    n_active = (pends[-1:] // G).astype(I32)
    tail = pends[-1] + jnp.arange(N_EXPERTS, dtype=I32) * G
    zero_tiles = jnp.concatenate([jnp.where(padded > 0, pends - G, -1),
                                  jnp.where(tail < n_tiles * G, tail, -1)]).astype(I32)
    xs = dispatch(h, ln_ffn, dest, zero_tiles, n_tiles * G)
    y = gmm(xs, w1, w3, w2, tile_e, n_active)
    return gates, y, dest


def kernel(x, p, positions, ln_mix, ln_ffn, ln_ple, ln_final, w_in_even, w_out_even, lam_q1, lam_k1, lam_q2, lam_k2, subln_gain, conv_w_short, w_in_odd, conv_w_qkv, a_log, dt_bias, onorm_gain, w_out_odd, w1_dense, w3_dense, w2_dense, w_router, w1_moe, w3_moe, w2_moe, w_ple_gate, w_ple_proj):
    B, S, D = x.shape
    T = B * S
    depth = p.shape[0]
    h = x.reshape(T, D)
    pos_col = positions.reshape(T, 1).astype(I32)
    pos_row = positions.reshape(1, T).astype(I32)
    inv_freq = ROPE_THETA ** (-jnp.arange(0, ROT_DIM, 2, dtype=F32) / ROT_DIM)
    invf = jnp.tile(inv_freq, LANES // (ROT_DIM // 2)).reshape(1, LANES)
    w1d, w3d, w2d = (w.astype(BF16) for w in (w1_dense, w3_dense, w2_dense))
    wpg, wpp = w_ple_gate.astype(BF16), w_ple_proj.astype(BF16)
    p_rows = p.reshape(depth, T, PLE_DIM)
    n_odd, n_exp, _, ff = w1_moe.shape
    moe_f32 = [w.reshape(n_odd, -1, w.shape[-1]) for w in (w1_moe, w3_moe, w2_moe)]
    moe_bf16 = None
    for i in range(depth):
        j = i // 2
        if i % 2 == 0:
            lam_params = jnp.stack([lam_q1[j], lam_k1[j], lam_q2[j], lam_k2[j]]).astype(F32)
            cast = tuple((w, j) for w in moe_f32) if i + 1 < depth else ()
            h, narrowed = _even_layer(h, pos_col, pos_row, invf, ln_mix[i], w_in_even[j], w_out_even[j], lam_params,
                                      subln_gain[j], conv_w_short[j], i, cast)
            if narrowed:
                moe_bf16 = (narrowed[0].reshape(n_exp, D, ff), narrowed[1].reshape(n_exp, D, ff),
                            narrowed[2].reshape(n_exp, ff, D))
            h = ffn_ple(h, ln_ffn[i], w1d, w3d, w2d, j, ln_ple[i], wpg, p_rows, wpp, ln_final,
                        final_norm=(i == depth - 1), ple_layer=i)
        else:
            h, ri, gates, cnt = _odd_mixer_route(h, ln_mix[i], w_in_odd[j], conv_w_qkv[j], a_log[j], dt_bias[j],
                                                 onorm_gain[j], w_out_odd[j], ln_ffn[i], w_router[j])
            gates, y, dest = _moe(h, ln_ffn[i], ri, gates, cnt, *moe_bf16)
            h = combine_ple(h, gates, y, dest, ln_ple[i], wpg, p_rows, wpp, ln_final,
                            final_norm=(i == depth - 1), layer=i)
    return h.reshape(B, S, D)
```

```python
import functools
import math

import jax
import jax.numpy as jnp
import numpy as np
from jax import lax
from jax.experimental import pallas as pl
from jax.experimental.pallas import tpu as pltpu

F32 = jnp.float32
BF16 = jnp.bfloat16
I32 = jnp.int32

D_MODEL = 1024
DEPTH = 4
RMS_EPS = 1e-6
DA_HEADS = 4
DA_HEAD_DIM = 64
DA_V_DIM = 2 * DA_HEAD_DIM
ROPE_THETA = 500000.0
ROT_DIM = DA_HEAD_DIM // 4
SUBLN_EPS = 1e-5
SC_WIDTH = 512
DN_HEADS = 8
DN_HEAD_DIM = 128
DN_WIDTH = DN_HEADS * DN_HEAD_DIM
DN_CHUNK = 64
D_FF = 3584
N_EXPERTS = 8
PLE_DIM = 256

LANES = 128
SUBLANES = 8
HALO = 16
VMEM_LIMIT = 52 * 1024 * 1024

NEG_BIG = -1e30

ROW_TILE = 512
PROJ_COL_TILE = 1024
FFN_ROW_TILE = 512
FF_TILE = 1792
MOE_FF_TILE = 1792
ATT_TQ = 1024
ATT_TK = 1024
ATT_EXP_ROWS = 128
ATT_TQ_SPLIT = 256
GDN_TILE = 1024
GDN_PREP_HEADS = 2
GDN_SCAN_HEADS = 8
MOE_TILE = 512
ROW_DMA_UNROLL = 8


def _cparams(sem):
    return pltpu.CompilerParams(dimension_semantics=sem, vmem_limit_bytes=VMEM_LIMIT)


def _rms(x, g, eps):
    ms = jnp.mean(x * x, axis=-1, keepdims=True)
    return x * lax.rsqrt(ms + eps) * g


def _silu(x):
    return x * (1.0 / (1.0 + jnp.exp(-x)))


def _sigmoid(x):
    return 1.0 / (1.0 + jnp.exp(-x))


def _norm_matmul_kernel(x_ref, g_ref, w_ref, *rest):
    xn = _rms(x_ref[...], g_ref[...], RMS_EPS).astype(BF16)
    if len(rest) == 1:
        (o_ref,) = rest
    else:
        ws_ref, o_ref, os_ref = rest
        os_ref[...] = jnp.dot(xn, ws_ref[...], preferred_element_type=F32)
    tn = PROJ_COL_TILE
    for c in range(o_ref.shape[1] // tn):
        o_ref[:, c * tn:(c + 1) * tn] = jnp.dot(
            xn, w_ref[:, c * tn:(c + 1) * tn], preferred_element_type=F32).astype(o_ref.dtype)


def norm_matmul(x, g, w, w_side=None, out_dtype=BF16):
    T, D = x.shape
    N = w.shape[1]
    tm = min(ROW_TILE, T)
    in_specs = [
        pl.BlockSpec((tm, D), lambda i: (i, 0)),
        pl.BlockSpec((1, D), lambda i: (0, 0)),
        pl.BlockSpec((D, N), lambda i: (0, 0)),
    ]
    out_specs = [pl.BlockSpec((tm, N), lambda i: (i, 0))]
    out_shape = [jax.ShapeDtypeStruct((T, N), out_dtype)]
    args = [x, g.reshape(1, D), w]
    if w_side is not None:
        ns = w_side.shape[1]
        in_specs.append(pl.BlockSpec((D, ns), lambda i: (0, 0)))
        out_specs.append(pl.BlockSpec((tm, ns), lambda i: (i, 0)))
        out_shape.append(jax.ShapeDtypeStruct((T, ns), F32))
        args.append(w_side)
    return pl.pallas_call(
        _norm_matmul_kernel,
        grid=(T // tm,),
        in_specs=in_specs,
        out_specs=out_specs,
        out_shape=out_shape,
        compiler_params=_cparams(("parallel",)),
        name="norm_matmul",
    )(*args)


def _even_proj_kernel(x_ref, g_ref, w_ref, pos_ref, invf_ref, qk_ref, vt_ref, rest_ref):
    xn = _rms(x_ref[...], g_ref[...], RMS_EPS).astype(BF16)
    WQK = qk_ref.shape[1]
    WV = vt_ref.shape[0]

    def proj(c0, c1):
        return jnp.dot(xn, w_ref[:, c0:c1], preferred_element_type=F32)

    pos = pos_ref[...].astype(F32)
    ang = pos * invf_ref[...]
    cos_t = jnp.cos(ang)
    sin_t = jnp.sin(ang)
    d = lax.broadcasted_iota(I32, ang.shape, 1) & (DA_HEAD_DIM - 1)
    half = ROT_DIM // 2
    c_mul = jnp.where(d < ROT_DIM, cos_t, 1.0)
    s_mul = jnp.where(d < half, -sin_t, jnp.where(d < ROT_DIM, sin_t, 0.0))
    qk = proj(0, WQK)
    n_q = WQK // (2 * LANES)
    for c in range(WQK // LANES):
        x = qk[:, c * LANES:(c + 1) * LANES]
        swapped = jnp.where(d < half, pltpu.roll(x, LANES - half, 1), pltpu.roll(x, half, 1))
        r = x * c_mul + swapped * s_mul
        if c < n_q:
            r = r * (DA_HEAD_DIM ** -0.5 * math.log2(math.e))
        qk_ref[:, c * LANES:(c + 1) * LANES] = r.astype(BF16)
    v = proj(WQK, WQK + WV)
    for hd in range(DA_HEADS):
        vt_ref[hd * DA_V_DIM:(hd + 1) * DA_V_DIM, :] = v[:, hd * DA_V_DIM:(hd + 1) * DA_V_DIM].T.astype(BF16)
    WR = rest_ref.shape[1]
    for c0 in range(0, WR, PROJ_COL_TILE):
        c1 = min(c0 + PROJ_COL_TILE, WR)
        rest_ref[:, c0:c1] = proj(WQK + WV + c0, WQK + WV + c1).astype(BF16)


def even_proj(x, g, w, positions_col, invf):
    T, D = x.shape
    WQK = 2 * 2 * DA_HEADS * DA_HEAD_DIM
    WV = DA_HEADS * DA_V_DIM
    WR = w.shape[1] - WQK - WV
    tm = min(ROW_TILE, T)
    return pl.pallas_call(
        _even_proj_kernel,
        grid=(T // tm,),
        in_specs=[
            pl.BlockSpec((tm, D), lambda i: (i, 0)),
            pl.BlockSpec((1, D), lambda i: (0, 0)),
            pl.BlockSpec(w.shape, lambda i: (0, 0)),
            pl.BlockSpec((tm, 1), lambda i: (i, 0)),
            pl.BlockSpec((1, LANES), lambda i: (0, 0)),
        ],
        out_specs=[pl.BlockSpec((tm, WQK), lambda i: (i, 0)),
                   pl.BlockSpec((WV, tm), lambda i: (0, i)),
                   pl.BlockSpec((tm, WR), lambda i: (i, 0))],
        out_shape=[jax.ShapeDtypeStruct((T, WQK), BF16), jax.ShapeDtypeStruct((WV, T), BF16),
                   jax.ShapeDtypeStruct((T, WR), BF16)],
        compiler_params=_cparams(("parallel",)),
        name="even_proj",
    )(x, g.reshape(1, D), w, positions_col, invf)


def _attn_kernel(qt_ref, kt_ref, q_ref, k_ref, v_ref, pq_ref, pk_ref, lam_ref, g_ref, *rest, tq, tk, lam0, n_cast):
    cast_in = rest[:n_cast]
    o_ref = rest[n_cast]
    cast_out = rest[n_cast + 1:2 * n_cast + 1]
    m_ref, l_ref, acc_ref = rest[2 * n_cast + 1:]
    for wi_ref, wo_ref in zip(cast_in, cast_out):
        wo_ref[...] = wi_ref[...].astype(BF16)

    p = pl.program_id(1)
    qi = qt_ref[p]
    ki = kt_ref[p]

    @pl.when(ki == 0)
    def _():
        m_ref[...] = jnp.full(m_ref.shape, NEG_BIG, F32)
        l_ref[...] = jnp.zeros(l_ref.shape, F32)
        acc_ref[...] = jnp.zeros(acc_ref.shape, F32)

    tqs = min(tq, ATT_TQ_SPLIT)
    items = [(s, j) for s in range(2) for j in range(tq // tqs)]

    def step(masked):
        q = q_ref[...]
        k = k_ref[...]
        v = v_ref[...]
        def nkeys(j):
            return (j + 1) * tqs if (masked and tq == tk) else tk

        st = {}
        for s, j in items:
            cols = slice(j * tqs, (j + 1) * tqs)
            qs = q[j * tqs:(j + 1) * tqs, s * DA_HEAD_DIM:(s + 1) * DA_HEAD_DIM]
            ks = k[:nkeys(j), s * DA_HEAD_DIM:(s + 1) * DA_HEAD_DIM]
            sc = lax.dot_general(ks, qs, (((1,), (1,)), ((), ())), preferred_element_type=F32)
            if masked:
                sc = jnp.where(pk_ref[0:nkeys(j), :] <= pq_ref[:, cols], sc, NEG_BIG)
            st[s, j] = sc
        for s, j in items:
            cols = slice(j * tqs, (j + 1) * tqs)
            sc = st[s, j]
            nk = nkeys(j)
            m_prev = m_ref[s, :, cols]
            m_new = jnp.maximum(m_prev, jnp.max(sc, axis=0, keepdims=True))
            alpha = jnp.exp2(m_prev - m_new)
            psum = jnp.zeros((1, tqs), F32)
            pchunks = []
            for r0 in range(0, nk, ATT_EXP_ROWS):
                pc = jnp.exp2(sc[r0:r0 + ATT_EXP_ROWS] - m_new)
                psum = psum + jnp.sum(pc, axis=0, keepdims=True)
                pchunks.append(pc.astype(BF16))
            pb = jnp.concatenate(pchunks, axis=0)
            l_ref[s, :, cols] = alpha * l_ref[s, :, cols] + psum
            pv = jnp.dot(v[:, :nk], pb, preferred_element_type=F32)
            acc_ref[s, :, cols] = alpha * acc_ref[s, :, cols] + pv
            m_ref[s, :, cols] = m_new

    crosses = (ki + 1) * tk - 1 > qi * tq

    @pl.when(crosses)
    def _():
        step(True)

    @pl.when(jnp.logical_not(crosses))
    def _():
        step(False)

    @pl.when((ki + 1) * tk >= (qi + 1) * tq)
    def _():
        lm = lam_ref[...]
        s1 = jnp.sum(lm[0:1] * lm[1:2], axis=-1, keepdims=True)
        s2 = jnp.sum(lm[2:3] * lm[3:4], axis=-1, keepdims=True)
        lam = jnp.exp(s1) - jnp.exp(s2) + lam0
        ot = acc_ref[0] * (1.0 / l_ref[0]) - lam * (acc_ref[1] * (1.0 / l_ref[1]))
        ms = jnp.mean(ot * ot, axis=0, keepdims=True)
        ot = ot * lax.rsqrt(ms + SUBLN_EPS) * g_ref[...] * (1.0 - lam0)
        o_ref[...] = ot.T.astype(o_ref.dtype)


def _cast_blocks(rows, steps):
    for nb in range(min(steps, rows), 0, -1):
        if rows % nb == 0 and (rows // nb) % HALO == 0:
            return nb
    raise ValueError((rows, steps))


def diff_attention(qk, vt, pos_col, pos_row, lam_params, subln_g, lam0, cast=()):
    T = qk.shape[0]
    tq = min(ATT_TQ, T)
    tk = min(ATT_TK, T)
    nq = T // tq
    pairs = [(qi, ki) for qi in range(nq) for ki in range(-(-((qi + 1) * tq) // tk))]
    qt = jnp.asarray(np.array([a for a, _ in pairs], np.int32))
    kt = jnp.asarray(np.array([b for _, b in pairs], np.int32))
    H = DA_HEADS
    npairs = len(pairs)
    kern = functools.partial(_attn_kernel, tq=tq, tk=tk, lam0=lam0, n_cast=len(cast))
    cast_in_specs, cast_out_specs, cast_out_shapes = [], [], []
    for w, layer in cast:
        _, rows, cols = w.shape
        nb = _cast_blocks(rows, H * npairs)
        br = rows // nb

        def blk(h, p, nb=nb):
            return jnp.minimum(h * npairs + p, nb - 1)

        cast_in_specs.append(pl.BlockSpec((None, br, cols), lambda h, p, qt, kt, layer=layer, blk=blk: (layer, blk(h, p), 0)))
        cast_out_specs.append(pl.BlockSpec((br, cols), lambda h, p, qt, kt, blk=blk: (blk(h, p), 0)))
        cast_out_shapes.append(jax.ShapeDtypeStruct((rows, cols), BF16))
    grid_spec = pltpu.PrefetchScalarGridSpec(
        num_scalar_prefetch=2,
        grid=(H, npairs),
        in_specs=[
            pl.BlockSpec((tq, LANES), lambda h, p, qt, kt: (qt[p], h)),
            pl.BlockSpec((tk, LANES), lambda h, p, qt, kt: (kt[p], H + h)),
            pl.BlockSpec((DA_V_DIM, tk), lambda h, p, qt, kt: (h, kt[p])),
            pl.BlockSpec((1, tq), lambda h, p, qt, kt: (0, qt[p])),
            pl.BlockSpec((tk, 1), lambda h, p, qt, kt: (kt[p], 0)),
            pl.BlockSpec((4, DA_HEAD_DIM), lambda h, p, qt, kt: (0, 0)),
            pl.BlockSpec((DA_V_DIM, 1), lambda h, p, qt, kt: (0, 0)),
        ] + cast_in_specs,
        out_specs=[pl.BlockSpec((tq, DA_V_DIM), lambda h, p, qt, kt: (qt[p], h))] + cast_out_specs,
        scratch_shapes=[
            pltpu.VMEM((2, 1, tq), F32),
            pltpu.VMEM((2, 1, tq), F32),
            pltpu.VMEM((2, DA_V_DIM, tq), F32),
        ],
    )
    outs = pl.pallas_call(
        kern,
        grid_spec=grid_spec,
        out_shape=[jax.ShapeDtypeStruct((T, H * DA_V_DIM), BF16)] + cast_out_shapes,
        compiler_params=_cparams(("arbitrary", "arbitrary")),
        name="diff_attention",
    )(qt, kt, qk, qk, vt, pos_row, pos_col, lam_params, subln_g.reshape(DA_V_DIM, 1), *[w for w, _ in cast])
    return outs[0], outs[1:]


def _even_out_kernel(o_ref, b_ref, c_ref, x_ref, ch_ref, xh_ref, cw_ref, w_ref, h_ref, out_ref, u_ref):
    tm = o_ref.shape[0]
    u_prev = ch_ref[...].astype(F32) * xh_ref[...].astype(F32)
    u_prev = jnp.where(pl.program_id(0) == 0, 0.0, u_prev)
    u_ref[0:HALO, :] = u_prev
    u_ref[HALO:, :] = c_ref[...].astype(F32) * x_ref[...].astype(F32)
    cw = cw_ref[...]
    conv = (u_ref[HALO - 2:HALO - 2 + tm, :] * cw[0:1]
            + u_ref[HALO - 1:HALO - 1 + tm, :] * cw[1:2]
            + u_ref[HALO:, :] * cw[2:3])
    sc = (b_ref[...].astype(F32) * conv).astype(BF16)
    na = o_ref.shape[1]
    acc = jnp.dot(o_ref[...], w_ref[0:na, :], preferred_element_type=F32)
    acc = acc + jnp.dot(sc, w_ref[na:, :], preferred_element_type=F32)
    out_ref[...] = h_ref[...] + acc


def even_out(attn_o, proj, conv_w, w_out, h):
    T = h.shape[0]
    tm = min(ROW_TILE, T)
    W = SC_WIDTH
    cb = (proj.shape[1] - 3 * W) // W
    hb = tm // HALO
    halo = lambda col: pl.BlockSpec((HALO, W), lambda i: (jnp.maximum(i * hb - 1, 0), col))
    return pl.pallas_call(
        _even_out_kernel,
        grid=(T // tm,),
        in_specs=[
            pl.BlockSpec((tm, attn_o.shape[1]), lambda i: (i, 0)),
            pl.BlockSpec((tm, W), lambda i: (i, cb)),
            pl.BlockSpec((tm, W), lambda i: (i, cb + 1)),
            pl.BlockSpec((tm, W), lambda i: (i, cb + 2)),
            halo(cb + 1),
            halo(cb + 2),
            pl.BlockSpec(conv_w.shape, lambda i: (0, 0)),
            pl.BlockSpec(w_out.shape, lambda i: (0, 0)),
            pl.BlockSpec((tm, D_MODEL), lambda i: (i, 0)),
        ],
        out_specs=pl.BlockSpec((tm, D_MODEL), lambda i: (i, 0)),
        out_shape=jax.ShapeDtypeStruct((T, D_MODEL), F32),
        scratch_shapes=[pltpu.VMEM((tm + HALO, W), F32)],
        compiler_params=_cparams(("parallel",)),
        name="even_out",
    )(attn_o, proj, proj, proj, proj, proj, conv_w, w_out, h)


def _ffn_kernel(h_ref, g_ref, w1_ref, w3_ref, w2_ref, gp_ref, wg_ref, p_ref, wp_ref, gf_ref, o_ref, xn_ref, acc_ref,
                *, final_norm):
    f = pl.program_id(1)

    @pl.when(f == 0)
    def _():
        xn_ref[...] = _rms(h_ref[...], g_ref[...], RMS_EPS).astype(BF16)
        acc_ref[...] = jnp.zeros(acc_ref.shape, F32)

    xn = xn_ref[...]
    a = jnp.dot(xn, w1_ref[...], preferred_element_type=F32)
    b = jnp.dot(xn, w3_ref[...], preferred_element_type=F32)
    hm = (_silu(a) * b).astype(BF16)
    acc_ref[...] += jnp.dot(hm, w2_ref[...], preferred_element_type=F32)

    @pl.when(f == pl.num_programs(1) - 1)
    def _():
        o_ref[...] = _ple_math(h_ref[...] + acc_ref[...], gp_ref, wg_ref, p_ref, wp_ref, gf_ref, final_norm)


def ffn_ple(h, g, w1, w3, w2, layer, g_ple, wg, p, wp, g_final, final_norm, ple_layer):
    T, D = h.shape
    F = w1.shape[2]
    tm = min(FFN_ROW_TILE, T)
    tf = FF_TILE
    return pl.pallas_call(
        functools.partial(_ffn_kernel, final_norm=final_norm),
        grid=(T // tm, F // tf),
        in_specs=[
            pl.BlockSpec((tm, D), lambda i, f: (i, 0)),
            pl.BlockSpec((1, D), lambda i, f: (0, 0)),
            pl.BlockSpec((None, D, tf), lambda i, f: (layer, 0, f)),
            pl.BlockSpec((None, D, tf), lambda i, f: (layer, 0, f)),
            pl.BlockSpec((None, tf, D), lambda i, f: (layer, f, 0)),
            pl.BlockSpec((1, D), lambda i, f: (0, 0)),
            pl.BlockSpec((None, D, D), lambda i, f: (ple_layer, 0, 0)),
            pl.BlockSpec((None, tm, PLE_DIM), lambda i, f: (ple_layer, i, 0)),
            pl.BlockSpec((None, PLE_DIM, D), lambda i, f: (ple_layer, 0, 0)),
            pl.BlockSpec((1, D), lambda i, f: (0, 0)),
        ],
        out_specs=pl.BlockSpec((tm, D), lambda i, f: (i, 0)),
        out_shape=jax.ShapeDtypeStruct((T, D), F32),
        scratch_shapes=[pltpu.VMEM((tm, D), BF16), pltpu.VMEM((tm, D), F32)],
        compiler_params=_cparams(("parallel", "arbitrary")),
        name="ffn_ple",
    )(h, g.reshape(1, D), w1, w3, w2, g_ple.reshape(1, D), wg, p, wp, g_final.reshape(1, D))


def _ple_math(h, g_ref, wg_ref, p_ref, wp_ref, gf_ref, final_norm):
    xn = _rms(h, g_ref[...], RMS_EPS).astype(BF16)
    gate = _sigmoid(jnp.dot(xn, wg_ref[...], preferred_element_type=F32))
    emb = jnp.dot(p_ref[...].astype(BF16), wp_ref[...], preferred_element_type=F32)
    out = h + gate * emb
    if final_norm:
        out = _rms(out, gf_ref[...], RMS_EPS)
    return out


def _dot_hi(a, b):
    return jnp.dot(a, b, preferred_element_type=F32, precision=lax.Precision.HIGHEST)


def _gdn_prep_kernel(q_ref, k_ref, v_ref, qh_ref, kh_ref, vh_ref, ba_ref, cw_ref, alog_ref, dtb_ref,
                     qg_ref, kg_ref, w_ref, u_ref, ai_ref, dl_ref,
                     xs_ref, gc_ref, gl_ref, sg_ref, gct_ref):
    i = pl.program_id(0)
    h = pl.program_id(1)
    R = q_ref.shape[0]
    C = DN_CHUNK
    KC = cw_ref.shape[1]

    def conv_silu(x_ref, halo_ref, j, hp):
        cs = slice(hp * LANES, (hp + 1) * LANES)
        buf = xs_ref.at[3 * hp + j]
        buf[0:HALO, :] = jnp.where(i == 0, 0.0, halo_ref[:, cs].astype(F32))
        buf[HALO:, :] = x_ref[:, cs].astype(F32)
        acc = buf[HALO:, :] * cw_ref[hp, KC - 1:KC, j * LANES:(j + 1) * LANES]
        for t in range(1, KC):
            acc = acc + (buf[HALO - t:HALO - t + R, :]
                         * cw_ref[hp, KC - 1 - t:KC - t, j * LANES:(j + 1) * LANES])
        return _silu(acc)

    def l2n(x):
        return x * lax.rsqrt(jnp.sum(x * x, axis=-1, keepdims=True) + 1e-6)

    chunks = range(R // C)
    rows = [slice(c * C, (c + 1) * C) for c in chunks]
    r64 = lax.broadcasted_iota(I32, (C, C), 0)
    c64 = lax.broadcasted_iota(I32, (C, C), 1)
    incl = r64 >= c64
    strict = r64 > c64

    @pl.when(h == 0)
    def _():
        ba = ba_ref[...]
        sp_in = ba + dtb_ref[...]
        softplus = jnp.maximum(sp_in, 0.0) + jnp.log(1.0 + jnp.exp(-jnp.abs(sp_in)))
        g_all = -jnp.exp(alog_ref[...]) * softplus
        g_wide = jnp.concatenate([g_all[r] for r in rows], axis=1)
        gc_wide = _dot_hi(jnp.where(incl, 1.0, 0.0), g_wide)
        for c in chunks:
            gc_c = gc_wide[:, c * LANES:(c + 1) * LANES]
            gc_ref[rows[c], :] = gc_c
            gl_ref[rows[c], :] = jnp.broadcast_to(gc_c[C - 1:C, :], (C, LANES))
        sg_ref[...] = _sigmoid(ba)
        gct_ref[...] = gc_ref[...].T

    lane = lax.broadcasted_iota(I32, (R, LANES), 1)

    def pick(ref, l):
        return jnp.sum(jnp.where(lane == l, ref[...], 0.0), axis=-1, keepdims=True)

    def bdot(a, b):
        return jnp.dot(a.astype(BF16), b.astype(BF16), preferred_element_type=F32)

    heads = range(qg_ref.shape[1] // LANES)
    items = [(hp, c) for hp in heads for c in chunks]
    decay, kq, rhs, edl = {}, {}, {}, {}
    for hp in heads:
        hh = h * len(heads) + hp
        cs = slice(hp * LANES, (hp + 1) * LANES)
        q = l2n(conv_silu(q_ref, qh_ref, 0, hp)) * (DN_HEAD_DIM ** -0.5)
        k = l2n(conv_silu(k_ref, kh_ref, 1, hp))
        v = conv_silu(v_ref, vh_ref, 2, hp)
        beta = pick(sg_ref, hh)
        gcol = pick(gc_ref, DN_HEADS + hh)
        glast = pick(gl_ref, DN_HEADS + hh)
        grow = gct_ref[pl.ds(DN_HEADS + hh, 1), :]
        eg = jnp.exp(gcol)
        kb = k * beta
        qg_ref[:, cs] = (q * eg).astype(BF16)
        kg_ref[:, cs] = (k * jnp.exp(glast - gcol)).astype(BF16)
        edl[hp] = jnp.broadcast_to(jnp.exp(glast), (R, LANES))
        rhs[hp] = jnp.concatenate([v * beta, kb * eg], axis=-1)
        kbf = k.astype(BF16)
        lhs = jnp.concatenate([kb.astype(BF16).reshape(R // C, C, LANES),
                               q.astype(BF16).reshape(R // C, C, LANES)], axis=1)
        for c in chunks:
            r = rows[c]
            decay[hp, c] = jnp.where(incl, jnp.exp(jnp.where(incl, gcol[r] - grow[:, r], 0.0)), 0.0)
            kq[hp, c] = lax.dot_general(lhs[c], kbf[r], (((1,), (1,)), ((), ())), preferred_element_type=F32)
    pw = {it: -jnp.where(strict, kq[it][:C] * decay[it], 0.0) for it in items}
    n = dict(pw)
    for _ in range(5):
        pw = {it: bdot(pw[it], pw[it]) for it in items}
        n = {it: n[it] + pw[it] + bdot(n[it], pw[it]) for it in items}
    for hp, c in items:
        cs = slice(hp * LANES, (hp + 1) * LANES)
        rc = rhs[hp][rows[c]]
        uw = rc + bdot(n[hp, c], rc)
        u_ref[rows[c], cs] = uw[:, :DN_HEAD_DIM]
        w_ref[rows[c], cs] = uw[:, DN_HEAD_DIM:].astype(BF16)
        ai_ref[hp, rows[c], :] = (kq[hp, c][C:] * decay[hp, c]).astype(BF16)
        dl_ref[hp, c:c + 1, :] = edl[hp][c * C:c * C + 1, :]


def gdn_prep(proj, ba, conv_w, alog_l, dtb_l):
    T = proj.shape[0]
    R = min(GDN_TILE, T)
    H = DN_HEADS
    nchunk = T // DN_CHUNK
    hb = R // HALO
    HP = GDN_PREP_HEADS
    W = HP * LANES
    col = lambda j: pl.BlockSpec((R, W), lambda i, h: (i, j * (H // HP) + h))
    halo = lambda j: pl.BlockSpec((HALO, W), lambda i, h: (jnp.maximum(i * hb - 1, 0), j * (H // HP) + h))
    KC = conv_w.shape[0]
    cw = conv_w.reshape(KC, 3, H, LANES).transpose(2, 0, 1, 3).reshape(H, KC, 3 * LANES)
    cw_spec = pl.BlockSpec((HP, KC, 3 * LANES), lambda i, h: (h, 0, 0))
    row_out = lambda dt: jax.ShapeDtypeStruct((T, DN_WIDTH), dt)
    out_col = pl.BlockSpec((R, W), lambda i, h: (i, h))
    return pl.pallas_call(
        _gdn_prep_kernel,
        grid=(T // R, H // HP),
        in_specs=[col(0), col(1), col(2), halo(0), halo(1), halo(2),
                  pl.BlockSpec((R, LANES), lambda i, h: (i, 0)),
                  cw_spec,
                  pl.BlockSpec((1, LANES), lambda i, h: (0, 0)),
                  pl.BlockSpec((1, LANES), lambda i, h: (0, 0))],
        out_specs=[out_col, out_col, out_col, out_col,
                   pl.BlockSpec((HP, R, DN_CHUNK), lambda i, h: (h, i, 0)),
                   pl.BlockSpec((HP, R // DN_CHUNK, LANES), lambda i, h: (h, i, 0))],
        out_shape=[row_out(BF16), row_out(BF16), row_out(BF16), row_out(F32),
                   jax.ShapeDtypeStruct((H, T, DN_CHUNK), BF16),
                   jax.ShapeDtypeStruct((H, nchunk, LANES), F32)],
        scratch_shapes=[pltpu.VMEM((3 * HP, R + HALO, LANES), F32),
                        pltpu.VMEM((R, LANES), F32), pltpu.VMEM((R, LANES), F32), pltpu.VMEM((R, LANES), F32),
                        pltpu.VMEM((LANES, R), F32)],
        compiler_params=_cparams(("parallel", "arbitrary")),
        name="gdn_prep",
    )(proj, proj, proj, proj, proj, proj, ba, cw, alog_l, dtb_l)


def _gdn_scan_kernel(qg_ref, kg_ref, w_ref, u_ref, ai_ref, dl_ref, o_ref, s_ref):
    @pl.when(pl.program_id(1) == 0)
    def _():
        s_ref[...] = jnp.zeros(s_ref.shape, F32)

    C = DN_CHUNK
    D = DN_HEAD_DIM
    heads = range(s_ref.shape[0])

    def chunk(c, carry):
        rows = pl.ds(pl.multiple_of(c * C, C), C)
        s = [s_ref[hh] for hh in heads]
        sb = [x.astype(BF16) for x in s]
        cols = [slice(hh * D, (hh + 1) * D) for hh in heads]
        r1 = [jnp.dot(jnp.concatenate([w_ref[rows, cols[hh]], qg_ref[rows, cols[hh]]], axis=0), sb[hh],
                      preferred_element_type=F32) for hh in heads]
        vb = [(u_ref[rows, cols[hh]] - r1[hh][:C]).astype(BF16) for hh in heads]
        for hh in heads:
            o_ref[rows, cols[hh]] = r1[hh][C:] + jnp.dot(ai_ref[hh, rows, :], vb[hh], preferred_element_type=F32)
        upd = [lax.dot_general(kg_ref[rows, cols[hh]], vb[hh], (((0,), (0,)), ((), ())),
                               preferred_element_type=F32) for hh in heads]
        for hh in heads:
            s_ref[hh] = s[hh] * dl_ref[hh, pl.ds(c, 1), :] + upd[hh]
        return carry

    lax.fori_loop(0, qg_ref.shape[0] // C, chunk, 0)


def gdn_scan(qg, kg, w, u, ai, dl):
    T = qg.shape[0]
    R = min(GDN_TILE, T)
    H = DN_HEADS
    HB = GDN_SCAN_HEADS
    col = pl.BlockSpec((R, HB * LANES), lambda h, i: (i, h))
    return pl.pallas_call(
        _gdn_scan_kernel,
        grid=(H // HB, T // R),
        in_specs=[col, col, col, col,
                  pl.BlockSpec((HB, R, DN_CHUNK), lambda h, i: (h, i, 0)),
                  pl.BlockSpec((HB, R // DN_CHUNK, LANES), lambda h, i: (h, i, 0))],
        out_specs=col,
        out_shape=jax.ShapeDtypeStruct((T, DN_WIDTH), F32),
        scratch_shapes=[pltpu.VMEM((HB, DN_HEAD_DIM, DN_HEAD_DIM), F32)],
        compiler_params=_cparams(("parallel", "arbitrary")),
        name="gdn_scan",
    )(qg, kg, w, u, ai, dl)


def _odd_out_math(o_ref, z_ref, g_ref, w_ref, h_ref):
    g = g_ref[...]
    parts = []
    for hd in range(DN_HEADS):
        sl = slice(hd * DN_HEAD_DIM, (hd + 1) * DN_HEAD_DIM)
        parts.append((_rms(o_ref[:, sl], g, RMS_EPS) * _silu(z_ref[:, sl].astype(F32))).astype(BF16))
    y = jnp.concatenate(parts, axis=-1)
    return h_ref[...] + jnp.dot(y, w_ref[...], preferred_element_type=F32)


def _route_math(h, g_ref, wr_ref, ri_ref, gate_ref, cnt_ref, carry_ref):
    i = pl.program_id(0)

    @pl.when(i == 0)
    def _():
        carry_ref[...] = jnp.zeros(carry_ref.shape, F32)

    xn = _rms(h, g_ref[...], RMS_EPS)
    logits = jnp.dot(xn.astype(BF16), wr_ref[...].astype(BF16), preferred_element_type=F32)
    tm = logits.shape[0]
    lane = lax.broadcasted_iota(I32, logits.shape, 1)
    logits = jnp.where(lane < N_EXPERTS, logits, NEG_BIG)
    lane_f = lane.astype(F32)
    m1 = jnp.max(logits, axis=-1, keepdims=True)
    i1 = jnp.min(jnp.where(logits == m1, lane_f, float(LANES)), axis=-1, keepdims=True)
    rest = jnp.where(lane_f == i1, NEG_BIG, logits)
    m2 = jnp.max(rest, axis=-1, keepdims=True)
    i2 = jnp.min(jnp.where(rest == m2, lane_f, float(LANES)), axis=-1, keepdims=True)
    e = jnp.exp(m2 - m1)
    g1 = 1.0 / (1.0 + e)
    g2 = e / (1.0 + e)
    oh1 = lane_f == i1
    oh2 = lane_f == i2
    i1 = i1.astype(I32)
    i2 = i2.astype(I32)
    oh = jnp.where(oh1 | oh2, 1.0, 0.0)
    ri_ = lax.broadcasted_iota(I32, (tm, tm), 0)
    ci_ = lax.broadcasted_iota(I32, (tm, tm), 1)
    below = jnp.where(ri_ > ci_, 1.0, 0.0).astype(BF16)
    ex = jnp.dot(below, oh.astype(BF16), preferred_element_type=F32) + carry_ref[0:1, :]
    r1 = jnp.sum(jnp.where(oh1, ex, 0.0), axis=-1, keepdims=True).astype(I32)
    r2 = jnp.sum(jnp.where(oh2, ex, 0.0), axis=-1, keepdims=True).astype(I32)
    packed = jnp.where(lane == 0, i1, jnp.where(lane == 1, i2, jnp.where(lane == 2, r1, r2))).astype(F32)
    pick_rows = jnp.where(lax.broadcasted_iota(I32, (SUBLANES, LANES), 0)
                          == lax.broadcasted_iota(I32, (SUBLANES, LANES), 1), 1.0, 0.0)
    ri_ref[...] = lax.dot_general(pick_rows, packed, (((1,), (1,)), ((), ())), preferred_element_type=F32,
                                  precision=lax.Precision.HIGHEST).astype(I32)
    gate_ref[...] = jnp.where(lane == 0, g1, g2)
    carry_ref[...] = carry_ref[...] + jnp.sum(oh, axis=0, keepdims=True)
    cnt_ref[...] = carry_ref[...]


def _odd_out_route_kernel(o_ref, z_ref, g_ref, w_ref, h_ref, gr_ref, wr_ref,
                          out_ref, ri_ref, gate_ref, cnt_ref, carry_ref):
    h = _odd_out_math(o_ref, z_ref, g_ref, w_ref, h_ref)
    out_ref[...] = h
    _route_math(h, gr_ref, wr_ref, ri_ref, gate_ref, cnt_ref, carry_ref)


def odd_out_route(o, proj, onorm_g, w_out, h, g_route, wr):
    T, D = h.shape
    tm = min(ROW_TILE, T)
    zb = 3 * DN_WIDTH // DN_WIDTH
    return pl.pallas_call(
        _odd_out_route_kernel,
        grid=(T // tm,),
        in_specs=[
            pl.BlockSpec((tm, DN_WIDTH), lambda i: (i, 0)),
            pl.BlockSpec((tm, DN_WIDTH), lambda i: (i, zb)),
            pl.BlockSpec((1, DN_HEAD_DIM), lambda i: (0, 0)),
            pl.BlockSpec(w_out.shape, lambda i: (0, 0)),
            pl.BlockSpec((tm, D), lambda i: (i, 0)),
            pl.BlockSpec((1, D), lambda i: (0, 0)),
            pl.BlockSpec((D, LANES), lambda i: (0, 0)),
        ],
        out_specs=[
            pl.BlockSpec((tm, D), lambda i: (i, 0)),
            pl.BlockSpec((SUBLANES, tm), lambda i: (0, i)),
            pl.BlockSpec((tm, LANES), lambda i: (i, 0)),
            pl.BlockSpec((SUBLANES, LANES), lambda i: (0, 0)),
        ],
        out_shape=[
            jax.ShapeDtypeStruct((T, D), F32),
            jax.ShapeDtypeStruct((SUBLANES, T), I32),
            jax.ShapeDtypeStruct((T, LANES), F32),
            jax.ShapeDtypeStruct((SUBLANES, LANES), F32),
        ],
        scratch_shapes=[pltpu.VMEM((SUBLANES, LANES), F32)],
        compiler_params=_cparams(("arbitrary",)),
        name="odd_out_route",
    )(o, proj, onorm_g.reshape(1, DN_HEAD_DIM), w_out, h, g_route.reshape(1, D), wr)


def _row_copy(src_ref, s, dst_ref, d, sem):
    return pltpu.make_async_copy(src_ref.at[pl.ds(s, 1), :], dst_ref.at[pl.ds(d, 1), :], sem)


def _dispatch_kernel(pz_ref, dest_ref, h_ref, g_ref, xs_ref, xn_ref, zb_ref, sem, zsem):
    tm = h_ref.shape[0]
    G = zb_ref.shape[0]

    @pl.when(pl.program_id(0) == 0)
    def _():
        zb_ref[...] = jnp.zeros(zb_ref.shape, F32)

        def zero_copy(e):
            return pltpu.make_async_copy(zb_ref, xs_ref.at[pl.ds(pl.multiple_of(pz_ref[e], G), G), :], zsem)

        for e in range(2 * N_EXPERTS):
            @pl.when(pz_ref[e] >= 0)
            def _():
                zero_copy(e).start()

        for e in range(2 * N_EXPERTS):
            @pl.when(pz_ref[e] >= 0)
            def _():
                zero_copy(e).wait()

    xn_ref[...] = _rms(h_ref[...], g_ref[...], RMS_EPS)

    def issue(r, carry):
        _row_copy(xn_ref, r, xs_ref, dest_ref[2 * r], sem).start(priority=0)
        _row_copy(xn_ref, r, xs_ref, dest_ref[2 * r + 1], sem).start(priority=1)
        return carry

    lax.fori_loop(0, tm, issue, 0, unroll=ROW_DMA_UNROLL)

    def drain(r, carry):
        _row_copy(xn_ref, 0, xs_ref, 0, sem).wait()
        _row_copy(xn_ref, 0, xs_ref, 0, sem).wait()
        return carry

    lax.fori_loop(0, tm, drain, 0, unroll=ROW_DMA_UNROLL)


def dispatch(h, g, dest_flat, zero_tiles, n_slots):
    T, D = h.shape
    tm = min(ROW_TILE, T)
    grid_spec = pltpu.PrefetchScalarGridSpec(
        num_scalar_prefetch=1,
        grid=(T // tm,),
        in_specs=[
            pl.BlockSpec((2 * tm,), lambda i, pz: (i,), memory_space=pltpu.SMEM),
            pl.BlockSpec((tm, D), lambda i, pz: (i, 0)),
            pl.BlockSpec((1, D), lambda i, pz: (0, 0)),
        ],
        out_specs=pl.BlockSpec(memory_space=pl.ANY),
        scratch_shapes=[pltpu.VMEM((tm, D), F32), pltpu.VMEM((MOE_TILE, D), F32),
                        pltpu.SemaphoreType.DMA, pltpu.SemaphoreType.DMA],
    )
    return pl.pallas_call(
        _dispatch_kernel,
        grid_spec=grid_spec,
        out_shape=jax.ShapeDtypeStruct((n_slots, D), F32),
        compiler_params=_cparams(("arbitrary",)),
        name="moe_dispatch",
    )(zero_tiles, dest_flat, h, g.reshape(1, D))


def _gmm_kernel(te_ref, na_ref, x_ref, w1_ref, w3_ref, w2_ref, y_ref, xb_ref, acc_ref):
    b = pl.program_id(0)
    f = pl.program_id(1)

    @pl.when(b < na_ref[0])
    def _():
        @pl.when(f == 0)
        def _():
            xb_ref[...] = x_ref[...].astype(BF16)
            acc_ref[...] = jnp.zeros(acc_ref.shape, F32)

        xb = xb_ref[...]
        a = jnp.dot(xb, w1_ref[...], preferred_element_type=F32)
        c = jnp.dot(xb, w3_ref[...], preferred_element_type=F32)
        hm = (_silu(a) * c).astype(BF16)
        acc_ref[...] += jnp.dot(hm, w2_ref[...], preferred_element_type=F32)

        @pl.when(f == pl.num_programs(1) - 1)
        def _():
            y_ref[...] = acc_ref[...]

    @pl.when((b >= na_ref[0]) & (f == pl.num_programs(1) - 1))
    def _():
        y_ref[...] = jnp.zeros(y_ref.shape, F32)


def gmm(xs, w1, w3, w2, tile_e, n_active):
    P, D = xs.shape
    G = MOE_TILE
    F = w1.shape[2]
    tf = MOE_FF_TILE
    nf = F // tf

    def row_idx(b, f, te, na):
        return (jnp.minimum(b, na[0] - 1), 0)

    def f_idx(b, f, na):
        return jnp.where(b < na[0], f, nf - 1)

    grid_spec = pltpu.PrefetchScalarGridSpec(
        num_scalar_prefetch=2,
        grid=(P // G, nf),
        in_specs=[
            pl.BlockSpec((G, D), row_idx),
            pl.BlockSpec((None, D, tf), lambda b, f, te, na: (te[b], 0, f_idx(b, f, na))),
            pl.BlockSpec((None, D, tf), lambda b, f, te, na: (te[b], 0, f_idx(b, f, na))),
            pl.BlockSpec((None, tf, D), lambda b, f, te, na: (te[b], f_idx(b, f, na), 0)),
        ],
        out_specs=pl.BlockSpec((G, D), lambda b, f, te, na: (b, 0)),
        scratch_shapes=[pltpu.VMEM((G, D), BF16), pltpu.VMEM((G, D), F32)],
    )
    return pl.pallas_call(
        _gmm_kernel,
        grid_spec=grid_spec,
        out_shape=jax.ShapeDtypeStruct((P, D), F32),
        compiler_params=_cparams(("arbitrary", "arbitrary")),
        name="moe_gmm",
    )(tile_e, n_active, xs, w1, w3, w2)


def _combine_ple_kernel(dest_ref, dnext_ref, h_ref, gate_ref, y_ref, g_ref, wg_ref, p_ref, wp_ref, gf_ref,
                        o_ref, yab_ref, sem, *, final_norm):
    i = pl.program_id(0)
    n = pl.num_programs(0)
    tm = h_ref.shape[0]
    slot = lax.rem(i, 2)

    def gather(d_ref, sl):
        def issue(r, carry):
            _row_copy(y_ref, d_ref[2 * r], yab_ref.at[sl, 0], r, sem.at[sl]).start(priority=0)
            _row_copy(y_ref, d_ref[2 * r + 1], yab_ref.at[sl, 1], r, sem.at[sl]).start(priority=1)
            return carry

        lax.fori_loop(0, tm, issue, 0, unroll=ROW_DMA_UNROLL)

    @pl.when(i == 0)
    def _():
        gather(dest_ref, 0)

    @pl.when(i + 1 < n)
    def _():
        gather(dnext_ref, 1 - slot)

    def drain(r, carry):
        _row_copy(y_ref, 0, yab_ref.at[slot, 0], 0, sem.at[slot]).wait()
        _row_copy(y_ref, 0, yab_ref.at[slot, 1], 0, sem.at[slot]).wait()
        return carry

    lax.fori_loop(0, tm, drain, 0, unroll=ROW_DMA_UNROLL)
    gt = gate_ref[...]
    h = h_ref[...] + gt[:, 0:1] * yab_ref[slot, 0] + gt[:, 1:2] * yab_ref[slot, 1]
    o_ref[...] = _ple_math(h, g_ref, wg_ref, p_ref, wp_ref, gf_ref, final_norm)


def combine_ple(h, gates, y, dest_flat, g, wg, p, wp, g_final, final_norm, layer):
    T, D = h.shape
    tm = min(ROW_TILE, T)
    n = T // tm
    return pl.pallas_call(
        functools.partial(_combine_ple_kernel, final_norm=final_norm),
        grid=(n,),
        in_specs=[
            pl.BlockSpec((2 * tm,), lambda i: (i,), memory_space=pltpu.SMEM),
            pl.BlockSpec((2 * tm,), lambda i: (jnp.minimum(i + 1, n - 1),), memory_space=pltpu.SMEM),
            pl.BlockSpec((tm, D), lambda i: (i, 0)),
            pl.BlockSpec((tm, LANES), lambda i: (i, 0)),
            pl.BlockSpec(memory_space=pl.ANY),
            pl.BlockSpec((1, D), lambda i: (0, 0)),
            pl.BlockSpec((None, D, D), lambda i: (layer, 0, 0)),
            pl.BlockSpec((None, tm, PLE_DIM), lambda i: (layer, i, 0)),
            pl.BlockSpec((None, PLE_DIM, D), lambda i: (layer, 0, 0)),
            pl.BlockSpec((1, D), lambda i: (0, 0)),
        ],
        out_specs=pl.BlockSpec((tm, D), lambda i: (i, 0)),
        out_shape=jax.ShapeDtypeStruct((T, D), F32),
        scratch_shapes=[pltpu.VMEM((2, 2, tm, D), F32), pltpu.SemaphoreType.DMA((2,))],
        compiler_params=_cparams(("arbitrary",)),
        name="moe_combine_ple",
    )(dest_flat, dest_flat, h, gates, y, g.reshape(1, D), wg, p, wp, g_final.reshape(1, D))


def _lambda_init(layer_idx):
    return 0.8 - 0.6 * math.exp(-0.3 * layer_idx)


def _even_layer(h, pos_col, pos_row, invf, ln_mix, w_in, w_out, lam_params, subln_g, conv_w, layer_idx, cast):
    qk, vt, gates_conv = even_proj(h, ln_mix, w_in.astype(BF16), pos_col, invf)
    o, narrowed = diff_attention(qk, vt, pos_col, pos_row, lam_params, subln_g, _lambda_init(layer_idx), cast)
    return even_out(o, gates_conv, conv_w, w_out.astype(BF16), h), narrowed


def _odd_mixer_route(h, ln_mix, w_in, conv_w, a_log, dt_bias, onorm_g, w_out, ln_ffn, w_router):
    main_w = 4 * DN_WIDTH
    w_main = w_in[:, :main_w].astype(BF16)
    w_ba = jnp.pad(w_in[:, main_w:], ((0, 0), (0, LANES - 2 * DN_HEADS))).astype(BF16)
    proj, ba = norm_matmul(h, ln_mix, w_main, w_side=w_ba)
    pad8 = lambda v: jnp.pad(v.astype(F32), (DN_HEADS, LANES - 2 * DN_HEADS)).reshape(1, LANES)
    qg, kg, w, u, ai, dl = gdn_prep(proj, ba, conv_w, pad8(a_log), pad8(dt_bias))
    o = gdn_scan(qg, kg, w, u, ai, dl)
    wr = jnp.pad(w_router, ((0, 0), (0, LANES - N_EXPERTS)))
    return odd_out_route(o, proj, onorm_g, w_out.astype(BF16), h, ln_ffn, wr)


def _moe(h, ln_ffn, ri, gates, cnt, w1, w3, w2):
    T = h.shape[0]
    G = MOE_TILE
    counts = cnt[0, :N_EXPERTS].astype(I32)
    padded = ((counts + G - 1) // G) * G
    pends = jnp.cumsum(padded)
    pstarts = pends - padded
    region = jnp.sum(jnp.where(ri[0:2, :, None] == jnp.arange(N_EXPERTS, dtype=I32), pstarts, 0), axis=-1)
    dest = (region + ri[2:4]).T.reshape(-1)
    n_tiles = (2 * T) // G + N_EXPERTS
    tile_start = jnp.arange(n_tiles, dtype=I32) * G
    tile_e = jnp.minimum(jnp.sum(pends[None, :] <= tile_start[:, None], axis=1), N_EXPERTS - 1).astype(I32)
    n_active = (pends[-1:] // G).astype(I32)
    tail = pends[-1] + jnp.arange(N_EXPERTS, dtype=I32) * G
    zero_tiles = jnp.concatenate([jnp.where(padded > 0, pends - G, -1),
                                  jnp.where(tail < n_tiles * G, tail, -1)]).astype(I32)
    xs = dispatch(h, ln_ffn, dest, zero_tiles, n_tiles * G)
    y = gmm(xs, w1, w3, w2, tile_e, n_active)
    return gates, y, dest


def kernel(x, p, positions, ln_mix, ln_ffn, ln_ple, ln_final, w_in_even, w_out_even, lam_q1, lam_k1, lam_q2, lam_k2, subln_gain, conv_w_short, w_in_odd, conv_w_qkv, a_log, dt_bias, onorm_gain, w_out_odd, w1_dense, w3_dense, w2_dense, w_router, w1_moe, w3_moe, w2_moe, w_ple_gate, w_ple_proj):
    B, S, D = x.shape
    T = B * S
    depth = p.shape[0]
    h = x.reshape(T, D)
    pos_col = positions.reshape(T, 1).astype(I32)
    pos_row = positions.reshape(1, T).astype(I32)
    inv_freq = ROPE_THETA ** (-jnp.arange(0, ROT_DIM, 2, dtype=F32) / ROT_DIM)
    invf = jnp.tile(inv_freq, LANES // (ROT_DIM // 2)).reshape(1, LANES)
    w1d, w3d, w2d = (w.astype(BF16) for w in (w1_dense, w3_dense, w2_dense))
    wpg, wpp = w_ple_gate.astype(BF16), w_ple_proj.astype(BF16)
    p_rows = p.reshape(depth, T, PLE_DIM)
    n_odd, n_exp, _, ff = w1_moe.shape
    moe_f32 = [w.reshape(n_odd, -1, w.shape[-1]) for w in (w1_moe, w3_moe, w2_moe)]
    moe_bf16 = None
    for i in range(depth):
        j = i // 2
        if i % 2 == 0:
            lam_params = jnp.stack([lam_q1[j], lam_k1[j], lam_q2[j], lam_k2[j]]).astype(F32)
            cast = tuple((w, j) for w in moe_f32) if i + 1 < depth else ()
            h, narrowed = _even_layer(h, pos_col, pos_row, invf, ln_mix[i], w_in_even[j], w_out_even[j], lam_params,
                                      subln_gain[j], conv_w_short[j], i, cast)
            if narrowed:
                moe_bf16 = (narrowed[0].reshape(n_exp, D, ff), narrowed[1].reshape(n_exp, D, ff),
                            narrowed[2].reshape(n_exp, ff, D))
            h = ffn_ple(h, ln_ffn[i], w1d, w3d, w2d, j, ln_ple[i], wpg, p_rows, wpp, ln_final,
                        final_norm=(i == depth - 1), ple_layer=i)
        else:
            h, ri, gates, cnt = _odd_mixer_route(h, ln_mix[i], w_in_odd[j], conv_w_qkv[j], a_log[j], dt_bias[j],
                                                 onorm_gain[j], w_out_odd[j], ln_ffn[i], w_router[j])
            gates, y, dest = _moe(h, ln_ffn[i], ri, gates, cnt, *moe_bf16)
            h = combine_ple(h, gates, y, dest, ln_ple[i], wpg, p_rows, wpp, ln_final,
                            final_norm=(i == depth - 1), layer=i)
    return h.reshape(B, S, D)
```
